```python
import jax, jax.numpy as jnp
from jax import lax
import numpy as np

D_MODEL = 1024
BATCH = 2
SEQ = 8192
DEPTH = 1

EPS = 1e-6
N_HEADS_ATT = 8
N_KV_HEADS = 2
HEAD_DIM_ATT = 64
GROUP_SIZE = N_HEADS_ATT // N_KV_HEADS
WINDOW = 128
BLOCK = 128
N_HEADS_M = 4
HEAD_DIM_M = 128
CHUNK = 128
CONV_WIDTH = 3
ATT_Q = N_HEADS_ATT * HEAD_DIM_ATT
ATT_KV = N_KV_HEADS * HEAD_DIM_ATT
M_W = N_HEADS_M * HEAD_DIM_M
MIX_WIDTH = ATT_Q + M_W
N_GATES = 4 * N_HEADS_M
IN_SIZES = [ATT_Q, ATT_KV, ATT_KV, M_W, M_W, M_W, M_W, N_GATES]
IN_COLS = int(sum(IN_SIZES))
IN_SPLITS = [int(s) for s in np.cumsum(IN_SIZES)[:-1]]
D_FF = ((8 * D_MODEL // 3 + 255) // 256) * 256
N_MOD = 6

kernel_name = "hymba_style_attn_mlstm_encoder_layer"


def rms_norm(x, g):
    xf = x.astype(jnp.float32)
    out = xf * lax.rsqrt(jnp.mean(xf * xf, axis=-1, keepdims=True) + EPS)
    return (out * g.astype(jnp.float32)).astype(x.dtype)


def centred_dwconv(u, w, b):
    k = w.shape[0]
    pad = k // 2
    s = u.shape[1]
    up = jnp.pad(u, ((0, 0), (pad, pad), (0, 0)))
    out = up[:, 0:s] * w[0]
    for j in range(1, k):
        out = out + up[:, j:j + s] * w[j]
    return out + b


def banded_attention(q, k, v, sink):
    b, s, _, hd = q.shape
    nb = s // BLOCK
    qb = q.astype(jnp.float32).reshape(b, nb, BLOCK, N_KV_HEADS, GROUP_SIZE, hd)

    def key_blocks(t):
        tp = jnp.pad(t.astype(jnp.float32), ((0, 0), (BLOCK, BLOCK), (0, 0), (0, 0)))
        tp = tp.reshape(b, nb + 2, BLOCK, N_KV_HEADS, hd)
        return jnp.concatenate([tp[:, :-2], tp[:, 1:-1], tp[:, 2:]], axis=2)

    kb = key_blocks(k)
    vb = key_blocks(v)
    scores = jnp.einsum('bnqkgd,bnskd->bnkgqs', qb, kb) * (hd ** -0.5)
    qpos = jnp.arange(nb)[:, None] * BLOCK + jnp.arange(BLOCK)[None, :]
    kpos = jnp.arange(nb)[:, None] * BLOCK - BLOCK + jnp.arange(3 * BLOCK)[None]
    dist = jnp.abs(kpos[:, None, :] - qpos[:, :, None])
    valid = (dist <= WINDOW) & (kpos >= 0)[:, None, :] & (kpos < s)[:, None, :]
    slopes = 2.0 ** (-8.0 * (jnp.arange(N_HEADS_ATT, dtype=jnp.float32) + 1.0) / N_HEADS_ATT)
    slopes = slopes.reshape(N_KV_HEADS, GROUP_SIZE)
    logits = scores - slopes[None, None, :, :, None, None] * dist[None, :, None, None].astype(jnp.float32)
    logits = jnp.where(valid[None, :, None, None], logits, -jnp.inf)
    sink_l = sink.astype(jnp.float32).reshape(N_KV_HEADS, GROUP_SIZE)[None, None, :, :, None]
    mx = jnp.maximum(jnp.max(logits, axis=-1), sink_l)
    p = jnp.exp(logits - mx[..., None])
    den = jnp.sum(p, axis=-1) + jnp.exp(sink_l - mx)
    p = p / den[..., None]
    out = jnp.einsum('bnkgqs,bnskd->bnqkgd', p, vb)
    return out.reshape(b, s, N_HEADS_ATT * hd)


def mlstm_chunkwise(q, k, v, i_pre, f_pre):
    b, s, h, d = q.shape
    nc = s // CHUNK

    def blocks(t):
        t = t.reshape((b, nc, CHUNK, h) + t.shape[3:])
        return jnp.moveaxis(t, 3, 2)

    qc = blocks(q) * (d ** -0.5)
    kc = blocks(k)
    vc = blocks(v)
    ig = blocks(i_pre)
    logf = blocks(jax.nn.log_sigmoid(f_pre))
    bcum = jnp.cumsum(logf, axis=-1)
    b_last = bcum[..., -1]
    a = b_last[..., None] - bcum + ig
    a_max = jnp.max(a, axis=-1)
    w = jnp.exp(a - a_max[..., None])
    kv_c = jnp.einsum('bchs,bchsk,bchsv->bchkv', w, kc, vc)
    n_c = jnp.einsum('bchs,bchsk->bchk', w, kc)

    def step(carry, inp):
        c_st, n_st, m_st = carry
        bl, am, kv, nn = inp
        m_new = jnp.maximum(bl + m_st, am)
        decay = jnp.exp(bl + m_st - m_new)
        inj = jnp.exp(am - m_new)
        c_new = decay[..., None, None] * c_st + inj[..., None, None] * kv
        n_new = decay[..., None] * n_st + inj[..., None] * nn
        return (c_new, n_new, m_new), (c_st, n_st, m_st)

    init = (jnp.zeros((b, h, d, d), jnp.float32),
            jnp.zeros((b, h, d), jnp.float32),
            jnp.zeros((b, h), jnp.float32))
    xs = (jnp.moveaxis(b_last, 1, 0), jnp.moveaxis(a_max, 1, 0),
          jnp.moveaxis(kv_c, 1, 0), jnp.moveaxis(n_c, 1, 0))
    _, (c_prev, n_prev, m_prev) = lax.scan(step, init, xs)
    c_prev = jnp.moveaxis(c_prev, 0, 1)
    n_prev = jnp.moveaxis(n_prev, 0, 1)
    m_prev = jnp.moveaxis(m_prev, 0, 1)

    tril = jnp.tril(jnp.ones((CHUNK, CHUNK), dtype=bool))
    dlog = bcum[..., :, None] - bcum[..., None, :] + ig[..., None, :]
    dlog = jnp.where(tril, dlog, -jnp.inf)
    g = bcum + m_prev[..., None]
    m_t = jnp.maximum(g, jnp.max(dlog, axis=-1))
    dw = jnp.exp(dlog - m_t[..., None])
    sc = jnp.einsum('bchtd,bchsd->bchts', qc, kc) * dw
    inter = jnp.exp(g - m_t)
    num = (jnp.einsum('bchts,bchsv->bchtv', sc, vc)
           + inter[..., None] * jnp.einsum('bchtk,bchkv->bchtv', qc, c_prev))
    den = jnp.sum(sc, axis=-1) + inter * jnp.einsum('bchtk,bchk->bcht', qc, n_prev)
    hout = num / jnp.maximum(jnp.abs(den), jnp.exp(-m_t))[..., None]
    return jnp.moveaxis(hout, 2, 3).reshape(b, s, h, d)


def hybrid_layer(x, c, w_mod, b_mod, g_norm1, w_in, conv_w, conv_b, b_gates, sink,
                 g_attn_out, g_mlstm_out, w_out, g_norm2, w_ffn_in, w_ffn_out):
    b, s, _ = x.shape
    mod = jax.nn.silu(c) @ w_mod + b_mod
    shift1, scale1, gate1, shift2, scale2, gate2 = [m[:, None, :] for m in jnp.split(mod, N_MOD, axis=-1)]

    hmix = rms_norm(x, g_norm1) * (1.0 + scale1) + shift1
    proj = hmix @ w_in
    q_a, k_a, v_a, q_m, k_m, v_m, o_m, gates = jnp.split(proj, IN_SPLITS, axis=-1)

    att = banded_attention(q_a.reshape(b, s, N_HEADS_ATT, HEAD_DIM_ATT),
                           k_a.reshape(b, s, N_KV_HEADS, HEAD_DIM_ATT),
                           v_a.reshape(b, s, N_KV_HEADS, HEAD_DIM_ATT), sink)
    att = rms_norm(att, g_attn_out).astype(x.dtype)

    qk = jax.nn.silu(centred_dwconv(jnp.concatenate([q_m, k_m], axis=-1), conv_w, conv_b))
    q_m, k_m = jnp.split(qk, 2, axis=-1)
    gates = gates.astype(jnp.float32) + b_gates.astype(jnp.float32)
    i_f, f_f, i_b, f_b = jnp.split(gates, 4, axis=-1)
    qm = q_m.astype(jnp.float32).reshape(b, s, N_HEADS_M, HEAD_DIM_M)
    km = k_m.astype(jnp.float32).reshape(b, s, N_HEADS_M, HEAD_DIM_M)
    vm = v_m.astype(jnp.float32).reshape(b, s, N_HEADS_M, HEAD_DIM_M)
    flip = lambda t: jnp.flip(t, axis=1)
    h_fwd = mlstm_chunkwise(qm, km, vm, i_f, f_f)
    h_bwd = flip(mlstm_chunkwise(flip(qm), flip(km), flip(vm), flip(i_b), flip(f_b)))
    h_m = rms_norm(h_fwd + h_bwd, g_mlstm_out.reshape(N_HEADS_M, HEAD_DIM_M))
    h_m = (jax.nn.sigmoid(o_m.astype(jnp.float32)) * h_m.reshape(b, s, M_W)).astype(x.dtype)

    mix = jnp.concatenate([att, h_m], axis=-1) @ w_out
    x = x + gate1 * mix

    hff = rms_norm(x, g_norm2) * (1.0 + scale2) + shift2
    gte, up = jnp.split(hff @ w_ffn_in, 2, axis=-1)
    ff = (jax.nn.silu(gte) * up) @ w_ffn_out
    return x + gate2 * ff


def setup_inputs(seed: int = 0) -> dict:
    key = jax.random.key(seed)
    ks = jax.random.split(key, 20)
    nrm = lambda k, shape, scale: jax.random.normal(k, shape, jnp.float32) * scale
    gain = lambda k, shape: 1.0 + 0.02 * jax.random.normal(k, shape, jnp.float32)
    fb = jnp.linspace(3.0, 6.0, N_HEADS_M, dtype=jnp.float32)
    b_gates = jnp.concatenate([
        nrm(ks[8], (DEPTH, N_HEADS_M), 0.1),
        fb + nrm(ks[9], (DEPTH, N_HEADS_M), 0.1),
        nrm(ks[10], (DEPTH, N_HEADS_M), 0.1),
        fb + nrm(ks[11], (DEPTH, N_HEADS_M), 0.1),
    ], axis=-1)
    return {
        "x": nrm(ks[0], (BATCH, SEQ, D_MODEL), 1.0),
        "c": nrm(ks[1], (BATCH, D_MODEL), 1.0),
        "w_mod": nrm(ks[2], (DEPTH, D_MODEL, N_MOD * D_MODEL), D_MODEL ** -0.5),
        "b_mod": nrm(ks[3], (DEPTH, N_MOD * D_MODEL), 0.02),
        "g_norm1": gain(ks[4], (DEPTH, D_MODEL)),
        "w_in": nrm(ks[5], (DEPTH, D_MODEL, IN_COLS), D_MODEL ** -0.5),
        "conv_w": nrm(ks[6], (DEPTH, CONV_WIDTH, 2 * M_W), CONV_WIDTH ** -0.5),
        "conv_b": nrm(ks[7], (DEPTH, 2 * M_W), 0.02),
        "b_gates": b_gates,
        "sink": nrm(ks[12], (DEPTH, N_HEADS_ATT), 0.5),
        "g_attn_out": gain(ks[13], (DEPTH, ATT_Q)),
        "g_mlstm_out": gain(ks[14], (DEPTH, M_W)),
        "w_out": nrm(ks[15], (DEPTH, MIX_WIDTH, D_MODEL), MIX_WIDTH ** -0.5),
        "g_norm2": gain(ks[16], (DEPTH, D_MODEL)),
        "w_ffn_in": nrm(ks[17], (DEPTH, D_MODEL, 2 * D_FF), D_MODEL ** -0.5),
        "w_ffn_out": nrm(ks[18], (DEPTH, D_FF, D_MODEL), D_FF ** -0.5),
        "g_final": gain(ks[19], (D_MODEL,)),
    }


def reference(x, c, w_mod, b_mod, g_norm1, w_in, conv_w, conv_b, b_gates, sink,
              g_attn_out, g_mlstm_out, w_out, g_norm2, w_ffn_in, w_ffn_out, g_final):
    for l in range(DEPTH):
        x = hybrid_layer(x, c, w_mod[l], b_mod[l], g_norm1[l], w_in[l], conv_w[l], conv_b[l],
                         b_gates[l], sink[l], g_attn_out[l], g_mlstm_out[l], w_out[l],
                         g_norm2[l], w_ffn_in[l], w_ffn_out[l])
    return rms_norm(x, g_final)
```

```python
import functools

import jax
import jax.numpy as jnp
from jax import lax
from jax.experimental import pallas as pl
from jax.experimental.pallas import tpu as pltpu

F32 = jnp.float32
BF16 = jnp.bfloat16

D_MODEL = 1024
EPS = 1e-6
N_HEADS_ATT = 8
N_KV_HEADS = 2
HEAD_DIM_ATT = 64
GROUP_SIZE = N_HEADS_ATT // N_KV_HEADS
WINDOW = 128
BLOCK = 128
N_HEADS_M = 4
HEAD_DIM_M = 128
CHUNK = 128
ATT_Q = N_HEADS_ATT * HEAD_DIM_ATT
ATT_KV = N_KV_HEADS * HEAD_DIM_ATT
M_W = N_HEADS_M * HEAD_DIM_M
N_GATES = 4 * N_HEADS_M
D_FF = 2816
N_MOD = 6

LANES = 128
GATE_PAD = LANES
FF_CHUNK = 256
N_FF_CHUNKS = D_FF // FF_CHUNK
VMEM_LIMIT = 56 * 1024 * 1024

C_QA = 0
C_KA = ATT_Q
C_VA = ATT_Q + ATT_KV
C_QKM = ATT_Q + 2 * ATT_KV
C_VM = C_QKM + 2 * M_W
C_OM = C_VM + M_W
C_G = C_OM + M_W
IN_COLS_PAD = C_G + GATE_PAD
QKVA_W = ATT_Q + 4 * ATT_KV


def _dot(a, b):
    return jnp.dot(a, b, preferred_element_type=F32)


def _dot_nt(a, b):
    return lax.dot_general(a, b, (((1,), (1,)), ((), ())), preferred_element_type=F32)


def _rms(x, g):
    return x * lax.rsqrt(jnp.mean(x * x, axis=-1, keepdims=True) + EPS) * g


def _mod_kernel(c_ref, w_ref, b_ref, o_ref):
    s = jax.nn.silu(c_ref[...]).astype(BF16)
    o_ref[...] = _dot(s, w_ref[...].astype(BF16)) + b_ref[...]


def _mod_call(c, w_mod, b_mod):
    bsz = c.shape[0]
    rows = 8
    cp = jnp.pad(c, ((0, rows - bsz), (0, 0)))
    n = w_mod.shape[1]
    bn = 1024
    out = pl.pallas_call(
        _mod_kernel,
        grid=(n // bn,),
        in_specs=[
            pl.BlockSpec((rows, D_MODEL), lambda i: (0, 0)),
            pl.BlockSpec((D_MODEL, bn), lambda i: (0, i)),
            pl.BlockSpec((1, bn), lambda i: (0, i)),
        ],
        out_specs=pl.BlockSpec((rows, bn), lambda i: (0, i)),
        out_shape=jax.ShapeDtypeStruct((rows, n), F32),
        compiler_params=pltpu.CompilerParams(dimension_semantics=("arbitrary",)),
        name="mod",
    )(cp, w_mod, b_mod.reshape(1, n))
    return out[:bsz].reshape(bsz, N_MOD, D_MODEL)


def _inproj_kernel(x_ref, mod_ref, g_ref, w_ref, qkva_ref, qkm_ref, vm_ref, om_ref, gat_ref):
    x = x_ref[...]
    h = _rms(x, g_ref[...]) * (1.0 + mod_ref[1:2, :]) + mod_ref[0:1, :]
    hb = h.astype(BF16)
    qa = _dot(hb, w_ref[:, C_QA:C_KA]) * (HEAD_DIM_ATT ** -0.5)
    ka = _dot(hb, w_ref[:, C_KA:C_VA])
    va = _dot(hb, w_ref[:, C_VA:C_QKM])
    half = HEAD_DIM_ATT
    qkva_ref[:, 0:ATT_Q] = qa.astype(BF16)
    qkva_ref[:, ATT_Q:ATT_Q + ATT_KV] = ka.astype(BF16)
    qkva_ref[:, ATT_Q + ATT_KV:ATT_Q + 2 * ATT_KV] = pltpu.roll(ka, half, axis=1).astype(BF16)
    qkva_ref[:, ATT_Q + 2 * ATT_KV:ATT_Q + 3 * ATT_KV] = va.astype(BF16)
    qkva_ref[:, ATT_Q + 3 * ATT_KV:ATT_Q + 4 * ATT_KV] = pltpu.roll(va, half, axis=1).astype(BF16)
    qkm_ref[...] = _dot(hb, w_ref[:, C_QKM:C_VM])
    vm_ref[...] = _dot(hb, w_ref[:, C_VM:C_OM]).astype(BF16)
    om_ref[...] = _dot(hb, w_ref[:, C_OM:C_G])
    gat_ref[...] = _dot(hb, w_ref[:, C_G:IN_COLS_PAD])


def _inproj_call(x, mod, g1, w_in_p, tm):
    bsz, s, d = x.shape
    outs = (
        jax.ShapeDtypeStruct((bsz, s, QKVA_W), BF16),
        jax.ShapeDtypeStruct((bsz, s, 2 * M_W), F32),
        jax.ShapeDtypeStruct((bsz, s, M_W), BF16),
        jax.ShapeDtypeStruct((bsz, s, M_W), F32),
        jax.ShapeDtypeStruct((bsz, s, GATE_PAD), F32),
    )
    tile = lambda w: pl.BlockSpec((None, tm, w), lambda b, i: (b, i, 0))
    return pl.pallas_call(
        _inproj_kernel,
        grid=(bsz, s // tm),
        in_specs=[
            tile(d),
            pl.BlockSpec((None, N_MOD, d), lambda b, i: (b, 0, 0)),
            pl.BlockSpec((1, d), lambda b, i: (0, 0)),
            pl.BlockSpec((d, IN_COLS_PAD), lambda b, i: (0, 0), pipeline_mode=pl.Buffered(1)),
        ],
        out_specs=[tile(QKVA_W), tile(2 * M_W), tile(M_W), tile(M_W), tile(GATE_PAD)],
        out_shape=outs,
        compiler_params=pltpu.CompilerParams(
            dimension_semantics=("arbitrary", "arbitrary"), vmem_limit_bytes=VMEM_LIMIT),
        name="inproj",
    )(x, mod, g1, w_in_p)


def _mlstm_kernel(qk_ref, qkp_ref, qkn_ref, v_ref, gat_ref, cw_ref, cb_ref, bg_ref,
                  h_ref, q_scr, k_scr, c_scr, m_scr, *, reverse, nblk, nchunk):
    j = pl.program_id(1)
    blk = (nblk - 1 - j) if reverse else j

    @pl.when(j == 0)
    def _init():
        c_scr[...] = jnp.zeros_like(c_scr)
        m_scr[...] = jnp.zeros_like(m_scr)

    x = qk_ref[...]
    tm = x.shape[0]
    prev_row = jnp.where(blk == 0, 0.0, qkp_ref[7:8, :])
    next_row = jnp.where(blk == nblk - 1, 0.0, qkn_ref[0:1, :])
    row = lax.broadcasted_iota(jnp.int32, (tm, 1), 0)
    xm1 = jnp.where(row == 0, prev_row, pltpu.roll(x, 1, axis=0))
    xp1 = jnp.where(row == tm - 1, next_row, pltpu.roll(x, tm - 1, axis=0))
    y = xm1 * cw_ref[0:1, :] + x * cw_ref[1:2, :] + xp1 * cw_ref[2:3, :] + cb_ref[...]
    y = jax.nn.silu(y)
    q_scr[...] = (y[:, :M_W] * (HEAD_DIM_M ** -0.5)).astype(BF16)
    k_scr[...] = y[:, M_W:]

    L = CHUNK
    ti = lax.broadcasted_iota(jnp.int32, (L, L), 0)
    si = lax.broadcasted_iota(jnp.int32, (L, L), 1)
    causal = (si >= ti) if reverse else (si <= ti)
    tri = jnp.where(causal, 1.0, 0.0).astype(BF16)
    lane = lax.broadcasted_iota(jnp.int32, (1, L), 1)
    last = 0 if reverse else L - 1
    i_off, f_off = (2 * N_HEADS_M, 3 * N_HEADS_M) if reverse else (0, N_HEADS_M)
    ones = jnp.ones((L, HEAD_DIM_M), BF16)

    order = range(nchunk - 1, -1, -1) if reverse else range(nchunk)
    for g in order:
        rows = slice(g * L, (g + 1) * L)
        gc = gat_ref[rows, :] + bg_ref[...]
        logf = jax.nn.log_sigmoid(gc)
        hi = logf.astype(BF16)
        r1 = logf - hi.astype(F32)
        mid = r1.astype(BF16)
        lo = (r1 - mid.astype(F32)).astype(BF16)
        bcum = _dot(tri, hi) + _dot(tri, mid) + _dot(tri, lo)
        gct = gc.T
        bct = bcum.T
        for h in range(N_HEADS_M):
            cols = slice(h * HEAD_DIM_M, (h + 1) * HEAD_DIM_M)
            ig_row = gct[i_off + h:i_off + h + 1, :]
            bc_row = bct[f_off + h:f_off + h + 1, :]
            r_row = ig_row - bc_row
            btot = jnp.sum(jnp.where(lane == last, bc_row, 0.0), axis=1, keepdims=True)
            rmax = jnp.max(r_row, axis=1, keepdims=True)
            a_max = btot + rmax
            w_row = jnp.exp(r_row - rmax)
            m_prev = m_scr[h:h + 1, 0:1]

            rm = jnp.where(causal, r_row, -jnp.inf)
            u = jnp.maximum(jnp.max(rm, axis=1, keepdims=True), m_prev)
            dm = jnp.exp(rm - u)
            q = q_scr[rows, cols]
            kt = k_scr[rows, cols].T
            sc = (_dot(q, kt.astype(BF16)) * dm).astype(BF16)
            inter = jnp.exp(m_prev - u)
            qi = (q.astype(F32) * inter).astype(BF16)
            vaug = jnp.concatenate([v_ref[rows, cols], ones], axis=1)
            cprev = c_scr[h]
            lhs = jnp.concatenate([sc, qi], axis=1)
            rhs = jnp.concatenate([vaug, cprev.astype(BF16)], axis=0)
            out = _dot(lhs, rhs)
            bc_col = bcum[:, f_off + h:f_off + h + 1]
            floor = jnp.exp(-(bc_col + u))
            h_ref[rows, cols] = out[:, :HEAD_DIM_M] / jnp.maximum(jnp.abs(out[:, HEAD_DIM_M:]), floor)

            m_new = jnp.maximum(btot + m_prev, a_max)
            decay = jnp.exp(btot + m_prev - m_new)
            inj = jnp.exp(a_max - m_new)
            upd = _dot((kt * w_row).astype(BF16), vaug)
            c_scr[h] = decay * cprev + inj * upd
            m_scr[h:h + 1, :] = jnp.broadcast_to(m_new, (1, LANES))


def _mlstm_call(qkm, vm, gates, conv_w, conv_b, bg_pad, tm, reverse):
    bsz, s, _ = qkm.shape
    nblk = s // tm
    r8 = tm // 8
    pos = (lambda j: nblk - 1 - j) if reverse else (lambda j: j)
    tile = lambda w: pl.BlockSpec((None, tm, w), lambda b, j: (b, pos(j), 0))
    prev_spec = pl.BlockSpec((None, 8, 2 * M_W), lambda b, j: (b, jnp.maximum(pos(j) * r8 - 1, 0), 0))
    next_spec = pl.BlockSpec((None, 8, 2 * M_W),
                             lambda b, j: (b, jnp.minimum((pos(j) + 1) * r8, s // 8 - 1), 0))
    const = lambda shp: pl.BlockSpec(shp, lambda b, j: (0, 0))
    kern = functools.partial(_mlstm_kernel, reverse=reverse, nblk=nblk, nchunk=tm // CHUNK)
    return pl.pallas_call(
        kern,
        grid=(bsz, nblk),
        in_specs=[tile(2 * M_W), prev_spec, next_spec, tile(M_W), tile(GATE_PAD),
                  const((3, 2 * M_W)), const((1, 2 * M_W)), const((1, GATE_PAD))],
        out_specs=tile(M_W),
        out_shape=jax.ShapeDtypeStruct((bsz, s, M_W), F32),
        scratch_shapes=[
            pltpu.VMEM((tm, M_W), BF16),
            pltpu.VMEM((tm, M_W), F32),
            pltpu.VMEM((N_HEADS_M, HEAD_DIM_M, 2 * HEAD_DIM_M), F32),
            pltpu.VMEM((8, LANES), F32),
        ],
        compiler_params=pltpu.CompilerParams(
            dimension_semantics=("arbitrary", "arbitrary"), vmem_limit_bytes=VMEM_LIMIT),
        name="mlstm_bwd" if reverse else "mlstm_fwd",
    )(qkm, qkm, qkm, vm, gates, conv_w, conv_b, bg_pad)


def _attn_kernel(sink_ref, q_ref, kvp_ref, kvn_ref, g_ref, o_ref, bias_scr, *, nt, nsub):
    b = pl.program_id(0)
    j = pl.program_id(1)
    nk = 3 * BLOCK

    @pl.when((b == 0) & (j == 0))
    def _init():
        row = lax.broadcasted_iota(jnp.int32, (BLOCK, nk), 0)
        col = lax.broadcasted_iota(jnp.int32, (BLOCK, nk), 1)
        dist = jnp.abs(col - BLOCK - row)
        distf = dist.astype(F32)
        for var in range(3):
            ok = dist <= WINDOW
            if var == 1:
                ok = ok & (col >= BLOCK)
            elif var == 2:
                ok = ok & (col < 2 * BLOCK)
            for h in range(N_HEADS_ATT):
                slope = 2.0 ** (-8.0 * (h + 1.0) / N_HEADS_ATT)
                bias_scr[var * N_HEADS_ATT + h] = jnp.where(ok, -slope * distf, -jnp.inf)

    lane_k = lax.broadcasted_iota(jnp.int32, (nk, LANES), 1)
    ones_a = jnp.where(lane_k < HEAD_DIM_ATT, 1.0, 0.0).astype(BF16)
    ones_b = jnp.where(lane_k < HEAD_DIM_ATT, 0.0, 1.0).astype(BF16)
    lo_half_q = lax.broadcasted_iota(jnp.int32, (BLOCK, LANES), 1) < HEAD_DIM_ATT

    def kv_block(idx):
        if idx < 0:
            return kvp_ref[...]
        if idx >= nsub:
            return kvn_ref[...]
        return q_ref[idx * BLOCK:(idx + 1) * BLOCK, ATT_Q:QKVA_W]

    for n in range(nsub):
        rows = slice(n * BLOCK, (n + 1) * BLOCK)
        kv = jnp.concatenate([kv_block(n - 1), kv_block(n), kv_block(n + 1)], axis=0)
        if n == 0:
            var = jnp.where(j == 0, 1, 0)
        elif n == nsub - 1:
            var = jnp.where(j == nt - 1, 2, 0)
        else:
            var = 0
        pieces = []
        for kvh in range(N_KV_HEADS):
            k_st, k_sw = kv[:, 0:LANES], kv[:, LANES:2 * LANES]
            v_st, v_sw = kv[:, 2 * LANES:3 * LANES], kv[:, 3 * LANES:4 * LANES]
            if kvh == 0:
                k_lo, k_hi, v_lo, v_hi = k_st, k_sw, v_st, v_sw
            else:
                k_lo, k_hi, v_lo, v_hi = k_sw, k_st, v_sw, v_st
            kk = jnp.concatenate([k_lo * ones_a, k_hi * ones_b], axis=0)
            vv = jnp.concatenate([
                jnp.concatenate([v_lo * ones_a, ones_a], axis=1),
                jnp.concatenate([v_hi * ones_b, ones_b], axis=1)], axis=0)
            for pair in range(GROUP_SIZE // 2):
                h0 = kvh * GROUP_SIZE + 2 * pair
                qp = q_ref[rows, h0 * HEAD_DIM_ATT:(h0 + 2) * HEAD_DIM_ATT]
                s2 = _dot_nt(qp, kk)
                ps, es = [], []
                for t in range(2):
                    logits = s2[:, t * nk:(t + 1) * nk] + bias_scr[var * N_HEADS_ATT + h0 + t]
                    sink = sink_ref[h0 + t]
                    mx = jnp.maximum(jnp.max(logits, axis=-1, keepdims=True), sink)
                    ps.append(jnp.exp(logits - mx).astype(BF16))
                    es.append(jnp.exp(sink - mx))
                res = _dot(jnp.concatenate(ps, axis=1), vv)
                den = res[:, LANES:] + jnp.where(lo_half_q, es[0], es[1])
                pieces.append(res[:, :LANES] / den)
        att = jnp.concatenate(pieces, axis=1)
        o_ref[rows, :] = _rms(att, g_ref[...]).astype(o_ref.dtype)


def _attn_call(sink, qkva, g_attn, tq):
    bsz, s, _ = qkva.shape
    nt = s // tq
    nsub = tq // BLOCK
    nb = s // BLOCK
    kvw = QKVA_W - ATT_Q
    kern = functools.partial(_attn_kernel, nt=nt, nsub=nsub)
    return pl.pallas_call(
        kern,
        grid=(bsz, nt),
        in_specs=[
            pl.BlockSpec(memory_space=pltpu.SMEM),
            pl.BlockSpec((None, tq, QKVA_W), lambda b, j: (b, j, 0)),
            pl.BlockSpec((None, BLOCK, kvw), lambda b, j: (b, jnp.maximum(j * nsub - 1, 0), 1)),
            pl.BlockSpec((None, BLOCK, kvw), lambda b, j: (b, jnp.minimum((j + 1) * nsub, nb - 1), 1)),
            pl.BlockSpec((1, ATT_Q), lambda b, j: (0, 0)),
        ],
        out_specs=pl.BlockSpec((None, tq, ATT_Q), lambda b, j: (b, j, 0)),
        out_shape=jax.ShapeDtypeStruct((bsz, s, ATT_Q), BF16),
        scratch_shapes=[pltpu.VMEM((3 * N_HEADS_ATT, BLOCK, 3 * BLOCK), F32)],
        compiler_params=pltpu.CompilerParams(
            dimension_semantics=("arbitrary", "arbitrary"), vmem_limit_bytes=VMEM_LIMIT),
        name="attn",
    )(sink, qkva, qkva, qkva, g_attn)


def _outffn_kernel(x_ref, att_ref, hf_ref, hb_ref, om_ref, mod_ref, gm_ref, wo_ref, g2_ref,
                   w1_ref, w2_ref, gf_ref, o_ref, hid_scr, *, final):
    hs = hf_ref[...] + hb_ref[...]
    parts = []
    for h in range(N_HEADS_M):
        cols = slice(h * HEAD_DIM_M, (h + 1) * HEAD_DIM_M)
        parts.append(_rms(hs[:, cols], gm_ref[:, cols]))
    hm = jax.nn.sigmoid(om_ref[...]) * jnp.concatenate(parts, axis=1)
    mixin = jnp.concatenate([att_ref[...], hm.astype(BF16)], axis=1)
    x1 = x_ref[...] + mod_ref[2:3, :] * _dot(mixin, wo_ref[...])

    hff = (_rms(x1, g2_ref[...]) * (1.0 + mod_ref[4:5, :]) + mod_ref[3:4, :]).astype(BF16)
    for c in range(N_FF_CHUNKS):
        gu = _dot(hff, w1_ref[:, 2 * FF_CHUNK * c:2 * FF_CHUNK * (c + 1)])
        hid_scr[:, FF_CHUNK * c:FF_CHUNK * (c + 1)] = (
            jax.nn.silu(gu[:, :FF_CHUNK]) * gu[:, FF_CHUNK:]).astype(BF16)
    x2 = x1 + mod_ref[5:6, :] * _dot(hid_scr[...], w2_ref[...])
    if final:
        x2 = _rms(x2, gf_ref[...])
    o_ref[...] = x2


def _outffn_call(x, att, hf, hb, om, mod, g_m, w_out, g2, w1r, w2, g_final, tm, final):
    bsz, s, d = x.shape
    tile = lambda w: pl.BlockSpec((None, tm, w), lambda b, i: (b, i, 0))
    const = lambda shp: pl.BlockSpec(shp, lambda b, i: (0, 0))
    weight = lambda shp: pl.BlockSpec(shp, lambda b, i: (0, 0), pipeline_mode=pl.Buffered(1))
    return pl.pallas_call(
        functools.partial(_outffn_kernel, final=final),
        grid=(bsz, s // tm),
        in_specs=[
            tile(d), tile(ATT_Q), tile(M_W), tile(M_W), tile(M_W),
            pl.BlockSpec((None, N_MOD, d), lambda b, i: (b, 0, 0)),
            const((1, M_W)), weight((ATT_Q + M_W, d)), const((1, d)),
            weight((d, 2 * D_FF)), weight((D_FF, d)), const((1, d)),
        ],
        out_specs=tile(d),
        out_shape=jax.ShapeDtypeStruct((bsz, s, d), F32),
        scratch_shapes=[pltpu.VMEM((tm, D_FF), BF16)],
        compiler_params=pltpu.CompilerParams(
            dimension_semantics=("arbitrary", "arbitrary"), vmem_limit_bytes=VMEM_LIMIT),
        name="outffn",
    )(x, att, hf, hb, om, mod, g_m, w_out, g2, w1r, w2, g_final)


def _layer(x, c, w_mod, b_mod, g_norm1, w_in, conv_w, conv_b, b_gates, sink,
           g_attn_out, g_mlstm_out, w_out, g_norm2, w_ffn_in, w_ffn_out, g_final, final):
    d = x.shape[-1]
    mod = _mod_call(c, w_mod, b_mod)

    n_real = C_G + N_GATES
    w_in_p = jnp.pad(w_in, ((0, 0), (0, IN_COLS_PAD - n_real))).astype(BF16)
    qkva, qkm, vm, om, gates = _inproj_call(x, mod, g_norm1.reshape(1, d), w_in_p, tm=512)

    bg_pad = jnp.pad(b_gates, (0, GATE_PAD - N_GATES)).reshape(1, GATE_PAD)
    cb = conv_b.reshape(1, 2 * M_W)
    hf = _mlstm_call(qkm, vm, gates, conv_w, cb, bg_pad, tm=512, reverse=False)
    hb = _mlstm_call(qkm, vm, gates, conv_w, cb, bg_pad, tm=512, reverse=True)

    att = _attn_call(sink, qkva, g_attn_out.reshape(1, ATT_Q), tq=512)

    w1r = jnp.stack([w_ffn_in[:, :D_FF].reshape(d, N_FF_CHUNKS, FF_CHUNK),
                     w_ffn_in[:, D_FF:].reshape(d, N_FF_CHUNKS, FF_CHUNK)], axis=2)
    w1r = w1r.reshape(d, 2 * D_FF).astype(BF16)
    return _outffn_call(x, att, hf, hb, om, mod, g_mlstm_out.reshape(1, M_W), w_out.astype(BF16),
                        g_norm2.reshape(1, d), w1r, w_ffn_out.astype(BF16), g_final.reshape(1, d),
                        tm=512, final=final)


def kernel(x, c, w_mod, b_mod, g_norm1, w_in, conv_w, conv_b, b_gates, sink, g_attn_out,
           g_mlstm_out, w_out, g_norm2, w_ffn_in, w_ffn_out, g_final):
    depth = w_mod.shape[0]
    for l in range(depth):
        x = _layer(x, c, w_mod[l], b_mod[l], g_norm1[l], w_in[l], conv_w[l], conv_b[l], b_gates[l],
                   sink[l], g_attn_out[l], g_mlstm_out[l], w_out[l], g_norm2[l], w_ffn_in[l],
                   w_ffn_out[l], g_final, final=(l == depth - 1))
    return x
```

```python
import functools

import jax
import jax.numpy as jnp
from jax import lax
from jax.experimental import pallas as pl
from jax.experimental.pallas import tpu as pltpu

F32 = jnp.float32
BF16 = jnp.bfloat16

D_MODEL = 1024
EPS = 1e-6
N_HEADS_ATT = 8
N_KV_HEADS = 2
HEAD_DIM_ATT = 64
GROUP_SIZE = N_HEADS_ATT // N_KV_HEADS
WINDOW = 128
BLOCK = 128
N_HEADS_M = 4
HEAD_DIM_M = 128
CHUNK = 128
ATT_Q = N_HEADS_ATT * HEAD_DIM_ATT
ATT_KV = N_KV_HEADS * HEAD_DIM_ATT
M_W = N_HEADS_M * HEAD_DIM_M
N_GATES = 4 * N_HEADS_M
D_FF = 2816
N_MOD = 6

LANES = 128
GATE_PAD = LANES
FF_CHUNK = 256
N_FF_CHUNKS = D_FF // FF_CHUNK
VMEM_LIMIT = 56 * 1024 * 1024

C_QA = 0
C_KA = ATT_Q
C_VA = ATT_Q + ATT_KV
C_QKM = ATT_Q + 2 * ATT_KV
C_VM = C_QKM + 2 * M_W
C_OM = C_VM + M_W
C_G = C_OM + M_W
IN_COLS_PAD = C_G + GATE_PAD
QKVA_W = ATT_Q + 4 * ATT_KV


def _dot(a, b):
    return jnp.dot(a, b, preferred_element_type=F32)


def _dot_nt(a, b):
    return lax.dot_general(a, b, (((1,), (1,)), ((), ())), preferred_element_type=F32)


def _rms(x, g):
    return x * lax.rsqrt(jnp.mean(x * x, axis=-1, keepdims=True) + EPS) * g


def _cast_kernel(w_ref, o_ref):
    n = w_ref.shape[1]
    o_ref[:, :n] = w_ref[...].astype(o_ref.dtype)
    if o_ref.shape[1] > n:
        o_ref[:, n:] = jnp.zeros((o_ref.shape[0], o_ref.shape[1] - n), o_ref.dtype)


def _cast_call(w, n_out=None, bm=256):
    k, n = w.shape
    n_out = n if n_out is None else n_out
    return pl.pallas_call(
        _cast_kernel,
        grid=(k // bm,),
        in_specs=[pl.BlockSpec((bm, n), lambda i: (i, 0))],
        out_specs=pl.BlockSpec((bm, n_out), lambda i: (i, 0)),
        out_shape=jax.ShapeDtypeStruct((k, n_out), BF16),
        compiler_params=pltpu.CompilerParams(dimension_semantics=("arbitrary",)),
        name="cast",
    )(w)


def _mod_kernel(c_ref, w_ref, b_ref, o_ref):
    s = jax.nn.silu(c_ref[...]).astype(BF16)
    o_ref[...] = _dot(s, w_ref[...].astype(BF16)) + b_ref[...]


def _mod_call(c, w_mod, b_mod):
    bsz = c.shape[0]
    rows = 8
    cp = jnp.pad(c, ((0, rows - bsz), (0, 0)))
    n = w_mod.shape[1]
    bn = 1024
    out = pl.pallas_call(
        _mod_kernel,
        grid=(n // bn,),
        in_specs=[
            pl.BlockSpec((rows, D_MODEL), lambda i: (0, 0)),
            pl.BlockSpec((D_MODEL, bn), lambda i: (0, i)),
            pl.BlockSpec((1, bn), lambda i: (0, i)),
        ],
        out_specs=pl.BlockSpec((rows, bn), lambda i: (0, i)),
        out_shape=jax.ShapeDtypeStruct((rows, n), F32),
        compiler_params=pltpu.CompilerParams(dimension_semantics=("arbitrary",)),
        name="mod",
    )(cp, w_mod, b_mod.reshape(1, n))
    return out[:bsz].reshape(bsz, N_MOD, D_MODEL)


def _inproj_kernel(x_ref, mod_ref, g_ref, w_ref, qkva_ref, qkm_ref, vm_ref, om_ref, gat_ref):
    x = x_ref[...]
    h = _rms(x, g_ref[...]) * (1.0 + mod_ref[1:2, :]) + mod_ref[0:1, :]
    hb = h.astype(BF16)
    qa = _dot(hb, w_ref[:, C_QA:C_KA]) * (HEAD_DIM_ATT ** -0.5)
    ka = _dot(hb, w_ref[:, C_KA:C_VA])
    va = _dot(hb, w_ref[:, C_VA:C_QKM])
    half = HEAD_DIM_ATT
    qkva_ref[:, 0:ATT_Q] = qa.astype(BF16)
    qkva_ref[:, ATT_Q:ATT_Q + ATT_KV] = ka.astype(BF16)
    qkva_ref[:, ATT_Q + ATT_KV:ATT_Q + 2 * ATT_KV] = pltpu.roll(ka, half, axis=1).astype(BF16)
    qkva_ref[:, ATT_Q + 2 * ATT_KV:ATT_Q + 3 * ATT_KV] = va.astype(BF16)
    qkva_ref[:, ATT_Q + 3 * ATT_KV:ATT_Q + 4 * ATT_KV] = pltpu.roll(va, half, axis=1).astype(BF16)
    qkm_ref[...] = _dot(hb, w_ref[:, C_QKM:C_VM])
    vm_ref[...] = _dot(hb, w_ref[:, C_VM:C_OM]).astype(BF16)
    om_ref[...] = _dot(hb, w_ref[:, C_OM:C_G])
    gat_ref[...] = _dot(hb, w_ref[:, C_G:IN_COLS_PAD])


def _inproj_call(x, mod, g1, w_in_p, tm):
    bsz, s, d = x.shape
    outs = (
        jax.ShapeDtypeStruct((bsz, s, QKVA_W), BF16),
        jax.ShapeDtypeStruct((bsz, s, 2 * M_W), F32),
        jax.ShapeDtypeStruct((bsz, s, M_W), BF16),
        jax.ShapeDtypeStruct((bsz, s, M_W), F32),
        jax.ShapeDtypeStruct((bsz, s, GATE_PAD), F32),
    )
    tile = lambda w: pl.BlockSpec((None, tm, w), lambda b, i: (b, i, 0))
    return pl.pallas_call(
        _inproj_kernel,
        grid=(bsz, s // tm),
        in_specs=[
            tile(d),
            pl.BlockSpec((None, N_MOD, d), lambda b, i: (b, 0, 0)),
            pl.BlockSpec((1, d), lambda b, i: (0, 0)),
            pl.BlockSpec((d, IN_COLS_PAD), lambda b, i: (0, 0), pipeline_mode=pl.Buffered(1)),
        ],
        out_specs=[tile(QKVA_W), tile(2 * M_W), tile(M_W), tile(M_W), tile(GATE_PAD)],
        out_shape=outs,
        compiler_params=pltpu.CompilerParams(
            dimension_semantics=("arbitrary", "arbitrary"), vmem_limit_bytes=VMEM_LIMIT),
        name="inproj",
    )(x, mod, g1, w_in_p)


def _mlstm_kernel(qk_ref, qkp_ref, qkn_ref, v_ref, gat_ref, cw_ref, cb_ref, bg_ref,
                  h_ref, q_scr, k_scr, c_scr, m_scr, *, reverse, nblk, nchunk):
    j = pl.program_id(1)
    blk = (nblk - 1 - j) if reverse else j

    @pl.when(j == 0)
    def _init():
        c_scr[...] = jnp.zeros_like(c_scr)
        m_scr[...] = jnp.zeros_like(m_scr)

    x = qk_ref[...]
    tm = x.shape[0]
    prev_row = jnp.where(blk == 0, 0.0, qkp_ref[7:8, :])
    next_row = jnp.where(blk == nblk - 1, 0.0, qkn_ref[0:1, :])
    row = lax.broadcasted_iota(jnp.int32, (tm, 1), 0)
    xm1 = jnp.where(row == 0, prev_row, pltpu.roll(x, 1, axis=0))
    xp1 = jnp.where(row == tm - 1, next_row, pltpu.roll(x, tm - 1, axis=0))
    y = xm1 * cw_ref[0:1, :] + x * cw_ref[1:2, :] + xp1 * cw_ref[2:3, :] + cb_ref[...]
    y = jax.nn.silu(y)
    q_scr[...] = (y[:, :M_W] * (HEAD_DIM_M ** -0.5)).astype(BF16)
    k_scr[...] = y[:, M_W:]

    L = CHUNK
    ti = lax.broadcasted_iota(jnp.int32, (L, L), 0)
    si = lax.broadcasted_iota(jnp.int32, (L, L), 1)
    causal = (si >= ti) if reverse else (si <= ti)
    tri = jnp.where(causal, 1.0, 0.0).astype(BF16)
    lane = lax.broadcasted_iota(jnp.int32, (1, L), 1)
    last = 0 if reverse else L - 1
    i_off, f_off = (2 * N_HEADS_M, 3 * N_HEADS_M) if reverse else (0, N_HEADS_M)
    ones = jnp.ones((L, HEAD_DIM_M), BF16)

    order = range(nchunk - 1, -1, -1) if reverse else range(nchunk)
    for g in order:
        rows = slice(g * L, (g + 1) * L)
        gc = gat_ref[rows, :] + bg_ref[...]
        logf = jax.nn.log_sigmoid(gc)
        hi = logf.astype(BF16)
        r1 = logf - hi.astype(F32)
        mid = r1.astype(BF16)
        lo = (r1 - mid.astype(F32)).astype(BF16)
        bcum = _dot(tri, hi) + _dot(tri, mid) + _dot(tri, lo)
        gct = gc.T
        bct = bcum.T
        for h in range(N_HEADS_M):
            cols = slice(h * HEAD_DIM_M, (h + 1) * HEAD_DIM_M)
            ig_row = gct[i_off + h:i_off + h + 1, :]
            bc_row = bct[f_off + h:f_off + h + 1, :]
            r_row = ig_row - bc_row
            btot = jnp.sum(jnp.where(lane == last, bc_row, 0.0), axis=1, keepdims=True)
            rmax = jnp.max(r_row, axis=1, keepdims=True)
            a_max = btot + rmax
            w_row = jnp.exp(r_row - rmax)
            m_prev = m_scr[h:h + 1, 0:1]

            rm = jnp.where(causal, r_row, -jnp.inf)
            u = jnp.maximum(jnp.max(rm, axis=1, keepdims=True), m_prev)
            dm = jnp.exp(rm - u)
            q = q_scr[rows, cols]
            kt = k_scr[rows, cols].T
            sc = (_dot(q, kt.astype(BF16)) * dm).astype(BF16)
            inter = jnp.exp(m_prev - u)
            qi = (q.astype(F32) * inter).astype(BF16)
            vaug = jnp.concatenate([v_ref[rows, cols], ones], axis=1)
            cprev = c_scr[h]
            lhs = jnp.concatenate([sc, qi], axis=1)
            rhs = jnp.concatenate([vaug, cprev.astype(BF16)], axis=0)
            out = _dot(lhs, rhs)
            bc_col = bcum[:, f_off + h:f_off + h + 1]
            floor = jnp.exp(-(bc_col + u))
            h_ref[rows, cols] = out[:, :HEAD_DIM_M] / jnp.maximum(jnp.abs(out[:, HEAD_DIM_M:]), floor)

            m_new = jnp.maximum(btot + m_prev, a_max)
            decay = jnp.exp(btot + m_prev - m_new)
            inj = jnp.exp(a_max - m_new)
            upd = _dot((kt * w_row).astype(BF16), vaug)
            c_scr[h] = decay * cprev + inj * upd
            m_scr[h:h + 1, :] = jnp.broadcast_to(m_new, (1, LANES))


def _mlstm_call(qkm, vm, gates, conv_w, conv_b, bg_pad, tm, reverse):
    bsz, s, _ = qkm.shape
    nblk = s // tm
    r8 = tm // 8
    pos = (lambda j: nblk - 1 - j) if reverse else (lambda j: j)
    tile = lambda w: pl.BlockSpec((None, tm, w), lambda b, j: (b, pos(j), 0))
    prev_spec = pl.BlockSpec((None, 8, 2 * M_W), lambda b, j: (b, jnp.maximum(pos(j) * r8 - 1, 0), 0))
    next_spec = pl.BlockSpec((None, 8, 2 * M_W),
                             lambda b, j: (b, jnp.minimum((pos(j) + 1) * r8, s // 8 - 1), 0))
    const = lambda shp: pl.BlockSpec(shp, lambda b, j: (0, 0))
    kern = functools.partial(_mlstm_kernel, reverse=reverse, nblk=nblk, nchunk=tm // CHUNK)
    return pl.pallas_call(
        kern,
        grid=(bsz, nblk),
        in_specs=[tile(2 * M_W), prev_spec, next_spec, tile(M_W), tile(GATE_PAD),
                  const((3, 2 * M_W)), const((1, 2 * M_W)), const((1, GATE_PAD))],
        out_specs=tile(M_W),
        out_shape=jax.ShapeDtypeStruct((bsz, s, M_W), F32),
        scratch_shapes=[
            pltpu.VMEM((tm, M_W), BF16),
            pltpu.VMEM((tm, M_W), F32),
            pltpu.VMEM((N_HEADS_M, HEAD_DIM_M, 2 * HEAD_DIM_M), F32),
            pltpu.VMEM((8, LANES), F32),
        ],
        compiler_params=pltpu.CompilerParams(
            dimension_semantics=("arbitrary", "arbitrary"), vmem_limit_bytes=VMEM_LIMIT),
        name="mlstm_bwd" if reverse else "mlstm_fwd",
    )(qkm, qkm, qkm, vm, gates, conv_w, conv_b, bg_pad)


def _attn_kernel(sink_ref, q_ref, kvp_ref, kvn_ref, g_ref, o_ref, bias_scr, *, nt, nsub):
    b = pl.program_id(0)
    j = pl.program_id(1)
    nk = 3 * BLOCK

    @pl.when((b == 0) & (j == 0))
    def _init():
        row = lax.broadcasted_iota(jnp.int32, (BLOCK, nk), 0)
        col = lax.broadcasted_iota(jnp.int32, (BLOCK, nk), 1)
        dist = jnp.abs(col - BLOCK - row)
        distf = dist.astype(F32)
        for var in range(3):
            ok = dist <= WINDOW
            if var == 1:
                ok = ok & (col >= BLOCK)
            elif var == 2:
                ok = ok & (col < 2 * BLOCK)
            for h in range(N_HEADS_ATT):
                slope = 2.0 ** (-8.0 * (h + 1.0) / N_HEADS_ATT)
                bias_scr[var * N_HEADS_ATT + h] = jnp.where(ok, -slope * distf, -jnp.inf)

    lane_k = lax.broadcasted_iota(jnp.int32, (nk, LANES), 1)
    ones_a = jnp.where(lane_k < HEAD_DIM_ATT, 1.0, 0.0).astype(BF16)
    ones_b = jnp.where(lane_k < HEAD_DIM_ATT, 0.0, 1.0).astype(BF16)
    lo_half_q = lax.broadcasted_iota(jnp.int32, (BLOCK, LANES), 1) < HEAD_DIM_ATT

    def kv_block(idx):
        if idx < 0:
            return kvp_ref[...]
        if idx >= nsub:
            return kvn_ref[...]
        return q_ref[idx * BLOCK:(idx + 1) * BLOCK, ATT_Q:QKVA_W]

    for n in range(nsub):
        rows = slice(n * BLOCK, (n + 1) * BLOCK)
        kv = jnp.concatenate([kv_block(n - 1), kv_block(n), kv_block(n + 1)], axis=0)
        if n == 0:
            var = jnp.where(j == 0, 1, 0)
        elif n == nsub - 1:
            var = jnp.where(j == nt - 1, 2, 0)
        else:
            var = 0
        pieces = []
        for kvh in range(N_KV_HEADS):
            k_st, k_sw = kv[:, 0:LANES], kv[:, LANES:2 * LANES]
            v_st, v_sw = kv[:, 2 * LANES:3 * LANES], kv[:, 3 * LANES:4 * LANES]
            if kvh == 0:
                k_lo, k_hi, v_lo, v_hi = k_st, k_sw, v_st, v_sw
            else:
                k_lo, k_hi, v_lo, v_hi = k_sw, k_st, v_sw, v_st
            kk = jnp.concatenate([k_lo * ones_a, k_hi * ones_b], axis=0)
            vv = jnp.concatenate([
                jnp.concatenate([v_lo * ones_a, ones_a], axis=1),
                jnp.concatenate([v_hi * ones_b, ones_b], axis=1)], axis=0)
            for pair in range(GROUP_SIZE // 2):
                h0 = kvh * GROUP_SIZE + 2 * pair
                qp = q_ref[rows, h0 * HEAD_DIM_ATT:(h0 + 2) * HEAD_DIM_ATT]
                s2 = _dot_nt(qp, kk)
                ps, es = [], []
                for t in range(2):
                    logits = s2[:, t * nk:(t + 1) * nk] + bias_scr[var * N_HEADS_ATT + h0 + t]
                    sink = sink_ref[h0 + t]
                    mx = jnp.maximum(jnp.max(logits, axis=-1, keepdims=True), sink)
                    ps.append(jnp.exp(logits - mx).astype(BF16))
                    es.append(jnp.exp(sink - mx))
                res = _dot(jnp.concatenate(ps, axis=1), vv)
                den = res[:, LANES:] + jnp.where(lo_half_q, es[0], es[1])
                pieces.append(res[:, :LANES] / den)
        att = jnp.concatenate(pieces, axis=1)
        o_ref[rows, :] = _rms(att, g_ref[...]).astype(o_ref.dtype)


def _attn_call(sink, qkva, g_attn, tq):
    bsz, s, _ = qkva.shape
    nt = s // tq
    nsub = tq // BLOCK
    nb = s // BLOCK
    kvw = QKVA_W - ATT_Q
    kern = functools.partial(_attn_kernel, nt=nt, nsub=nsub)
    return pl.pallas_call(
        kern,
        grid=(bsz, nt),
        in_specs=[
            pl.BlockSpec(memory_space=pltpu.SMEM),
            pl.BlockSpec((None, tq, QKVA_W), lambda b, j: (b, j, 0)),
            pl.BlockSpec((None, BLOCK, kvw), lambda b, j: (b, jnp.maximum(j * nsub - 1, 0), 1)),
            pl.BlockSpec((None, BLOCK, kvw), lambda b, j: (b, jnp.minimum((j + 1) * nsub, nb - 1), 1)),
            pl.BlockSpec((1, ATT_Q), lambda b, j: (0, 0)),
        ],
        out_specs=pl.BlockSpec((None, tq, ATT_Q), lambda b, j: (b, j, 0)),
        out_shape=jax.ShapeDtypeStruct((bsz, s, ATT_Q), BF16),
        scratch_shapes=[pltpu.VMEM((3 * N_HEADS_ATT, BLOCK, 3 * BLOCK), F32)],
        compiler_params=pltpu.CompilerParams(
            dimension_semantics=("arbitrary", "arbitrary"), vmem_limit_bytes=VMEM_LIMIT),
        name="attn",
    )(sink, qkva, qkva, qkva, g_attn)


def _outffn_kernel(x_ref, att_ref, hf_ref, hb_ref, om_ref, mod_ref, gm_ref, wo_ref, g2_ref,
                   w1_ref, w2_ref, gf_ref, o_ref, hid_scr, *, final):
    hs = hf_ref[...] + hb_ref[...]
    parts = []
    for h in range(N_HEADS_M):
        cols = slice(h * HEAD_DIM_M, (h + 1) * HEAD_DIM_M)
        parts.append(_rms(hs[:, cols], gm_ref[:, cols]))
    hm = jax.nn.sigmoid(om_ref[...]) * jnp.concatenate(parts, axis=1)
    mixin = jnp.concatenate([att_ref[...], hm.astype(BF16)], axis=1)
    x1 = x_ref[...] + mod_ref[2:3, :] * _dot(mixin, wo_ref[...])

    hff = (_rms(x1, g2_ref[...]) * (1.0 + mod_ref[4:5, :]) + mod_ref[3:4, :]).astype(BF16)
    for c in range(N_FF_CHUNKS):
        gate = _dot(hff, w1_ref[:, FF_CHUNK * c:FF_CHUNK * (c + 1)])
        up = _dot(hff, w1_ref[:, D_FF + FF_CHUNK * c:D_FF + FF_CHUNK * (c + 1)])
        hid_scr[:, FF_CHUNK * c:FF_CHUNK * (c + 1)] = (jax.nn.silu(gate) * up).astype(BF16)
    x2 = x1 + mod_ref[5:6, :] * _dot(hid_scr[...], w2_ref[...])
    if final:
        x2 = _rms(x2, gf_ref[...])
    o_ref[...] = x2


def _outffn_call(x, att, hf, hb, om, mod, g_m, w_out, g2, w1r, w2, g_final, tm, final):
    bsz, s, d = x.shape
    tile = lambda w: pl.BlockSpec((None, tm, w), lambda b, i: (b, i, 0))
    const = lambda shp: pl.BlockSpec(shp, lambda b, i: (0, 0))
    weight = lambda shp: pl.BlockSpec(shp, lambda b, i: (0, 0), pipeline_mode=pl.Buffered(1))
    return pl.pallas_call(
        functools.partial(_outffn_kernel, final=final),
        grid=(bsz, s // tm),
        in_specs=[
            tile(d), tile(ATT_Q), tile(M_W), tile(M_W), tile(M_W),
            pl.BlockSpec((None, N_MOD, d), lambda b, i: (b, 0, 0)),
            const((1, M_W)), weight((ATT_Q + M_W, d)), const((1, d)),
            weight((d, 2 * D_FF)), weight((D_FF, d)), const((1, d)),
        ],
        out_specs=tile(d),
        out_shape=jax.ShapeDtypeStruct((bsz, s, d), F32),
        scratch_shapes=[pltpu.VMEM((tm, D_FF), BF16)],
        compiler_params=pltpu.CompilerParams(
            dimension_semantics=("arbitrary", "arbitrary"), vmem_limit_bytes=VMEM_LIMIT),
        name="outffn",
    )(x, att, hf, hb, om, mod, g_m, w_out, g2, w1r, w2, g_final)


def _layer(x, c, w_mod, b_mod, g_norm1, w_in, conv_w, conv_b, b_gates, sink,
           g_attn_out, g_mlstm_out, w_out, g_norm2, w_ffn_in, w_ffn_out, g_final, final):
    d = x.shape[-1]
    mod = _mod_call(c, w_mod, b_mod)

    w_in_p = _cast_call(w_in, IN_COLS_PAD)
    qkva, qkm, vm, om, gates = _inproj_call(x, mod, g_norm1.reshape(1, d), w_in_p, tm=512)

    bg_pad = jnp.pad(b_gates, (0, GATE_PAD - N_GATES)).reshape(1, GATE_PAD)
    cb = conv_b.reshape(1, 2 * M_W)
    hf = _mlstm_call(qkm, vm, gates, conv_w, cb, bg_pad, tm=512, reverse=False)
    hb = _mlstm_call(qkm, vm, gates, conv_w, cb, bg_pad, tm=512, reverse=True)

    att = _attn_call(sink, qkva, g_attn_out.reshape(1, ATT_Q), tq=512)

    return _outffn_call(x, att, hf, hb, om, mod, g_mlstm_out.reshape(1, M_W), _cast_call(w_out),
                        g_norm2.reshape(1, d), _cast_call(w_ffn_in), _cast_call(w_ffn_out),
                        g_final.reshape(1, d), tm=512, final=final)


def kernel(x, c, w_mod, b_mod, g_norm1, w_in, conv_w, conv_b, b_gates, sink, g_attn_out,
           g_mlstm_out, w_out, g_norm2, w_ffn_in, w_ffn_out, g_final):
    depth = w_mod.shape[0]
    for l in range(depth):
        x = _layer(x, c, w_mod[l], b_mod[l], g_norm1[l], w_in[l], conv_w[l], conv_b[l], b_gates[l],
                   sink[l], g_attn_out[l], g_mlstm_out[l], w_out[l], g_norm2[l], w_ffn_in[l],
                   w_ffn_out[l], g_final, final=(l == depth - 1))
    return x
```

```python
import functools

import jax
import jax.numpy as jnp
from jax import lax
from jax.experimental import pallas as pl
from jax.experimental.pallas import tpu as pltpu

F32 = jnp.float32
BF16 = jnp.bfloat16

D_MODEL = 1024
EPS = 1e-6
N_HEADS_ATT = 8
N_KV_HEADS = 2
HEAD_DIM_ATT = 64
GROUP_SIZE = N_HEADS_ATT // N_KV_HEADS
WINDOW = 128
BLOCK = 128
N_HEADS_M = 4
HEAD_DIM_M = 128
CHUNK = 128
ATT_Q = N_HEADS_ATT * HEAD_DIM_ATT
ATT_KV = N_KV_HEADS * HEAD_DIM_ATT
M_W = N_HEADS_M * HEAD_DIM_M
N_GATES = 4 * N_HEADS_M
D_FF = 2816
N_MOD = 6

LANES = 128
GATE_PAD = LANES
FF_CHUNK = 256
N_FF_CHUNKS = D_FF // FF_CHUNK
VMEM_LIMIT = 56 * 1024 * 1024

C_QA = 0
C_KA = ATT_Q
C_VA = ATT_Q + ATT_KV
C_QKM = ATT_Q + 2 * ATT_KV
C_VM = C_QKM + 2 * M_W
C_OM = C_VM + M_W
C_G = C_OM + M_W
IN_COLS_PAD = C_G + GATE_PAD
QKVA_W = ATT_Q + 4 * ATT_KV


def _dot(a, b):
    return jnp.dot(a, b, preferred_element_type=F32)


def _dot_nt(a, b):
    return lax.dot_general(a, b, (((1,), (1,)), ((), ())), preferred_element_type=F32)


def _rms(x, g):
    return x * lax.rsqrt(jnp.mean(x * x, axis=-1, keepdims=True) + EPS) * g


def _cast_kernel(w_ref, o_ref):
    n = w_ref.shape[1]
    o_ref[:, :n] = w_ref[...].astype(o_ref.dtype)
    if o_ref.shape[1] > n:
        o_ref[:, n:] = jnp.zeros((o_ref.shape[0], o_ref.shape[1] - n), o_ref.dtype)


def _cast_call(w, layer, n_out=None, bm=256):
    _, k, n = w.shape
    n_out = n if n_out is None else n_out
    return pl.pallas_call(
        _cast_kernel,
        grid=(k // bm,),
        in_specs=[pl.BlockSpec((None, bm, n), lambda i: (layer, i, 0))],
        out_specs=pl.BlockSpec((bm, n_out), lambda i: (i, 0)),
        out_shape=jax.ShapeDtypeStruct((k, n_out), BF16),
        compiler_params=pltpu.CompilerParams(dimension_semantics=("arbitrary",)),
        name="cast",
    )(w)


def _mod_kernel(c_ref, w_ref, b_ref, o_ref):
    s = jax.nn.silu(c_ref[...]).astype(BF16)
    o_ref[...] = _dot(s, w_ref[...].astype(BF16)) + b_ref[...]


def _mod_call(c, w_mod, b_mod):
    bsz = c.shape[0]
    rows = 8
    cp = jnp.pad(c, ((0, rows - bsz), (0, 0)))
    n = w_mod.shape[1]
    bn = 1024
    out = pl.pallas_call(
        _mod_kernel,
        grid=(n // bn,),
        in_specs=[
            pl.BlockSpec((rows, D_MODEL), lambda i: (0, 0)),
            pl.BlockSpec((D_MODEL, bn), lambda i: (0, i)),
            pl.BlockSpec((1, bn), lambda i: (0, i)),
        ],
        out_specs=pl.BlockSpec((rows, bn), lambda i: (0, i)),
        out_shape=jax.ShapeDtypeStruct((rows, n), F32),
        compiler_params=pltpu.CompilerParams(dimension_semantics=("arbitrary",)),
        name="mod",
    )(cp, w_mod, b_mod.reshape(1, n))
    return out[:bsz].reshape(bsz, N_MOD, D_MODEL)


def _inproj_kernel(x_ref, mod_ref, g_ref, w_ref, qkva_ref, qkm_ref, vm_ref, om_ref, gat_ref):
    x = x_ref[...]
    h = _rms(x, g_ref[...]) * (1.0 + mod_ref[1:2, :]) + mod_ref[0:1, :]
    hb = h.astype(BF16)
    qa = _dot(hb, w_ref[:, C_QA:C_KA]) * (HEAD_DIM_ATT ** -0.5)
    ka = _dot(hb, w_ref[:, C_KA:C_VA])
    va = _dot(hb, w_ref[:, C_VA:C_QKM])
    half = HEAD_DIM_ATT
    qkva_ref[:, 0:ATT_Q] = qa.astype(BF16)
    qkva_ref[:, ATT_Q:ATT_Q + ATT_KV] = ka.astype(BF16)
    qkva_ref[:, ATT_Q + ATT_KV:ATT_Q + 2 * ATT_KV] = pltpu.roll(ka, half, axis=1).astype(BF16)
    qkva_ref[:, ATT_Q + 2 * ATT_KV:ATT_Q + 3 * ATT_KV] = va.astype(BF16)
    qkva_ref[:, ATT_Q + 3 * ATT_KV:ATT_Q + 4 * ATT_KV] = pltpu.roll(va, half, axis=1).astype(BF16)
    qkm_ref[...] = _dot(hb, w_ref[:, C_QKM:C_VM])
    vm_ref[...] = _dot(hb, w_ref[:, C_VM:C_OM]).astype(BF16)
    om_ref[...] = _dot(hb, w_ref[:, C_OM:C_G])
    gat_ref[...] = _dot(hb, w_ref[:, C_G:IN_COLS_PAD])


def _inproj_call(x, mod, g1, w_in_p, tm):
    bsz, s, d = x.shape
    outs = (
        jax.ShapeDtypeStruct((bsz, s, QKVA_W), BF16),
        jax.ShapeDtypeStruct((bsz, s, 2 * M_W), F32),
        jax.ShapeDtypeStruct((bsz, s, M_W), BF16),
        jax.ShapeDtypeStruct((bsz, s, M_W), F32),
        jax.ShapeDtypeStruct((bsz, s, GATE_PAD), F32),
    )
    tile = lambda w: pl.BlockSpec((None, tm, w), lambda b, i: (b, i, 0))
    return pl.pallas_call(
        _inproj_kernel,
        grid=(bsz, s // tm),
        in_specs=[
            tile(d),
            pl.BlockSpec((None, N_MOD, d), lambda b, i: (b, 0, 0)),
            pl.BlockSpec((1, d), lambda b, i: (0, 0)),
            pl.BlockSpec((d, IN_COLS_PAD), lambda b, i: (0, 0), pipeline_mode=pl.Buffered(1)),
        ],
        out_specs=[tile(QKVA_W), tile(2 * M_W), tile(M_W), tile(M_W), tile(GATE_PAD)],
        out_shape=outs,
        compiler_params=pltpu.CompilerParams(
            dimension_semantics=("arbitrary", "arbitrary"), vmem_limit_bytes=VMEM_LIMIT),
        name="inproj",
    )(x, mod, g1, w_in_p)


def _interleave(main, side, main_costs, side_costs):
    ta, tb = float(sum(main_costs)), float(sum(side_costs))
    ia = ib = 0
    da = db = 0.0
    while ia < len(main_costs) or ib < len(side_costs):
        if ib >= len(side_costs) or (ia < len(main_costs) and da / ta <= db / tb):
            next(main)
            da += main_costs[ia]
            ia += 1
        else:
            next(side)
            db += side_costs[ib]
            ib += 1


def _mlstm_kernel(qk_ref, qkp_ref, qkn_ref, v_ref, gat_ref, cw_ref, cb_ref, bg_ref,
                  h_ref, q_scr, k_scr, c_scr, m_scr, *, reverse, nblk, nchunk):
    j = pl.program_id(1)
    blk = (nblk - 1 - j) if reverse else j
    _mlstm_reset(j == 0, c_scr, m_scr)
    for _ in _mlstm_steps(blk, qk_ref, qkp_ref, qkn_ref, v_ref, gat_ref, cw_ref, cb_ref, bg_ref,
                          h_ref, q_scr, k_scr, c_scr, m_scr, reverse=reverse, nblk=nblk, nchunk=nchunk):
        pass


def _mlstm_reset(first, c_scr, m_scr):
    @pl.when(first)
    def _init():
        c_scr[...] = jnp.zeros_like(c_scr)
        m_scr[...] = jnp.zeros_like(m_scr)


def _mlstm_step_costs(nchunk):
    return [1.0] * (nchunk * (1 + N_HEADS_M))


def _mlstm_steps(blk, qk_ref, qkp_ref, qkn_ref, v_ref, gat_ref, cw_ref, cb_ref, bg_ref,
                 h_ref, q_scr, k_scr, c_scr, m_scr, *, reverse, nblk, nchunk):
    L = CHUNK
    ti = lax.broadcasted_iota(jnp.int32, (L, L), 0)
    si = lax.broadcasted_iota(jnp.int32, (L, L), 1)
    causal = (si >= ti) if reverse else (si <= ti)
    tri = jnp.where(causal, 1.0, 0.0).astype(BF16)
    lane = lax.broadcasted_iota(jnp.int32, (1, L), 1)
    row = lax.broadcasted_iota(jnp.int32, (L, 1), 0)
    last = 0 if reverse else L - 1
    i_off, f_off = (2 * N_HEADS_M, 3 * N_HEADS_M) if reverse else (0, N_HEADS_M)
    ones = jnp.ones((L, HEAD_DIM_M), BF16)

    order = range(nchunk - 1, -1, -1) if reverse else range(nchunk)
    for g in order:
        rows = slice(g * L, (g + 1) * L)
        x = qk_ref[rows, :]
        if g == 0:
            prev_row = jnp.where(blk == 0, 0.0, qkp_ref[7:8, :])
        else:
            prev_row = qk_ref[g * L - 1:g * L, :]
        if g == nchunk - 1:
            next_row = jnp.where(blk == nblk - 1, 0.0, qkn_ref[0:1, :])
        else:
            next_row = qk_ref[(g + 1) * L:(g + 1) * L + 1, :]
        xm1 = jnp.where(row == 0, prev_row, pltpu.roll(x, 1, axis=0))
        xp1 = jnp.where(row == L - 1, next_row, pltpu.roll(x, L - 1, axis=0))
        y = xm1 * cw_ref[0:1, :] + x * cw_ref[1:2, :] + xp1 * cw_ref[2:3, :] + cb_ref[...]
        y = jax.nn.silu(y)
        q_scr[rows, :] = (y[:, :M_W] * (HEAD_DIM_M ** -0.5)).astype(BF16)
        k_scr[rows, :] = y[:, M_W:]
        yield

        gc = gat_ref[rows, :] + bg_ref[...]
        logf = jax.nn.log_sigmoid(gc)
        hi = logf.astype(BF16)
        r1 = logf - hi.astype(F32)
        mid = r1.astype(BF16)
        lo = (r1 - mid.astype(F32)).astype(BF16)
        bcum = _dot(tri, hi) + _dot(tri, mid) + _dot(tri, lo)
        gct = gc.T
        bct = bcum.T
        for h in range(N_HEADS_M):
            cols = slice(h * HEAD_DIM_M, (h + 1) * HEAD_DIM_M)
            ig_row = gct[i_off + h:i_off + h + 1, :]
            bc_row = bct[f_off + h:f_off + h + 1, :]
            r_row = ig_row - bc_row
            btot = jnp.sum(jnp.where(lane == last, bc_row, 0.0), axis=1, keepdims=True)
            rmax = jnp.max(r_row, axis=1, keepdims=True)
            a_max = btot + rmax
            m_prev = m_scr[h:h + 1, 0:1]
            m_new = jnp.maximum(btot + m_prev, a_max)
            decay = jnp.exp(btot + m_prev - m_new)
            w_row = jnp.exp(r_row - rmax) * jnp.exp(a_max - m_new)

            rm = jnp.where(causal, r_row, -jnp.inf)
            u = jnp.maximum(jnp.max(rm, axis=1, keepdims=True), m_prev)
            dm = jnp.exp(rm - u)
            q = q_scr[rows, cols]
            kt = k_scr[rows, cols].T
            sc = (_dot(q, kt.astype(BF16)) * dm).astype(BF16)
            inter = jnp.exp(m_prev - u)
            qi = (q.astype(F32) * inter).astype(BF16)
            vaug = jnp.concatenate([v_ref[rows, cols], ones], axis=1)
            cprev = c_scr[h]
            lhs = jnp.concatenate([sc, qi], axis=1)
            rhs = jnp.concatenate([vaug, cprev.astype(BF16)], axis=0)
            out = _dot(lhs, rhs)
            bc_col = bcum[:, f_off + h:f_off + h + 1]
            floor = jnp.exp(-(bc_col + u))
            h_ref[rows, cols] = out[:, :HEAD_DIM_M] / jnp.maximum(jnp.abs(out[:, HEAD_DIM_M:]), floor)

            c_scr[h] = decay * cprev + _dot((kt * w_row).astype(BF16), vaug)
            m_scr[h:h + 1, :] = jnp.broadcast_to(m_new, (1, LANES))
            yield


def _mlstm_call(qkm, vm, gates, conv_w, conv_b, bg_pad, tm, reverse):
    bsz, s, _ = qkm.shape
    nblk = s // tm
    r8 = tm // 8
    pos = (lambda j: nblk - 1 - j) if reverse else (lambda j: j)
    tile = lambda w: pl.BlockSpec((None, tm, w), lambda b, j: (b, pos(j), 0))
    prev_spec = pl.BlockSpec((None, 8, 2 * M_W), lambda b, j: (b, jnp.maximum(pos(j) * r8 - 1, 0), 0))
    next_spec = pl.BlockSpec((None, 8, 2 * M_W),
                             lambda b, j: (b, jnp.minimum((pos(j) + 1) * r8, s // 8 - 1), 0))
    const = lambda shp: pl.BlockSpec(shp, lambda b, j: (0, 0))
    kern = functools.partial(_mlstm_kernel, reverse=reverse, nblk=nblk, nchunk=tm // CHUNK)
    return pl.pallas_call(
        kern,
        grid=(bsz, nblk),
        in_specs=[tile(2 * M_W), prev_spec, next_spec, tile(M_W), tile(GATE_PAD),
                  const((3, 2 * M_W)), const((1, 2 * M_W)), const((1, GATE_PAD))],
        out_specs=tile(M_W),
        out_shape=jax.ShapeDtypeStruct((bsz, s, M_W), F32),
        scratch_shapes=[
            pltpu.VMEM((tm, M_W), BF16),
            pltpu.VMEM((tm, M_W), F32),
            pltpu.VMEM((N_HEADS_M, HEAD_DIM_M, 2 * HEAD_DIM_M), F32),
            pltpu.VMEM((8, LANES), F32),
        ],
        compiler_params=pltpu.CompilerParams(
            dimension_semantics=("arbitrary", "arbitrary"), vmem_limit_bytes=VMEM_LIMIT),
        name="mlstm_bwd" if reverse else "mlstm_fwd",
    )(qkm, qkm, qkm, vm, gates, conv_w, conv_b, bg_pad)


def _attn_kernel(sink_ref, q_ref, kvp_ref, kvn_ref, g_ref, o_ref, bias_scr, *, nt, nsub):
    b = pl.program_id(0)
    j = pl.program_id(1)
    nk = 3 * BLOCK

    @pl.when((b == 0) & (j == 0))
    def _init():
        row = lax.broadcasted_iota(jnp.int32, (BLOCK, nk), 0)
        col = lax.broadcasted_iota(jnp.int32, (BLOCK, nk), 1)
        dist = jnp.abs(col - BLOCK - row)
        distf = dist.astype(F32)
        for var in range(3):
            ok = dist <= WINDOW
            if var == 1:
                ok = ok & (col >= BLOCK)
            elif var == 2:
                ok = ok & (col < 2 * BLOCK)
            for h in range(N_HEADS_ATT):
                slope = 2.0 ** (-8.0 * (h + 1.0) / N_HEADS_ATT)
                bias_scr[var * N_HEADS_ATT + h] = jnp.where(ok, -slope * distf, -jnp.inf)

    lane_k = lax.broadcasted_iota(jnp.int32, (nk, LANES), 1)
    ones_a = jnp.where(lane_k < HEAD_DIM_ATT, 1.0, 0.0).astype(BF16)
    ones_b = jnp.where(lane_k < HEAD_DIM_ATT, 0.0, 1.0).astype(BF16)
    lo_half_q = lax.broadcasted_iota(jnp.int32, (BLOCK, LANES), 1) < HEAD_DIM_ATT

    def kv_block(idx):
        if idx < 0:
            return kvp_ref[...]
        if idx >= nsub:
            return kvn_ref[...]
        return q_ref[idx * BLOCK:(idx + 1) * BLOCK, ATT_Q:QKVA_W]

    for n in range(nsub):
        rows = slice(n * BLOCK, (n + 1) * BLOCK)
        kv = jnp.concatenate([kv_block(n - 1), kv_block(n), kv_block(n + 1)], axis=0)
        if n == 0:
            var = jnp.where(j == 0, 1, 0)
        elif n == nsub - 1:
            var = jnp.where(j == nt - 1, 2, 0)
        else:
            var = 0
        pieces = []
        for kvh in range(N_KV_HEADS):
            k_st, k_sw = kv[:, 0:LANES], kv[:, LANES:2 * LANES]
            v_st, v_sw = kv[:, 2 * LANES:3 * LANES], kv[:, 3 * LANES:4 * LANES]
            if kvh == 0:
                k_lo, k_hi, v_lo, v_hi = k_st, k_sw, v_st, v_sw
            else:
                k_lo, k_hi, v_lo, v_hi = k_sw, k_st, v_sw, v_st
            kk = jnp.concatenate([k_lo * ones_a, k_hi * ones_b], axis=0)
            vv = jnp.concatenate([
                jnp.concatenate([v_lo * ones_a, ones_a], axis=1),
                jnp.concatenate([v_hi * ones_b, ones_b], axis=1)], axis=0)
            for pair in range(GROUP_SIZE // 2):
                h0 = kvh * GROUP_SIZE + 2 * pair
                qp = q_ref[rows, h0 * HEAD_DIM_ATT:(h0 + 2) * HEAD_DIM_ATT]
                s2 = _dot_nt(qp, kk)
                ps, es = [], []
                for t in range(2):
                    logits = s2[:, t * nk:(t + 1) * nk] + bias_scr[var * N_HEADS_ATT + h0 + t]
                    sink = sink_ref[h0 + t]
                    mx = jnp.maximum(jnp.max(logits, axis=-1, keepdims=True), sink)
                    ps.append(jnp.exp(logits - mx).astype(BF16))
                    es.append(jnp.exp(sink - mx))
                res = _dot(jnp.concatenate(ps, axis=1), vv)
                den = res[:, LANES:] + jnp.where(lo_half_q, es[0], es[1])
                pieces.append(res[:, :LANES] / den)
        att = jnp.concatenate(pieces, axis=1)
        o_ref[rows, :] = _rms(att, g_ref[...]).astype(o_ref.dtype)


def _attn_call(sink, qkva, g_attn, tq):
    bsz, s, _ = qkva.shape
    nt = s // tq
    nsub = tq // BLOCK
    nb = s // BLOCK
    kvw = QKVA_W - ATT_Q
    kern = functools.partial(_attn_kernel, nt=nt, nsub=nsub)
    return pl.pallas_call(
        kern,
        grid=(bsz, nt),
        in_specs=[
            pl.BlockSpec(memory_space=pltpu.SMEM),
            pl.BlockSpec((None, tq, QKVA_W), lambda b, j: (b, j, 0)),
            pl.BlockSpec((None, BLOCK, kvw), lambda b, j: (b, jnp.maximum(j * nsub - 1, 0), 1)),
            pl.BlockSpec((None, BLOCK, kvw), lambda b, j: (b, jnp.minimum((j + 1) * nsub, nb - 1), 1)),
            pl.BlockSpec((1, ATT_Q), lambda b, j: (0, 0)),
        ],
        out_specs=pl.BlockSpec((None, tq, ATT_Q), lambda b, j: (b, j, 0)),
        out_shape=jax.ShapeDtypeStruct((bsz, s, ATT_Q), BF16),
        scratch_shapes=[pltpu.VMEM((3 * N_HEADS_ATT, BLOCK, 3 * BLOCK), F32)],
        compiler_params=pltpu.CompilerParams(
            dimension_semantics=("arbitrary", "arbitrary"), vmem_limit_bytes=VMEM_LIMIT),
        name="attn",
    )(sink, qkva, qkva, qkva, g_attn)


def _bwd_outffn_kernel(qk_ref, qkp_ref, qkn_ref, v_ref, gat_ref, cw_ref, cb_ref, bg_ref,
                       x_ref, att_ref, hf_ref, om_ref, mod_ref, gm_ref, wo_ref, g2_ref,
                       w1_ref, w2_ref, gf_ref, o_ref,
                       q_scr, k_scr, c_scr, m_scr, hb_scr, x1_scr, hff_scr, hid_scr,
                       *, nblk, ntiles, nchunk, final):
    j = pl.program_id(0)

    @pl.when(j == 0)
    def _init():
        hb_scr[...] = jnp.zeros_like(hb_scr)

    t = jnp.minimum(j, ntiles - 1)
    within = t % nblk
    _mlstm_reset(within == 0, c_scr, m_scr)
    ffn = _outffn_steps(x_ref, att_ref, hf_ref, hb_scr, om_ref, mod_ref, gm_ref, wo_ref, g2_ref,
                        w1_ref, w2_ref, gf_ref, o_ref, x1_scr, hff_scr, hid_scr, final=final)
    mls = _mlstm_steps(nblk - 1 - within, qk_ref, qkp_ref, qkn_ref, v_ref, gat_ref, cw_ref, cb_ref,
                       bg_ref, hb_scr, q_scr, k_scr, c_scr, m_scr, reverse=True, nblk=nblk,
                       nchunk=nchunk)
    _interleave(ffn, mls, _OUTFFN_STEP_COSTS, _mlstm_step_costs(nchunk))


def _bwd_outffn_call(qkm, vm, gates, conv_w, conv_b, bg_pad, x, att, hf, om, mod, g_m, w_out, g2,
                     w1, w2, g_final, tm, final):
    bsz, s, d = x.shape
    nblk = s // tm
    ntiles = bsz * nblk
    r8 = tm // 8

    def where(t):
        return t // nblk, nblk - 1 - t % nblk

    def m_idx(j):
        return where(jnp.minimum(j, ntiles - 1))

    def f_idx(j):
        return where(jnp.maximum(j - 1, 0))

    m_tile = lambda w: pl.BlockSpec((None, tm, w), lambda j: (*m_idx(j), 0))
    f_tile = lambda w: pl.BlockSpec((None, tm, w), lambda j: (*f_idx(j), 0))
    prev_spec = pl.BlockSpec(
        (None, 8, 2 * M_W), lambda j: (m_idx(j)[0], jnp.maximum(m_idx(j)[1] * r8 - 1, 0), 0))
    next_spec = pl.BlockSpec(
        (None, 8, 2 * M_W), lambda j: (m_idx(j)[0], jnp.minimum((m_idx(j)[1] + 1) * r8, s // 8 - 1), 0))
    const = lambda shp: pl.BlockSpec(shp, lambda j: (0, 0))
    weight = lambda shp: pl.BlockSpec(shp, lambda j: (0, 0), pipeline_mode=pl.Buffered(1))
    kern = functools.partial(_bwd_outffn_kernel, nblk=nblk, ntiles=ntiles, nchunk=tm // CHUNK,
                             final=final)
    return pl.pallas_call(
        kern,
        grid=(ntiles + 1,),
        in_specs=[
            m_tile(2 * M_W), prev_spec, next_spec, m_tile(M_W), m_tile(GATE_PAD),
            const((3, 2 * M_W)), const((1, 2 * M_W)), const((1, GATE_PAD)),
            f_tile(d), f_tile(ATT_Q), f_tile(M_W), f_tile(M_W),
            pl.BlockSpec((None, N_MOD, d), lambda j: (f_idx(j)[0], 0, 0)),
            const((1, M_W)), weight((ATT_Q + M_W, d)), const((1, d)),
            weight((d, 2 * D_FF)), weight((D_FF, d)), const((1, d)),
        ],
        out_specs=f_tile(d),
        out_shape=jax.ShapeDtypeStruct((bsz, s, d), F32),
        scratch_shapes=[
            pltpu.VMEM((tm, M_W), BF16),
            pltpu.VMEM((tm, M_W), F32),
            pltpu.VMEM((N_HEADS_M, HEAD_DIM_M, 2 * HEAD_DIM_M), F32),
            pltpu.VMEM((8, LANES), F32),
            pltpu.VMEM((tm, M_W), F32),
            pltpu.VMEM((tm, d), F32),
            pltpu.VMEM((tm, d), BF16),
            pltpu.VMEM((tm, D_FF), BF16),
        ],
        compiler_params=pltpu.CompilerParams(
            dimension_semantics=("arbitrary",), vmem_limit_bytes=VMEM_LIMIT),
        name="bwd_outffn",
    )(qkm, qkm, qkm, vm, gates, conv_w, conv_b, bg_pad, x, att, hf, om, mod, g_m, w_out, g2, w1, w2,
      g_final)


OUT_COL_BLOCK = 256
_OUTFFN_STEP_COSTS = [2.5] + [1.0] * N_FF_CHUNKS + [1.4] * (D_MODEL // OUT_COL_BLOCK) + [0.3]


def _outffn_steps(x_ref, att_ref, hf_ref, hb_ref, om_ref, mod_ref, gm_ref, wo_ref, g2_ref,
                  w1_ref, w2_ref, gf_ref, o_ref, x1_scr, hff_scr, hid_scr, *, final):
    hs = hf_ref[...] + hb_ref[...]
    parts = []
    for h in range(N_HEADS_M):
        cols = slice(h * HEAD_DIM_M, (h + 1) * HEAD_DIM_M)
        parts.append(_rms(hs[:, cols], gm_ref[:, cols]))
    hm = jax.nn.sigmoid(om_ref[...]) * jnp.concatenate(parts, axis=1)
    mixin = jnp.concatenate([att_ref[...], hm.astype(BF16)], axis=1)
    x1 = x_ref[...] + mod_ref[2:3, :] * _dot(mixin, wo_ref[...])
    x1_scr[...] = x1
    hff_scr[...] = (_rms(x1, g2_ref[...]) * (1.0 + mod_ref[4:5, :]) + mod_ref[3:4, :]).astype(BF16)
    yield

    for c in range(N_FF_CHUNKS):
        hff = hff_scr[...]
        gate = _dot(hff, w1_ref[:, FF_CHUNK * c:FF_CHUNK * (c + 1)])
        up = _dot(hff, w1_ref[:, D_FF + FF_CHUNK * c:D_FF + FF_CHUNK * (c + 1)])
        hid_scr[:, FF_CHUNK * c:FF_CHUNK * (c + 1)] = (jax.nn.silu(gate) * up).astype(BF16)
        yield

    ssq = None
    for n in range(D_MODEL // OUT_COL_BLOCK):
        cols = slice(n * OUT_COL_BLOCK, (n + 1) * OUT_COL_BLOCK)
        x2 = x1_scr[:, cols] + mod_ref[5:6, cols] * _dot(hid_scr[...], w2_ref[:, cols])
        o_ref[:, cols] = x2
        if final:
            part = jnp.sum(x2 * x2, axis=-1, keepdims=True)
            ssq = part if ssq is None else ssq + part
        yield

    if final:
        o_ref[...] = o_ref[...] * lax.rsqrt(ssq * (1.0 / D_MODEL) + EPS) * gf_ref[...]
    yield


def _layer(l, x, c, w_mod, b_mod, g_norm1, w_in, conv_w, conv_b, b_gates, sink,
           g_attn_out, g_mlstm_out, w_out, g_norm2, w_ffn_in, w_ffn_out, g_final, final):
    d = x.shape[-1]
    mod = _mod_call(c, w_mod[l], b_mod[l])

    w_in_p = _cast_call(w_in, l, IN_COLS_PAD)
    qkva, qkm, vm, om, gates = _inproj_call(x, mod, g_norm1[l].reshape(1, d), w_in_p, tm=512)

    bg_pad = jnp.pad(b_gates[l], (0, GATE_PAD - N_GATES)).reshape(1, GATE_PAD)
    cb = conv_b[l].reshape(1, 2 * M_W)
    hf = _mlstm_call(qkm, vm, gates, conv_w[l], cb, bg_pad, tm=512, reverse=False)

    att = _attn_call(sink[l], qkva, g_attn_out[l].reshape(1, ATT_Q), tq=512)

    return _bwd_outffn_call(qkm, vm, gates, conv_w[l], cb, bg_pad, x, att, hf, om, mod,
                            g_mlstm_out[l].reshape(1, M_W), _cast_call(w_out, l),
                            g_norm2[l].reshape(1, d), _cast_call(w_ffn_in, l), _cast_call(w_ffn_out, l),
                            g_final.reshape(1, d), tm=512, final=final)


def kernel(x, c, w_mod, b_mod, g_norm1, w_in, conv_w, conv_b, b_gates, sink, g_attn_out,
           g_mlstm_out, w_out, g_norm2, w_ffn_in, w_ffn_out, g_final):
    depth = w_mod.shape[0]
    for l in range(depth):
        x = _layer(l, x, c, w_mod, b_mod, g_norm1, w_in, conv_w, conv_b, b_gates, sink, g_attn_out,
                   g_mlstm_out, w_out, g_norm2, w_ffn_in, w_ffn_out, g_final, final=(l == depth - 1))
    return x
```

```python
import functools

import jax
import jax.numpy as jnp
from jax import lax
from jax.experimental import pallas as pl
from jax.experimental.pallas import tpu as pltpu

F32 = jnp.float32
BF16 = jnp.bfloat16

D_MODEL = 1024
EPS = 1e-6
N_HEADS_ATT = 8
N_KV_HEADS = 2
HEAD_DIM_ATT = 64
GROUP_SIZE = N_HEADS_ATT // N_KV_HEADS
WINDOW = 128
BLOCK = 128
N_HEADS_M = 4
HEAD_DIM_M = 128
CHUNK = 128
ATT_Q = N_HEADS_ATT * HEAD_DIM_ATT
ATT_KV = N_KV_HEADS * HEAD_DIM_ATT
M_W = N_HEADS_M * HEAD_DIM_M
N_GATES = 4 * N_HEADS_M
D_FF = 2816
N_MOD = 6

LANES = 128
GATE_PAD = LANES
FF_CHUNK = 256
N_FF_CHUNKS = D_FF // FF_CHUNK
OUT_COL_BLOCK = 256
VMEM_LIMIT = 56 * 1024 * 1024

C_QA = 0
C_KA = ATT_Q
C_VA = ATT_Q + ATT_KV
C_QKM = ATT_Q + 2 * ATT_KV
C_VM = C_QKM + 2 * M_W
C_OM = C_VM + M_W
C_G = C_OM + M_W
IN_COLS_PAD = C_G + GATE_PAD
QKVA_W = ATT_Q + 4 * ATT_KV


def _dot(a, b):
    return jnp.dot(a, b, preferred_element_type=F32)


def _dot_nt(a, b):
    return lax.dot_general(a, b, (((1,), (1,)), ((), ())), preferred_element_type=F32)


def _rms(x, g):
    return x * lax.rsqrt(jnp.mean(x * x, axis=-1, keepdims=True) + EPS) * g


def _alternate(*streams):
    live = list(streams)
    while live:
        for s in list(live):
            if next(s, StopIteration) is StopIteration:
                live.remove(s)


def _cast_kernel(w_ref, o_ref):
    n = w_ref.shape[1]
    o_ref[:, :n] = w_ref[...].astype(o_ref.dtype)
    if o_ref.shape[1] > n:
        o_ref[:, n:] = jnp.zeros((o_ref.shape[0], o_ref.shape[1] - n), o_ref.dtype)


def _cast_call(w, layer, n_out=None, bm=256):
    _, k, n = w.shape
    n_out = n if n_out is None else n_out
    return pl.pallas_call(
        _cast_kernel,
        grid=(k // bm,),
        in_specs=[pl.BlockSpec((None, bm, n), lambda i: (layer, i, 0))],
        out_specs=pl.BlockSpec((bm, n_out), lambda i: (i, 0)),
        out_shape=jax.ShapeDtypeStruct((k, n_out), BF16),
        compiler_params=pltpu.CompilerParams(dimension_semantics=("arbitrary",)),
        name="cast",
    )(w)


def _mod_kernel(c_ref, w_ref, b_ref, o_ref):
    s = jax.nn.silu(c_ref[...]).astype(BF16)
    o_ref[...] = _dot(s, w_ref[...].astype(BF16)) + b_ref[...]


def _mod_call(c, w_mod, b_mod):
    bsz = c.shape[0]
    rows = 8
    cp = jnp.pad(c, ((0, rows - bsz), (0, 0)))
    n = w_mod.shape[1]
    bn = 1024
    out = pl.pallas_call(
        _mod_kernel,
        grid=(n // bn,),
        in_specs=[
            pl.BlockSpec((rows, D_MODEL), lambda i: (0, 0)),
            pl.BlockSpec((D_MODEL, bn), lambda i: (0, i)),
            pl.BlockSpec((1, bn), lambda i: (0, i)),
        ],
        out_specs=pl.BlockSpec((rows, bn), lambda i: (0, i)),
        out_shape=jax.ShapeDtypeStruct((rows, n), F32),
        compiler_params=pltpu.CompilerParams(dimension_semantics=("arbitrary",)),
        name="mod",
    )(cp, w_mod, b_mod.reshape(1, n))
    return out[:bsz].reshape(bsz, N_MOD, D_MODEL)


def _inproj_kernel(x_ref, mod_ref, g_ref, w_ref, qkva_ref, qkm_ref, vm_ref, om_ref, gat_ref):
    x = x_ref[...]
    h = _rms(x, g_ref[...]) * (1.0 + mod_ref[1:2, :]) + mod_ref[0:1, :]
    hb = h.astype(BF16)
    qa = _dot(hb, w_ref[:, C_QA:C_KA]) * (HEAD_DIM_ATT ** -0.5)
    ka = _dot(hb, w_ref[:, C_KA:C_VA])
    va = _dot(hb, w_ref[:, C_VA:C_QKM])
    half = HEAD_DIM_ATT
    qkva_ref[:, 0:ATT_Q] = qa.astype(BF16)
    qkva_ref[:, ATT_Q:ATT_Q + ATT_KV] = ka.astype(BF16)
    qkva_ref[:, ATT_Q + ATT_KV:ATT_Q + 2 * ATT_KV] = pltpu.roll(ka, half, axis=1).astype(BF16)
    qkva_ref[:, ATT_Q + 2 * ATT_KV:ATT_Q + 3 * ATT_KV] = va.astype(BF16)
    qkva_ref[:, ATT_Q + 3 * ATT_KV:ATT_Q + 4 * ATT_KV] = pltpu.roll(va, half, axis=1).astype(BF16)
    qkm_ref[...] = _dot(hb, w_ref[:, C_QKM:C_VM])
    vm_ref[...] = _dot(hb, w_ref[:, C_VM:C_OM]).astype(BF16)
    om_ref[...] = _dot(hb, w_ref[:, C_OM:C_G])
    gat_ref[...] = _dot(hb, w_ref[:, C_G:IN_COLS_PAD])


def _inproj_call(x, mod, g1, w_in_p, tm):
    bsz, s, d = x.shape
    outs = (
        jax.ShapeDtypeStruct((bsz, s, QKVA_W), BF16),
        jax.ShapeDtypeStruct((bsz, s, 2 * M_W), F32),
        jax.ShapeDtypeStruct((bsz, s, M_W), BF16),
        jax.ShapeDtypeStruct((bsz, s, M_W), F32),
        jax.ShapeDtypeStruct((bsz, s, GATE_PAD), F32),
    )
    tile = lambda w: pl.BlockSpec((None, tm, w), lambda b, i: (b, i, 0))
    return pl.pallas_call(
        _inproj_kernel,
        grid=(bsz, s // tm),
        in_specs=[
            tile(d),
            pl.BlockSpec((None, N_MOD, d), lambda b, i: (b, 0, 0)),
            pl.BlockSpec((1, d), lambda b, i: (0, 0)),
            pl.BlockSpec((d, IN_COLS_PAD), lambda b, i: (0, 0), pipeline_mode=pl.Buffered(1)),
        ],
        out_specs=[tile(QKVA_W), tile(2 * M_W), tile(M_W), tile(M_W), tile(GATE_PAD)],
        out_shape=outs,
        compiler_params=pltpu.CompilerParams(
            dimension_semantics=("arbitrary", "arbitrary"), vmem_limit_bytes=VMEM_LIMIT),
        name="inproj",
    )(x, mod, g1, w_in_p)


def _mlstm_reset(first, c_scr, m_scr):
    @pl.when(first)
    def _init():
        c_scr[...] = jnp.zeros_like(c_scr)
        m_scr[...] = jnp.zeros_like(m_scr)


def _mlstm_stream(blk, qk_ref, qkp_ref, qkn_ref, v_ref, gat_ref, cw_ref, cb_ref, bg_ref,
                  h_ref, q_scr, k_scr, c_scr, m_scr, *, reverse, nblk, nchunk):
    L = CHUNK
    ti = lax.broadcasted_iota(jnp.int32, (L, L), 0)
    si = lax.broadcasted_iota(jnp.int32, (L, L), 1)
    causal = (si >= ti) if reverse else (si <= ti)
    tri = jnp.where(causal, 1.0, 0.0).astype(BF16)
    lane = lax.broadcasted_iota(jnp.int32, (1, L), 1)
    row = lax.broadcasted_iota(jnp.int32, (L, 1), 0)
    last = 0 if reverse else L - 1
    i_off, f_off = (2 * N_HEADS_M, 3 * N_HEADS_M) if reverse else (0, N_HEADS_M)
    ones = jnp.ones((L, HEAD_DIM_M), BF16)
    order = list(range(nchunk - 1, -1, -1) if reverse else range(nchunk))
    rows_of = lambda g: slice(g * L, (g + 1) * L)

    gates = {}
    for g in order:
        gc = gat_ref[rows_of(g), :] + bg_ref[...]
        logf = jax.nn.log_sigmoid(gc)
        hi = logf.astype(BF16)
        r1 = logf - hi.astype(F32)
        mid = r1.astype(BF16)
        lo = (r1 - mid.astype(F32)).astype(BF16)
        gates[g] = (gc, _dot(tri, hi) + _dot(tri, mid) + _dot(tri, lo))
        yield

    for g in order:
        x = qk_ref[rows_of(g), :]
        if g == 0:
            prev_row = jnp.where(blk == 0, 0.0, qkp_ref[7:8, :])
        else:
            prev_row = qk_ref[g * L - 1:g * L, :]
        if g == nchunk - 1:
            next_row = jnp.where(blk == nblk - 1, 0.0, qkn_ref[0:1, :])
        else:
            next_row = qk_ref[(g + 1) * L:(g + 1) * L + 1, :]
        xm1 = jnp.where(row == 0, prev_row, pltpu.roll(x, 1, axis=0))
        xp1 = jnp.where(row == L - 1, next_row, pltpu.roll(x, L - 1, axis=0))
        y = xm1 * cw_ref[0:1, :] + x * cw_ref[1:2, :] + xp1 * cw_ref[2:3, :] + cb_ref[...]
        y = jax.nn.silu(y)
        q_scr[rows_of(g), :] = (y[:, :M_W] * (HEAD_DIM_M ** -0.5)).astype(BF16)
        k_scr[rows_of(g), :] = y[:, M_W:]
        yield

    for g in order:
        gc, bcum = gates[g]
        gates[g] = (gc.T, bcum.T, bcum)
        yield

    def head(g, h):
        rows = rows_of(g)
        cols = slice(h * HEAD_DIM_M, (h + 1) * HEAD_DIM_M)
        gct, bct, bcum = gates[g]
        r_row = gct[i_off + h:i_off + h + 1, :] - bct[f_off + h:f_off + h + 1, :]
        btot = jnp.sum(jnp.where(lane == last, bct[f_off + h:f_off + h + 1, :], 0.0),
                       axis=1, keepdims=True)
        rmax = jnp.max(r_row, axis=1, keepdims=True)
        a_max = btot + rmax
        m_prev = m_scr[h:h + 1, 0:1]
        m_new = jnp.maximum(btot + m_prev, a_max)
        m_scr[h:h + 1, :] = jnp.broadcast_to(m_new, (1, LANES))
        decay = jnp.exp(btot + m_prev - m_new)
        w_row = jnp.exp(r_row - rmax) * jnp.exp(a_max - m_new)
        kt = k_scr[rows, cols].T
        rm = jnp.where(causal, r_row, -jnp.inf)
        cm = jnp.max(rm, axis=1, keepdims=True)
        yield
        q = q_scr[rows, cols]
        vaug = jnp.concatenate([v_ref[rows, cols], ones], axis=1)
        s = _dot(q, kt.astype(BF16))
        upd = _dot((kt * w_row).astype(BF16), vaug)
        u = jnp.maximum(cm, m_prev)
        dm = jnp.exp(rm - u)
        qi = (q.astype(F32) * jnp.exp(m_prev - u)).astype(BF16)
        yield
        cprev = c_scr[h]
        lhs = jnp.concatenate([(s * dm).astype(BF16), qi], axis=1)
        rhs = jnp.concatenate([vaug, cprev.astype(BF16)], axis=0)
        out = _dot(lhs, rhs)
        c_scr[h] = decay * cprev + upd
        yield
        floor = jnp.exp(-(bcum[:, f_off + h:f_off + h + 1] + u))
        h_ref[rows, cols] = out[:, :HEAD_DIM_M] / jnp.maximum(jnp.abs(out[:, HEAD_DIM_M:]), floor)
        yield

    n_stage, lag = 4, 2
    heads = {}
    for slot in range(lag * (nchunk - 1) + n_stage):
        for i, g in enumerate(order):
            stage = slot - lag * i
            if 0 <= stage < n_stage:
                for h in range(N_HEADS_M):
                    if stage == 0:
                        heads[g, h] = head(g, h)
                    next(heads[g, h])
                    yield


def _mlstm_kernel(qkf_ref, qkfp_ref, qkfn_ref, vf_ref, gf_ref,
                  qkb_ref, qkbp_ref, qkbn_ref, vb_ref, gb_ref, cw_ref, cb_ref, bg_ref,
                  hf_ref, hb_ref, qf_scr, kf_scr, cf_scr, mf_scr, qb_scr, kb_scr, cb_scr, mb_scr,
                  *, nblk, nchunk):
    j = pl.program_id(1)
    _mlstm_reset(j == 0, cf_scr, mf_scr)
    _mlstm_reset(j == 0, cb_scr, mb_scr)
    fwd = _mlstm_stream(j, qkf_ref, qkfp_ref, qkfn_ref, vf_ref, gf_ref, cw_ref, cb_ref, bg_ref,
                        hf_ref, qf_scr, kf_scr, cf_scr, mf_scr, reverse=False, nblk=nblk, nchunk=nchunk)
    bwd = _mlstm_stream(nblk - 1 - j, qkb_ref, qkbp_ref, qkbn_ref, vb_ref, gb_ref, cw_ref, cb_ref,
                        bg_ref, hb_ref, qb_scr, kb_scr, cb_scr, mb_scr, reverse=True, nblk=nblk,
                        nchunk=nchunk)
    _alternate(fwd, bwd)


def _mlstm_call(qkm, vm, gates, conv_w, conv_b, bg_pad, tm):
    bsz, s, _ = qkm.shape
    nblk = s // tm
    r8 = tm // 8
    const = lambda shp: pl.BlockSpec(shp, lambda b, j: (0, 0))

    def specs(pos):
        tile = lambda w: pl.BlockSpec((None, tm, w), lambda b, j: (b, pos(j), 0))
        prev_spec = pl.BlockSpec((None, 8, 2 * M_W),
                                 lambda b, j: (b, jnp.maximum(pos(j) * r8 - 1, 0), 0))
        next_spec = pl.BlockSpec((None, 8, 2 * M_W),
                                 lambda b, j: (b, jnp.minimum((pos(j) + 1) * r8, s // 8 - 1), 0))
        return [tile(2 * M_W), prev_spec, next_spec, tile(M_W), tile(GATE_PAD)], tile(M_W)

    in_f, out_f = specs(lambda j: j)
    in_b, out_b = specs(lambda j: nblk - 1 - j)
    state = [pltpu.VMEM((tm, M_W), BF16), pltpu.VMEM((tm, M_W), F32),
             pltpu.VMEM((N_HEADS_M, HEAD_DIM_M, 2 * HEAD_DIM_M), F32), pltpu.VMEM((8, LANES), F32)]
    return pl.pallas_call(
        functools.partial(_mlstm_kernel, nblk=nblk, nchunk=tm // CHUNK),
        grid=(bsz, nblk),
        in_specs=in_f + in_b + [const((3, 2 * M_W)), const((1, 2 * M_W)), const((1, GATE_PAD))],
        out_specs=[out_f, out_b],
        out_shape=[jax.ShapeDtypeStruct((bsz, s, M_W), F32)] * 2,
        scratch_shapes=state + state,
        compiler_params=pltpu.CompilerParams(
            dimension_semantics=("arbitrary", "arbitrary"), vmem_limit_bytes=VMEM_LIMIT),
        name="mlstm",
    )(qkm, qkm, qkm, vm, gates, qkm, qkm, qkm, vm, gates, conv_w, conv_b, bg_pad)


def _attn_kernel(sink_ref, q_ref, kvp_ref, kvn_ref, g_ref, o_ref, bias_scr, *, nt, nsub):
    b = pl.program_id(0)
    j = pl.program_id(1)
    nk = 3 * BLOCK

    @pl.when((b == 0) & (j == 0))
    def _init():
        row = lax.broadcasted_iota(jnp.int32, (BLOCK, nk), 0)
        col = lax.broadcasted_iota(jnp.int32, (BLOCK, nk), 1)
        dist = jnp.abs(col - BLOCK - row)
        distf = dist.astype(F32)
        for var in range(3):
            ok = dist <= WINDOW
            if var == 1:
                ok = ok & (col >= BLOCK)
            elif var == 2:
                ok = ok & (col < 2 * BLOCK)
            for h in range(N_HEADS_ATT):
                slope = 2.0 ** (-8.0 * (h + 1.0) / N_HEADS_ATT)
                bias_scr[var * N_HEADS_ATT + h] = jnp.where(ok, -slope * distf, -jnp.inf)

    lane_k = lax.broadcasted_iota(jnp.int32, (nk, LANES), 1)
    ones_a = jnp.where(lane_k < HEAD_DIM_ATT, 1.0, 0.0).astype(BF16)
    ones_b = jnp.where(lane_k < HEAD_DIM_ATT, 0.0, 1.0).astype(BF16)
    lo_half_q = lax.broadcasted_iota(jnp.int32, (BLOCK, LANES), 1) < HEAD_DIM_ATT

    def kv_block(idx):
        if idx < 0:
            return kvp_ref[...]
        if idx >= nsub:
            return kvn_ref[...]
        return q_ref[idx * BLOCK:(idx + 1) * BLOCK, ATT_Q:QKVA_W]

    for n in range(nsub):
        rows = slice(n * BLOCK, (n + 1) * BLOCK)
        kv = jnp.concatenate([kv_block(n - 1), kv_block(n), kv_block(n + 1)], axis=0)
        if n == 0:
            var = jnp.where(j == 0, 1, 0)
        elif n == nsub - 1:
            var = jnp.where(j == nt - 1, 2, 0)
        else:
            var = 0
        pieces = []
        for kvh in range(N_KV_HEADS):
            k_st, k_sw = kv[:, 0:LANES], kv[:, LANES:2 * LANES]
            v_st, v_sw = kv[:, 2 * LANES:3 * LANES], kv[:, 3 * LANES:4 * LANES]
            if kvh == 0:
                k_lo, k_hi, v_lo, v_hi = k_st, k_sw, v_st, v_sw
            else:
                k_lo, k_hi, v_lo, v_hi = k_sw, k_st, v_sw, v_st
            kk = jnp.concatenate([k_lo * ones_a, k_hi * ones_b], axis=0)
            vv = jnp.concatenate([
                jnp.concatenate([v_lo * ones_a, ones_a], axis=1),
                jnp.concatenate([v_hi * ones_b, ones_b], axis=1)], axis=0)
            for pair in range(GROUP_SIZE // 2):
                h0 = kvh * GROUP_SIZE + 2 * pair
                qp = q_ref[rows, h0 * HEAD_DIM_ATT:(h0 + 2) * HEAD_DIM_ATT]
                s2 = _dot_nt(qp, kk)
                ps, es = [], []
                for t in range(2):
                    logits = s2[:, t * nk:(t + 1) * nk] + bias_scr[var * N_HEADS_ATT + h0 + t]
                    sink = sink_ref[h0 + t]
                    mx = jnp.maximum(jnp.max(logits, axis=-1, keepdims=True), sink)
                    ps.append(jnp.exp(logits - mx).astype(BF16))
                    es.append(jnp.exp(sink - mx))
                res = _dot(jnp.concatenate(ps, axis=1), vv)
                den = res[:, LANES:] + jnp.where(lo_half_q, es[0], es[1])
                pieces.append(res[:, :LANES] / den)
        att = jnp.concatenate(pieces, axis=1)
        o_ref[rows, :] = _rms(att, g_ref[...]).astype(o_ref.dtype)


def _attn_call(sink, qkva, g_attn, tq):
    bsz, s, _ = qkva.shape
    nt = s // tq
    nsub = tq // BLOCK
    nb = s // BLOCK
    kvw = QKVA_W - ATT_Q
    kern = functools.partial(_attn_kernel, nt=nt, nsub=nsub)
    return pl.pallas_call(
        kern,
        grid=(bsz, nt),
        in_specs=[
            pl.BlockSpec(memory_space=pltpu.SMEM),
            pl.BlockSpec((None, tq, QKVA_W), lambda b, j: (b, j, 0)),
            pl.BlockSpec((None, BLOCK, kvw), lambda b, j: (b, jnp.maximum(j * nsub - 1, 0), 1)),
            pl.BlockSpec((None, BLOCK, kvw), lambda b, j: (b, jnp.minimum((j + 1) * nsub, nb - 1), 1)),
            pl.BlockSpec((1, ATT_Q), lambda b, j: (0, 0)),
        ],
        out_specs=pl.BlockSpec((None, tq, ATT_Q), lambda b, j: (b, j, 0)),
        out_shape=jax.ShapeDtypeStruct((bsz, s, ATT_Q), BF16),
        scratch_shapes=[pltpu.VMEM((3 * N_HEADS_ATT, BLOCK, 3 * BLOCK), F32)],
        compiler_params=pltpu.CompilerParams(
            dimension_semantics=("arbitrary", "arbitrary"), vmem_limit_bytes=VMEM_LIMIT),
        name="attn",
    )(sink, qkva, qkva, qkva, g_attn)


def _outffn_kernel(x_ref, att_ref, hf_ref, hb_ref, om_ref, mod_ref, gm_ref, wo_ref, g2_ref,
                   w1_ref, w2_ref, gf_ref, o_ref, hid_scr, *, final):
    hs = hf_ref[...] + hb_ref[...]
    parts = []
    for h in range(N_HEADS_M):
        cols = slice(h * HEAD_DIM_M, (h + 1) * HEAD_DIM_M)
        parts.append(_rms(hs[:, cols], gm_ref[:, cols]))
    hm = jax.nn.sigmoid(om_ref[...]) * jnp.concatenate(parts, axis=1)
    mixin = jnp.concatenate([att_ref[...], hm.astype(BF16)], axis=1)
    x1 = x_ref[...] + mod_ref[2:3, :] * _dot(mixin, wo_ref[...])

    hff = (_rms(x1, g2_ref[...]) * (1.0 + mod_ref[4:5, :]) + mod_ref[3:4, :]).astype(BF16)
    for c in range(N_FF_CHUNKS):
        gate = _dot(hff, w1_ref[:, FF_CHUNK * c:FF_CHUNK * (c + 1)])
        up = _dot(hff, w1_ref[:, D_FF + FF_CHUNK * c:D_FF + FF_CHUNK * (c + 1)])
        hid_scr[:, FF_CHUNK * c:FF_CHUNK * (c + 1)] = (jax.nn.silu(gate) * up).astype(BF16)
    x2 = x1 + mod_ref[5:6, :] * _dot(hid_scr[...], w2_ref[...])
    if final:
        x2 = _rms(x2, gf_ref[...])
    o_ref[...] = x2


def _outffn_call(x, att, hf, hb, om, mod, g_m, w_out, g2, w1, w2, g_final, tm, final):
    bsz, s, d = x.shape
    tile = lambda w: pl.BlockSpec((None, tm, w), lambda b, i: (b, i, 0))
    const = lambda shp: pl.BlockSpec(shp, lambda b, i: (0, 0))
    weight = lambda shp: pl.BlockSpec(shp, lambda b, i: (0, 0), pipeline_mode=pl.Buffered(1))
    return pl.pallas_call(
        functools.partial(_outffn_kernel, final=final),
        grid=(bsz, s // tm),
        in_specs=[
            tile(d), tile(ATT_Q), tile(M_W), tile(M_W), tile(M_W),
            pl.BlockSpec((None, N_MOD, d), lambda b, i: (b, 0, 0)),
            const((1, M_W)), weight((ATT_Q + M_W, d)), const((1, d)),
            weight((d, 2 * D_FF)), weight((D_FF, d)), const((1, d)),
        ],
        out_specs=tile(d),
        out_shape=jax.ShapeDtypeStruct((bsz, s, d), F32),
        scratch_shapes=[pltpu.VMEM((tm, D_FF), BF16)],
        compiler_params=pltpu.CompilerParams(
            dimension_semantics=("arbitrary", "arbitrary"), vmem_limit_bytes=VMEM_LIMIT),
        name="outffn",
    )(x, att, hf, hb, om, mod, g_m, w_out, g2, w1, w2, g_final)


def _layer(l, x, c, w_mod, b_mod, g_norm1, w_in, conv_w, conv_b, b_gates, sink,
           g_attn_out, g_mlstm_out, w_out, g_norm2, w_ffn_in, w_ffn_out, g_final, final):
    d = x.shape[-1]
    mod = _mod_call(c, w_mod[l], b_mod[l])

    w_in_p = _cast_call(w_in, l, IN_COLS_PAD)
    qkva, qkm, vm, om, gates = _inproj_call(x, mod, g_norm1[l].reshape(1, d), w_in_p, tm=512)

    bg_pad = jnp.pad(b_gates[l], (0, GATE_PAD - N_GATES)).reshape(1, GATE_PAD)
    hf, hb = _mlstm_call(qkm, vm, gates, conv_w[l], conv_b[l].reshape(1, 2 * M_W), bg_pad, tm=512)

    att = _attn_call(sink[l], qkva, g_attn_out[l].reshape(1, ATT_Q), tq=512)

    return _outffn_call(x, att, hf, hb, om, mod, g_mlstm_out[l].reshape(1, M_W), _cast_call(w_out, l),
                        g_norm2[l].reshape(1, d), _cast_call(w_ffn_in, l), _cast_call(w_ffn_out, l),
                        g_final.reshape(1, d), tm=512, final=final)


def kernel(x, c, w_mod, b_mod, g_norm1, w_in, conv_w, conv_b, b_gates, sink, g_attn_out,
           g_mlstm_out, w_out, g_norm2, w_ffn_in, w_ffn_out, g_final):
    depth = w_mod.shape[0]
    for l in range(depth):
        x = _layer(l, x, c, w_mod, b_mod, g_norm1, w_in, conv_w, conv_b, b_gates, sink, g_attn_out,
                   g_mlstm_out, w_out, g_norm2, w_ffn_in, w_ffn_out, g_final, final=(l == depth - 1))
    return x
```

```python
import functools

import jax
import jax.numpy as jnp
from jax import lax
from jax.experimental import pallas as pl
from jax.experimental.pallas import tpu as pltpu

F32 = jnp.float32
BF16 = jnp.bfloat16

D_MODEL = 1024
EPS = 1e-6
N_HEADS_ATT = 8
N_KV_HEADS = 2
HEAD_DIM_ATT = 64
GROUP_SIZE = N_HEADS_ATT // N_KV_HEADS
WINDOW = 128
BLOCK = 128
N_HEADS_M = 4
HEAD_DIM_M = 128
CHUNK = 128
ATT_Q = N_HEADS_ATT * HEAD_DIM_ATT
ATT_KV = N_KV_HEADS * HEAD_DIM_ATT
M_W = N_HEADS_M * HEAD_DIM_M
N_GATES = 4 * N_HEADS_M
D_FF = 2816
N_MOD = 6

LANES = 128
GATE_PAD = LANES
FF_CHUNK = 256
N_FF_CHUNKS = D_FF // FF_CHUNK
OUT_COL_BLOCK = 256
VMEM_LIMIT = 56 * 1024 * 1024

C_QA = 0
C_KA = ATT_Q
C_VA = ATT_Q + ATT_KV
C_QKM = ATT_Q + 2 * ATT_KV
C_VM = C_QKM + 2 * M_W
C_OM = C_VM + M_W
C_G = C_OM + M_W
IN_COLS_PAD = C_G + GATE_PAD
QKVA_W = ATT_Q + 4 * ATT_KV


def _dot(a, b):
    return jnp.dot(a, b, preferred_element_type=F32)


def _dot_nt(a, b):
    return lax.dot_general(a, b, (((1,), (1,)), ((), ())), preferred_element_type=F32)


def _rms(x, g):
    return x * lax.rsqrt(jnp.mean(x * x, axis=-1, keepdims=True) + EPS) * g


def _alternate(*streams):
    live = list(streams)
    while live:
        for s in list(live):
            if next(s, StopIteration) is StopIteration:
                live.remove(s)


def _cast_kernel(w_ref, o_ref):
    n = w_ref.shape[1]
    o_ref[:, :n] = w_ref[...].astype(o_ref.dtype)
    if o_ref.shape[1] > n:
        o_ref[:, n:] = jnp.zeros((o_ref.shape[0], o_ref.shape[1] - n), o_ref.dtype)


def _cast_call(w, layer, n_out=None, bm=256):
    _, k, n = w.shape
    n_out = n if n_out is None else n_out
    return pl.pallas_call(
        _cast_kernel,
        grid=(k // bm,),
        in_specs=[pl.BlockSpec((None, bm, n), lambda i: (layer, i, 0))],
        out_specs=pl.BlockSpec((bm, n_out), lambda i: (i, 0)),
        out_shape=jax.ShapeDtypeStruct((k, n_out), BF16),
        compiler_params=pltpu.CompilerParams(dimension_semantics=("arbitrary",)),
        name="cast",
    )(w)


def _mod_kernel(c_ref, w_ref, b_ref, o_ref):
    s = jax.nn.silu(c_ref[...]).astype(BF16)
    o_ref[...] = _dot(s, w_ref[...].astype(BF16)) + b_ref[...]


def _mod_call(c, w_mod, b_mod):
    bsz = c.shape[0]
    rows = 8
    cp = jnp.pad(c, ((0, rows - bsz), (0, 0)))
    n = w_mod.shape[1]
    bn = 1024
    out = pl.pallas_call(
        _mod_kernel,
        grid=(n // bn,),
        in_specs=[
            pl.BlockSpec((rows, D_MODEL), lambda i: (0, 0)),
            pl.BlockSpec((D_MODEL, bn), lambda i: (0, i)),
            pl.BlockSpec((1, bn), lambda i: (0, i)),
        ],
        out_specs=pl.BlockSpec((rows, bn), lambda i: (0, i)),
        out_shape=jax.ShapeDtypeStruct((rows, n), F32),
        compiler_params=pltpu.CompilerParams(dimension_semantics=("arbitrary",)),
        name="mod",
    )(cp, w_mod, b_mod.reshape(1, n))
    return out[:bsz].reshape(bsz, N_MOD, D_MODEL)


def _inproj_kernel(x_ref, mod_ref, g_ref, w_ref, cw_ref, cb_ref,
                   qkva_ref, qc_ref, kc_ref, vm_ref, om_ref, gat_ref, raw_scr, carry_scr, *, nblk):
    j = pl.program_id(0)

    @pl.when(j == 0)
    def _init():
        raw_scr[...] = jnp.zeros_like(raw_scr)
        carry_scr[...] = jnp.zeros_like(carry_scr)

    x = x_ref[...]
    h = _rms(x, g_ref[...]) * (1.0 + mod_ref[1:2, :]) + mod_ref[0:1, :]
    hb = h.astype(BF16)
    raw = _dot(hb, w_ref[:, C_QKM:C_VM])
    tm = raw.shape[0]
    nchunk = tm // CHUNK
    within = (j + nblk - 1) % nblk

    def project():
        qa = _dot(hb, w_ref[:, C_QA:C_KA]) * (HEAD_DIM_ATT ** -0.5)
        qkva_ref[:, 0:ATT_Q] = qa.astype(BF16)
        yield
        ka = _dot(hb, w_ref[:, C_KA:C_VA])
        va = _dot(hb, w_ref[:, C_VA:C_QKM])
        half = HEAD_DIM_ATT
        qkva_ref[:, ATT_Q:ATT_Q + ATT_KV] = ka.astype(BF16)
        qkva_ref[:, ATT_Q + ATT_KV:ATT_Q + 2 * ATT_KV] = pltpu.roll(ka, half, axis=1).astype(BF16)
        qkva_ref[:, ATT_Q + 2 * ATT_KV:ATT_Q + 3 * ATT_KV] = va.astype(BF16)
        qkva_ref[:, ATT_Q + 3 * ATT_KV:ATT_Q + 4 * ATT_KV] = pltpu.roll(va, half, axis=1).astype(BF16)
        gat_ref[...] = _dot(hb, w_ref[:, C_G:IN_COLS_PAD])
        yield
        vm_ref[...] = _dot(hb, w_ref[:, C_VM:C_OM]).astype(BF16)
        yield
        om_ref[...] = _dot(hb, w_ref[:, C_OM:C_G])
        yield

    def conv():
        L = CHUNK
        row = lax.broadcasted_iota(jnp.int32, (L, 1), 0)
        for g in range(nchunk):
            xg = raw_scr[g * L:(g + 1) * L, :]
            if g == 0:
                prev_row = jnp.where(within == 0, 0.0, carry_scr[0:1, :])
            else:
                prev_row = raw_scr[g * L - 1:g * L, :]
            if g == nchunk - 1:
                next_row = jnp.where(within == nblk - 1, 0.0, raw[0:1, :])
            else:
                next_row = raw_scr[(g + 1) * L:(g + 1) * L + 1, :]
            xm1 = jnp.where(row == 0, prev_row, pltpu.roll(xg, 1, axis=0))
            xp1 = jnp.where(row == L - 1, next_row, pltpu.roll(xg, L - 1, axis=0))
            y = xm1 * cw_ref[0:1, :] + xg * cw_ref[1:2, :] + xp1 * cw_ref[2:3, :] + cb_ref[...]
            y = jax.nn.silu(y)
            qc_ref[g * L:(g + 1) * L, :] = (y[:, :M_W] * (HEAD_DIM_M ** -0.5)).astype(BF16)
            kc_ref[g * L:(g + 1) * L, :] = y[:, M_W:]
            yield

    _alternate(project(), conv())
    carry_scr[0:1, :] = raw_scr[tm - 1:tm, :]
    raw_scr[...] = raw


def _inproj_call(x, mod, g1, w_in_p, conv_w, conv_b, tm):
    bsz, s, d = x.shape
    nblk = s // tm
    ntiles = bsz * nblk
    outs = (
        jax.ShapeDtypeStruct((bsz, s, QKVA_W), BF16),
        jax.ShapeDtypeStruct((bsz, s, M_W), BF16),
        jax.ShapeDtypeStruct((bsz, s, M_W), F32),
        jax.ShapeDtypeStruct((bsz, s, M_W), BF16),
        jax.ShapeDtypeStruct((bsz, s, M_W), F32),
        jax.ShapeDtypeStruct((bsz, s, GATE_PAD), F32),
    )

    def cur(j):
        t = jnp.minimum(j, ntiles - 1)
        return t // nblk, t % nblk

    def old(j):
        t = jnp.maximum(j - 1, 0)
        return t // nblk, t % nblk

    cur_tile = lambda w: pl.BlockSpec((None, tm, w), lambda j: (*cur(j), 0))
    old_tile = lambda w: pl.BlockSpec((None, tm, w), lambda j: (*old(j), 0))
    const = lambda shp: pl.BlockSpec(shp, lambda j: (0, 0))
    return pl.pallas_call(
        functools.partial(_inproj_kernel, nblk=nblk),
        grid=(ntiles + 1,),
        in_specs=[
            cur_tile(d),
            pl.BlockSpec((None, N_MOD, d), lambda j: (cur(j)[0], 0, 0)),
            const((1, d)),
            pl.BlockSpec((d, IN_COLS_PAD), lambda j: (0, 0), pipeline_mode=pl.Buffered(1)),
            const((3, 2 * M_W)), const((1, 2 * M_W)),
        ],
        out_specs=[cur_tile(QKVA_W), old_tile(M_W), old_tile(M_W), cur_tile(M_W), cur_tile(M_W),
                   cur_tile(GATE_PAD)],
        out_shape=outs,
        scratch_shapes=[pltpu.VMEM((tm, 2 * M_W), F32), pltpu.VMEM((8, 2 * M_W), F32)],
        compiler_params=pltpu.CompilerParams(
            dimension_semantics=("arbitrary",), vmem_limit_bytes=VMEM_LIMIT),
        name="inproj",
    )(x, mod, g1, w_in_p, conv_w, conv_b)


def _mlstm_reset(first, c_scr, m_scr):
    @pl.when(first)
    def _init():
        c_scr[...] = jnp.zeros_like(c_scr)
        m_scr[...] = jnp.zeros_like(m_scr)


def _mlstm_stream(q_ref, k_ref, v_ref, gat_ref, bg_ref, h_ref, c_scr, m_scr, *, reverse, nchunk):
    L = CHUNK
    ti = lax.broadcasted_iota(jnp.int32, (L, L), 0)
    si = lax.broadcasted_iota(jnp.int32, (L, L), 1)
    causal = (si >= ti) if reverse else (si <= ti)
    tri = jnp.where(causal, 1.0, 0.0).astype(BF16)
    lane = lax.broadcasted_iota(jnp.int32, (1, L), 1)
    last = 0 if reverse else L - 1
    i_off, f_off = (2 * N_HEADS_M, 3 * N_HEADS_M) if reverse else (0, N_HEADS_M)
    ones = jnp.ones((L, HEAD_DIM_M), BF16)
    order = list(range(nchunk - 1, -1, -1) if reverse else range(nchunk))
    rows_of = lambda g: slice(g * L, (g + 1) * L)

    gates = {}
    for g in order:
        gc = gat_ref[rows_of(g), :] + bg_ref[...]
        logf = jax.nn.log_sigmoid(gc)
        hi = logf.astype(BF16)
        r1 = logf - hi.astype(F32)
        mid = r1.astype(BF16)
        lo = (r1 - mid.astype(F32)).astype(BF16)
        gates[g] = (gc, _dot(tri, hi) + _dot(tri, mid) + _dot(tri, lo))
        yield

    for g in order:
        gc, bcum = gates[g]
        gates[g] = (gc.T, bcum.T, bcum)
        yield

    def head(g, h):
        rows = rows_of(g)
        cols = slice(h * HEAD_DIM_M, (h + 1) * HEAD_DIM_M)
        gct, bct, bcum = gates[g]
        r_row = gct[i_off + h:i_off + h + 1, :] - bct[f_off + h:f_off + h + 1, :]
        btot = jnp.sum(jnp.where(lane == last, bct[f_off + h:f_off + h + 1, :], 0.0),
                       axis=1, keepdims=True)
        rmax = jnp.max(r_row, axis=1, keepdims=True)
        a_max = btot + rmax
        m_prev = m_scr[h:h + 1, 0:1]
        m_new = jnp.maximum(btot + m_prev, a_max)
        m_scr[h:h + 1, :] = jnp.broadcast_to(m_new, (1, LANES))
        decay = jnp.exp(btot + m_prev - m_new)
        w_row = jnp.exp(r_row - rmax) * jnp.exp(a_max - m_new)
        kt = k_ref[rows, cols].T
        rm = jnp.where(causal, r_row, -jnp.inf)
        cm = jnp.max(rm, axis=1, keepdims=True)
        yield
        q = q_ref[rows, cols]
        vaug = jnp.concatenate([v_ref[rows, cols], ones], axis=1)
        s = _dot(q, kt.astype(BF16))
        upd = _dot((kt * w_row).astype(BF16), vaug)
        u = jnp.maximum(cm, m_prev)
        dm = jnp.exp(rm - u)
        qi = (q.astype(F32) * jnp.exp(m_prev - u)).astype(BF16)
        yield
        cprev = c_scr[h]
        lhs = jnp.concatenate([(s * dm).astype(BF16), qi], axis=1)
        rhs = jnp.concatenate([vaug, cprev.astype(BF16)], axis=0)
        out = _dot(lhs, rhs)
        c_scr[h] = decay * cprev + upd
        yield
        floor = jnp.exp(-(bcum[:, f_off + h:f_off + h + 1] + u))
        h_ref[rows, cols] = out[:, :HEAD_DIM_M] / jnp.maximum(jnp.abs(out[:, HEAD_DIM_M:]), floor)
        yield

    n_stage, lag = 4, 2
    heads = {}
    for slot in range(lag * (nchunk - 1) + n_stage):
        for i, g in enumerate(order):
            stage = slot - lag * i
            if 0 <= stage < n_stage:
                for h in range(N_HEADS_M):
                    if stage == 0:
                        heads[g, h] = head(g, h)
                    next(heads[g, h])
                    yield


def _mlstm_kernel(qf_ref, kf_ref, vf_ref, gf_ref, qb_ref, kb_ref, vb_ref, gb_ref, bg_ref,
                  hf_ref, hb_ref, cf_scr, mf_scr, cb_scr, mb_scr, *, nchunk):
    j = pl.program_id(1)
    _mlstm_reset(j == 0, cf_scr, mf_scr)
    _mlstm_reset(j == 0, cb_scr, mb_scr)
    fwd = _mlstm_stream(qf_ref, kf_ref, vf_ref, gf_ref, bg_ref, hf_ref, cf_scr, mf_scr,
                        reverse=False, nchunk=nchunk)
    bwd = _mlstm_stream(qb_ref, kb_ref, vb_ref, gb_ref, bg_ref, hb_ref, cb_scr, mb_scr,
                        reverse=True, nchunk=nchunk)
    _alternate(fwd, bwd)


def _mlstm_call(qc, kc, vm, gates, bg_pad, tm):
    bsz, s, _ = qc.shape
    nblk = s // tm

    def specs(pos):
        tile = lambda w: pl.BlockSpec((None, tm, w), lambda b, j: (b, pos(j), 0))
        return [tile(M_W), tile(M_W), tile(M_W), tile(GATE_PAD)], tile(M_W)

    in_f, out_f = specs(lambda j: j)
    in_b, out_b = specs(lambda j: nblk - 1 - j)
    state = [pltpu.VMEM((N_HEADS_M, HEAD_DIM_M, 2 * HEAD_DIM_M), F32), pltpu.VMEM((8, LANES), F32)]
    return pl.pallas_call(
        functools.partial(_mlstm_kernel, nchunk=tm // CHUNK),
        grid=(bsz, nblk),
        in_specs=in_f + in_b + [pl.BlockSpec((1, GATE_PAD), lambda b, j: (0, 0))],
        out_specs=[out_f, out_b],
        out_shape=[jax.ShapeDtypeStruct((bsz, s, M_W), F32)] * 2,
        scratch_shapes=state + state,
        compiler_params=pltpu.CompilerParams(
            dimension_semantics=("arbitrary", "arbitrary"), vmem_limit_bytes=VMEM_LIMIT),
        name="mlstm",
    )(qc, kc, vm, gates, qc, kc, vm, gates, bg_pad)


def _attn_kernel(sink_ref, q_ref, kvp_ref, kvn_ref, g_ref, o_ref, bias_scr, *, nt, nsub):
    b = pl.program_id(0)
    j = pl.program_id(1)
    nk = 3 * BLOCK

    @pl.when((b == 0) & (j == 0))
    def _init():
        row = lax.broadcasted_iota(jnp.int32, (BLOCK, nk), 0)
        col = lax.broadcasted_iota(jnp.int32, (BLOCK, nk), 1)
        dist = jnp.abs(col - BLOCK - row)
        distf = dist.astype(F32)
        for var in range(3):
            ok = dist <= WINDOW
            if var == 1:
                ok = ok & (col >= BLOCK)
            elif var == 2:
                ok = ok & (col < 2 * BLOCK)
            for h in range(N_HEADS_ATT):
                slope = 2.0 ** (-8.0 * (h + 1.0) / N_HEADS_ATT)
                bias_scr[var * N_HEADS_ATT + h] = jnp.where(ok, -slope * distf, -jnp.inf)

    lane_k = lax.broadcasted_iota(jnp.int32, (nk, LANES), 1)
    ones_a = jnp.where(lane_k < HEAD_DIM_ATT, 1.0, 0.0).astype(BF16)
    ones_b = jnp.where(lane_k < HEAD_DIM_ATT, 0.0, 1.0).astype(BF16)
    lo_half_q = lax.broadcasted_iota(jnp.int32, (BLOCK, LANES), 1) < HEAD_DIM_ATT

    def kv_block(idx):
        if idx < 0:
            return kvp_ref[...]
        if idx >= nsub:
            return kvn_ref[...]
        return q_ref[idx * BLOCK:(idx + 1) * BLOCK, ATT_Q:QKVA_W]

    for n in range(nsub):
        rows = slice(n * BLOCK, (n + 1) * BLOCK)
        kv = jnp.concatenate([kv_block(n - 1), kv_block(n), kv_block(n + 1)], axis=0)
        if n == 0:
            var = jnp.where(j == 0, 1, 0)
        elif n == nsub - 1:
            var = jnp.where(j == nt - 1, 2, 0)
        else:
            var = 0
        pieces = []
        for kvh in range(N_KV_HEADS):
            k_st, k_sw = kv[:, 0:LANES], kv[:, LANES:2 * LANES]
            v_st, v_sw = kv[:, 2 * LANES:3 * LANES], kv[:, 3 * LANES:4 * LANES]
            if kvh == 0:
                k_lo, k_hi, v_lo, v_hi = k_st, k_sw, v_st, v_sw
            else:
                k_lo, k_hi, v_lo, v_hi = k_sw, k_st, v_sw, v_st
            kk = jnp.concatenate([k_lo * ones_a, k_hi * ones_b], axis=0)
            vv = jnp.concatenate([
                jnp.concatenate([v_lo * ones_a, ones_a], axis=1),
                jnp.concatenate([v_hi * ones_b, ones_b], axis=1)], axis=0)
            for pair in range(GROUP_SIZE // 2):
                h0 = kvh * GROUP_SIZE + 2 * pair
                qp = q_ref[rows, h0 * HEAD_DIM_ATT:(h0 + 2) * HEAD_DIM_ATT]
                s2 = _dot_nt(qp, kk)
                ps, es = [], []
                for t in range(2):
                    logits = s2[:, t * nk:(t + 1) * nk] + bias_scr[var * N_HEADS_ATT + h0 + t]
                    sink = sink_ref[h0 + t]
                    mx = jnp.maximum(jnp.max(logits, axis=-1, keepdims=True), sink)
                    ps.append(jnp.exp(logits - mx).astype(BF16))
                    es.append(jnp.exp(sink - mx))
                res = _dot(jnp.concatenate(ps, axis=1), vv)
                den = res[:, LANES:] + jnp.where(lo_half_q, es[0], es[1])
                pieces.append(res[:, :LANES] / den)
        att = jnp.concatenate(pieces, axis=1)
        o_ref[rows, :] = _rms(att, g_ref[...]).astype(o_ref.dtype)


def _attn_call(sink, qkva, g_attn, tq):
    bsz, s, _ = qkva.shape
    nt = s // tq
    nsub = tq // BLOCK
    nb = s // BLOCK
    kvw = QKVA_W - ATT_Q
    kern = functools.partial(_attn_kernel, nt=nt, nsub=nsub)
    return pl.pallas_call(
        kern,
        grid=(bsz, nt),
        in_specs=[
            pl.BlockSpec(memory_space=pltpu.SMEM),
            pl.BlockSpec((None, tq, QKVA_W), lambda b, j: (b, j, 0)),
            pl.BlockSpec((None, BLOCK, kvw), lambda b, j: (b, jnp.maximum(j * nsub - 1, 0), 1)),
            pl.BlockSpec((None, BLOCK, kvw), lambda b, j: (b, jnp.minimum((j + 1) * nsub, nb - 1), 1)),
            pl.BlockSpec((1, ATT_Q), lambda b, j: (0, 0)),
        ],
        out_specs=pl.BlockSpec((None, tq, ATT_Q), lambda b, j: (b, j, 0)),
        out_shape=jax.ShapeDtypeStruct((bsz, s, ATT_Q), BF16),
        scratch_shapes=[pltpu.VMEM((3 * N_HEADS_ATT, BLOCK, 3 * BLOCK), F32)],
        compiler_params=pltpu.CompilerParams(
            dimension_semantics=("arbitrary", "arbitrary"), vmem_limit_bytes=VMEM_LIMIT),
        name="attn",
    )(sink, qkva, qkva, qkva, g_attn)


def _outffn_kernel(x_ref, att_ref, hf_ref, hb_ref, om_ref, mod_ref, gm_ref, wo_ref, g2_ref,
                   w1_ref, w2_ref, gf_ref, o_ref, hid_scr, *, final):
    hs = hf_ref[...] + hb_ref[...]
    parts = []
    for h in range(N_HEADS_M):
        cols = slice(h * HEAD_DIM_M, (h + 1) * HEAD_DIM_M)
        parts.append(_rms(hs[:, cols], gm_ref[:, cols]))
    hm = jax.nn.sigmoid(om_ref[...]) * jnp.concatenate(parts, axis=1)
    mixin = jnp.concatenate([att_ref[...], hm.astype(BF16)], axis=1)
    x1 = x_ref[...] + mod_ref[2:3, :] * _dot(mixin, wo_ref[...])

    hff = (_rms(x1, g2_ref[...]) * (1.0 + mod_ref[4:5, :]) + mod_ref[3:4, :]).astype(BF16)
    for c in range(N_FF_CHUNKS):
        gate = _dot(hff, w1_ref[:, FF_CHUNK * c:FF_CHUNK * (c + 1)])
        up = _dot(hff, w1_ref[:, D_FF + FF_CHUNK * c:D_FF + FF_CHUNK * (c + 1)])
        hid_scr[:, FF_CHUNK * c:FF_CHUNK * (c + 1)] = (jax.nn.silu(gate) * up).astype(BF16)
    x2 = x1 + mod_ref[5:6, :] * _dot(hid_scr[...], w2_ref[...])
    if final:
        x2 = _rms(x2, gf_ref[...])
    o_ref[...] = x2


def _outffn_call(x, att, hf, hb, om, mod, g_m, w_out, g2, w1, w2, g_final, tm, final):
    bsz, s, d = x.shape
    tile = lambda w: pl.BlockSpec((None, tm, w), lambda b, i: (b, i, 0))
    const = lambda shp: pl.BlockSpec(shp, lambda b, i: (0, 0))
    weight = lambda shp: pl.BlockSpec(shp, lambda b, i: (0, 0), pipeline_mode=pl.Buffered(1))
    return pl.pallas_call(
        functools.partial(_outffn_kernel, final=final),
        grid=(bsz, s // tm),
        in_specs=[
            tile(d), tile(ATT_Q), tile(M_W), tile(M_W), tile(M_W),
            pl.BlockSpec((None, N_MOD, d), lambda b, i: (b, 0, 0)),
            const((1, M_W)), weight((ATT_Q + M_W, d)), const((1, d)),
            weight((d, 2 * D_FF)), weight((D_FF, d)), const((1, d)),
        ],
        out_specs=tile(d),
        out_shape=jax.ShapeDtypeStruct((bsz, s, d), F32),
        scratch_shapes=[pltpu.VMEM((tm, D_FF), BF16)],
        compiler_params=pltpu.CompilerParams(
            dimension_semantics=("arbitrary", "arbitrary"), vmem_limit_bytes=VMEM_LIMIT),
        name="outffn",
    )(x, att, hf, hb, om, mod, g_m, w_out, g2, w1, w2, g_final)


def _layer(l, x, c, w_mod, b_mod, g_norm1, w_in, conv_w, conv_b, b_gates, sink,
           g_attn_out, g_mlstm_out, w_out, g_norm2, w_ffn_in, w_ffn_out, g_final, final):
    d = x.shape[-1]
    mod = _mod_call(c, w_mod[l], b_mod[l])

    w_in_p = _cast_call(w_in, l, IN_COLS_PAD)
    qkva, qc, kc, vm, om, gates = _inproj_call(x, mod, g_norm1[l].reshape(1, d), w_in_p, conv_w[l],
                                               conv_b[l].reshape(1, 2 * M_W), tm=512)

    bg_pad = jnp.pad(b_gates[l], (0, GATE_PAD - N_GATES)).reshape(1, GATE_PAD)
    hf, hb = _mlstm_call(qc, kc, vm, gates, bg_pad, tm=512)

    att = _attn_call(sink[l], qkva, g_attn_out[l].reshape(1, ATT_Q), tq=512)

    return _outffn_call(x, att, hf, hb, om, mod, g_mlstm_out[l].reshape(1, M_W), _cast_call(w_out, l),
                        g_norm2[l].reshape(1, d), _cast_call(w_ffn_in, l), _cast_call(w_ffn_out, l),
                        g_final.reshape(1, d), tm=512, final=final)


def kernel(x, c, w_mod, b_mod, g_norm1, w_in, conv_w, conv_b, b_gates, sink, g_attn_out,
           g_mlstm_out, w_out, g_norm2, w_ffn_in, w_ffn_out, g_final):
    depth = w_mod.shape[0]
    for l in range(depth):
        x = _layer(l, x, c, w_mod, b_mod, g_norm1, w_in, conv_w, conv_b, b_gates, sink, g_attn_out,
                   g_mlstm_out, w_out, g_norm2, w_ffn_in, w_ffn_out, g_final, final=(l == depth - 1))
    return x
```

```python
import functools

import jax
import jax.numpy as jnp
from jax import lax
from jax.experimental import pallas as pl
from jax.experimental.pallas import tpu as pltpu

F32 = jnp.float32
BF16 = jnp.bfloat16

D_MODEL = 1024
EPS = 1e-6
N_HEADS_ATT = 8
N_KV_HEADS = 2
HEAD_DIM_ATT = 64
GROUP_SIZE = N_HEADS_ATT // N_KV_HEADS
WINDOW = 128
BLOCK = 128
N_HEADS_M = 4
HEAD_DIM_M = 128
CHUNK = 128
ATT_Q = N_HEADS_ATT * HEAD_DIM_ATT
ATT_KV = N_KV_HEADS * HEAD_DIM_ATT
M_W = N_HEADS_M * HEAD_DIM_M
N_GATES = 4 * N_HEADS_M
D_FF = 2816
N_MOD = 6

LANES = 128
GATE_PAD = LANES
FF_CHUNK = 256
N_FF_CHUNKS = D_FF // FF_CHUNK
OUT_COL_BLOCK = 256
VMEM_LIMIT = 56 * 1024 * 1024

C_QA = 0
C_KA = ATT_Q
C_VA = ATT_Q + ATT_KV
C_QKM = ATT_Q + 2 * ATT_KV
C_VM = C_QKM + 2 * M_W
C_OM = C_VM + M_W
C_G = C_OM + M_W
IN_COLS_PAD = C_G + GATE_PAD
QKVA_W = ATT_Q + 4 * ATT_KV


def _dot(a, b):
    return jnp.dot(a, b, preferred_element_type=F32)


def _dot_nt(a, b):
    return lax.dot_general(a, b, (((1,), (1,)), ((), ())), preferred_element_type=F32)


def _rms(x, g):
    return x * lax.rsqrt(jnp.mean(x * x, axis=-1, keepdims=True) + EPS) * g


def _alternate(*streams):
    live = list(streams)
    while live:
        for s in list(live):
            if next(s, StopIteration) is StopIteration:
                live.remove(s)


def _cast_kernel(w_ref, o_ref):
    n = w_ref.shape[1]
    o_ref[:, :n] = w_ref[...].astype(o_ref.dtype)
    if o_ref.shape[1] > n:
        o_ref[:, n:] = jnp.zeros((o_ref.shape[0], o_ref.shape[1] - n), o_ref.dtype)


def _cast_call(w, layer, n_out=None, bm=256):
    _, k, n = w.shape
    n_out = n if n_out is None else n_out
    return pl.pallas_call(
        _cast_kernel,
        grid=(k // bm,),
        in_specs=[pl.BlockSpec((None, bm, n), lambda i: (layer, i, 0))],
        out_specs=pl.BlockSpec((bm, n_out), lambda i: (i, 0)),
        out_shape=jax.ShapeDtypeStruct((k, n_out), BF16),
        compiler_params=pltpu.CompilerParams(dimension_semantics=("arbitrary",)),
        name="cast",
    )(w)


def _mod_kernel(c_ref, w_ref, b_ref, o_ref):
    s = jax.nn.silu(c_ref[...]).astype(BF16)
    o_ref[...] = _dot(s, w_ref[...].astype(BF16)) + b_ref[...]


def _mod_call(c, w_mod, b_mod):
    bsz = c.shape[0]
    rows = 8
    cp = jnp.pad(c, ((0, rows - bsz), (0, 0)))
    n = w_mod.shape[1]
    bn = 1024
    out = pl.pallas_call(
        _mod_kernel,
        grid=(n // bn,),
        in_specs=[
            pl.BlockSpec((rows, D_MODEL), lambda i: (0, 0)),
            pl.BlockSpec((D_MODEL, bn), lambda i: (0, i)),
            pl.BlockSpec((1, bn), lambda i: (0, i)),
        ],
        out_specs=pl.BlockSpec((rows, bn), lambda i: (0, i)),
        out_shape=jax.ShapeDtypeStruct((rows, n), F32),
        compiler_params=pltpu.CompilerParams(dimension_semantics=("arbitrary",)),
        name="mod",
    )(cp, w_mod, b_mod.reshape(1, n))
    return out[:bsz].reshape(bsz, N_MOD, D_MODEL)


MXU_COLS = 256
CONV_ROWS = 64


def _inproj_kernel(x_ref, mod_ref, g_ref, w_ref, cw_ref, cb_ref,
                   qkva_ref, qc_ref, kct_ref, vm_ref, om_ref, gat_ref, gt_ref,
                   raw_scr, new_scr, k_scr, carry_scr, *, nblk):
    j = pl.program_id(0)

    @pl.when(j == 0)
    def _init():
        raw_scr[...] = jnp.zeros_like(raw_scr)
        carry_scr[...] = jnp.zeros_like(carry_scr)

    x = x_ref[...]
    h = _rms(x, g_ref[...]) * (1.0 + mod_ref[1:2, :]) + mod_ref[0:1, :]
    hb = h.astype(BF16)
    tm = x.shape[0]
    nchunk = tm // CHUNK
    within = (j + nblk - 1) % nblk

    def project():
        order = list(range(C_QKM, C_VM, MXU_COLS)) + list(range(0, C_QKM, MXU_COLS)) + \
            list(range(C_VM, IN_COLS_PAD, MXU_COLS))
        for c0 in order:
            c1 = min(c0 + MXU_COLS, IN_COLS_PAD)
            res = _dot(hb, w_ref[:, c0:c1])
            if c0 < C_KA:
                qkva_ref[:, c0:c1] = (res * (HEAD_DIM_ATT ** -0.5)).astype(BF16)
            elif c0 < C_QKM:
                ka, va = res[:, :ATT_KV], res[:, ATT_KV:]
                half = HEAD_DIM_ATT
                qkva_ref[:, ATT_Q:ATT_Q + ATT_KV] = ka.astype(BF16)
                qkva_ref[:, ATT_Q + ATT_KV:ATT_Q + 2 * ATT_KV] = pltpu.roll(ka, half, axis=1).astype(BF16)
                qkva_ref[:, ATT_Q + 2 * ATT_KV:ATT_Q + 3 * ATT_KV] = va.astype(BF16)
                qkva_ref[:, ATT_Q + 3 * ATT_KV:ATT_Q + 4 * ATT_KV] = pltpu.roll(va, half, axis=1).astype(BF16)
            elif c0 < C_VM:
                new_scr[:, c0 - C_QKM:c1 - C_QKM] = res
            elif c0 < C_OM:
                vm_ref[:, c0 - C_VM:c1 - C_VM] = res.astype(BF16)
            elif c0 < C_G:
                om_ref[:, c0 - C_OM:c1 - C_OM] = res
            else:
                gat_ref[...] = res
                for g in range(nchunk):
                    blk = res[g * CHUNK:(g + 1) * CHUNK, :].T
                    gt_ref[:, g * CHUNK:(g + 1) * CHUNK] = blk[0:N_GATES, :]
            yield

    def conv():
        R = CONV_ROWS
        row = lax.broadcasted_iota(jnp.int32, (R, 1), 0)
        for p in range(tm // R):
            r0 = p * R
            xg = raw_scr[r0:r0 + R, :]
            if p == 0:
                prev_row = jnp.where(within == 0, 0.0, carry_scr[0:1, :])
            else:
                prev_row = raw_scr[r0 - 1:r0, :]
            if r0 + R == tm:
                next_row = jnp.where(within == nblk - 1, 0.0, new_scr[0:1, :])
            else:
                next_row = raw_scr[r0 + R:r0 + R + 1, :]
            xm1 = jnp.where(row == 0, prev_row, pltpu.roll(xg, 1, axis=0))
            xp1 = jnp.where(row == R - 1, next_row, pltpu.roll(xg, R - 1, axis=0))
            y = xm1 * cw_ref[0:1, :] + xg * cw_ref[1:2, :] + xp1 * cw_ref[2:3, :] + cb_ref[...]
            y = jax.nn.silu(y)
            qc_ref[r0:r0 + R, :] = (y[:, :M_W] * (HEAD_DIM_M ** -0.5)).astype(BF16)
            k_scr[r0:r0 + R, :] = y[:, M_W:]
            yield
            if (r0 + R) % CHUNK == 0:
                g = r0 // CHUNK
                for hd in range(N_HEADS_M):
                    blk = k_scr[g * CHUNK:(g + 1) * CHUNK, hd * HEAD_DIM_M:(hd + 1) * HEAD_DIM_M]
                    base = (g * N_HEADS_M + hd) * HEAD_DIM_M
                    kct_ref[base:base + HEAD_DIM_M, :] = blk.T
                yield

    _alternate(project(), conv())
    carry_scr[0:1, :] = raw_scr[tm - 1:tm, :]
    raw_scr[...] = new_scr[...]


def _inproj_call(x, mod, g1, w_in_p, conv_w, conv_b, tm):
    bsz, s, d = x.shape
    nblk = s // tm
    ntiles = bsz * nblk
    outs = (
        jax.ShapeDtypeStruct((bsz, s, QKVA_W), BF16),
        jax.ShapeDtypeStruct((bsz, s, M_W), BF16),
        jax.ShapeDtypeStruct((bsz, s * N_HEADS_M, HEAD_DIM_M), F32),
        jax.ShapeDtypeStruct((bsz, s, M_W), BF16),
        jax.ShapeDtypeStruct((bsz, s, M_W), F32),
        jax.ShapeDtypeStruct((bsz, s, GATE_PAD), F32),
        jax.ShapeDtypeStruct((bsz, N_GATES, s), F32),
    )

    def cur(j):
        t = jnp.minimum(j, ntiles - 1)
        return t // nblk, t % nblk

    def old(j):
        t = jnp.maximum(j - 1, 0)
        return t // nblk, t % nblk

    cur_tile = lambda w: pl.BlockSpec((None, tm, w), lambda j: (*cur(j), 0))
    old_tile = lambda w: pl.BlockSpec((None, tm, w), lambda j: (*old(j), 0))
    const = lambda shp: pl.BlockSpec(shp, lambda j: (0, 0))
    return pl.pallas_call(
        functools.partial(_inproj_kernel, nblk=nblk),
        grid=(ntiles + 1,),
        in_specs=[
            cur_tile(d),
            pl.BlockSpec((None, N_MOD, d), lambda j: (cur(j)[0], 0, 0)),
            const((1, d)),
            pl.BlockSpec((d, IN_COLS_PAD), lambda j: (0, 0), pipeline_mode=pl.Buffered(1)),
            const((3, 2 * M_W)), const((1, 2 * M_W)),
        ],
        out_specs=[cur_tile(QKVA_W), old_tile(M_W),
                   pl.BlockSpec((None, tm * N_HEADS_M, HEAD_DIM_M), lambda j: (*old(j), 0)),
                   cur_tile(M_W), cur_tile(M_W), cur_tile(GATE_PAD),
                   pl.BlockSpec((None, N_GATES, tm), lambda j: (cur(j)[0], 0, cur(j)[1]))],
        out_shape=outs,
        scratch_shapes=[pltpu.VMEM((tm, 2 * M_W), F32), pltpu.VMEM((tm, 2 * M_W), F32),
                        pltpu.VMEM((tm, M_W), F32), pltpu.VMEM((8, 2 * M_W), F32)],
        compiler_params=pltpu.CompilerParams(
            dimension_semantics=("arbitrary",), vmem_limit_bytes=VMEM_LIMIT),
        name="inproj",
    )(x, mod, g1, w_in_p, conv_w, conv_b)


def _mlstm_reset(first, c_scr, m_scr):
    @pl.when(first)
    def _init():
        c_scr[...] = jnp.zeros_like(c_scr)
        m_scr[...] = jnp.zeros_like(m_scr)


def _mlstm_stream(q_ref, kt_ref, v_ref, gat_ref, gt_ref, bg_ref, bgr_ref, h_ref, c_scr, m_scr,
                  *, reverse, nchunk):
    L = CHUNK
    ti = lax.broadcasted_iota(jnp.int32, (L, L), 0)
    si = lax.broadcasted_iota(jnp.int32, (L, L), 1)
    causal = (si >= ti) if reverse else (si <= ti)
    tri = jnp.where(causal, 1.0, 0.0).astype(BF16)
    tri_t = jnp.where((ti >= si) if reverse else (ti <= si), 1.0, 0.0).astype(BF16)
    lane = lax.broadcasted_iota(jnp.int32, (1, L), 1)
    last = 0 if reverse else L - 1
    i_off, f_off = (2 * N_HEADS_M, 3 * N_HEADS_M) if reverse else (0, N_HEADS_M)
    ones = jnp.ones((L, HEAD_DIM_M), BF16)
    order = list(range(nchunk - 1, -1, -1) if reverse else range(nchunk))
    rows_of = lambda g: slice(g * L, (g + 1) * L)

    def split3(a):
        hi = a.astype(BF16)
        r1 = a - hi.astype(F32)
        mid = r1.astype(BF16)
        return hi, mid, (r1 - mid.astype(F32)).astype(BF16)

    gates = {}
    for g in order:
        hi, mid, lo = split3(jax.nn.log_sigmoid(gat_ref[rows_of(g), :] + bg_ref[...]))
        bcum = _dot(tri, hi) + _dot(tri, mid) + _dot(tri, lo)
        gr = gt_ref[:, rows_of(g)] + bgr_ref[...]
        parts = _dot(jnp.concatenate(split3(jax.nn.log_sigmoid(gr)), axis=0), tri_t)
        bcr = parts[0:N_GATES] + parts[N_GATES:2 * N_GATES] + parts[2 * N_GATES:3 * N_GATES]
        gates[g] = (gr, bcr, bcum)
        yield

    def head(g, h):
        rows = rows_of(g)
        cols = slice(h * HEAD_DIM_M, (h + 1) * HEAD_DIM_M)
        gct, bct, bcum = gates[g]
        r_row = gct[i_off + h:i_off + h + 1, :] - bct[f_off + h:f_off + h + 1, :]
        btot = jnp.sum(jnp.where(lane == last, bct[f_off + h:f_off + h + 1, :], 0.0),
                       axis=1, keepdims=True)
        rmax = jnp.max(r_row, axis=1, keepdims=True)
        a_max = btot + rmax
        m_prev = m_scr[h:h + 1, 0:1]
        m_new = jnp.maximum(btot + m_prev, a_max)
        m_scr[h:h + 1, :] = jnp.broadcast_to(m_new, (1, LANES))
        decay = jnp.exp(btot + m_prev - m_new)
        w_row = jnp.exp(r_row - rmax) * jnp.exp(a_max - m_new)
        kt = kt_ref[(g * N_HEADS_M + h) * HEAD_DIM_M:(g * N_HEADS_M + h + 1) * HEAD_DIM_M, :]
        rm = jnp.where(causal, r_row, -jnp.inf)
        cm = jnp.max(rm, axis=1, keepdims=True)
        yield
        q = q_ref[rows, cols]
        vaug = jnp.concatenate([v_ref[rows, cols], ones], axis=1)
        s = _dot(q, kt.astype(BF16))
        upd = _dot((kt * w_row).astype(BF16), vaug)
        u = jnp.maximum(cm, m_prev)
        ub = jnp.broadcast_to(u, (L, L))
        dm = jnp.exp(rm - ub)
        qi = (q.astype(F32) * jnp.exp(m_prev - ub)).astype(BF16)
        yield
        cprev = c_scr[h]
        lhs = jnp.concatenate([(s * dm).astype(BF16), qi], axis=1)
        rhs = jnp.concatenate([vaug, cprev.astype(BF16)], axis=0)
        out = _dot(lhs, rhs)
        c_scr[h] = decay * cprev + upd
        yield
        floor = jnp.exp(-(bcum[:, f_off + h:f_off + h + 1] + u))
        h_ref[rows, cols] = out[:, :HEAD_DIM_M] / jnp.maximum(jnp.abs(out[:, HEAD_DIM_M:]), floor)
        yield

    n_stage, lag = 4, 2
    heads = {}
    for slot in range(lag * (nchunk - 1) + n_stage):
        for i, g in enumerate(order):
            stage = slot - lag * i
            if 0 <= stage < n_stage:
                for h in range(N_HEADS_M):
                    if stage == 0:
                        heads[g, h] = head(g, h)
                    next(heads[g, h])
                    yield


def _mlstm_kernel(qf_ref, kf_ref, vf_ref, gf_ref, gtf_ref, qb_ref, kb_ref, vb_ref, gb_ref, gtb_ref,
                  bg_ref, bgr_ref, hf_ref, hb_ref, cf_scr, mf_scr, cb_scr, mb_scr, *, nchunk):
    j = pl.program_id(1)
    _mlstm_reset(j == 0, cf_scr, mf_scr)
    _mlstm_reset(j == 0, cb_scr, mb_scr)
    fwd = _mlstm_stream(qf_ref, kf_ref, vf_ref, gf_ref, gtf_ref, bg_ref, bgr_ref, hf_ref, cf_scr, mf_scr,
                        reverse=False, nchunk=nchunk)
    bwd = _mlstm_stream(qb_ref, kb_ref, vb_ref, gb_ref, gtb_ref, bg_ref, bgr_ref, hb_ref, cb_scr, mb_scr,
                        reverse=True, nchunk=nchunk)
    _alternate(fwd, bwd)


def _mlstm_call(qc, kct, vm, gates, gates_t, bg_pad, bg_rows, tm):
    bsz, s, _ = qc.shape
    nblk = s // tm

    def specs(pos):
        tile = lambda w: pl.BlockSpec((None, tm, w), lambda b, j: (b, pos(j), 0))
        kt_spec = pl.BlockSpec((None, tm * N_HEADS_M, HEAD_DIM_M), lambda b, j: (b, pos(j), 0))
        gt_spec = pl.BlockSpec((None, N_GATES, tm), lambda b, j: (b, 0, pos(j)))
        return [tile(M_W), kt_spec, tile(M_W), tile(GATE_PAD), gt_spec], tile(M_W)

    in_f, out_f = specs(lambda j: j)
    in_b, out_b = specs(lambda j: nblk - 1 - j)
    state = [pltpu.VMEM((N_HEADS_M, HEAD_DIM_M, 2 * HEAD_DIM_M), F32), pltpu.VMEM((8, LANES), F32)]
    return pl.pallas_call(
        functools.partial(_mlstm_kernel, nchunk=tm // CHUNK),
        grid=(bsz, nblk),
        in_specs=in_f + in_b + [pl.BlockSpec((1, GATE_PAD), lambda b, j: (0, 0)),
                                pl.BlockSpec((N_GATES, LANES), lambda b, j: (0, 0))],
        out_specs=[out_f, out_b],
        out_shape=[jax.ShapeDtypeStruct((bsz, s, M_W), F32)] * 2,
        scratch_shapes=state + state,
        compiler_params=pltpu.CompilerParams(
            dimension_semantics=("arbitrary", "arbitrary"), vmem_limit_bytes=VMEM_LIMIT),
        name="mlstm",
    )(qc, kct, vm, gates, gates_t, qc, kct, vm, gates, gates_t, bg_pad, bg_rows)


def _attn_kernel(sink_ref, q_ref, kvp_ref, kvn_ref, g_ref, o_ref, bias_scr, *, nt, nsub):
    b = pl.program_id(0)
    j = pl.program_id(1)
    nk = 3 * BLOCK

    @pl.when((b == 0) & (j == 0))
    def _init():
        row = lax.broadcasted_iota(jnp.int32, (BLOCK, nk), 0)
        col = lax.broadcasted_iota(jnp.int32, (BLOCK, nk), 1)
        dist = jnp.abs(col - BLOCK - row)
        distf = dist.astype(F32)
        for var in range(3):
            ok = dist <= WINDOW
            if var == 1:
                ok = ok & (col >= BLOCK)
            elif var == 2:
                ok = ok & (col < 2 * BLOCK)
            for h in range(N_HEADS_ATT):
                slope = 2.0 ** (-8.0 * (h + 1.0) / N_HEADS_ATT)
                bias_scr[var * N_HEADS_ATT + h] = jnp.where(ok, -slope * distf, -jnp.inf)

    lane_k = lax.broadcasted_iota(jnp.int32, (nk, LANES), 1)
    ones_a = jnp.where(lane_k < HEAD_DIM_ATT, 1.0, 0.0).astype(BF16)
    ones_b = jnp.where(lane_k < HEAD_DIM_ATT, 0.0, 1.0).astype(BF16)
    lo_half_q = lax.broadcasted_iota(jnp.int32, (BLOCK, LANES), 1) < HEAD_DIM_ATT

    def kv_block(idx):
        if idx < 0:
            return kvp_ref[...]
        if idx >= nsub:
            return kvn_ref[...]
        return q_ref[idx * BLOCK:(idx + 1) * BLOCK, ATT_Q:QKVA_W]

    for n in range(nsub):
        rows = slice(n * BLOCK, (n + 1) * BLOCK)
        kv = jnp.concatenate([kv_block(n - 1), kv_block(n), kv_block(n + 1)], axis=0)
        if n == 0:
            var = jnp.where(j == 0, 1, 0)
        elif n == nsub - 1:
            var = jnp.where(j == nt - 1, 2, 0)
        else:
            var = 0
        pieces = []
        for kvh in range(N_KV_HEADS):
            k_st, k_sw = kv[:, 0:LANES], kv[:, LANES:2 * LANES]
            v_st, v_sw = kv[:, 2 * LANES:3 * LANES], kv[:, 3 * LANES:4 * LANES]
            if kvh == 0:
                k_lo, k_hi, v_lo, v_hi = k_st, k_sw, v_st, v_sw
            else:
                k_lo, k_hi, v_lo, v_hi = k_sw, k_st, v_sw, v_st
            kk = jnp.concatenate([k_lo * ones_a, k_hi * ones_b], axis=0)
            vv = jnp.concatenate([
                jnp.concatenate([v_lo * ones_a, ones_a], axis=1),
                jnp.concatenate([v_hi * ones_b, ones_b], axis=1)], axis=0)
            for pair in range(GROUP_SIZE // 2):
                h0 = kvh * GROUP_SIZE + 2 * pair
                qp = q_ref[rows, h0 * HEAD_DIM_ATT:(h0 + 2) * HEAD_DIM_ATT]
                s2 = _dot_nt(qp, kk)
                ps, es = [], []
                for t in range(2):
                    logits = s2[:, t * nk:(t + 1) * nk] + bias_scr[var * N_HEADS_ATT + h0 + t]
                    sink = sink_ref[h0 + t]
                    mx = jnp.maximum(jnp.max(logits, axis=-1, keepdims=True), sink)
                    ps.append(jnp.exp(logits - mx).astype(BF16))
                    es.append(jnp.exp(sink - mx))
                res = _dot(jnp.concatenate(ps, axis=1), vv)
                den = res[:, LANES:] + jnp.where(lo_half_q, es[0], es[1])
                pieces.append(res[:, :LANES] / den)
        att = jnp.concatenate(pieces, axis=1)
        o_ref[rows, :] = _rms(att, g_ref[...]).astype(o_ref.dtype)


def _attn_call(sink, qkva, g_attn, tq):
    bsz, s, _ = qkva.shape
    nt = s // tq
    nsub = tq // BLOCK
    nb = s // BLOCK
    kvw = QKVA_W - ATT_Q
    kern = functools.partial(_attn_kernel, nt=nt, nsub=nsub)
    return pl.pallas_call(
        kern,
        grid=(bsz, nt),
        in_specs=[
            pl.BlockSpec(memory_space=pltpu.SMEM),
            pl.BlockSpec((None, tq, QKVA_W), lambda b, j: (b, j, 0)),
            pl.BlockSpec((None, BLOCK, kvw), lambda b, j: (b, jnp.maximum(j * nsub - 1, 0), 1)),
            pl.BlockSpec((None, BLOCK, kvw), lambda b, j: (b, jnp.minimum((j + 1) * nsub, nb - 1), 1)),
            pl.BlockSpec((1, ATT_Q), lambda b, j: (0, 0)),
        ],
        out_specs=pl.BlockSpec((None, tq, ATT_Q), lambda b, j: (b, j, 0)),
        out_shape=jax.ShapeDtypeStruct((bsz, s, ATT_Q), BF16),
        scratch_shapes=[pltpu.VMEM((3 * N_HEADS_ATT, BLOCK, 3 * BLOCK), F32)],
        compiler_params=pltpu.CompilerParams(
            dimension_semantics=("arbitrary", "arbitrary"), vmem_limit_bytes=VMEM_LIMIT),
        name="attn",
    )(sink, qkva, qkva, qkva, g_attn)


def _outffn_kernel(x_ref, att_ref, hf_ref, hb_ref, om_ref, mod_ref, gm_ref, wo_ref, g2_ref,
                   w1_ref, w2_ref, gf_ref, o_ref, hid_scr, *, final):
    hs = hf_ref[...] + hb_ref[...]
    parts = []
    for h in range(N_HEADS_M):
        cols = slice(h * HEAD_DIM_M, (h + 1) * HEAD_DIM_M)
        parts.append(_rms(hs[:, cols], gm_ref[:, cols]))
    hm = jax.nn.sigmoid(om_ref[...]) * jnp.concatenate(parts, axis=1)
    mixin = jnp.concatenate([att_ref[...], hm.astype(BF16)], axis=1)
    x1 = x_ref[...] + mod_ref[2:3, :] * _dot(mixin, wo_ref[...])

    hff = (_rms(x1, g2_ref[...]) * (1.0 + mod_ref[4:5, :]) + mod_ref[3:4, :]).astype(BF16)
    for c in range(N_FF_CHUNKS):
        gate = _dot(hff, w1_ref[:, FF_CHUNK * c:FF_CHUNK * (c + 1)])
        up = _dot(hff, w1_ref[:, D_FF + FF_CHUNK * c:D_FF + FF_CHUNK * (c + 1)])
        hid_scr[:, FF_CHUNK * c:FF_CHUNK * (c + 1)] = (jax.nn.silu(gate) * up).astype(BF16)
    x2 = x1 + mod_ref[5:6, :] * _dot(hid_scr[...], w2_ref[...])
    if final:
        x2 = _rms(x2, gf_ref[...])
    o_ref[...] = x2


def _outffn_call(x, att, hf, hb, om, mod, g_m, w_out, g2, w1, w2, g_final, tm, final):
    bsz, s, d = x.shape
    tile = lambda w: pl.BlockSpec((None, tm, w), lambda b, i: (b, i, 0))
    const = lambda shp: pl.BlockSpec(shp, lambda b, i: (0, 0))
    weight = lambda shp: pl.BlockSpec(shp, lambda b, i: (0, 0), pipeline_mode=pl.Buffered(1))
    return pl.pallas_call(
        functools.partial(_outffn_kernel, final=final),
        grid=(bsz, s // tm),
        in_specs=[
            tile(d), tile(ATT_Q), tile(M_W), tile(M_W), tile(M_W),
            pl.BlockSpec((None, N_MOD, d), lambda b, i: (b, 0, 0)),
            const((1, M_W)), weight((ATT_Q + M_W, d)), const((1, d)),
            weight((d, 2 * D_FF)), weight((D_FF, d)), const((1, d)),
        ],
        out_specs=tile(d),
        out_shape=jax.ShapeDtypeStruct((bsz, s, d), F32),
        scratch_shapes=[pltpu.VMEM((tm, D_FF), BF16)],
        compiler_params=pltpu.CompilerParams(
            dimension_semantics=("arbitrary", "arbitrary"), vmem_limit_bytes=VMEM_LIMIT),
        name="outffn",
    )(x, att, hf, hb, om, mod, g_m, w_out, g2, w1, w2, g_final)


def _layer(l, x, c, w_mod, b_mod, g_norm1, w_in, conv_w, conv_b, b_gates, sink,
           g_attn_out, g_mlstm_out, w_out, g_norm2, w_ffn_in, w_ffn_out, g_final, final):
    d = x.shape[-1]
    mod = _mod_call(c, w_mod[l], b_mod[l])

    w_in_p = _cast_call(w_in, l, IN_COLS_PAD)
    qkva, qc, kct, vm, om, gates, gates_t = _inproj_call(
        x, mod, g_norm1[l].reshape(1, d), w_in_p, conv_w[l], conv_b[l].reshape(1, 2 * M_W), tm=512)

    bg_pad = jnp.pad(b_gates[l], (0, GATE_PAD - N_GATES)).reshape(1, GATE_PAD)
    bg_rows = jnp.broadcast_to(b_gates[l][:, None], (N_GATES, LANES))
    hf, hb = _mlstm_call(qc, kct, vm, gates, gates_t, bg_pad, bg_rows, tm=512)

    att = _attn_call(sink[l], qkva, g_attn_out[l].reshape(1, ATT_Q), tq=512)

    return _outffn_call(x, att, hf, hb, om, mod, g_mlstm_out[l].reshape(1, M_W), _cast_call(w_out, l),
                        g_norm2[l].reshape(1, d), _cast_call(w_ffn_in, l), _cast_call(w_ffn_out, l),
                        g_final.reshape(1, d), tm=512, final=final)


def kernel(x, c, w_mod, b_mod, g_norm1, w_in, conv_w, conv_b, b_gates, sink, g_attn_out,
           g_mlstm_out, w_out, g_norm2, w_ffn_in, w_ffn_out, g_final):
    depth = w_mod.shape[0]
    for l in range(depth):
        x = _layer(l, x, c, w_mod, b_mod, g_norm1, w_in, conv_w, conv_b, b_gates, sink, g_attn_out,
                   g_mlstm_out, w_out, g_norm2, w_ffn_in, w_ffn_out, g_final, final=(l == depth - 1))
    return x
```

```python
import functools

import jax
import jax.numpy as jnp
from jax import lax
from jax.experimental import pallas as pl
from jax.experimental.pallas import tpu as pltpu

F32 = jnp.float32
BF16 = jnp.bfloat16

D_MODEL = 1024
EPS = 1e-6
N_HEADS_ATT = 8
N_KV_HEADS = 2
HEAD_DIM_ATT = 64
GROUP_SIZE = N_HEADS_ATT // N_KV_HEADS
WINDOW = 128
BLOCK = 128
N_HEADS_M = 4
HEAD_DIM_M = 128
CHUNK = 128
ATT_Q = N_HEADS_ATT * HEAD_DIM_ATT
ATT_KV = N_KV_HEADS * HEAD_DIM_ATT
M_W = N_HEADS_M * HEAD_DIM_M
N_GATES = 4 * N_HEADS_M
D_FF = 2816
N_MOD = 6

LANES = 128
GATE_PAD = LANES
FF_CHUNK = 256
N_FF_CHUNKS = D_FF // FF_CHUNK
OUT_COL_BLOCK = 256
VMEM_LIMIT = 56 * 1024 * 1024

C_QA = 0
C_KA = ATT_Q
C_VA = ATT_Q + ATT_KV
C_QKM = ATT_Q + 2 * ATT_KV
C_VM = C_QKM + 2 * M_W
C_OM = C_VM + M_W
C_G = C_OM + M_W
IN_COLS_PAD = C_G + GATE_PAD
QKVA_W = ATT_Q + 4 * ATT_KV


def _dot(a, b):
    return jnp.dot(a, b, preferred_element_type=F32)


def _dot_nt(a, b):
    return lax.dot_general(a, b, (((1,), (1,)), ((), ())), preferred_element_type=F32)


def _rms(x, g):
    return x * lax.rsqrt(jnp.mean(x * x, axis=-1, keepdims=True) + EPS) * g


def _alternate(*streams):
    live = list(streams)
    while live:
        for s in list(live):
            if next(s, StopIteration) is StopIteration:
                live.remove(s)


def _cast_kernel(w_ref, o_ref):
    n = w_ref.shape[1]
    o_ref[:, :n] = w_ref[...].astype(o_ref.dtype)
    if o_ref.shape[1] > n:
        o_ref[:, n:] = jnp.zeros((o_ref.shape[0], o_ref.shape[1] - n), o_ref.dtype)


def _cast_call(w, layer, n_out=None, bm=256):
    _, k, n = w.shape
    n_out = n if n_out is None else n_out
    return pl.pallas_call(
        _cast_kernel,
        grid=(k // bm,),
        in_specs=[pl.BlockSpec((None, bm, n), lambda i: (layer, i, 0))],
        out_specs=pl.BlockSpec((bm, n_out), lambda i: (i, 0)),
        out_shape=jax.ShapeDtypeStruct((k, n_out), BF16),
        compiler_params=pltpu.CompilerParams(dimension_semantics=("arbitrary",)),
        name="cast",
    )(w)


def _mod_kernel(c_ref, w_ref, b_ref, o_ref):
    s = jax.nn.silu(c_ref[...]).astype(BF16)
    o_ref[...] = _dot(s, w_ref[...].astype(BF16)) + b_ref[...]


def _mod_call(c, w_mod, b_mod):
    bsz = c.shape[0]
    rows = 8
    cp = jnp.pad(c, ((0, rows - bsz), (0, 0)))
    n = w_mod.shape[1]
    bn = 1024
    out = pl.pallas_call(
        _mod_kernel,
        grid=(n // bn,),
        in_specs=[
            pl.BlockSpec((rows, D_MODEL), lambda i: (0, 0)),
            pl.BlockSpec((D_MODEL, bn), lambda i: (0, i)),
            pl.BlockSpec((1, bn), lambda i: (0, i)),
        ],
        out_specs=pl.BlockSpec((rows, bn), lambda i: (0, i)),
        out_shape=jax.ShapeDtypeStruct((rows, n), F32),
        compiler_params=pltpu.CompilerParams(dimension_semantics=("arbitrary",)),
        name="mod",
    )(cp, w_mod, b_mod.reshape(1, n))
    return out[:bsz].reshape(bsz, N_MOD, D_MODEL)


MXU_COLS = 256
CONV_ROWS = 64


def _inproj_kernel(x_ref, mod_ref, g_ref, w_ref, cw_ref, cb_ref,
                   qkva_ref, qc_ref, kct_ref, vm_ref, om_ref, gat_ref, gt_ref,
                   raw_scr, new_scr, k_scr, carry_scr, *, nblk):
    j = pl.program_id(0)

    @pl.when(j == 0)
    def _init():
        raw_scr[...] = jnp.zeros_like(raw_scr)
        carry_scr[...] = jnp.zeros_like(carry_scr)

    x = x_ref[...]
    h = _rms(x, g_ref[...]) * (1.0 + mod_ref[1:2, :]) + mod_ref[0:1, :]
    hb = h.astype(BF16)
    tm = x.shape[0]
    nchunk = tm // CHUNK
    within = (j + nblk - 1) % nblk

    def project():
        order = list(range(C_QKM, C_VM, MXU_COLS)) + list(range(0, C_QKM, MXU_COLS)) + \
            list(range(C_VM, IN_COLS_PAD, MXU_COLS))
        for c0 in order:
            c1 = min(c0 + MXU_COLS, IN_COLS_PAD)
            res = _dot(hb, w_ref[:, c0:c1])
            if c0 < C_KA:
                qkva_ref[:, c0:c1] = (res * (HEAD_DIM_ATT ** -0.5)).astype(BF16)
            elif c0 < C_QKM:
                ka, va = res[:, :ATT_KV], res[:, ATT_KV:]
                half = HEAD_DIM_ATT
                qkva_ref[:, ATT_Q:ATT_Q + ATT_KV] = ka.astype(BF16)
                qkva_ref[:, ATT_Q + ATT_KV:ATT_Q + 2 * ATT_KV] = pltpu.roll(ka, half, axis=1).astype(BF16)
                qkva_ref[:, ATT_Q + 2 * ATT_KV:ATT_Q + 3 * ATT_KV] = va.astype(BF16)
                qkva_ref[:, ATT_Q + 3 * ATT_KV:ATT_Q + 4 * ATT_KV] = pltpu.roll(va, half, axis=1).astype(BF16)
            elif c0 < C_VM:
                new_scr[:, c0 - C_QKM:c1 - C_QKM] = res
            elif c0 < C_OM:
                vm_ref[:, c0 - C_VM:c1 - C_VM] = res.astype(BF16)
            elif c0 < C_G:
                om_ref[:, c0 - C_OM:c1 - C_OM] = res
            else:
                gat_ref[...] = res
                for g in range(nchunk):
                    blk = res[g * CHUNK:(g + 1) * CHUNK, :].T
                    gt_ref[:, g * CHUNK:(g + 1) * CHUNK] = blk[0:N_GATES, :]
            yield

    def conv():
        R = CONV_ROWS
        row = lax.broadcasted_iota(jnp.int32, (R, 1), 0)
        for p in range(tm // R):
            r0 = p * R
            xg = raw_scr[r0:r0 + R, :]
            if p == 0:
                prev_row = jnp.where(within == 0, 0.0, carry_scr[0:1, :])
            else:
                prev_row = raw_scr[r0 - 1:r0, :]
            if r0 + R == tm:
                next_row = jnp.where(within == nblk - 1, 0.0, new_scr[0:1, :])
            else:
                next_row = raw_scr[r0 + R:r0 + R + 1, :]
            xm1 = jnp.where(row == 0, prev_row, pltpu.roll(xg, 1, axis=0))
            xp1 = jnp.where(row == R - 1, next_row, pltpu.roll(xg, R - 1, axis=0))
            y = xm1 * cw_ref[0:1, :] + xg * cw_ref[1:2, :] + xp1 * cw_ref[2:3, :] + cb_ref[...]
            y = jax.nn.silu(y)
            qc_ref[r0:r0 + R, :] = (y[:, :M_W] * (HEAD_DIM_M ** -0.5)).astype(BF16)
            k_scr[r0:r0 + R, :] = y[:, M_W:]
            yield
            if (r0 + R) % CHUNK == 0:
                g = r0 // CHUNK
                for hd in range(N_HEADS_M):
                    blk = k_scr[g * CHUNK:(g + 1) * CHUNK, hd * HEAD_DIM_M:(hd + 1) * HEAD_DIM_M]
                    base = (g * N_HEADS_M + hd) * HEAD_DIM_M
                    kct_ref[base:base + HEAD_DIM_M, :] = blk.T
                yield

    _alternate(project(), conv())
    carry_scr[0:1, :] = raw_scr[tm - 1:tm, :]
    raw_scr[...] = new_scr[...]


def _inproj_call(x, mod, g1, w_in_p, conv_w, conv_b, tm):
    bsz, s, d = x.shape
    nblk = s // tm
    ntiles = bsz * nblk
    outs = (
        jax.ShapeDtypeStruct((bsz, s, QKVA_W), BF16),
        jax.ShapeDtypeStruct((bsz, s, M_W), BF16),
        jax.ShapeDtypeStruct((bsz, s * N_HEADS_M, HEAD_DIM_M), F32),
        jax.ShapeDtypeStruct((bsz, s, M_W), BF16),
        jax.ShapeDtypeStruct((bsz, s, M_W), F32),
        jax.ShapeDtypeStruct((bsz, s, GATE_PAD), F32),
        jax.ShapeDtypeStruct((bsz, N_GATES, s), F32),
    )

    def cur(j):
        t = jnp.minimum(j, ntiles - 1)
        return t // nblk, t % nblk

    def old(j):
        t = jnp.maximum(j - 1, 0)
        return t // nblk, t % nblk

    cur_tile = lambda w: pl.BlockSpec((None, tm, w), lambda j: (*cur(j), 0))
    old_tile = lambda w: pl.BlockSpec((None, tm, w), lambda j: (*old(j), 0))
    const = lambda shp: pl.BlockSpec(shp, lambda j: (0, 0))
    return pl.pallas_call(
        functools.partial(_inproj_kernel, nblk=nblk),
        grid=(ntiles + 1,),
        in_specs=[
            cur_tile(d),
            pl.BlockSpec((None, N_MOD, d), lambda j: (cur(j)[0], 0, 0)),
            const((1, d)),
            pl.BlockSpec((d, IN_COLS_PAD), lambda j: (0, 0), pipeline_mode=pl.Buffered(1)),
            const((3, 2 * M_W)), const((1, 2 * M_W)),
        ],
        out_specs=[cur_tile(QKVA_W), old_tile(M_W),
                   pl.BlockSpec((None, tm * N_HEADS_M, HEAD_DIM_M), lambda j: (*old(j), 0)),
                   cur_tile(M_W), cur_tile(M_W), cur_tile(GATE_PAD),
                   pl.BlockSpec((None, N_GATES, tm), lambda j: (cur(j)[0], 0, cur(j)[1]))],
        out_shape=outs,
        scratch_shapes=[pltpu.VMEM((tm, 2 * M_W), F32), pltpu.VMEM((tm, 2 * M_W), F32),
                        pltpu.VMEM((tm, M_W), F32), pltpu.VMEM((8, 2 * M_W), F32)],
        compiler_params=pltpu.CompilerParams(
            dimension_semantics=("arbitrary",), vmem_limit_bytes=VMEM_LIMIT),
        name="inproj",
    )(x, mod, g1, w_in_p, conv_w, conv_b)


def _mlstm_reset(first, c_scr, m_scr):
    @pl.when(first)
    def _init():
        c_scr[...] = jnp.zeros_like(c_scr)
        m_scr[...] = jnp.zeros_like(m_scr)


def _mlstm_stream(q_ref, kt_ref, v_ref, gat_ref, gt_ref, bg_ref, bgr_ref, h_ref, c_scr, m_scr,
                  *, reverse, nchunk):
    L = CHUNK
    ti = lax.broadcasted_iota(jnp.int32, (L, L), 0)
    si = lax.broadcasted_iota(jnp.int32, (L, L), 1)
    causal = (si >= ti) if reverse else (si <= ti)
    tri = jnp.where(causal, 1.0, 0.0).astype(BF16)
    tri_t = jnp.where((ti >= si) if reverse else (ti <= si), 1.0, 0.0).astype(BF16)
    lane = lax.broadcasted_iota(jnp.int32, (1, L), 1)
    last = 0 if reverse else L - 1
    i_off, f_off = (2 * N_HEADS_M, 3 * N_HEADS_M) if reverse else (0, N_HEADS_M)
    ones = jnp.ones((L, HEAD_DIM_M), BF16)
    order = list(range(nchunk - 1, -1, -1) if reverse else range(nchunk))
    rows_of = lambda g: slice(g * L, (g + 1) * L)

    def split3(a):
        hi = a.astype(BF16)
        r1 = a - hi.astype(F32)
        mid = r1.astype(BF16)
        return hi, mid, (r1 - mid.astype(F32)).astype(BF16)

    gates = {}
    for g in order:
        hi, mid, lo = split3(jax.nn.log_sigmoid(gat_ref[rows_of(g), :] + bg_ref[...]))
        bcum = _dot(tri, hi) + _dot(tri, mid) + _dot(tri, lo)
        gr = gt_ref[:, rows_of(g)] + bgr_ref[...]
        parts = _dot(jnp.concatenate(split3(jax.nn.log_sigmoid(gr)), axis=0), tri_t)
        bcr = parts[0:N_GATES] + parts[N_GATES:2 * N_GATES] + parts[2 * N_GATES:3 * N_GATES]
        gates[g] = (gr, bcr, bcum)
        yield

    def head(g, h):
        rows = rows_of(g)
        cols = slice(h * HEAD_DIM_M, (h + 1) * HEAD_DIM_M)
        gct, bct, bcum = gates[g]
        r_row = gct[i_off + h:i_off + h + 1, :] - bct[f_off + h:f_off + h + 1, :]
        btot = jnp.sum(jnp.where(lane == last, bct[f_off + h:f_off + h + 1, :], 0.0),
                       axis=1, keepdims=True)
        rmax = jnp.max(r_row, axis=1, keepdims=True)
        a_max = btot + rmax
        m_prev = m_scr[h:h + 1, 0:1]
        m_new = jnp.maximum(btot + m_prev, a_max)
        m_scr[h:h + 1, :] = jnp.broadcast_to(m_new, (1, LANES))
        decay = jnp.exp(btot + m_prev - m_new)
        w_row = jnp.exp(r_row - rmax) * jnp.exp(a_max - m_new)
        kt = kt_ref[(g * N_HEADS_M + h) * HEAD_DIM_M:(g * N_HEADS_M + h + 1) * HEAD_DIM_M, :]
        rm = jnp.where(causal, r_row, -jnp.inf)
        cm = jnp.max(rm, axis=1, keepdims=True)
        yield
        q = q_ref[rows, cols]
        vaug = jnp.concatenate([v_ref[rows, cols], ones], axis=1)
        s = _dot(q, kt.astype(BF16))
        upd = _dot((kt * w_row).astype(BF16), vaug)
        u = jnp.maximum(cm, m_prev)
        ub = jnp.broadcast_to(u, (L, L))
        dm = jnp.exp(rm - ub)
        qi = (q.astype(F32) * jnp.exp(m_prev - ub)).astype(BF16)
        floor = jnp.exp(-(jnp.broadcast_to(bcum[:, f_off + h:f_off + h + 1], (L, L)) + ub))
        yield
        cprev = c_scr[h]
        lhs = jnp.concatenate([(s * dm).astype(BF16), qi], axis=1)
        rhs = jnp.concatenate([vaug, cprev.astype(BF16)], axis=0)
        out = _dot(lhs, rhs)
        c_scr[h] = decay * cprev + upd
        yield
        h_ref[rows, cols] = out[:, :HEAD_DIM_M] / jnp.maximum(jnp.abs(out[:, HEAD_DIM_M:]), floor)
        yield

    n_stage, lag = 4, 2
    heads = {}
    for slot in range(lag * (nchunk - 1) + n_stage):
        for i, g in enumerate(order):
            stage = slot - lag * i
            if 0 <= stage < n_stage:
                for h in range(N_HEADS_M):
                    if stage == 0:
                        heads[g, h] = head(g, h)
                    next(heads[g, h])
                    yield


def _mlstm_kernel(qf_ref, kf_ref, vf_ref, gf_ref, gtf_ref, qb_ref, kb_ref, vb_ref, gb_ref, gtb_ref,
                  bg_ref, bgr_ref, hf_ref, hb_ref, cf_scr, mf_scr, cb_scr, mb_scr, *, nchunk):
    j = pl.program_id(1)
    _mlstm_reset(j == 0, cf_scr, mf_scr)
    _mlstm_reset(j == 0, cb_scr, mb_scr)
    fwd = _mlstm_stream(qf_ref, kf_ref, vf_ref, gf_ref, gtf_ref, bg_ref, bgr_ref, hf_ref, cf_scr, mf_scr,
                        reverse=False, nchunk=nchunk)
    bwd = _mlstm_stream(qb_ref, kb_ref, vb_ref, gb_ref, gtb_ref, bg_ref, bgr_ref, hb_ref, cb_scr, mb_scr,
                        reverse=True, nchunk=nchunk)
    _alternate(fwd, bwd)


def _mlstm_call(qc, kct, vm, gates, gates_t, bg_pad, bg_rows, tm):
    bsz, s, _ = qc.shape
    nblk = s // tm

    def specs(pos):
        tile = lambda w: pl.BlockSpec((None, tm, w), lambda b, j: (b, pos(j), 0))
        kt_spec = pl.BlockSpec((None, tm * N_HEADS_M, HEAD_DIM_M), lambda b, j: (b, pos(j), 0))
        gt_spec = pl.BlockSpec((None, N_GATES, tm), lambda b, j: (b, 0, pos(j)))
        return [tile(M_W), kt_spec, tile(M_W), tile(GATE_PAD), gt_spec], tile(M_W)

    in_f, out_f = specs(lambda j: j)
    in_b, out_b = specs(lambda j: nblk - 1 - j)
    state = [pltpu.VMEM((N_HEADS_M, HEAD_DIM_M, 2 * HEAD_DIM_M), F32), pltpu.VMEM((8, LANES), F32)]
    return pl.pallas_call(
        functools.partial(_mlstm_kernel, nchunk=tm // CHUNK),
        grid=(bsz, nblk),
        in_specs=in_f + in_b + [pl.BlockSpec((1, GATE_PAD), lambda b, j: (0, 0)),
                                pl.BlockSpec((N_GATES, LANES), lambda b, j: (0, 0))],
        out_specs=[out_f, out_b],
        out_shape=[jax.ShapeDtypeStruct((bsz, s, M_W), F32)] * 2,
        scratch_shapes=state + state,
        compiler_params=pltpu.CompilerParams(
            dimension_semantics=("arbitrary", "arbitrary"), vmem_limit_bytes=VMEM_LIMIT),
        name="mlstm",
    )(qc, kct, vm, gates, gates_t, qc, kct, vm, gates, gates_t, bg_pad, bg_rows)


def _attn_kernel(sink_ref, q_ref, kvp_ref, kvn_ref, g_ref, o_ref, bias_scr, *, nt, nsub):
    b = pl.program_id(0)
    j = pl.program_id(1)
    nk = 3 * BLOCK

    @pl.when((b == 0) & (j == 0))
    def _init():
        row = lax.broadcasted_iota(jnp.int32, (BLOCK, nk), 0)
        col = lax.broadcasted_iota(jnp.int32, (BLOCK, nk), 1)
        dist = jnp.abs(col - BLOCK - row)
        distf = dist.astype(F32)
        for var in range(3):
            ok = dist <= WINDOW
            if var == 1:
                ok = ok & (col >= BLOCK)
            elif var == 2:
                ok = ok & (col < 2 * BLOCK)
            for h in range(N_HEADS_ATT):
                slope = 2.0 ** (-8.0 * (h + 1.0) / N_HEADS_ATT)
                bias_scr[var * N_HEADS_ATT + h] = jnp.where(ok, -slope * distf, -jnp.inf)

    lane_k = lax.broadcasted_iota(jnp.int32, (nk, LANES), 1)
    ones_a = jnp.where(lane_k < HEAD_DIM_ATT, 1.0, 0.0).astype(BF16)
    ones_b = jnp.where(lane_k < HEAD_DIM_ATT, 0.0, 1.0).astype(BF16)
    lo_half_q = lax.broadcasted_iota(jnp.int32, (BLOCK, LANES), 1) < HEAD_DIM_ATT

    def kv_block(idx):
        if idx < 0:
            return kvp_ref[...]
        if idx >= nsub:
            return kvn_ref[...]
        return q_ref[idx * BLOCK:(idx + 1) * BLOCK, ATT_Q:QKVA_W]

    for n in range(nsub):
        rows = slice(n * BLOCK, (n + 1) * BLOCK)
        kv = jnp.concatenate([kv_block(n - 1), kv_block(n), kv_block(n + 1)], axis=0)
        if n == 0:
            var = jnp.where(j == 0, 1, 0)
        elif n == nsub - 1:
            var = jnp.where(j == nt - 1, 2, 0)
        else:
            var = 0
        pieces = []
        for kvh in range(N_KV_HEADS):
            k_st, k_sw = kv[:, 0:LANES], kv[:, LANES:2 * LANES]
            v_st, v_sw = kv[:, 2 * LANES:3 * LANES], kv[:, 3 * LANES:4 * LANES]
            if kvh == 0:
                k_lo, k_hi, v_lo, v_hi = k_st, k_sw, v_st, v_sw
            else:
                k_lo, k_hi, v_lo, v_hi = k_sw, k_st, v_sw, v_st
            kk = jnp.concatenate([k_lo * ones_a, k_hi * ones_b], axis=0)
            vv = jnp.concatenate([
                jnp.concatenate([v_lo * ones_a, ones_a], axis=1),
                jnp.concatenate([v_hi * ones_b, ones_b], axis=1)], axis=0)
            for pair in range(GROUP_SIZE // 2):
                h0 = kvh * GROUP_SIZE + 2 * pair
                qp = q_ref[rows, h0 * HEAD_DIM_ATT:(h0 + 2) * HEAD_DIM_ATT]
                s2 = _dot_nt(qp, kk)
                ps, es = [], []
                for t in range(2):
                    logits = s2[:, t * nk:(t + 1) * nk] + bias_scr[var * N_HEADS_ATT + h0 + t]
                    sink = sink_ref[h0 + t]
                    mx = jnp.maximum(jnp.max(logits, axis=-1, keepdims=True), sink)
                    ps.append(jnp.exp(logits - mx).astype(BF16))
                    es.append(jnp.exp(sink - mx))
                res = _dot(jnp.concatenate(ps, axis=1), vv)
                den = res[:, LANES:] + jnp.where(lo_half_q, es[0], es[1])
                pieces.append(res[:, :LANES] / den)
        att = jnp.concatenate(pieces, axis=1)
        o_ref[rows, :] = _rms(att, g_ref[...]).astype(o_ref.dtype)


def _attn_call(sink, qkva, g_attn, tq):
    bsz, s, _ = qkva.shape
    nt = s // tq
    nsub = tq // BLOCK
    nb = s // BLOCK
    kvw = QKVA_W - ATT_Q
    kern = functools.partial(_attn_kernel, nt=nt, nsub=nsub)
    return pl.pallas_call(
        kern,
        grid=(bsz, nt),
        in_specs=[
            pl.BlockSpec(memory_space=pltpu.SMEM),
            pl.BlockSpec((None, tq, QKVA_W), lambda b, j: (b, j, 0)),
            pl.BlockSpec((None, BLOCK, kvw), lambda b, j: (b, jnp.maximum(j * nsub - 1, 0), 1)),
            pl.BlockSpec((None, BLOCK, kvw), lambda b, j: (b, jnp.minimum((j + 1) * nsub, nb - 1), 1)),
            pl.BlockSpec((1, ATT_Q), lambda b, j: (0, 0)),
        ],
        out_specs=pl.BlockSpec((None, tq, ATT_Q), lambda b, j: (b, j, 0)),
        out_shape=jax.ShapeDtypeStruct((bsz, s, ATT_Q), BF16),
        scratch_shapes=[pltpu.VMEM((3 * N_HEADS_ATT, BLOCK, 3 * BLOCK), F32)],
        compiler_params=pltpu.CompilerParams(
            dimension_semantics=("arbitrary", "arbitrary"), vmem_limit_bytes=VMEM_LIMIT),
        name="attn",
    )(sink, qkva, qkva, qkva, g_attn)


def _outffn_kernel(x_ref, att_ref, hf_ref, hb_ref, om_ref, mod_ref, gm_ref, wo_ref, g2_ref,
                   w1_ref, w2_ref, gf_ref, o_ref, hid_scr, *, final):
    hs = hf_ref[...] + hb_ref[...]
    parts = []
    for h in range(N_HEADS_M):
        cols = slice(h * HEAD_DIM_M, (h + 1) * HEAD_DIM_M)
        parts.append(_rms(hs[:, cols], gm_ref[:, cols]))
    hm = jax.nn.sigmoid(om_ref[...]) * jnp.concatenate(parts, axis=1)
    mixin = jnp.concatenate([att_ref[...], hm.astype(BF16)], axis=1)
    x1 = x_ref[...] + mod_ref[2:3, :] * _dot(mixin, wo_ref[...])

    hff = (_rms(x1, g2_ref[...]) * (1.0 + mod_ref[4:5, :]) + mod_ref[3:4, :]).astype(BF16)
    for c in range(N_FF_CHUNKS):
        gate = _dot(hff, w1_ref[:, FF_CHUNK * c:FF_CHUNK * (c + 1)])
        up = _dot(hff, w1_ref[:, D_FF + FF_CHUNK * c:D_FF + FF_CHUNK * (c + 1)])
        hid_scr[:, FF_CHUNK * c:FF_CHUNK * (c + 1)] = (jax.nn.silu(gate) * up).astype(BF16)
    x2 = x1 + mod_ref[5:6, :] * _dot(hid_scr[...], w2_ref[...])
    if final:
        x2 = _rms(x2, gf_ref[...])
    o_ref[...] = x2


def _outffn_call(x, att, hf, hb, om, mod, g_m, w_out, g2, w1, w2, g_final, tm, final):
    bsz, s, d = x.shape
    tile = lambda w: pl.BlockSpec((None, tm, w), lambda b, i: (b, i, 0))
    const = lambda shp: pl.BlockSpec(shp, lambda b, i: (0, 0))
    weight = lambda shp: pl.BlockSpec(shp, lambda b, i: (0, 0), pipeline_mode=pl.Buffered(1))
    return pl.pallas_call(
        functools.partial(_outffn_kernel, final=final),
        grid=(bsz, s // tm),
        in_specs=[
            tile(d), tile(ATT_Q), tile(M_W), tile(M_W), tile(M_W),
            pl.BlockSpec((None, N_MOD, d), lambda b, i: (b, 0, 0)),
            const((1, M_W)), weight((ATT_Q + M_W, d)), const((1, d)),
            weight((d, 2 * D_FF)), weight((D_FF, d)), const((1, d)),
        ],
        out_specs=tile(d),
        out_shape=jax.ShapeDtypeStruct((bsz, s, d), F32),
        scratch_shapes=[pltpu.VMEM((tm, D_FF), BF16)],
        compiler_params=pltpu.CompilerParams(
            dimension_semantics=("arbitrary", "arbitrary"), vmem_limit_bytes=VMEM_LIMIT),
        name="outffn",
    )(x, att, hf, hb, om, mod, g_m, w_out, g2, w1, w2, g_final)


def _layer(l, x, c, w_mod, b_mod, g_norm1, w_in, conv_w, conv_b, b_gates, sink,
           g_attn_out, g_mlstm_out, w_out, g_norm2, w_ffn_in, w_ffn_out, g_final, final):
    d = x.shape[-1]
    mod = _mod_call(c, w_mod[l], b_mod[l])

    w_in_p = _cast_call(w_in, l, IN_COLS_PAD)
    qkva, qc, kct, vm, om, gates, gates_t = _inproj_call(
        x, mod, g_norm1[l].reshape(1, d), w_in_p, conv_w[l], conv_b[l].reshape(1, 2 * M_W), tm=512)

    bg_pad = jnp.pad(b_gates[l], (0, GATE_PAD - N_GATES)).reshape(1, GATE_PAD)
    bg_rows = jnp.broadcast_to(b_gates[l][:, None], (N_GATES, LANES))
    hf, hb = _mlstm_call(qc, kct, vm, gates, gates_t, bg_pad, bg_rows, tm=1024)

    att = _attn_call(sink[l], qkva, g_attn_out[l].reshape(1, ATT_Q), tq=512)

    return _outffn_call(x, att, hf, hb, om, mod, g_mlstm_out[l].reshape(1, M_W), _cast_call(w_out, l),
                        g_norm2[l].reshape(1, d), _cast_call(w_ffn_in, l), _cast_call(w_ffn_out, l),
                        g_final.reshape(1, d), tm=512, final=final)


def kernel(x, c, w_mod, b_mod, g_norm1, w_in, conv_w, conv_b, b_gates, sink, g_attn_out,
           g_mlstm_out, w_out, g_norm2, w_ffn_in, w_ffn_out, g_final):
    depth = w_mod.shape[0]
    for l in range(depth):
        x = _layer(l, x, c, w_mod, b_mod, g_norm1, w_in, conv_w, conv_b, b_gates, sink, g_attn_out,
                   g_mlstm_out, w_out, g_norm2, w_ffn_in, w_ffn_out, g_final, final=(l == depth - 1))
    return x
```

```python
import functools

import jax
import jax.numpy as jnp
from jax import lax
from jax.experimental import pallas as pl
from jax.experimental.pallas import tpu as pltpu

F32 = jnp.float32
BF16 = jnp.bfloat16

D_MODEL = 1024
EPS = 1e-6
N_HEADS_ATT = 8
N_KV_HEADS = 2
HEAD_DIM_ATT = 64
GROUP_SIZE = N_HEADS_ATT // N_KV_HEADS
WINDOW = 128
BLOCK = 128
N_HEADS_M = 4
HEAD_DIM_M = 128
CHUNK = 128
ATT_Q = N_HEADS_ATT * HEAD_DIM_ATT
ATT_KV = N_KV_HEADS * HEAD_DIM_ATT
M_W = N_HEADS_M * HEAD_DIM_M
N_GATES = 4 * N_HEADS_M
D_FF = 2816
N_MOD = 6

LANES = 128
GATE_PAD = LANES
FF_CHUNK = 256
N_FF_CHUNKS = D_FF // FF_CHUNK
OUT_COL_BLOCK = 256
VMEM_LIMIT = 56 * 1024 * 1024

C_QA = 0
C_KA = ATT_Q
C_VA = ATT_Q + ATT_KV
C_QKM = ATT_Q + 2 * ATT_KV
C_VM = C_QKM + 2 * M_W
C_OM = C_VM + M_W
C_G = C_OM + M_W
IN_COLS_PAD = C_G + GATE_PAD
QKVA_W = ATT_Q + 4 * ATT_KV


def _dot(a, b):
    return jnp.dot(a, b, preferred_element_type=F32)


def _dot_nt(a, b):
    return lax.dot_general(a, b, (((1,), (1,)), ((), ())), preferred_element_type=F32)


def _rms(x, g):
    return x * lax.rsqrt(jnp.mean(x * x, axis=-1, keepdims=True) + EPS) * g


def _alternate(*streams):
    live = list(streams)
    while live:
        for s in list(live):
            if next(s, StopIteration) is StopIteration:
                live.remove(s)


def _cast_kernel(w_ref, o_ref):
    n = w_ref.shape[1]
    o_ref[:, :n] = w_ref[...].astype(o_ref.dtype)
    if o_ref.shape[1] > n:
        o_ref[:, n:] = jnp.zeros((o_ref.shape[0], o_ref.shape[1] - n), o_ref.dtype)


def _cast_call(w, layer, n_out=None, bm=256):
    _, k, n = w.shape
    n_out = n if n_out is None else n_out
    return pl.pallas_call(
        _cast_kernel,
        grid=(k // bm,),
        in_specs=[pl.BlockSpec((None, bm, n), lambda i: (layer, i, 0))],
        out_specs=pl.BlockSpec((bm, n_out), lambda i: (i, 0)),
        out_shape=jax.ShapeDtypeStruct((k, n_out), BF16),
        compiler_params=pltpu.CompilerParams(dimension_semantics=("arbitrary",)),
        name="cast",
    )(w)


def _mod_kernel(c_ref, w_ref, b_ref, o_ref):
    s = jax.nn.silu(c_ref[...]).astype(BF16)
    o_ref[...] = _dot(s, w_ref[...].astype(BF16)) + b_ref[...]


def _mod_call(c, w_mod, b_mod):
    bsz = c.shape[0]
    rows = 8
    cp = jnp.pad(c, ((0, rows - bsz), (0, 0)))
    n = w_mod.shape[1]
    bn = 1024
    out = pl.pallas_call(
        _mod_kernel,
        grid=(n // bn,),
        in_specs=[
            pl.BlockSpec((rows, D_MODEL), lambda i: (0, 0)),
            pl.BlockSpec((D_MODEL, bn), lambda i: (0, i)),
            pl.BlockSpec((1, bn), lambda i: (0, i)),
        ],
        out_specs=pl.BlockSpec((rows, bn), lambda i: (0, i)),
        out_shape=jax.ShapeDtypeStruct((rows, n), F32),
        compiler_params=pltpu.CompilerParams(dimension_semantics=("arbitrary",)),
        name="mod",
    )(cp, w_mod, b_mod.reshape(1, n))
    return out[:bsz].reshape(bsz, N_MOD, D_MODEL)


MXU_COLS = 256
CONV_ROWS = 64


def _inproj_kernel(x_ref, mod_ref, g_ref, w_ref, cw_ref, cb_ref,
                   qkva_ref, qc_ref, kct_ref, vm_ref, om_ref, gat_ref, gt_ref,
                   raw_scr, new_scr, k_scr, carry_scr, *, nblk):
    j = pl.program_id(0)

    @pl.when(j == 0)
    def _init():
        raw_scr[...] = jnp.zeros_like(raw_scr)
        carry_scr[...] = jnp.zeros_like(carry_scr)

    x = x_ref[...]
    h = _rms(x, g_ref[...]) * (1.0 + mod_ref[1:2, :]) + mod_ref[0:1, :]
    hb = h.astype(BF16)
    tm = x.shape[0]
    nchunk = tm // CHUNK
    within = (j + nblk - 1) % nblk

    anchors = []

    def zero_after(a):
        bits = pltpu.bitcast(a, jnp.uint32)
        z = lax.shift_right_logical(lax.shift_right_logical(bits, jnp.uint32(16)), jnp.uint32(16))
        return jnp.concatenate([pltpu.bitcast(z, F32)] * (2 * M_W // LANES), axis=1)

    def project():
        order = list(range(C_QKM, C_VM, MXU_COLS)) + list(range(0, C_QKM, MXU_COLS)) + \
            list(range(C_VM, IN_COLS_PAD, MXU_COLS))
        for c0 in order:
            c1 = min(c0 + MXU_COLS, IN_COLS_PAD)
            res = _dot(hb, w_ref[:, c0:c1])
            anchors.append(res[0:1, 0:LANES])
            if c0 < C_KA:
                qkva_ref[:, c0:c1] = (res * (HEAD_DIM_ATT ** -0.5)).astype(BF16)
            elif c0 < C_QKM:
                ka, va = res[:, :ATT_KV], res[:, ATT_KV:]
                half = HEAD_DIM_ATT
                qkva_ref[:, ATT_Q:ATT_Q + ATT_KV] = ka.astype(BF16)
                qkva_ref[:, ATT_Q + ATT_KV:ATT_Q + 2 * ATT_KV] = pltpu.roll(ka, half, axis=1).astype(BF16)
                qkva_ref[:, ATT_Q + 2 * ATT_KV:ATT_Q + 3 * ATT_KV] = va.astype(BF16)
                qkva_ref[:, ATT_Q + 3 * ATT_KV:ATT_Q + 4 * ATT_KV] = pltpu.roll(va, half, axis=1).astype(BF16)
            elif c0 < C_VM:
                new_scr[:, c0 - C_QKM:c1 - C_QKM] = res
            elif c0 < C_OM:
                vm_ref[:, c0 - C_VM:c1 - C_VM] = res.astype(BF16)
            elif c0 < C_G:
                om_ref[:, c0 - C_OM:c1 - C_OM] = res
            else:
                gat_ref[...] = res
                for g in range(nchunk):
                    blk = res[g * CHUNK:(g + 1) * CHUNK, :].T
                    gt_ref[:, g * CHUNK:(g + 1) * CHUNK] = blk[0:N_GATES, :]
            yield

    def conv():
        R = CONV_ROWS
        row = lax.broadcasted_iota(jnp.int32, (R, 1), 0)
        for p in range(tm // R):
            r0 = p * R
            xg = raw_scr[r0:r0 + R, :]
            if p == 0:
                prev_row = jnp.where(within == 0, 0.0, carry_scr[0:1, :])
            else:
                prev_row = raw_scr[r0 - 1:r0, :]
            if r0 + R == tm:
                next_row = jnp.where(within == nblk - 1, 0.0, new_scr[0:1, :])
            else:
                next_row = raw_scr[r0 + R:r0 + R + 1, :]
            xm1 = jnp.where(row == 0, prev_row, pltpu.roll(xg, 1, axis=0))
            xp1 = jnp.where(row == R - 1, next_row, pltpu.roll(xg, R - 1, axis=0))
            z = zero_after(anchors[-1])
            y =xm1 * (cw_ref[0:1, :] + z) + xg * (cw_ref[1:2, :] + z) + xp1 * (cw_ref[2:3, :] + z) \
                + cb_ref[...]
            y = jax.nn.silu(y)
            qc_ref[r0:r0 + R, :] = (y[:, :M_W] * (HEAD_DIM_M ** -0.5)).astype(BF16)
            k_scr[r0:r0 + R, :] = y[:, M_W:]
            yield
            if (r0 + R) % CHUNK == 0:
                g = r0 // CHUNK
                for hd in range(N_HEADS_M):
                    blk = k_scr[g * CHUNK:(g + 1) * CHUNK, hd * HEAD_DIM_M:(hd + 1) * HEAD_DIM_M]
                    base = (g * N_HEADS_M + hd) * HEAD_DIM_M
                    kct_ref[base:base + HEAD_DIM_M, :] = blk.T
                yield

    _alternate(project(), conv())
    carry_scr[0:1, :] = raw_scr[tm - 1:tm, :]
    raw_scr[...] = new_scr[...]


def _inproj_call(x, mod, g1, w_in_p, conv_w, conv_b, tm):
    bsz, s, d = x.shape
    nblk = s // tm
    ntiles = bsz * nblk
    outs = (
        jax.ShapeDtypeStruct((bsz, s, QKVA_W), BF16),
        jax.ShapeDtypeStruct((bsz, s, M_W), BF16),
        jax.ShapeDtypeStruct((bsz, s * N_HEADS_M, HEAD_DIM_M), F32),
        jax.ShapeDtypeStruct((bsz, s, M_W), BF16),
        jax.ShapeDtypeStruct((bsz, s, M_W), F32),
        jax.ShapeDtypeStruct((bsz, s, GATE_PAD), F32),
        jax.ShapeDtypeStruct((bsz, N_GATES, s), F32),
    )

    def cur(j):
        t = jnp.minimum(j, ntiles - 1)
        return t // nblk, t % nblk

    def old(j):
        t = jnp.maximum(j - 1, 0)
        return t // nblk, t % nblk

    cur_tile = lambda w: pl.BlockSpec((None, tm, w), lambda j: (*cur(j), 0))
    old_tile = lambda w: pl.BlockSpec((None, tm, w), lambda j: (*old(j), 0))
    const = lambda shp: pl.BlockSpec(shp, lambda j: (0, 0))
    return pl.pallas_call(
        functools.partial(_inproj_kernel, nblk=nblk),
        grid=(ntiles + 1,),
        in_specs=[
            cur_tile(d),
            pl.BlockSpec((None, N_MOD, d), lambda j: (cur(j)[0], 0, 0)),
            const((1, d)),
            pl.BlockSpec((d, IN_COLS_PAD), lambda j: (0, 0), pipeline_mode=pl.Buffered(1)),
            const((3, 2 * M_W)), const((1, 2 * M_W)),
        ],
        out_specs=[cur_tile(QKVA_W), old_tile(M_W),
                   pl.BlockSpec((None, tm * N_HEADS_M, HEAD_DIM_M), lambda j: (*old(j), 0)),
                   cur_tile(M_W), cur_tile(M_W), cur_tile(GATE_PAD),
                   pl.BlockSpec((None, N_GATES, tm), lambda j: (cur(j)[0], 0, cur(j)[1]))],
        out_shape=outs,
        scratch_shapes=[pltpu.VMEM((tm, 2 * M_W), F32), pltpu.VMEM((tm, 2 * M_W), F32),
                        pltpu.VMEM((tm, M_W), F32), pltpu.VMEM((8, 2 * M_W), F32)],
        compiler_params=pltpu.CompilerParams(
            dimension_semantics=("arbitrary",), vmem_limit_bytes=VMEM_LIMIT),
        name="inproj",
    )(x, mod, g1, w_in_p, conv_w, conv_b)


def _mlstm_reset(first, c_scr, m_scr):
    @pl.when(first)
    def _init():
        c_scr[...] = jnp.zeros_like(c_scr)
        m_scr[...] = jnp.zeros_like(m_scr)


def _mlstm_stream(q_ref, kt_ref, v_ref, gat_ref, gt_ref, bg_ref, bgr_ref, h_ref, c_scr, m_scr,
                  *, reverse, nchunk):
    L = CHUNK
    ti = lax.broadcasted_iota(jnp.int32, (L, L), 0)
    si = lax.broadcasted_iota(jnp.int32, (L, L), 1)
    causal = (si >= ti) if reverse else (si <= ti)
    tri = jnp.where(causal, 1.0, 0.0).astype(BF16)
    tri_t = jnp.where((ti >= si) if reverse else (ti <= si), 1.0, 0.0).astype(BF16)
    lane = lax.broadcasted_iota(jnp.int32, (1, L), 1)
    last = 0 if reverse else L - 1
    i_off, f_off = (2 * N_HEADS_M, 3 * N_HEADS_M) if reverse else (0, N_HEADS_M)
    ones = jnp.ones((L, HEAD_DIM_M), BF16)
    order = list(range(nchunk - 1, -1, -1) if reverse else range(nchunk))
    rows_of = lambda g: slice(g * L, (g + 1) * L)

    def split3(a):
        hi = a.astype(BF16)
        r1 = a - hi.astype(F32)
        mid = r1.astype(BF16)
        return hi, mid, (r1 - mid.astype(F32)).astype(BF16)

    gates = {}
    for g in order:
        hi, mid, lo = split3(jax.nn.log_sigmoid(gat_ref[rows_of(g), :] + bg_ref[...]))
        bcum = _dot(tri, hi) + _dot(tri, mid) + _dot(tri, lo)
        gr = gt_ref[:, rows_of(g)] + bgr_ref[...]
        parts = _dot(jnp.concatenate(split3(jax.nn.log_sigmoid(gr)), axis=0), tri_t)
        bcr = parts[0:N_GATES] + parts[N_GATES:2 * N_GATES] + parts[2 * N_GATES:3 * N_GATES]
        gates[g] = (gr, bcr, bcum)
        yield

    def head(g, h):
        rows = rows_of(g)
        cols = slice(h * HEAD_DIM_M, (h + 1) * HEAD_DIM_M)
        gct, bct, bcum = gates[g]
        r_row = gct[i_off + h:i_off + h + 1, :] - bct[f_off + h:f_off + h + 1, :]
        btot = jnp.sum(jnp.where(lane == last, bct[f_off + h:f_off + h + 1, :], 0.0),
                       axis=1, keepdims=True)
        rmax = jnp.max(r_row, axis=1, keepdims=True)
        a_max = btot + rmax
        m_prev = m_scr[h:h + 1, 0:1]
        m_new = jnp.maximum(btot + m_prev, a_max)
        m_scr[h:h + 1, :] = jnp.broadcast_to(m_new, (1, LANES))
        decay = jnp.exp(btot + m_prev - m_new)
        w_row = jnp.exp(r_row - rmax) * jnp.exp(a_max - m_new)
        kt = kt_ref[(g * N_HEADS_M + h) * HEAD_DIM_M:(g * N_HEADS_M + h + 1) * HEAD_DIM_M, :]
        rm = jnp.where(causal, r_row, -jnp.inf)
        cm = jnp.max(rm, axis=1, keepdims=True)
        yield
        q = q_ref[rows, cols]
        vaug = jnp.concatenate([v_ref[rows, cols], ones], axis=1)
        s = _dot(q, kt.astype(BF16))
        upd = _dot((kt * w_row).astype(BF16), vaug)
        u = jnp.maximum(cm, m_prev)
        ub = jnp.broadcast_to(u, (L, L))
        dm = jnp.exp(rm - ub)
        qi = (q.astype(F32) * jnp.exp(m_prev - ub)).astype(BF16)
        floor = jnp.exp(-(jnp.broadcast_to(bcum[:, f_off + h:f_off + h + 1], (L, L)) + ub))
        yield
        cprev = c_scr[h]
        lhs = jnp.concatenate([(s * dm).astype(BF16), qi], axis=1)
        rhs = jnp.concatenate([vaug, cprev.astype(BF16)], axis=0)
        out = _dot(lhs, rhs)
        c_scr[h] = decay * cprev + upd
        yield
        h_ref[rows, cols] = out[:, :HEAD_DIM_M] / jnp.maximum(jnp.abs(out[:, HEAD_DIM_M:]), floor)
        yield

    n_stage, lag = 4, 2
    heads = {}
    for slot in range(lag * (nchunk - 1) + n_stage):
        for i, g in enumerate(order):
            stage = slot - lag * i
            if 0 <= stage < n_stage:
                for h in range(N_HEADS_M):
                    if stage == 0:
                        heads[g, h] = head(g, h)
                    next(heads[g, h])
                    yield


def _mlstm_kernel(qf_ref, kf_ref, vf_ref, gf_ref, gtf_ref, qb_ref, kb_ref, vb_ref, gb_ref, gtb_ref,
                  bg_ref, bgr_ref, *rest, nchunk, ncast):
    w_refs, rest = rest[:ncast], rest[ncast:]
    hf_ref, hb_ref = rest[:2]
    wo_refs, (cf_scr, mf_scr, cb_scr, mb_scr) = rest[2:2 + ncast], rest[2 + ncast:]
    j = pl.program_id(1)
    _mlstm_reset(j == 0, cf_scr, mf_scr)
    _mlstm_reset(j == 0, cb_scr, mb_scr)
    for w_ref, wo_ref in zip(w_refs, wo_refs):
        wo_ref[...] = w_ref[...].astype(wo_ref.dtype)
    fwd = _mlstm_stream(qf_ref, kf_ref, vf_ref, gf_ref, gtf_ref, bg_ref, bgr_ref, hf_ref, cf_scr, mf_scr,
                        reverse=False, nchunk=nchunk)
    bwd = _mlstm_stream(qb_ref, kb_ref, vb_ref, gb_ref, gtb_ref, bg_ref, bgr_ref, hb_ref, cb_scr, mb_scr,
                        reverse=True, nchunk=nchunk)
    _alternate(fwd, bwd)


def _mlstm_call(qc, kct, vm, gates, gates_t, bg_pad, bg_rows, weights, layer, tm):
    bsz, s, _ = qc.shape
    nblk = s // tm
    nsteps = bsz * nblk
    w_specs, w_shapes = [], []
    for w in weights:
        _, k, n = w.shape
        rows = k // nsteps
        assert rows * nsteps == k and rows % 16 == 0
        w_specs.append(pl.BlockSpec((None, rows, n), lambda b, j: (layer, b * nblk + j, 0)))
        w_shapes.append(jax.ShapeDtypeStruct((k, n), BF16))
    wo_specs = [pl.BlockSpec((sp.block_shape[1], sp.block_shape[2]), lambda b, j: (b * nblk + j, 0))
                for sp in w_specs]

    def specs(pos):
        tile = lambda w: pl.BlockSpec((None, tm, w), lambda b, j: (b, pos(j), 0))
        kt_spec = pl.BlockSpec((None, tm * N_HEADS_M, HEAD_DIM_M), lambda b, j: (b, pos(j), 0))
        gt_spec = pl.BlockSpec((None, N_GATES, tm), lambda b, j: (b, 0, pos(j)))
        return [tile(M_W), kt_spec, tile(M_W), tile(GATE_PAD), gt_spec], tile(M_W)

    in_f, out_f = specs(lambda j: j)
    in_b, out_b = specs(lambda j: nblk - 1 - j)
    state = [pltpu.VMEM((N_HEADS_M, HEAD_DIM_M, 2 * HEAD_DIM_M), F32), pltpu.VMEM((8, LANES), F32)]
    outs = pl.pallas_call(
        functools.partial(_mlstm_kernel, nchunk=tm // CHUNK, ncast=len(weights)),
        grid=(bsz, nblk),
        in_specs=in_f + in_b + [pl.BlockSpec((1, GATE_PAD), lambda b, j: (0, 0)),
                                pl.BlockSpec((N_GATES, LANES), lambda b, j: (0, 0))] + w_specs,
        out_specs=[out_f, out_b] + wo_specs,
        out_shape=[jax.ShapeDtypeStruct((bsz, s, M_W), F32)] * 2 + w_shapes,
        scratch_shapes=state + state,
        compiler_params=pltpu.CompilerParams(
            dimension_semantics=("arbitrary", "arbitrary"), vmem_limit_bytes=VMEM_LIMIT),
        name="mlstm",
    )(qc, kct, vm, gates, gates_t, qc, kct, vm, gates, gates_t, bg_pad, bg_rows, *weights)
    return outs[0], outs[1], outs[2:]


def _attn_kernel(sink_ref, q_ref, kvp_ref, kvn_ref, g_ref, o_ref, bias_scr, *, nt, nsub):
    b = pl.program_id(0)
    j = pl.program_id(1)
    nk = 3 * BLOCK

    @pl.when((b == 0) & (j == 0))
    def _init():
        row = lax.broadcasted_iota(jnp.int32, (BLOCK, nk), 0)
        col = lax.broadcasted_iota(jnp.int32, (BLOCK, nk), 1)
        dist = jnp.abs(col - BLOCK - row)
        distf = dist.astype(F32)
        for var in range(3):
            ok = dist <= WINDOW
            if var == 1:
                ok = ok & (col >= BLOCK)
            elif var == 2:
                ok = ok & (col < 2 * BLOCK)
            for h in range(N_HEADS_ATT):
                slope = 2.0 ** (-8.0 * (h + 1.0) / N_HEADS_ATT)
                bias_scr[var * N_HEADS_ATT + h] = jnp.where(ok, -slope * distf, -jnp.inf)

    lane_k = lax.broadcasted_iota(jnp.int32, (nk, LANES), 1)
    ones_a = jnp.where(lane_k < HEAD_DIM_ATT, 1.0, 0.0).astype(BF16)
    ones_b = jnp.where(lane_k < HEAD_DIM_ATT, 0.0, 1.0).astype(BF16)
    lo_half_q = lax.broadcasted_iota(jnp.int32, (BLOCK, LANES), 1) < HEAD_DIM_ATT

    def kv_block(idx):
        if idx < 0:
            return kvp_ref[...]
        if idx >= nsub:
            return kvn_ref[...]
        return q_ref[idx * BLOCK:(idx + 1) * BLOCK, ATT_Q:QKVA_W]

    for n in range(nsub):
        rows = slice(n * BLOCK, (n + 1) * BLOCK)
        kv = jnp.concatenate([kv_block(n - 1), kv_block(n), kv_block(n + 1)], axis=0)
        if n == 0:
            var = jnp.where(j == 0, 1, 0)
        elif n == nsub - 1:
            var = jnp.where(j == nt - 1, 2, 0)
        else:
            var = 0
        pieces = []
        for kvh in range(N_KV_HEADS):
            k_st, k_sw = kv[:, 0:LANES], kv[:, LANES:2 * LANES]
            v_st, v_sw = kv[:, 2 * LANES:3 * LANES], kv[:, 3 * LANES:4 * LANES]
            if kvh == 0:
                k_lo, k_hi, v_lo, v_hi = k_st, k_sw, v_st, v_sw
            else:
                k_lo, k_hi, v_lo, v_hi = k_sw, k_st, v_sw, v_st
            kk = jnp.concatenate([k_lo * ones_a, k_hi * ones_b], axis=0)
            vv = jnp.concatenate([
                jnp.concatenate([v_lo * ones_a, ones_a], axis=1),
                jnp.concatenate([v_hi * ones_b, ones_b], axis=1)], axis=0)
            for pair in range(GROUP_SIZE // 2):
                h0 = kvh * GROUP_SIZE + 2 * pair
                qp = q_ref[rows, h0 * HEAD_DIM_ATT:(h0 + 2) * HEAD_DIM_ATT]
                s2 = _dot_nt(qp, kk)
                ps, es = [], []
                for t in range(2):
                    logits = s2[:, t * nk:(t + 1) * nk] + bias_scr[var * N_HEADS_ATT + h0 + t]
                    sink = sink_ref[h0 + t]
                    mx = jnp.maximum(jnp.max(logits, axis=-1, keepdims=True), sink)
                    ps.append(jnp.exp(logits - mx).astype(BF16))
                    es.append(jnp.exp(sink - mx))
                res = _dot(jnp.concatenate(ps, axis=1), vv)
                den = res[:, LANES:] + jnp.where(lo_half_q, es[0], es[1])
                pieces.append(res[:, :LANES] / den)
        att = jnp.concatenate(pieces, axis=1)
        o_ref[rows, :] = _rms(att, g_ref[...]).astype(o_ref.dtype)


def _attn_call(sink, qkva, g_attn, tq):
    bsz, s, _ = qkva.shape
    nt = s // tq
    nsub = tq // BLOCK
    nb = s // BLOCK
    kvw = QKVA_W - ATT_Q
    kern = functools.partial(_attn_kernel, nt=nt, nsub=nsub)
    return pl.pallas_call(
        kern,
        grid=(bsz, nt),
        in_specs=[
            pl.BlockSpec(memory_space=pltpu.SMEM),
            pl.BlockSpec((None, tq, QKVA_W), lambda b, j: (b, j, 0)),
            pl.BlockSpec((None, BLOCK, kvw), lambda b, j: (b, jnp.maximum(j * nsub - 1, 0), 1)),
            pl.BlockSpec((None, BLOCK, kvw), lambda b, j: (b, jnp.minimum((j + 1) * nsub, nb - 1), 1)),
            pl.BlockSpec((1, ATT_Q), lambda b, j: (0, 0)),
        ],
        out_specs=pl.BlockSpec((None, tq, ATT_Q), lambda b, j: (b, j, 0)),
        out_shape=jax.ShapeDtypeStruct((bsz, s, ATT_Q), BF16),
        scratch_shapes=[pltpu.VMEM((3 * N_HEADS_ATT, BLOCK, 3 * BLOCK), F32)],
        compiler_params=pltpu.CompilerParams(
            dimension_semantics=("arbitrary", "arbitrary"), vmem_limit_bytes=VMEM_LIMIT),
        name="attn",
    )(sink, qkva, qkva, qkva, g_attn)


def _outffn_kernel(x_ref, att_ref, hf_ref, hb_ref, om_ref, mod_ref, gm_ref, wo_ref, g2_ref,
                   w1_ref, w2_ref, gf_ref, o_ref, hid_scr, *, final):
    hs = hf_ref[...] + hb_ref[...]
    parts = []
    for h in range(N_HEADS_M):
        cols = slice(h * HEAD_DIM_M, (h + 1) * HEAD_DIM_M)
        parts.append(_rms(hs[:, cols], gm_ref[:, cols]))
    hm = jax.nn.sigmoid(om_ref[...]) * jnp.concatenate(parts, axis=1)
    mixin = jnp.concatenate([att_ref[...], hm.astype(BF16)], axis=1)
    x1 = x_ref[...] + mod_ref[2:3, :] * _dot(mixin, wo_ref[...])

    hff = (_rms(x1, g2_ref[...]) * (1.0 + mod_ref[4:5, :]) + mod_ref[3:4, :]).astype(BF16)
    for c in range(N_FF_CHUNKS):
        gate = _dot(hff, w1_ref[:, FF_CHUNK * c:FF_CHUNK * (c + 1)])
        up = _dot(hff, w1_ref[:, D_FF + FF_CHUNK * c:D_FF + FF_CHUNK * (c + 1)])
        hid_scr[:, FF_CHUNK * c:FF_CHUNK * (c + 1)] = (jax.nn.silu(gate) * up).astype(BF16)
    x2 = x1 + mod_ref[5:6, :] * _dot(hid_scr[...], w2_ref[...])
    if final:
        x2 = _rms(x2, gf_ref[...])
    o_ref[...] = x2


def _outffn_call(x, att, hf, hb, om, mod, g_m, w_out, g2, w1, w2, g_final, tm, final):
    bsz, s, d = x.shape
    tile = lambda w: pl.BlockSpec((None, tm, w), lambda b, i: (b, i, 0))
    const = lambda shp: pl.BlockSpec(shp, lambda b, i: (0, 0))
    weight = lambda shp: pl.BlockSpec(shp, lambda b, i: (0, 0), pipeline_mode=pl.Buffered(1))
    return pl.pallas_call(
        functools.partial(_outffn_kernel, final=final),
        grid=(bsz, s // tm),
        in_specs=[
            tile(d), tile(ATT_Q), tile(M_W), tile(M_W), tile(M_W),
            pl.BlockSpec((None, N_MOD, d), lambda b, i: (b, 0, 0)),
            const((1, M_W)), weight((ATT_Q + M_W, d)), const((1, d)),
            weight((d, 2 * D_FF)), weight((D_FF, d)), const((1, d)),
        ],
        out_specs=tile(d),
        out_shape=jax.ShapeDtypeStruct((bsz, s, d), F32),
        scratch_shapes=[pltpu.VMEM((tm, D_FF), BF16)],
        compiler_params=pltpu.CompilerParams(
            dimension_semantics=("arbitrary", "arbitrary"), vmem_limit_bytes=VMEM_LIMIT),
        name="outffn",
    )(x, att, hf, hb, om, mod, g_m, w_out, g2, w1, w2, g_final)


def _layer(l, x, c, w_mod, b_mod, g_norm1, w_in, conv_w, conv_b, b_gates, sink,
           g_attn_out, g_mlstm_out, w_out, g_norm2, w_ffn_in, w_ffn_out, g_final, final):
    d = x.shape[-1]
    mod = _mod_call(c, w_mod[l], b_mod[l])

    w_in_p = _cast_call(w_in, l, IN_COLS_PAD)
    qkva, qc, kct, vm, om, gates, gates_t = _inproj_call(
        x, mod, g_norm1[l].reshape(1, d), w_in_p, conv_w[l], conv_b[l].reshape(1, 2 * M_W), tm=512)

    bg_pad = jnp.pad(b_gates[l], (0, GATE_PAD - N_GATES)).reshape(1, GATE_PAD)
    bg_rows = jnp.broadcast_to(b_gates[l][:, None], (N_GATES, LANES))
    hf, hb, (w_out_b, w_ffn_in_b, w_ffn_out_b) = _mlstm_call(
        qc, kct, vm, gates, gates_t, bg_pad, bg_rows, (w_out, w_ffn_in, w_ffn_out), l, tm=1024)

    att = _attn_call(sink[l], qkva, g_attn_out[l].reshape(1, ATT_Q), tq=512)

    return _outffn_call(x, att, hf, hb, om, mod, g_mlstm_out[l].reshape(1, M_W), w_out_b,
                        g_norm2[l].reshape(1, d), w_ffn_in_b, w_ffn_out_b,
                        g_final.reshape(1, d), tm=512, final=final)


def kernel(x, c, w_mod, b_mod, g_norm1, w_in, conv_w, conv_b, b_gates, sink, g_attn_out,
           g_mlstm_out, w_out, g_norm2, w_ffn_in, w_ffn_out, g_final):
    depth = w_mod.shape[0]
    for l in range(depth):
        x = _layer(l, x, c, w_mod, b_mod, g_norm1, w_in, conv_w, conv_b, b_gates, sink, g_attn_out,
                   g_mlstm_out, w_out, g_norm2, w_ffn_in, w_ffn_out, g_final, final=(l == depth - 1))
    return x
```

```python
import functools

import jax
import jax.numpy as jnp
from jax import lax
from jax.experimental import pallas as pl
from jax.experimental.pallas import tpu as pltpu

F32 = jnp.float32
BF16 = jnp.bfloat16

D_MODEL = 1024
EPS = 1e-6
N_HEADS_ATT = 8
N_KV_HEADS = 2
HEAD_DIM_ATT = 64
GROUP_SIZE = N_HEADS_ATT // N_KV_HEADS
WINDOW = 128
BLOCK = 128
N_HEADS_M = 4
HEAD_DIM_M = 128
CHUNK = 128
ATT_Q = N_HEADS_ATT * HEAD_DIM_ATT
ATT_KV = N_KV_HEADS * HEAD_DIM_ATT
M_W = N_HEADS_M * HEAD_DIM_M
N_GATES = 4 * N_HEADS_M
D_FF = 2816
N_MOD = 6

LANES = 128
GATE_PAD = LANES
FF_CHUNK = 256
N_FF_CHUNKS = D_FF // FF_CHUNK
OUT_COL_BLOCK = 256
VMEM_LIMIT = 56 * 1024 * 1024

C_QA = 0
C_KA = ATT_Q
C_VA = ATT_Q + ATT_KV
C_QKM = ATT_Q + 2 * ATT_KV
C_VM = C_QKM + 2 * M_W
C_OM = C_VM + M_W
C_G = C_OM + M_W
IN_COLS_PAD = C_G + GATE_PAD
QKVA_W = ATT_Q + 4 * ATT_KV


def _dot(a, b):
    return jnp.dot(a, b, preferred_element_type=F32)


def _dot_nt(a, b):
    return lax.dot_general(a, b, (((1,), (1,)), ((), ())), preferred_element_type=F32)


def _rms(x, g):
    return x * lax.rsqrt(jnp.mean(x * x, axis=-1, keepdims=True) + EPS) * g


def _alternate(*streams):
    live = list(streams)
    while live:
        for s in list(live):
            if next(s, StopIteration) is StopIteration:
                live.remove(s)


def _cast_kernel(w_ref, o_ref):
    n = w_ref.shape[1]
    o_ref[:, :n] = w_ref[...].astype(o_ref.dtype)
    if o_ref.shape[1] > n:
        o_ref[:, n:] = jnp.zeros((o_ref.shape[0], o_ref.shape[1] - n), o_ref.dtype)


def _cast_call(w, layer, n_out=None, bm=256):
    _, k, n = w.shape
    n_out = n if n_out is None else n_out
    return pl.pallas_call(
        _cast_kernel,
        grid=(k // bm,),
        in_specs=[pl.BlockSpec((None, bm, n), lambda i: (layer, i, 0))],
        out_specs=pl.BlockSpec((bm, n_out), lambda i: (i, 0)),
        out_shape=jax.ShapeDtypeStruct((k, n_out), BF16),
        compiler_params=pltpu.CompilerParams(dimension_semantics=("arbitrary",)),
        name="cast",
    )(w)


def _mod_kernel(c_ref, w_ref, b_ref, o_ref):
    s = jax.nn.silu(c_ref[...]).astype(BF16)
    o_ref[...] = _dot(s, w_ref[...].astype(BF16)) + b_ref[...]


def _mod_call(c, w_mod, b_mod):
    bsz = c.shape[0]
    rows = 8
    cp = jnp.pad(c, ((0, rows - bsz), (0, 0)))
    n = w_mod.shape[1]
    bn = 1024
    out = pl.pallas_call(
        _mod_kernel,
        grid=(n // bn,),
        in_specs=[
            pl.BlockSpec((rows, D_MODEL), lambda i: (0, 0)),
            pl.BlockSpec((D_MODEL, bn), lambda i: (0, i)),
            pl.BlockSpec((1, bn), lambda i: (0, i)),
        ],
        out_specs=pl.BlockSpec((rows, bn), lambda i: (0, i)),
        out_shape=jax.ShapeDtypeStruct((rows, n), F32),
        compiler_params=pltpu.CompilerParams(dimension_semantics=("arbitrary",)),
        name="mod",
    )(cp, w_mod, b_mod.reshape(1, n))
    return out[:bsz].reshape(bsz, N_MOD, D_MODEL)


MXU_COLS = 256
CONV_ROWS = 64


def _inproj_kernel(x_ref, mod_ref, g_ref, w_ref, cw_ref, cb_ref,
                   qkva_ref, qc_ref, kct_ref, vm_ref, om_ref, gat_ref, gt_ref,
                   raw_scr, new_scr, k_scr, carry_scr, *, nblk):
    j = pl.program_id(0)

    @pl.when(j == 0)
    def _init():
        raw_scr[...] = jnp.zeros_like(raw_scr)
        carry_scr[...] = jnp.zeros_like(carry_scr)

    x = x_ref[...]
    h = _rms(x, g_ref[...]) * (1.0 + mod_ref[1:2, :]) + mod_ref[0:1, :]
    hb = h.astype(BF16)
    tm = x.shape[0]
    nchunk = tm // CHUNK
    within = (j + nblk - 1) % nblk

    anchors = []

    def zero_after(a):
        return jnp.concatenate([_zero_after(a)] * (2 * M_W // LANES), axis=1)

    def project():
        order = list(range(C_QKM, C_VM, MXU_COLS)) + list(range(0, C_QKM, MXU_COLS)) + \
            list(range(C_VM, IN_COLS_PAD, MXU_COLS))
        for c0 in order:
            c1 = min(c0 + MXU_COLS, IN_COLS_PAD)
            res = _dot(hb, w_ref[:, c0:c1])
            anchors.append(res[0:1, 0:LANES])
            if c0 < C_KA:
                qkva_ref[:, c0:c1] = (res * (HEAD_DIM_ATT ** -0.5)).astype(BF16)
            elif c0 < C_QKM:
                ka, va = res[:, :ATT_KV], res[:, ATT_KV:]
                half = HEAD_DIM_ATT
                qkva_ref[:, ATT_Q:ATT_Q + ATT_KV] = ka.astype(BF16)
                qkva_ref[:, ATT_Q + ATT_KV:ATT_Q + 2 * ATT_KV] = pltpu.roll(ka, half, axis=1).astype(BF16)
                qkva_ref[:, ATT_Q + 2 * ATT_KV:ATT_Q + 3 * ATT_KV] = va.astype(BF16)
                qkva_ref[:, ATT_Q + 3 * ATT_KV:ATT_Q + 4 * ATT_KV] = pltpu.roll(va, half, axis=1).astype(BF16)
            elif c0 < C_VM:
                new_scr[:, c0 - C_QKM:c1 - C_QKM] = res
            elif c0 < C_OM:
                vm_ref[:, c0 - C_VM:c1 - C_VM] = res.astype(BF16)
            elif c0 < C_G:
                om_ref[:, c0 - C_OM:c1 - C_OM] = res
            else:
                gat_ref[...] = res
                for g in range(nchunk):
                    blk = res[g * CHUNK:(g + 1) * CHUNK, :].T
                    gt_ref[:, g * CHUNK:(g + 1) * CHUNK] = blk[0:N_GATES, :]
            yield

    def conv():
        R = CONV_ROWS
        row = lax.broadcasted_iota(jnp.int32, (R, 1), 0)
        for p in range(tm // R):
            r0 = p * R
            xg = raw_scr[r0:r0 + R, :]
            if p == 0:
                prev_row = jnp.where(within == 0, 0.0, carry_scr[0:1, :])
            else:
                prev_row = raw_scr[r0 - 1:r0, :]
            if r0 + R == tm:
                next_row = jnp.where(within == nblk - 1, 0.0, new_scr[0:1, :])
            else:
                next_row = raw_scr[r0 + R:r0 + R + 1, :]
            xm1 = jnp.where(row == 0, prev_row, pltpu.roll(xg, 1, axis=0))
            xp1 = jnp.where(row == R - 1, next_row, pltpu.roll(xg, R - 1, axis=0))
            z = zero_after(anchors[-1])
            y =xm1 * (cw_ref[0:1, :] + z) + xg * (cw_ref[1:2, :] + z) + xp1 * (cw_ref[2:3, :] + z) \
                + cb_ref[...]
            y = jax.nn.silu(y)
            qc_ref[r0:r0 + R, :] = (y[:, :M_W] * (HEAD_DIM_M ** -0.5)).astype(BF16)
            k_scr[r0:r0 + R, :] = y[:, M_W:]
            yield
            if (r0 + R) % CHUNK == 0:
                g = r0 // CHUNK
                for hd in range(N_HEADS_M):
                    blk = k_scr[g * CHUNK:(g + 1) * CHUNK, hd * HEAD_DIM_M:(hd + 1) * HEAD_DIM_M]
                    base = (g * N_HEADS_M + hd) * HEAD_DIM_M
                    kct_ref[base:base + HEAD_DIM_M, :] = blk.T
                yield

    _alternate(project(), conv())
    carry_scr[0:1, :] = raw_scr[tm - 1:tm, :]
    raw_scr[...] = new_scr[...]


def _inproj_call(x, mod, g1, w_in_p, conv_w, conv_b, tm):
    bsz, s, d = x.shape
    nblk = s // tm
    ntiles = bsz * nblk
    outs = (
        jax.ShapeDtypeStruct((bsz, s, QKVA_W), BF16),
        jax.ShapeDtypeStruct((bsz, s, M_W), BF16),
        jax.ShapeDtypeStruct((bsz, s * N_HEADS_M, HEAD_DIM_M), F32),
        jax.ShapeDtypeStruct((bsz, s, M_W), BF16),
        jax.ShapeDtypeStruct((bsz, s, M_W), F32),
        jax.ShapeDtypeStruct((bsz, s, GATE_PAD), F32),
        jax.ShapeDtypeStruct((bsz, N_GATES, s), F32),
    )

    def cur(j):
        t = jnp.minimum(j, ntiles - 1)
        return t // nblk, t % nblk

    def old(j):
        t = jnp.maximum(j - 1, 0)
        return t // nblk, t % nblk

    cur_tile = lambda w: pl.BlockSpec((None, tm, w), lambda j: (*cur(j), 0))
    old_tile = lambda w: pl.BlockSpec((None, tm, w), lambda j: (*old(j), 0))
    const = lambda shp: pl.BlockSpec(shp, lambda j: (0, 0))
    return pl.pallas_call(
        functools.partial(_inproj_kernel, nblk=nblk),
        grid=(ntiles + 1,),
        in_specs=[
            cur_tile(d),
            pl.BlockSpec((None, N_MOD, d), lambda j: (cur(j)[0], 0, 0)),
            const((1, d)),
            pl.BlockSpec((d, IN_COLS_PAD), lambda j: (0, 0), pipeline_mode=pl.Buffered(1)),
            const((3, 2 * M_W)), const((1, 2 * M_W)),
        ],
        out_specs=[cur_tile(QKVA_W), old_tile(M_W),
                   pl.BlockSpec((None, tm * N_HEADS_M, HEAD_DIM_M), lambda j: (*old(j), 0)),
                   cur_tile(M_W), cur_tile(M_W), cur_tile(GATE_PAD),
                   pl.BlockSpec((None, N_GATES, tm), lambda j: (cur(j)[0], 0, cur(j)[1]))],
        out_shape=outs,
        scratch_shapes=[pltpu.VMEM((tm, 2 * M_W), F32), pltpu.VMEM((tm, 2 * M_W), F32),
                        pltpu.VMEM((tm, M_W), F32), pltpu.VMEM((8, 2 * M_W), F32)],
        compiler_params=pltpu.CompilerParams(
            dimension_semantics=("arbitrary",), vmem_limit_bytes=VMEM_LIMIT),
        name="inproj",
    )(x, mod, g1, w_in_p, conv_w, conv_b)


def _mlstm_reset(first, c_scr, m_scr):
    @pl.when(first)
    def _init():
        c_scr[...] = jnp.zeros_like(c_scr)
        m_scr[...] = jnp.zeros_like(m_scr)


def _mlstm_stream(q_ref, kt_ref, v_ref, gat_ref, gt_ref, bg_ref, bgr_ref, h_ref, c_scr, m_scr,
                  *, reverse, nchunk):
    L = CHUNK
    ti = lax.broadcasted_iota(jnp.int32, (L, L), 0)
    si = lax.broadcasted_iota(jnp.int32, (L, L), 1)
    causal = (si >= ti) if reverse else (si <= ti)
    tri = jnp.where(causal, 1.0, 0.0).astype(BF16)
    tri_t = jnp.where((ti >= si) if reverse else (ti <= si), 1.0, 0.0).astype(BF16)
    lane = lax.broadcasted_iota(jnp.int32, (1, L), 1)
    last = 0 if reverse else L - 1
    i_off, f_off = (2 * N_HEADS_M, 3 * N_HEADS_M) if reverse else (0, N_HEADS_M)
    ones = jnp.ones((L, HEAD_DIM_M), BF16)
    order = list(range(nchunk - 1, -1, -1) if reverse else range(nchunk))
    rows_of = lambda g: slice(g * L, (g + 1) * L)

    def split3(a):
        hi = a.astype(BF16)
        r1 = a - hi.astype(F32)
        mid = r1.astype(BF16)
        return hi, mid, (r1 - mid.astype(F32)).astype(BF16)

    gates = {}
    for g in order:
        hi, mid, lo = split3(jax.nn.log_sigmoid(gat_ref[rows_of(g), :] + bg_ref[...]))
        bcum = _dot(tri, hi) + _dot(tri, mid) + _dot(tri, lo)
        gr = gt_ref[:, rows_of(g)] + bgr_ref[...]
        parts = _dot(jnp.concatenate(split3(jax.nn.log_sigmoid(gr)), axis=0), tri_t)
        bcr = parts[0:N_GATES] + parts[N_GATES:2 * N_GATES] + parts[2 * N_GATES:3 * N_GATES]
        gates[g] = (gr, bcr, bcum)
        yield

    def head(g, h):
        rows = rows_of(g)
        cols = slice(h * HEAD_DIM_M, (h + 1) * HEAD_DIM_M)
        gct, bct, bcum = gates[g]
        r_row = gct[i_off + h:i_off + h + 1, :] - bct[f_off + h:f_off + h + 1, :]
        btot = jnp.sum(jnp.where(lane == last, bct[f_off + h:f_off + h + 1, :], 0.0),
                       axis=1, keepdims=True)
        rmax = jnp.max(r_row, axis=1, keepdims=True)
        a_max = btot + rmax
        m_prev = m_scr[h:h + 1, 0:1]
        m_new = jnp.maximum(btot + m_prev, a_max)
        m_scr[h:h + 1, :] = jnp.broadcast_to(m_new, (1, LANES))
        decay = jnp.exp(btot + m_prev - m_new)
        w_row = jnp.exp(r_row - rmax) * jnp.exp(a_max - m_new)
        kt = kt_ref[(g * N_HEADS_M + h) * HEAD_DIM_M:(g * N_HEADS_M + h + 1) * HEAD_DIM_M, :]
        rm = jnp.where(causal, r_row, -jnp.inf)
        cm = jnp.max(rm, axis=1, keepdims=True)
        yield
        q = q_ref[rows, cols]
        vaug = jnp.concatenate([v_ref[rows, cols], ones], axis=1)
        s = _dot(q, kt.astype(BF16))
        upd = _dot((kt * w_row).astype(BF16), vaug)
        u = jnp.maximum(cm, m_prev)
        ub = jnp.broadcast_to(u, (L, L))
        dm = jnp.exp(rm - ub)
        qi = (q.astype(F32) * jnp.exp(m_prev - ub)).astype(BF16)
        floor = jnp.exp(-(jnp.broadcast_to(bcum[:, f_off + h:f_off + h + 1], (L, L)) + ub))
        yield
        cprev = c_scr[h]
        lhs = jnp.concatenate([(s * dm).astype(BF16), qi], axis=1)
        rhs = jnp.concatenate([vaug, cprev.astype(BF16)], axis=0)
        out = _dot(lhs, rhs)
        c_scr[h] = decay * cprev + upd
        yield
        h_ref[rows, cols] = out[:, :HEAD_DIM_M] / jnp.maximum(jnp.abs(out[:, HEAD_DIM_M:]), floor)
        yield

    n_stage, lag = 4, 2
    heads = {}
    for slot in range(lag * (nchunk - 1) + n_stage):
        for i, g in enumerate(order):
            stage = slot - lag * i
            if 0 <= stage < n_stage:
                for h in range(N_HEADS_M):
                    if stage == 0:
                        heads[g, h] = head(g, h)
                    next(heads[g, h])
                    yield


def _mlstm_kernel(qf_ref, kf_ref, vf_ref, gf_ref, gtf_ref, qb_ref, kb_ref, vb_ref, gb_ref, gtb_ref,
                  bg_ref, bgr_ref, *rest, nchunk, ncast):
    w_refs, rest = rest[:ncast], rest[ncast:]
    hf_ref, hb_ref = rest[:2]
    wo_refs, (cf_scr, mf_scr, cb_scr, mb_scr) = rest[2:2 + ncast], rest[2 + ncast:]
    j = pl.program_id(1)
    _mlstm_reset(j == 0, cf_scr, mf_scr)
    _mlstm_reset(j == 0, cb_scr, mb_scr)
    for w_ref, wo_ref in zip(w_refs, wo_refs):
        wo_ref[...] = w_ref[...].astype(wo_ref.dtype)
    fwd = _mlstm_stream(qf_ref, kf_ref, vf_ref, gf_ref, gtf_ref, bg_ref, bgr_ref, hf_ref, cf_scr, mf_scr,
                        reverse=False, nchunk=nchunk)
    bwd = _mlstm_stream(qb_ref, kb_ref, vb_ref, gb_ref, gtb_ref, bg_ref, bgr_ref, hb_ref, cb_scr, mb_scr,
                        reverse=True, nchunk=nchunk)
    _alternate(fwd, bwd)


def _mlstm_call(qc, kct, vm, gates, gates_t, bg_pad, bg_rows, weights, layer, tm):
    bsz, s, _ = qc.shape
    nblk = s // tm
    nsteps = bsz * nblk
    w_specs, w_shapes = [], []
    for w in weights:
        _, k, n = w.shape
        rows = k // nsteps
        assert rows * nsteps == k and rows % 16 == 0
        w_specs.append(pl.BlockSpec((None, rows, n), lambda b, j: (layer, b * nblk + j, 0)))
        w_shapes.append(jax.ShapeDtypeStruct((k, n), BF16))
    wo_specs = [pl.BlockSpec((sp.block_shape[1], sp.block_shape[2]), lambda b, j: (b * nblk + j, 0))
                for sp in w_specs]

    def specs(pos):
        tile = lambda w: pl.BlockSpec((None, tm, w), lambda b, j: (b, pos(j), 0))
        kt_spec = pl.BlockSpec((None, tm * N_HEADS_M, HEAD_DIM_M), lambda b, j: (b, pos(j), 0))
        gt_spec = pl.BlockSpec((None, N_GATES, tm), lambda b, j: (b, 0, pos(j)))
        return [tile(M_W), kt_spec, tile(M_W), tile(GATE_PAD), gt_spec], tile(M_W)

    in_f, out_f = specs(lambda j: j)
    in_b, out_b = specs(lambda j: nblk - 1 - j)
    state = [pltpu.VMEM((N_HEADS_M, HEAD_DIM_M, 2 * HEAD_DIM_M), F32), pltpu.VMEM((8, LANES), F32)]
    outs = pl.pallas_call(
        functools.partial(_mlstm_kernel, nchunk=tm // CHUNK, ncast=len(weights)),
        grid=(bsz, nblk),
        in_specs=in_f + in_b + [pl.BlockSpec((1, GATE_PAD), lambda b, j: (0, 0)),
                                pl.BlockSpec((N_GATES, LANES), lambda b, j: (0, 0))] + w_specs,
        out_specs=[out_f, out_b] + wo_specs,
        out_shape=[jax.ShapeDtypeStruct((bsz, s, M_W), F32)] * 2 + w_shapes,
        scratch_shapes=state + state,
        compiler_params=pltpu.CompilerParams(
            dimension_semantics=("arbitrary", "arbitrary"), vmem_limit_bytes=VMEM_LIMIT),
        name="mlstm",
    )(qc, kct, vm, gates, gates_t, qc, kct, vm, gates, gates_t, bg_pad, bg_rows, *weights)
    return outs[0], outs[1], outs[2:]


def _zero_after(a):
    bits = pltpu.bitcast(a, jnp.uint32)
    z = lax.shift_right_logical(lax.shift_right_logical(bits, jnp.uint32(16)), jnp.uint32(16))
    return pltpu.bitcast(z, F32)


def _attn_bias_init(bias_scr):
    nk = 3 * BLOCK
    row = lax.broadcasted_iota(jnp.int32, (BLOCK, nk), 0)
    col = lax.broadcasted_iota(jnp.int32, (BLOCK, nk), 1)
    dist = jnp.abs(col - BLOCK - row)
    distf = dist.astype(F32)
    for var in range(3):
        ok = dist <= WINDOW
        if var == 1:
            ok = ok & (col >= BLOCK)
        elif var == 2:
            ok = ok & (col < 2 * BLOCK)
        for h in range(N_HEADS_ATT):
            slope = 2.0 ** (-8.0 * (h + 1.0) / N_HEADS_ATT)
            bias_scr[var * N_HEADS_ATT + h] = jnp.where(ok, -slope * distf, -jnp.inf)


def _attn_pieces(first, last, sink_ref, q_ref, kvp_ref, kvn_ref, g_ref, o_ref, bias_scr, anchors,
                 *, nsub):
    nk = 3 * BLOCK

    lane_k = lax.broadcasted_iota(jnp.int32, (nk, LANES), 1)
    ones_a = jnp.where(lane_k < HEAD_DIM_ATT, 1.0, 0.0).astype(BF16)
    ones_b = jnp.where(lane_k < HEAD_DIM_ATT, 0.0, 1.0).astype(BF16)
    lo_half_q = lax.broadcasted_iota(jnp.int32, (BLOCK, LANES), 1) < HEAD_DIM_ATT

    def kv_block(idx):
        if idx < 0:
            return kvp_ref[...]
        if idx >= nsub:
            return kvn_ref[...]
        return q_ref[idx * BLOCK:(idx + 1) * BLOCK, ATT_Q:QKVA_W]

    for n in range(nsub):
        rows = slice(n * BLOCK, (n + 1) * BLOCK)
        kv = jnp.concatenate([kv_block(n - 1), kv_block(n), kv_block(n + 1)], axis=0)
        if n == 0:
            var = jnp.where(first, 1, 0)
        elif n == nsub - 1:
            var = jnp.where(last, 2, 0)
        else:
            var = 0
        pieces = []
        for kvh in range(N_KV_HEADS):
            k_st, k_sw = kv[:, 0:LANES], kv[:, LANES:2 * LANES]
            v_st, v_sw = kv[:, 2 * LANES:3 * LANES], kv[:, 3 * LANES:4 * LANES]
            if kvh == 0:
                k_lo, k_hi, v_lo, v_hi = k_st, k_sw, v_st, v_sw
            else:
                k_lo, k_hi, v_lo, v_hi = k_sw, k_st, v_sw, v_st
            kk = jnp.concatenate([k_lo * ones_a, k_hi * ones_b], axis=0)
            vv = jnp.concatenate([
                jnp.concatenate([v_lo * ones_a, ones_a], axis=1),
                jnp.concatenate([v_hi * ones_b, ones_b], axis=1)], axis=0)
            for pair in range(GROUP_SIZE // 2):
                h0 = kvh * GROUP_SIZE + 2 * pair
                qp = q_ref[rows, h0 * HEAD_DIM_ATT:(h0 + 2) * HEAD_DIM_ATT]
                s2 = _dot_nt(qp, kk)
                tie = _zero_after(anchors[-1])[:, 0:1] if anchors else 0.0
                ps, es = [], []
                for t in range(2):
                    logits = s2[:, t * nk:(t + 1) * nk] + bias_scr[var * N_HEADS_ATT + h0 + t]
                    sink = sink_ref[h0 + t] + tie
                    mx = jnp.maximum(jnp.max(logits, axis=-1, keepdims=True), sink)
                    ps.append(jnp.exp(logits - mx).astype(BF16))
                    es.append(jnp.exp(sink - mx))
                res = _dot(jnp.concatenate(ps, axis=1), vv)
                den = res[:, LANES:] + jnp.where(lo_half_q, es[0], es[1])
                pieces.append(res[:, :LANES] / den)
                yield
        att = jnp.concatenate(pieces, axis=1)
        o_ref[rows, :] = _rms(att, g_ref[...]).astype(o_ref.dtype)
        yield


def _outffn_pieces(x_ref, att_ref, hf_ref, hb_ref, om_ref, mod_ref, gm_ref, wo_ref, g2_ref,
                   w1_ref, w2_ref, gf_ref, o_ref, hid_scr, anchors, *, final):
    hs = hf_ref[...] + hb_ref[...]
    parts = []
    for h in range(N_HEADS_M):
        cols = slice(h * HEAD_DIM_M, (h + 1) * HEAD_DIM_M)
        parts.append(_rms(hs[:, cols], gm_ref[:, cols]))
    hm = jax.nn.sigmoid(om_ref[...]) * jnp.concatenate(parts, axis=1)
    mixin = jnp.concatenate([att_ref[...], hm.astype(BF16)], axis=1)
    mix = _dot(mixin, wo_ref[...])
    anchors.append(mix[0:1, 0:LANES])
    x1 = x_ref[...] + mod_ref[2:3, :] * mix
    hff = (_rms(x1, g2_ref[...]) * (1.0 + mod_ref[4:5, :]) + mod_ref[3:4, :]).astype(BF16)
    yield
    for c in range(N_FF_CHUNKS):
        gate = _dot(hff, w1_ref[:, FF_CHUNK * c:FF_CHUNK * (c + 1)])
        up = _dot(hff, w1_ref[:, D_FF + FF_CHUNK * c:D_FF + FF_CHUNK * (c + 1)])
        anchors.append(gate[0:1, 0:LANES])
        hid_scr[:, FF_CHUNK * c:FF_CHUNK * (c + 1)] = (jax.nn.silu(gate) * up).astype(BF16)
        yield
    ff = _dot(hid_scr[...], w2_ref[...])
    anchors.append(ff[0:1, 0:LANES])
    x2 = x1 + mod_ref[5:6, :] * ff
    if final:
        x2 = _rms(x2, gf_ref[...])
    o_ref[...] = x2
    yield


N_OUTFFN_PIECES = N_FF_CHUNKS + 2


def _attn_outffn_kernel(sink_ref, q_ref, kvp_ref, kvn_ref, ga_ref,
                        x_ref, hf_ref, hb_ref, om_ref, mod_ref, gm_ref, wo_ref, g2_ref,
                        w1_ref, w2_ref, gf_ref, o_ref, bias_scr, att_scr, hid_scr,
                        *, nt, ntiles, nsub, final):
    j = pl.program_id(0)

    @pl.when(j == 0)
    def _init():
        _attn_bias_init(bias_scr)
        att_scr[...] = jnp.zeros_like(att_scr)

    within = jnp.minimum(j, ntiles - 1) % nt
    anchors = []
    ffn = _outffn_pieces(x_ref, att_scr, hf_ref, hb_ref, om_ref, mod_ref, gm_ref, wo_ref, g2_ref,
                         w1_ref, w2_ref, gf_ref, o_ref, hid_scr, anchors, final=final)
    att = _attn_pieces(within == 0, within == nt - 1, sink_ref, q_ref, kvp_ref, kvn_ref, ga_ref,
                       att_scr, bias_scr, anchors, nsub=nsub)
    n_att = nsub * (N_KV_HEADS * GROUP_SIZE // 2 + 1)
    done = 0
    for k in range(N_OUTFFN_PIECES):
        next(ffn)
        while done * N_OUTFFN_PIECES < (k + 1) * n_att:
            next(att)
            done += 1


def _attn_outffn_call(sink, qkva, g_attn, x, hf, hb, om, mod, g_m, w_out, g2, w1, w2, g_final, tm, final):
    bsz, s, d = x.shape
    nt = s // tm
    ntiles = bsz * nt
    nsub = tm // BLOCK
    nb = s // BLOCK
    kvw = QKVA_W - ATT_Q

    def cur(j):
        t = jnp.minimum(j, ntiles - 1)
        return t // nt, t % nt

    def old(j):
        t = jnp.maximum(j - 1, 0)
        return t // nt, t % nt

    old_tile = lambda w: pl.BlockSpec((None, tm, w), lambda j: (*old(j), 0))
    const = lambda shp: pl.BlockSpec(shp, lambda j: (0, 0))
    weight = lambda shp: pl.BlockSpec(shp, lambda j: (0, 0), pipeline_mode=pl.Buffered(1))
    return pl.pallas_call(
        functools.partial(_attn_outffn_kernel, nt=nt, ntiles=ntiles, nsub=nsub, final=final),
        grid=(ntiles + 1,),
        in_specs=[
            pl.BlockSpec(memory_space=pltpu.SMEM),
            pl.BlockSpec((None, tm, QKVA_W), lambda j: (*cur(j), 0)),
            pl.BlockSpec((None, BLOCK, kvw),
                         lambda j: (cur(j)[0], jnp.maximum(cur(j)[1] * nsub - 1, 0), 1)),
            pl.BlockSpec((None, BLOCK, kvw),
                         lambda j: (cur(j)[0], jnp.minimum((cur(j)[1] + 1) * nsub, nb - 1), 1)),
            const((1, ATT_Q)),
            old_tile(d), old_tile(M_W), old_tile(M_W), old_tile(M_W),
            pl.BlockSpec((None, N_MOD, d), lambda j: (old(j)[0], 0, 0)),
            const((1, M_W)), weight((ATT_Q + M_W, d)), const((1, d)),
            weight((d, 2 * D_FF)), weight((D_FF, d)), const((1, d)),
        ],
        out_specs=old_tile(d),
        out_shape=jax.ShapeDtypeStruct((bsz, s, d), F32),
        scratch_shapes=[pltpu.VMEM((3 * N_HEADS_ATT, BLOCK, 3 * BLOCK), F32),
                        pltpu.VMEM((tm, ATT_Q), BF16), pltpu.VMEM((tm, D_FF), BF16)],
        compiler_params=pltpu.CompilerParams(
            dimension_semantics=("arbitrary",), vmem_limit_bytes=VMEM_LIMIT),
        name="attn_outffn",
    )(sink, qkva, qkva, qkva, g_attn, x, hf, hb, om, mod, g_m, w_out, g2, w1, w2, g_final)


def _layer(l, x, c, w_mod, b_mod, g_norm1, w_in, conv_w, conv_b, b_gates, sink,
           g_attn_out, g_mlstm_out, w_out, g_norm2, w_ffn_in, w_ffn_out, g_final, final):
    d = x.shape[-1]
    mod = _mod_call(c, w_mod[l], b_mod[l])

    w_in_p = _cast_call(w_in, l, IN_COLS_PAD)
    qkva, qc, kct, vm, om, gates, gates_t = _inproj_call(
        x, mod, g_norm1[l].reshape(1, d), w_in_p, conv_w[l], conv_b[l].reshape(1, 2 * M_W), tm=512)

    bg_pad = jnp.pad(b_gates[l], (0, GATE_PAD - N_GATES)).reshape(1, GATE_PAD)
    bg_rows = jnp.broadcast_to(b_gates[l][:, None], (N_GATES, LANES))
    hf, hb, (w_out_b, w_ffn_in_b, w_ffn_out_b) = _mlstm_call(
        qc, kct, vm, gates, gates_t, bg_pad, bg_rows, (w_out, w_ffn_in, w_ffn_out), l, tm=1024)

    return _attn_outffn_call(sink[l], qkva, g_attn_out[l].reshape(1, ATT_Q), x, hf, hb, om, mod,
                             g_mlstm_out[l].reshape(1, M_W), w_out_b, g_norm2[l].reshape(1, d),
                             w_ffn_in_b, w_ffn_out_b, g_final.reshape(1, d), tm=512, final=final)


def kernel(x, c, w_mod, b_mod, g_norm1, w_in, conv_w, conv_b, b_gates, sink, g_attn_out,
           g_mlstm_out, w_out, g_norm2, w_ffn_in, w_ffn_out, g_final):
    depth = w_mod.shape[0]
    for l in range(depth):
        x = _layer(l, x, c, w_mod, b_mod, g_norm1, w_in, conv_w, conv_b, b_gates, sink, g_attn_out,
                   g_mlstm_out, w_out, g_norm2, w_ffn_in, w_ffn_out, g_final, final=(l == depth - 1))
    return x
```

```python
import functools

import jax
import jax.numpy as jnp
from jax import lax
from jax.experimental import pallas as pl
from jax.experimental.pallas import tpu as pltpu

F32 = jnp.float32
BF16 = jnp.bfloat16

D_MODEL = 1024
EPS = 1e-6
N_HEADS_ATT = 8
N_KV_HEADS = 2
HEAD_DIM_ATT = 64
GROUP_SIZE = N_HEADS_ATT // N_KV_HEADS
WINDOW = 128
BLOCK = 128
N_HEADS_M = 4
HEAD_DIM_M = 128
CHUNK = 128
ATT_Q = N_HEADS_ATT * HEAD_DIM_ATT
ATT_KV = N_KV_HEADS * HEAD_DIM_ATT
M_W = N_HEADS_M * HEAD_DIM_M
N_GATES = 4 * N_HEADS_M
D_FF = 2816
N_MOD = 6

LANES = 128
GATE_PAD = LANES
FF_CHUNK = 256
N_FF_CHUNKS = D_FF // FF_CHUNK
OUT_COL_BLOCK = 256
VMEM_LIMIT = 56 * 1024 * 1024

C_QA = 0
C_KA = ATT_Q
C_VA = ATT_Q + ATT_KV
C_QKM = ATT_Q + 2 * ATT_KV
C_VM = C_QKM + 2 * M_W
C_OM = C_VM + M_W
C_G = C_OM + M_W
IN_COLS_PAD = C_G + GATE_PAD
QKVA_W = ATT_Q + 4 * ATT_KV


def _dot(a, b):
    return jnp.dot(a, b, preferred_element_type=F32)


def _dot_nt(a, b):
    return lax.dot_general(a, b, (((1,), (1,)), ((), ())), preferred_element_type=F32)


def _rms(x, g):
    return x * lax.rsqrt(jnp.mean(x * x, axis=-1, keepdims=True) + EPS) * g


def _alternate(*streams):
    live = list(streams)
    while live:
        for s in list(live):
            if next(s, StopIteration) is StopIteration:
                live.remove(s)


def _cast_kernel(w_ref, o_ref):
    n = w_ref.shape[1]
    o_ref[:, :n] = w_ref[...].astype(o_ref.dtype)
    if o_ref.shape[1] > n:
        o_ref[:, n:] = jnp.zeros((o_ref.shape[0], o_ref.shape[1] - n), o_ref.dtype)


def _cast_call(w, layer, n_out=None, bm=256):
    _, k, n = w.shape
    n_out = n if n_out is None else n_out
    return pl.pallas_call(
        _cast_kernel,
        grid=(k // bm,),
        in_specs=[pl.BlockSpec((None, bm, n), lambda i: (layer, i, 0))],
        out_specs=pl.BlockSpec((bm, n_out), lambda i: (i, 0)),
        out_shape=jax.ShapeDtypeStruct((k, n_out), BF16),
        compiler_params=pltpu.CompilerParams(dimension_semantics=("arbitrary",)),
        name="cast",
    )(w)


def _mod_kernel(c_ref, w_ref, b_ref, o_ref):
    s = jax.nn.silu(c_ref[...]).astype(BF16)
    o_ref[...] = _dot(s, w_ref[...].astype(BF16)) + b_ref[...]


def _mod_call(c, w_mod, b_mod):
    bsz = c.shape[0]
    rows = 8
    cp = jnp.pad(c, ((0, rows - bsz), (0, 0)))
    n = w_mod.shape[1]
    bn = 1024
    out = pl.pallas_call(
        _mod_kernel,
        grid=(n // bn,),
        in_specs=[
            pl.BlockSpec((rows, D_MODEL), lambda i: (0, 0)),
            pl.BlockSpec((D_MODEL, bn), lambda i: (0, i)),
            pl.BlockSpec((1, bn), lambda i: (0, i)),
        ],
        out_specs=pl.BlockSpec((rows, bn), lambda i: (0, i)),
        out_shape=jax.ShapeDtypeStruct((rows, n), F32),
        compiler_params=pltpu.CompilerParams(dimension_semantics=("arbitrary",)),
        name="mod",
    )(cp, w_mod, b_mod.reshape(1, n))
    return out[:bsz].reshape(bsz, N_MOD, D_MODEL)


MXU_COLS = 256
CONV_ROWS = 64


def _inproj_kernel(x_ref, mod_ref, g_ref, w_ref, cw_ref, cb_ref,
                   qkva_ref, qc_ref, kct_ref, vm_ref, om_ref, gt_ref,
                   raw_scr, new_scr, k_scr, carry_scr, *, nblk):
    j = pl.program_id(0)

    @pl.when(j == 0)
    def _init():
        raw_scr[...] = jnp.zeros_like(raw_scr)
        carry_scr[...] = jnp.zeros_like(carry_scr)

    x = x_ref[...]
    h = _rms(x, g_ref[...]) * (1.0 + mod_ref[1:2, :]) + mod_ref[0:1, :]
    hb = h.astype(BF16)
    tm = x.shape[0]
    nchunk = tm // CHUNK
    within = (j + nblk - 1) % nblk

    anchors = []

    def zero_after(a):
        return jnp.concatenate([_zero_after(a)] * (2 * M_W // LANES), axis=1)

    def project():
        order = list(range(C_QKM, C_VM, MXU_COLS)) + list(range(0, C_QKM, MXU_COLS)) + \
            list(range(C_VM, IN_COLS_PAD, MXU_COLS))
        for c0 in order:
            c1 = min(c0 + MXU_COLS, IN_COLS_PAD)
            res = _dot(hb, w_ref[:, c0:c1])
            anchors.append(res[0:1, 0:LANES])
            if c0 < C_KA:
                qkva_ref[:, c0:c1] = (res * (HEAD_DIM_ATT ** -0.5)).astype(BF16)
            elif c0 < C_QKM:
                ka, va = res[:, :ATT_KV], res[:, ATT_KV:]
                half = HEAD_DIM_ATT
                qkva_ref[:, ATT_Q:ATT_Q + ATT_KV] = ka.astype(BF16)
                qkva_ref[:, ATT_Q + ATT_KV:ATT_Q + 2 * ATT_KV] = pltpu.roll(ka, half, axis=1).astype(BF16)
                qkva_ref[:, ATT_Q + 2 * ATT_KV:ATT_Q + 3 * ATT_KV] = va.astype(BF16)
                qkva_ref[:, ATT_Q + 3 * ATT_KV:ATT_Q + 4 * ATT_KV] = pltpu.roll(va, half, axis=1).astype(BF16)
            elif c0 < C_VM:
                new_scr[:, c0 - C_QKM:c1 - C_QKM] = res
            elif c0 < C_OM:
                vm_ref[:, c0 - C_VM:c1 - C_VM] = res.astype(BF16)
            elif c0 < C_G:
                om_ref[:, c0 - C_OM:c1 - C_OM] = res
            else:
                for g in range(nchunk):
                    blk = res[g * CHUNK:(g + 1) * CHUNK, :].T
                    gt_ref[:, g * CHUNK:(g + 1) * CHUNK] = blk[0:N_GATES, :]
            yield

    def conv():
        R = CONV_ROWS
        row = lax.broadcasted_iota(jnp.int32, (R, 1), 0)
        for p in range(tm // R):
            r0 = p * R
            xg = raw_scr[r0:r0 + R, :]
            if p == 0:
                prev_row = jnp.where(within == 0, 0.0, carry_scr[0:1, :])
            else:
                prev_row = raw_scr[r0 - 1:r0, :]
            if r0 + R == tm:
                next_row = jnp.where(within == nblk - 1, 0.0, new_scr[0:1, :])
            else:
                next_row = raw_scr[r0 + R:r0 + R + 1, :]
            xm1 = jnp.where(row == 0, prev_row, pltpu.roll(xg, 1, axis=0))
            xp1 = jnp.where(row == R - 1, next_row, pltpu.roll(xg, R - 1, axis=0))
            z = zero_after(anchors[-1])
            y =xm1 * (cw_ref[0:1, :] + z) + xg * (cw_ref[1:2, :] + z) + xp1 * (cw_ref[2:3, :] + z) \
                + cb_ref[...]
            y = jax.nn.silu(y)
            qc_ref[r0:r0 + R, :] = (y[:, :M_W] * (HEAD_DIM_M ** -0.5)).astype(BF16)
            k_scr[r0:r0 + R, :] = y[:, M_W:]
            yield
            if (r0 + R) % CHUNK == 0:
                g = r0 // CHUNK
                for hd in range(N_HEADS_M):
                    blk = k_scr[g * CHUNK:(g + 1) * CHUNK, hd * HEAD_DIM_M:(hd + 1) * HEAD_DIM_M]
                    base = (g * N_HEADS_M + hd) * HEAD_DIM_M
                    kct_ref[base:base + HEAD_DIM_M, :] = blk.T.astype(BF16)
                yield

    _alternate(project(), conv())
    carry_scr[0:1, :] = raw_scr[tm - 1:tm, :]
    raw_scr[...] = new_scr[...]


def _inproj_call(x, mod, g1, w_in_p, conv_w, conv_b, tm):
    bsz, s, d = x.shape
    nblk = s // tm
    ntiles = bsz * nblk
    outs = (
        jax.ShapeDtypeStruct((bsz, s, QKVA_W), BF16),
        jax.ShapeDtypeStruct((bsz, s, M_W), BF16),
        jax.ShapeDtypeStruct((bsz, s * N_HEADS_M, HEAD_DIM_M), BF16),
        jax.ShapeDtypeStruct((bsz, s, M_W), BF16),
        jax.ShapeDtypeStruct((bsz, s, M_W), F32),
        jax.ShapeDtypeStruct((bsz, N_GATES, s), F32),
    )

    def cur(j):
        t = jnp.minimum(j, ntiles - 1)
        return t // nblk, t % nblk

    def old(j):
        t = jnp.maximum(j - 1, 0)
        return t // nblk, t % nblk

    cur_tile = lambda w: pl.BlockSpec((None, tm, w), lambda j: (*cur(j), 0))
    old_tile = lambda w: pl.BlockSpec((None, tm, w), lambda j: (*old(j), 0))
    const = lambda shp: pl.BlockSpec(shp, lambda j: (0, 0))
    return pl.pallas_call(
        functools.partial(_inproj_kernel, nblk=nblk),
        grid=(ntiles + 1,),
        in_specs=[
            cur_tile(d),
            pl.BlockSpec((None, N_MOD, d), lambda j: (cur(j)[0], 0, 0)),
            const((1, d)),
            pl.BlockSpec((d, IN_COLS_PAD), lambda j: (0, 0), pipeline_mode=pl.Buffered(1)),
            const((3, 2 * M_W)), const((1, 2 * M_W)),
        ],
        out_specs=[cur_tile(QKVA_W), old_tile(M_W),
                   pl.BlockSpec((None, tm * N_HEADS_M, HEAD_DIM_M), lambda j: (*old(j), 0)),
                   cur_tile(M_W), cur_tile(M_W),
                   pl.BlockSpec((None, N_GATES, tm), lambda j: (cur(j)[0], 0, cur(j)[1]))],
        out_shape=outs,
        scratch_shapes=[pltpu.VMEM((tm, 2 * M_W), F32), pltpu.VMEM((tm, 2 * M_W), F32),
                        pltpu.VMEM((tm, M_W), F32), pltpu.VMEM((8, 2 * M_W), F32)],
        compiler_params=pltpu.CompilerParams(
            dimension_semantics=("arbitrary",), vmem_limit_bytes=VMEM_LIMIT),
        name="inproj",
    )(x, mod, g1, w_in_p, conv_w, conv_b)


def _mlstm_reset(first, c_scr, m_scr):
    @pl.when(first)
    def _init():
        c_scr[...] = jnp.zeros_like(c_scr)
        m_scr[...] = jnp.zeros_like(m_scr)


LOG2E = 1.4426950408889634


def _mlstm_stream(q_ref, kt_ref, v_ref, gt_ref, bgr_ref, h_ref, c_scr, m_scr, *, reverse, nchunk):
    L = CHUNK
    ti = lax.broadcasted_iota(jnp.int32, (L, L), 0)
    si = lax.broadcasted_iota(jnp.int32, (L, L), 1)
    causal = (si >= ti) if reverse else (si <= ti)
    tri_t = jnp.where((ti >= si) if reverse else (ti <= si), 1.0, 0.0).astype(BF16)
    lane = lax.broadcasted_iota(jnp.int32, (1, L), 1)
    last = 0 if reverse else L - 1
    i_off, f_off = (2 * N_HEADS_M, 3 * N_HEADS_M) if reverse else (0, N_HEADS_M)
    ones = jnp.ones((L, HEAD_DIM_M), BF16)
    order = list(range(nchunk - 1, -1, -1) if reverse else range(nchunk))
    rows_of = lambda g: slice(g * L, (g + 1) * L)

    def split3(a):
        hi = a.astype(BF16)
        r1 = a - hi.astype(F32)
        mid = r1.astype(BF16)
        return hi, mid, (r1 - mid.astype(F32)).astype(BF16)

    gates = {}
    pad = jnp.zeros((L - N_GATES, L), F32)
    for g in order:
        gr = gt_ref[:, rows_of(g)] + bgr_ref[...]
        parts = _dot(jnp.concatenate(split3(jax.nn.log_sigmoid(gr)), axis=0), tri_t)
        bcr = (parts[0:N_GATES] + parts[N_GATES:2 * N_GATES] + parts[2 * N_GATES:3 * N_GATES]) * LOG2E
        gates[g] = (gr * LOG2E, bcr, jnp.concatenate([-bcr, pad], axis=0).T)
        yield

    def head(g, h):
        rows = rows_of(g)
        cols = slice(h * HEAD_DIM_M, (h + 1) * HEAD_DIM_M)
        gct, bct, nbcum = gates[g]
        r_row = gct[i_off + h:i_off + h + 1, :] - bct[f_off + h:f_off + h + 1, :]
        btot = jnp.sum(jnp.where(lane == last, bct[f_off + h:f_off + h + 1, :], 0.0),
                       axis=1, keepdims=True)
        rmax = jnp.max(r_row, axis=1, keepdims=True)
        a_max = btot + rmax
        m_prev = m_scr[h:h + 1, 0:1]
        m_new = jnp.maximum(btot + m_prev, a_max)
        m_scr[h:h + 1, :] = jnp.broadcast_to(m_new, (1, LANES))
        decay = jnp.exp2(btot + m_prev - m_new)
        w_row = jnp.exp2(r_row - rmax) * jnp.exp2(a_max - m_new)
        kt = kt_ref[(g * N_HEADS_M + h) * HEAD_DIM_M:(g * N_HEADS_M + h + 1) * HEAD_DIM_M, :]
        rm = jnp.where(causal, r_row, -jnp.inf)
        cm = jnp.max(rm, axis=1, keepdims=True)
        yield
        q = q_ref[rows, cols]
        vaug = jnp.concatenate([v_ref[rows, cols], ones], axis=1)
        s = _dot(q, kt)
        upd = _dot(kt * w_row.astype(BF16), vaug)
        u = jnp.maximum(cm, m_prev)
        ub = jnp.broadcast_to(u, (L, L))
        dm = jnp.exp2(rm - ub)
        qi = q * jnp.exp2(m_prev - ub).astype(BF16)
        floor = jnp.exp2(jnp.broadcast_to(nbcum[:, f_off + h:f_off + h + 1], (L, L)) - ub)
        yield
        cprev = c_scr[h]
        lhs = jnp.concatenate([(s * dm).astype(BF16), qi], axis=1)
        rhs = jnp.concatenate([vaug, cprev.astype(BF16)], axis=0)
        out = _dot(lhs, rhs)
        c_scr[h] = decay * cprev + upd
        yield
        h_ref[rows, cols] = out[:, :HEAD_DIM_M] / jnp.maximum(jnp.abs(out[:, HEAD_DIM_M:]), floor)
        yield

    n_stage, lag = 4, 2
    heads = {}
    for slot in range(lag * (nchunk - 1) + n_stage):
        for i, g in enumerate(order):
            stage = slot - lag * i
            if 0 <= stage < n_stage:
                for h in range(N_HEADS_M):
                    if stage == 0:
                        heads[g, h] = head(g, h)
                    next(heads[g, h])
                    yield


def _mlstm_kernel(qf_ref, kf_ref, vf_ref, gtf_ref, qb_ref, kb_ref, vb_ref, gtb_ref, bgr_ref, *rest,
                  nchunk, ncast):
    w_refs, rest = rest[:ncast], rest[ncast:]
    hf_ref, hb_ref = rest[:2]
    wo_refs, (cf_scr, mf_scr, cb_scr, mb_scr) = rest[2:2 + ncast], rest[2 + ncast:]
    j = pl.program_id(1)
    _mlstm_reset(j == 0, cf_scr, mf_scr)
    _mlstm_reset(j == 0, cb_scr, mb_scr)
    for w_ref, wo_ref in zip(w_refs, wo_refs):
        wo_ref[...] = w_ref[...].astype(wo_ref.dtype)
    fwd = _mlstm_stream(qf_ref, kf_ref, vf_ref, gtf_ref, bgr_ref, hf_ref, cf_scr, mf_scr,
                        reverse=False, nchunk=nchunk)
    bwd = _mlstm_stream(qb_ref, kb_ref, vb_ref, gtb_ref, bgr_ref, hb_ref, cb_scr, mb_scr,
                        reverse=True, nchunk=nchunk)
    _alternate(fwd, bwd)


def _mlstm_call(qc, kct, vm, gates_t, bg_rows, weights, layer, tm):
    bsz, s, _ = qc.shape
    nblk = s // tm
    nsteps = bsz * nblk
    w_specs, w_shapes = [], []
    for w in weights:
        _, k, n = w.shape
        rows = k // nsteps
        assert rows * nsteps == k and rows % 16 == 0
        w_specs.append(pl.BlockSpec((None, rows, n), lambda b, j: (layer, b * nblk + j, 0)))
        w_shapes.append(jax.ShapeDtypeStruct((k, n), BF16))
    wo_specs = [pl.BlockSpec((sp.block_shape[1], sp.block_shape[2]), lambda b, j: (b * nblk + j, 0))
                for sp in w_specs]

    def specs(pos):
        tile = lambda w: pl.BlockSpec((None, tm, w), lambda b, j: (b, pos(j), 0))
        kt_spec = pl.BlockSpec((None, tm * N_HEADS_M, HEAD_DIM_M), lambda b, j: (b, pos(j), 0))
        gt_spec = pl.BlockSpec((None, N_GATES, tm), lambda b, j: (b, 0, pos(j)))
        return [tile(M_W), kt_spec, tile(M_W), gt_spec], tile(M_W)

    in_f, out_f = specs(lambda j: j)
    in_b, out_b = specs(lambda j: nblk - 1 - j)
    state = [pltpu.VMEM((N_HEADS_M, HEAD_DIM_M, 2 * HEAD_DIM_M), F32), pltpu.VMEM((8, LANES), F32)]
    outs = pl.pallas_call(
        functools.partial(_mlstm_kernel, nchunk=tm // CHUNK, ncast=len(weights)),
        grid=(bsz, nblk),
        in_specs=in_f + in_b + [pl.BlockSpec((N_GATES, LANES), lambda b, j: (0, 0))] + w_specs,
        out_specs=[out_f, out_b] + wo_specs,
        out_shape=[jax.ShapeDtypeStruct((bsz, s, M_W), F32)] * 2 + w_shapes,
        scratch_shapes=state + state,
        compiler_params=pltpu.CompilerParams(
            dimension_semantics=("arbitrary", "arbitrary"), vmem_limit_bytes=VMEM_LIMIT),
        name="mlstm",
    )(qc, kct, vm, gates_t, qc, kct, vm, gates_t, bg_rows, *weights)
    return outs[0], outs[1], outs[2:]


def _zero_after(a):
    bits = pltpu.bitcast(a, jnp.uint32)
    z = lax.shift_right_logical(lax.shift_right_logical(bits, jnp.uint32(16)), jnp.uint32(16))
    return pltpu.bitcast(z, F32)


def _attn_bias_init(bias_scr):
    nk = 3 * BLOCK
    row = lax.broadcasted_iota(jnp.int32, (BLOCK, nk), 0)
    col = lax.broadcasted_iota(jnp.int32, (BLOCK, nk), 1)
    dist = jnp.abs(col - BLOCK - row)
    distf = dist.astype(F32)
    for var in range(3):
        ok = dist <= WINDOW
        if var == 1:
            ok = ok & (col >= BLOCK)
        elif var == 2:
            ok = ok & (col < 2 * BLOCK)
        for h in range(N_HEADS_ATT):
            slope = 2.0 ** (-8.0 * (h + 1.0) / N_HEADS_ATT)
            bias_scr[var * N_HEADS_ATT + h] = jnp.where(ok, -slope * distf, -jnp.inf)


def _attn_pieces(first, last, sink_ref, q_ref, kvp_ref, kvn_ref, g_ref, o_ref, bias_scr, anchors,
                 *, nsub):
    nk = 3 * BLOCK

    lane_k = lax.broadcasted_iota(jnp.int32, (nk, LANES), 1)
    ones_a = jnp.where(lane_k < HEAD_DIM_ATT, 1.0, 0.0).astype(BF16)
    ones_b = jnp.where(lane_k < HEAD_DIM_ATT, 0.0, 1.0).astype(BF16)
    lo_half_q = lax.broadcasted_iota(jnp.int32, (BLOCK, LANES), 1) < HEAD_DIM_ATT

    def kv_block(idx):
        if idx < 0:
            return kvp_ref[...]
        if idx >= nsub:
            return kvn_ref[...]
        return q_ref[idx * BLOCK:(idx + 1) * BLOCK, ATT_Q:QKVA_W]

    for n in range(nsub):
        rows = slice(n * BLOCK, (n + 1) * BLOCK)
        kv = jnp.concatenate([kv_block(n - 1), kv_block(n), kv_block(n + 1)], axis=0)
        if n == 0:
            var = jnp.where(first, 1, 0)
        elif n == nsub - 1:
            var = jnp.where(last, 2, 0)
        else:
            var = 0
        pieces = []
        for kvh in range(N_KV_HEADS):
            k_st, k_sw = kv[:, 0:LANES], kv[:, LANES:2 * LANES]
            v_st, v_sw = kv[:, 2 * LANES:3 * LANES], kv[:, 3 * LANES:4 * LANES]
            if kvh == 0:
                k_lo, k_hi, v_lo, v_hi = k_st, k_sw, v_st, v_sw
            else:
                k_lo, k_hi, v_lo, v_hi = k_sw, k_st, v_sw, v_st
            kk = jnp.concatenate([k_lo * ones_a, k_hi * ones_b], axis=0)
            vv = jnp.concatenate([
                jnp.concatenate([v_lo * ones_a, ones_a], axis=1),
                jnp.concatenate([v_hi * ones_b, ones_b], axis=1)], axis=0)
            for pair in range(GROUP_SIZE // 2):
                h0 = kvh * GROUP_SIZE + 2 * pair
                qp = q_ref[rows, h0 * HEAD_DIM_ATT:(h0 + 2) * HEAD_DIM_ATT]
                s2 = _dot_nt(qp, kk)
                tie = _zero_after(anchors[-1])[:, 0:1] if anchors else 0.0
                ps, es = [], []
                for t in range(2):
                    logits = s2[:, t * nk:(t + 1) * nk] + bias_scr[var * N_HEADS_ATT + h0 + t]
                    sink = sink_ref[h0 + t] + tie
                    mx = jnp.maximum(jnp.max(logits, axis=-1, keepdims=True), sink)
                    ps.append(jnp.exp(logits - mx).astype(BF16))
                    es.append(jnp.exp(sink - mx))
                res = _dot(jnp.concatenate(ps, axis=1), vv)
                den = res[:, LANES:] + jnp.where(lo_half_q, es[0], es[1])
                pieces.append(res[:, :LANES] / den)
                yield
        att = jnp.concatenate(pieces, axis=1)
        o_ref[rows, :] = _rms(att, g_ref[...]).astype(o_ref.dtype)
        yield


def _outffn_pieces(x_ref, att_ref, hf_ref, hb_ref, om_ref, mod_ref, gm_ref, wo_ref, g2_ref,
                   w1_ref, w2_ref, gf_ref, o_ref, hid_scr, anchors, *, final):
    hs = hf_ref[...] + hb_ref[...]
    parts = []
    for h in range(N_HEADS_M):
        cols = slice(h * HEAD_DIM_M, (h + 1) * HEAD_DIM_M)
        parts.append(_rms(hs[:, cols], gm_ref[:, cols]))
    hm = jax.nn.sigmoid(om_ref[...]) * jnp.concatenate(parts, axis=1)
    mixin = jnp.concatenate([att_ref[...], hm.astype(BF16)], axis=1)
    mix = _dot(mixin, wo_ref[...])
    anchors.append(mix[0:1, 0:LANES])
    x1 = x_ref[...] + mod_ref[2:3, :] * mix
    hff = (_rms(x1, g2_ref[...]) * (1.0 + mod_ref[4:5, :]) + mod_ref[3:4, :]).astype(BF16)
    yield
    for c in range(N_FF_CHUNKS):
        gate = _dot(hff, w1_ref[:, FF_CHUNK * c:FF_CHUNK * (c + 1)])
        up = _dot(hff, w1_ref[:, D_FF + FF_CHUNK * c:D_FF + FF_CHUNK * (c + 1)])
        anchors.append(gate[0:1, 0:LANES])
        hid_scr[:, FF_CHUNK * c:FF_CHUNK * (c + 1)] = (jax.nn.silu(gate) * up).astype(BF16)
        yield
    ff = _dot(hid_scr[...], w2_ref[...])
    anchors.append(ff[0:1, 0:LANES])
    x2 = x1 + mod_ref[5:6, :] * ff
    if final:
        x2 = _rms(x2, gf_ref[...])
    o_ref[...] = x2
    yield


def _attn_kernel(sink_ref, q_ref, kvp_ref, kvn_ref, g_ref, o_ref, bias_scr, *, nt, nsub):
    b = pl.program_id(0)
    j = pl.program_id(1)

    @pl.when((b == 0) & (j == 0))
    def _init():
        _attn_bias_init(bias_scr)

    for _ in _attn_pieces(j == 0, j == nt - 1, sink_ref, q_ref, kvp_ref, kvn_ref, g_ref, o_ref,
                          bias_scr, [], nsub=nsub):
        pass


def _attn_call(sink, qkva, g_attn, tq):
    bsz, s, _ = qkva.shape
    nt = s // tq
    nsub = tq // BLOCK
    nb = s // BLOCK
    kvw = QKVA_W - ATT_Q
    return pl.pallas_call(
        functools.partial(_attn_kernel, nt=nt, nsub=nsub),
        grid=(bsz, nt),
        in_specs=[
            pl.BlockSpec(memory_space=pltpu.SMEM),
            pl.BlockSpec((None, tq, QKVA_W), lambda b, j: (b, j, 0)),
            pl.BlockSpec((None, BLOCK, kvw), lambda b, j: (b, jnp.maximum(j * nsub - 1, 0), 1)),
            pl.BlockSpec((None, BLOCK, kvw), lambda b, j: (b, jnp.minimum((j + 1) * nsub, nb - 1), 1)),
            pl.BlockSpec((1, ATT_Q), lambda b, j: (0, 0)),
        ],
        out_specs=pl.BlockSpec((None, tq, ATT_Q), lambda b, j: (b, j, 0)),
        out_shape=jax.ShapeDtypeStruct((bsz, s, ATT_Q), BF16),
        scratch_shapes=[pltpu.VMEM((3 * N_HEADS_ATT, BLOCK, 3 * BLOCK), F32)],
        compiler_params=pltpu.CompilerParams(
            dimension_semantics=("arbitrary", "arbitrary"), vmem_limit_bytes=VMEM_LIMIT),
        name="attn",
    )(sink, qkva, qkva, qkva, g_attn)


def _outffn_kernel(*refs, final):
    for _ in _outffn_pieces(*refs, [], final=final):
        pass


def _outffn_call(x, att, hf, hb, om, mod, g_m, w_out, g2, w1, w2, g_final, tm, final):
    bsz, s, d = x.shape
    tile = lambda w: pl.BlockSpec((None, tm, w), lambda b, i: (b, i, 0))
    const = lambda shp: pl.BlockSpec(shp, lambda b, i: (0, 0))
    weight = lambda shp: pl.BlockSpec(shp, lambda b, i: (0, 0), pipeline_mode=pl.Buffered(1))
    return pl.pallas_call(
        functools.partial(_outffn_kernel, final=final),
        grid=(bsz, s // tm),
        in_specs=[
            tile(d), tile(ATT_Q), tile(M_W), tile(M_W), tile(M_W),
            pl.BlockSpec((None, N_MOD, d), lambda b, i: (b, 0, 0)),
            const((1, M_W)), weight((ATT_Q + M_W, d)), const((1, d)),
            weight((d, 2 * D_FF)), weight((D_FF, d)), const((1, d)),
        ],
        out_specs=tile(d),
        out_shape=jax.ShapeDtypeStruct((bsz, s, d), F32),
        scratch_shapes=[pltpu.VMEM((tm, D_FF), BF16)],
        compiler_params=pltpu.CompilerParams(
            dimension_semantics=("arbitrary", "arbitrary"), vmem_limit_bytes=VMEM_LIMIT),
        name="outffn",
    )(x, att, hf, hb, om, mod, g_m, w_out, g2, w1, w2, g_final)


def _layer(l, x, c, w_mod, b_mod, g_norm1, w_in, conv_w, conv_b, b_gates, sink,
           g_attn_out, g_mlstm_out, w_out, g_norm2, w_ffn_in, w_ffn_out, g_final, final):
    d = x.shape[-1]
    mod = _mod_call(c, w_mod[l], b_mod[l])

    w_in_p = _cast_call(w_in, l, IN_COLS_PAD)
    qkva, qc, kct, vm, om, gates_t = _inproj_call(
        x, mod, g_norm1[l].reshape(1, d), w_in_p, conv_w[l], conv_b[l].reshape(1, 2 * M_W), tm=512)

    bg_rows = jnp.broadcast_to(b_gates[l][:, None], (N_GATES, LANES))
    hf, hb, (w_out_b, w_ffn_in_b, w_ffn_out_b) = _mlstm_call(
        qc, kct, vm, gates_t, bg_rows, (w_out, w_ffn_in, w_ffn_out), l, tm=1024)

    att = _attn_call(sink[l], qkva, g_attn_out[l].reshape(1, ATT_Q), tq=512)

    return _outffn_call(x, att, hf, hb, om, mod, g_mlstm_out[l].reshape(1, M_W), w_out_b,
                        g_norm2[l].reshape(1, d), w_ffn_in_b, w_ffn_out_b,
                        g_final.reshape(1, d), tm=512, final=final)


def kernel(x, c, w_mod, b_mod, g_norm1, w_in, conv_w, conv_b, b_gates, sink, g_attn_out,
           g_mlstm_out, w_out, g_norm2, w_ffn_in, w_ffn_out, g_final):
    depth = w_mod.shape[0]
    for l in range(depth):
        x = _layer(l, x, c, w_mod, b_mod, g_norm1, w_in, conv_w, conv_b, b_gates, sink, g_attn_out,
                   g_mlstm_out, w_out, g_norm2, w_ffn_in, w_ffn_out, g_final, final=(l == depth - 1))
    return x
```

```python
import functools

import jax
import jax.numpy as jnp
from jax import lax
from jax.experimental import pallas as pl
from jax.experimental.pallas import tpu as pltpu

F32 = jnp.float32
BF16 = jnp.bfloat16

D_MODEL = 1024
EPS = 1e-6
N_HEADS_ATT = 8
N_KV_HEADS = 2
HEAD_DIM_ATT = 64
GROUP_SIZE = N_HEADS_ATT // N_KV_HEADS
WINDOW = 128
BLOCK = 128
N_HEADS_M = 4
HEAD_DIM_M = 128
CHUNK = 128
ATT_Q = N_HEADS_ATT * HEAD_DIM_ATT
ATT_KV = N_KV_HEADS * HEAD_DIM_ATT
M_W = N_HEADS_M * HEAD_DIM_M
N_GATES = 4 * N_HEADS_M
D_FF = 2816
N_MOD = 6

LANES = 128
GATE_PAD = LANES
FF_CHUNK = 256
N_FF_CHUNKS = D_FF // FF_CHUNK
OUT_COL_BLOCK = 256
VMEM_LIMIT = 56 * 1024 * 1024

C_QA = 0
C_KA = ATT_Q
C_VA = ATT_Q + ATT_KV
C_QKM = ATT_Q + 2 * ATT_KV
C_VM = C_QKM + 2 * M_W
C_OM = C_VM + M_W
C_G = C_OM + M_W
IN_COLS_PAD = C_G + GATE_PAD
QKVA_W = ATT_Q + 4 * ATT_KV


def _dot(a, b):
    return jnp.dot(a, b, preferred_element_type=F32)


def _dot_nt(a, b):
    return lax.dot_general(a, b, (((1,), (1,)), ((), ())), preferred_element_type=F32)


def _rms(x, g):
    return x * lax.rsqrt(jnp.mean(x * x, axis=-1, keepdims=True) + EPS) * g


def _alternate(*streams):
    live = list(streams)
    while live:
        for s in list(live):
            if next(s, StopIteration) is StopIteration:
                live.remove(s)


def _cast_kernel(w_ref, o_ref):
    n = w_ref.shape[1]
    o_ref[:, :n] = w_ref[...].astype(o_ref.dtype)
    if o_ref.shape[1] > n:
        o_ref[:, n:] = jnp.zeros((o_ref.shape[0], o_ref.shape[1] - n), o_ref.dtype)


def _cast_call(w, layer, n_out=None, bm=256):
    _, k, n = w.shape
    n_out = n if n_out is None else n_out
    return pl.pallas_call(
        _cast_kernel,
        grid=(k // bm,),
        in_specs=[pl.BlockSpec((None, bm, n), lambda i: (layer, i, 0))],
        out_specs=pl.BlockSpec((bm, n_out), lambda i: (i, 0)),
        out_shape=jax.ShapeDtypeStruct((k, n_out), BF16),
        compiler_params=pltpu.CompilerParams(dimension_semantics=("arbitrary",)),
        name="cast",
    )(w)


def _mod_kernel(c_ref, w_ref, b_ref, o_ref):
    s = jax.nn.silu(c_ref[...]).astype(BF16)
    o_ref[...] = _dot(s, w_ref[...].astype(BF16)) + b_ref[...]


def _mod_call(c, w_mod, b_mod):
    bsz = c.shape[0]
    rows = 8
    cp = jnp.pad(c, ((0, rows - bsz), (0, 0)))
    n = w_mod.shape[1]
    bn = 1024
    out = pl.pallas_call(
        _mod_kernel,
        grid=(n // bn,),
        in_specs=[
            pl.BlockSpec((rows, D_MODEL), lambda i: (0, 0)),
            pl.BlockSpec((D_MODEL, bn), lambda i: (0, i)),
            pl.BlockSpec((1, bn), lambda i: (0, i)),
        ],
        out_specs=pl.BlockSpec((rows, bn), lambda i: (0, i)),
        out_shape=jax.ShapeDtypeStruct((rows, n), F32),
        compiler_params=pltpu.CompilerParams(dimension_semantics=("arbitrary",)),
        name="mod",
    )(cp, w_mod, b_mod.reshape(1, n))
    return out[:bsz].reshape(bsz, N_MOD, D_MODEL)


MXU_COLS = 256
CONV_ROWS = 64


def _inproj_kernel(x_ref, mod_ref, g_ref, w_ref, cw_ref, cb_ref,
                   qkva_ref, qc_ref, kct_ref, vm_ref, om_ref, gt_ref,
                   raw_scr, new_scr, k_scr, carry_scr, *, nblk):
    j = pl.program_id(0)

    @pl.when(j == 0)
    def _init():
        raw_scr[...] = jnp.zeros_like(raw_scr)
        carry_scr[...] = jnp.zeros_like(carry_scr)

    x = x_ref[...]
    h = _rms(x, g_ref[...]) * (1.0 + mod_ref[1:2, :]) + mod_ref[0:1, :]
    hb = h.astype(BF16)
    tm = x.shape[0]
    nchunk = tm // CHUNK
    within = (j + nblk - 1) % nblk

    anchors = []

    def zero_after(a):
        return jnp.concatenate([_zero_after(a)] * (2 * M_W // LANES), axis=1)

    def project():
        order = list(range(C_QKM, C_VM, MXU_COLS)) + list(range(0, C_QKM, MXU_COLS)) + \
            list(range(C_VM, IN_COLS_PAD, MXU_COLS))
        for c0 in order:
            c1 = min(c0 + MXU_COLS, IN_COLS_PAD)
            res = _dot(hb, w_ref[:, c0:c1])
            anchors.append(res[0:1, 0:LANES])
            if c0 < C_KA:
                qkva_ref[:, c0:c1] = (res * (HEAD_DIM_ATT ** -0.5)).astype(BF16)
            elif c0 < C_QKM:
                ka, va = res[:, :ATT_KV], res[:, ATT_KV:]
                half = HEAD_DIM_ATT
                qkva_ref[:, ATT_Q:ATT_Q + ATT_KV] = ka.astype(BF16)
                qkva_ref[:, ATT_Q + ATT_KV:ATT_Q + 2 * ATT_KV] = pltpu.roll(ka, half, axis=1).astype(BF16)
                qkva_ref[:, ATT_Q + 2 * ATT_KV:ATT_Q + 3 * ATT_KV] = va.astype(BF16)
                qkva_ref[:, ATT_Q + 3 * ATT_KV:ATT_Q + 4 * ATT_KV] = pltpu.roll(va, half, axis=1).astype(BF16)
            elif c0 < C_VM:
                new_scr[:, c0 - C_QKM:c1 - C_QKM] = res
            elif c0 < C_OM:
                vm_ref[:, c0 - C_VM:c1 - C_VM] = res.astype(BF16)
            elif c0 < C_G:
                om_ref[:, c0 - C_OM:c1 - C_OM] = res
            else:
                for g in range(nchunk):
                    blk = res[g * CHUNK:(g + 1) * CHUNK, :].T
                    gt_ref[:, g * CHUNK:(g + 1) * CHUNK] = blk[0:N_GATES, :]
            yield

    def conv():
        R = CONV_ROWS
        row = lax.broadcasted_iota(jnp.int32, (R, 1), 0)
        for p in range(tm // R):
            r0 = p * R
            xg = raw_scr[r0:r0 + R, :]
            if p == 0:
                prev_row = jnp.where(within == 0, 0.0, carry_scr[0:1, :])
            else:
                prev_row = raw_scr[r0 - 1:r0, :]
            if r0 + R == tm:
                next_row = jnp.where(within == nblk - 1, 0.0, new_scr[0:1, :])
            else:
                next_row = raw_scr[r0 + R:r0 + R + 1, :]
            xm1 = jnp.where(row == 0, prev_row, pltpu.roll(xg, 1, axis=0))
            xp1 = jnp.where(row == R - 1, next_row, pltpu.roll(xg, R - 1, axis=0))
            z = zero_after(anchors[-1])
            y =xm1 * (cw_ref[0:1, :] + z) + xg * (cw_ref[1:2, :] + z) + xp1 * (cw_ref[2:3, :] + z) \
                + cb_ref[...]
            y = jax.nn.silu(y)
            qc_ref[r0:r0 + R, :] = (y[:, :M_W] * (HEAD_DIM_M ** -0.5)).astype(BF16)
            k_scr[r0:r0 + R, :] = y[:, M_W:]
            yield
            if (r0 + R) % CHUNK == 0:
                g = r0 // CHUNK
                for hd in range(N_HEADS_M):
                    blk = k_scr[g * CHUNK:(g + 1) * CHUNK, hd * HEAD_DIM_M:(hd + 1) * HEAD_DIM_M]
                    base = (g * N_HEADS_M + hd) * HEAD_DIM_M
                    kct_ref[base:base + HEAD_DIM_M, :] = blk.T.astype(BF16)
                yield

    _alternate(project(), conv())
    carry_scr[0:1, :] = raw_scr[tm - 1:tm, :]
    raw_scr[...] = new_scr[...]


def _inproj_call(x, mod, g1, w_in_p, conv_w, conv_b, tm):
    bsz, s, d = x.shape
    nblk = s // tm
    ntiles = bsz * nblk
    outs = (
        jax.ShapeDtypeStruct((bsz, s, QKVA_W), BF16),
        jax.ShapeDtypeStruct((bsz, s, M_W), BF16),
        jax.ShapeDtypeStruct((bsz, s * N_HEADS_M, HEAD_DIM_M), BF16),
        jax.ShapeDtypeStruct((bsz, s, M_W), BF16),
        jax.ShapeDtypeStruct((bsz, s, M_W), F32),
        jax.ShapeDtypeStruct((bsz, N_GATES, s), F32),
    )

    def cur(j):
        t = jnp.minimum(j, ntiles - 1)
        return t // nblk, t % nblk

    def old(j):
        t = jnp.maximum(j - 1, 0)
        return t // nblk, t % nblk

    cur_tile = lambda w: pl.BlockSpec((None, tm, w), lambda j: (*cur(j), 0))
    old_tile = lambda w: pl.BlockSpec((None, tm, w), lambda j: (*old(j), 0))
    const = lambda shp: pl.BlockSpec(shp, lambda j: (0, 0))
    return pl.pallas_call(
        functools.partial(_inproj_kernel, nblk=nblk),
        grid=(ntiles + 1,),
        in_specs=[
            cur_tile(d),
            pl.BlockSpec((None, N_MOD, d), lambda j: (cur(j)[0], 0, 0)),
            const((1, d)),
            pl.BlockSpec((d, IN_COLS_PAD), lambda j: (0, 0), pipeline_mode=pl.Buffered(1)),
            const((3, 2 * M_W)), const((1, 2 * M_W)),
        ],
        out_specs=[cur_tile(QKVA_W), old_tile(M_W),
                   pl.BlockSpec((None, tm * N_HEADS_M, HEAD_DIM_M), lambda j: (*old(j), 0)),
                   cur_tile(M_W), cur_tile(M_W),
                   pl.BlockSpec((None, N_GATES, tm), lambda j: (cur(j)[0], 0, cur(j)[1]))],
        out_shape=outs,
        scratch_shapes=[pltpu.VMEM((tm, 2 * M_W), F32), pltpu.VMEM((tm, 2 * M_W), F32),
                        pltpu.VMEM((tm, M_W), F32), pltpu.VMEM((8, 2 * M_W), F32)],
        compiler_params=pltpu.CompilerParams(
            dimension_semantics=("arbitrary",), vmem_limit_bytes=VMEM_LIMIT),
        name="inproj",
    )(x, mod, g1, w_in_p, conv_w, conv_b)


def _mlstm_reset(first, c_scr, m_scr):
    @pl.when(first)
    def _init():
        c_scr[...] = jnp.zeros_like(c_scr)
        m_scr[...] = jnp.zeros_like(m_scr)


LOG2E = 1.4426950408889634


def _mlstm_stream(q_ref, kt_ref, v_ref, gt_ref, bgr_ref, h_ref, c_scr, m_scr, *, reverse, nchunk):
    L = CHUNK
    ti = lax.broadcasted_iota(jnp.int32, (L, L), 0)
    si = lax.broadcasted_iota(jnp.int32, (L, L), 1)
    causal = (si >= ti) if reverse else (si <= ti)
    tri_t = jnp.where((ti >= si) if reverse else (ti <= si), 1.0, 0.0).astype(BF16)
    lane = lax.broadcasted_iota(jnp.int32, (1, L), 1)
    last = 0 if reverse else L - 1
    i_off, f_off = (2 * N_HEADS_M, 3 * N_HEADS_M) if reverse else (0, N_HEADS_M)
    ones = jnp.ones((L, HEAD_DIM_M), BF16)
    order = list(range(nchunk - 1, -1, -1) if reverse else range(nchunk))
    rows_of = lambda g: slice(g * L, (g + 1) * L)

    def split3(a):
        hi = a.astype(BF16)
        r1 = a - hi.astype(F32)
        mid = r1.astype(BF16)
        return hi, mid, (r1 - mid.astype(F32)).astype(BF16)

    gates = {}
    pad = jnp.zeros((L - N_GATES, L), F32)
    for g in order:
        gr = gt_ref[:, rows_of(g)] + bgr_ref[...]
        parts = _dot(jnp.concatenate(split3(jax.nn.log_sigmoid(gr)), axis=0), tri_t)
        bcr = (parts[0:N_GATES] + parts[N_GATES:2 * N_GATES] + parts[2 * N_GATES:3 * N_GATES]) * LOG2E
        gates[g] = (gr * LOG2E, bcr, jnp.concatenate([-bcr, pad], axis=0).T)
        yield

    def head(g, h):
        rows = rows_of(g)
        cols = slice(h * HEAD_DIM_M, (h + 1) * HEAD_DIM_M)
        gct, bct, nbcum = gates[g]
        r_row = gct[i_off + h:i_off + h + 1, :] - bct[f_off + h:f_off + h + 1, :]
        btot = jnp.sum(jnp.where(lane == last, bct[f_off + h:f_off + h + 1, :], 0.0),
                       axis=1, keepdims=True)
        rmax = jnp.max(r_row, axis=1, keepdims=True)
        a_max = btot + rmax
        m_prev = m_scr[h:h + 1, 0:1]
        m_new = jnp.maximum(btot + m_prev, a_max)
        m_scr[h:h + 1, :] = jnp.broadcast_to(m_new, (1, LANES))
        decay = jnp.exp2(btot + m_prev - m_new)
        w_row = jnp.exp2(r_row - rmax) * jnp.exp2(a_max - m_new)
        kt = kt_ref[(g * N_HEADS_M + h) * HEAD_DIM_M:(g * N_HEADS_M + h + 1) * HEAD_DIM_M, :]
        rm = jnp.where(causal, r_row, -jnp.inf)
        cm = jnp.max(rm, axis=1, keepdims=True)
        yield
        q = q_ref[rows, cols]
        vaug = jnp.concatenate([v_ref[rows, cols], ones], axis=1)
        s = _dot(q, kt)
        upd = _dot(kt * w_row.astype(BF16), vaug)
        u = jnp.maximum(cm, m_prev)
        ub = jnp.broadcast_to(u, (L, L))
        dm = jnp.exp2(rm - ub)
        qi = q * jnp.exp2(m_prev - ub).astype(BF16)
        floor = jnp.exp2(jnp.broadcast_to(nbcum[:, f_off + h:f_off + h + 1], (L, L)) - ub)
        yield
        cprev = c_scr[h]
        lhs = jnp.concatenate([(s * dm).astype(BF16), qi], axis=1)
        rhs = jnp.concatenate([vaug, cprev.astype(BF16)], axis=0)
        out = _dot(lhs, rhs)
        c_scr[h] = decay * cprev + upd
        yield
        h_ref[rows, cols] = out[:, :HEAD_DIM_M] / jnp.maximum(jnp.abs(out[:, HEAD_DIM_M:]), floor)
        yield

    n_stage, lag = 4, 2
    heads = {}
    for slot in range(lag * (nchunk - 1) + n_stage):
        for i, g in enumerate(order):
            stage = slot - lag * i
            if 0 <= stage < n_stage:
                for h in range(N_HEADS_M):
                    if stage == 0:
                        heads[g, h] = head(g, h)
                    next(heads[g, h])
                    yield


def _mixer_kernel(qf_ref, kf_ref, vf_ref, gtf_ref, qb_ref, kb_ref, vb_ref, gtb_ref, bgr_ref,
                  sink_ref, qa_ref, kvp_ref, kvn_ref, ga_ref, *rest, nblk, nchunk, ncast):
    w_refs, rest = rest[:ncast], rest[ncast:]
    hf_ref, hb_ref, att_ref = rest[:3]
    wo_refs, (cf_scr, mf_scr, cb_scr, mb_scr, bias_scr) = rest[3:3 + ncast], rest[3 + ncast:]
    b = pl.program_id(0)
    j = pl.program_id(1)

    @pl.when((b == 0) & (j == 0))
    def _init():
        _attn_bias_init(bias_scr)

    _mlstm_reset(j == 0, cf_scr, mf_scr)
    _mlstm_reset(j == 0, cb_scr, mb_scr)
    for w_ref, wo_ref in zip(w_refs, wo_refs):
        wo_ref[...] = w_ref[...].astype(wo_ref.dtype)
    fwd = _mlstm_stream(qf_ref, kf_ref, vf_ref, gtf_ref, bgr_ref, hf_ref, cf_scr, mf_scr,
                        reverse=False, nchunk=nchunk)
    bwd = _mlstm_stream(qb_ref, kb_ref, vb_ref, gtb_ref, bgr_ref, hb_ref, cb_scr, mb_scr,
                        reverse=True, nchunk=nchunk)
    att = _attn_pieces(j == 0, j == nblk - 1, sink_ref, qa_ref, kvp_ref, kvn_ref, ga_ref, att_ref,
                       bias_scr, [], nsub=nchunk)
    _alternate(fwd, bwd)
    for _ in att:
        pass


def _mixer_call(qc, kct, vm, gates_t, bg_rows, sink, qkva, g_attn, weights, layer, tm):
    bsz, s, _ = qc.shape
    nblk = s // tm
    nsteps = bsz * nblk
    nsub = tm // BLOCK
    nb = s // BLOCK
    kvw = QKVA_W - ATT_Q
    attn_specs = [
        pl.BlockSpec(memory_space=pltpu.SMEM),
        pl.BlockSpec((None, tm, QKVA_W), lambda b, j: (b, j, 0)),
        pl.BlockSpec((None, BLOCK, kvw), lambda b, j: (b, jnp.maximum(j * nsub - 1, 0), 1)),
        pl.BlockSpec((None, BLOCK, kvw), lambda b, j: (b, jnp.minimum((j + 1) * nsub, nb - 1), 1)),
        pl.BlockSpec((1, ATT_Q), lambda b, j: (0, 0)),
    ]
    w_specs, w_shapes = [], []
    for w in weights:
        _, k, n = w.shape
        rows = k // nsteps
        assert rows * nsteps == k and rows % 16 == 0
        w_specs.append(pl.BlockSpec((None, rows, n), lambda b, j: (layer, b * nblk + j, 0)))
        w_shapes.append(jax.ShapeDtypeStruct((k, n), BF16))
    wo_specs = [pl.BlockSpec((sp.block_shape[1], sp.block_shape[2]), lambda b, j: (b * nblk + j, 0))
                for sp in w_specs]

    def specs(pos):
        tile = lambda w: pl.BlockSpec((None, tm, w), lambda b, j: (b, pos(j), 0))
        kt_spec = pl.BlockSpec((None, tm * N_HEADS_M, HEAD_DIM_M), lambda b, j: (b, pos(j), 0))
        gt_spec = pl.BlockSpec((None, N_GATES, tm), lambda b, j: (b, 0, pos(j)))
        return [tile(M_W), kt_spec, tile(M_W), gt_spec], tile(M_W)

    in_f, out_f = specs(lambda j: j)
    in_b, out_b = specs(lambda j: nblk - 1 - j)
    state = [pltpu.VMEM((N_HEADS_M, HEAD_DIM_M, 2 * HEAD_DIM_M), F32), pltpu.VMEM((8, LANES), F32)]
    outs = pl.pallas_call(
        functools.partial(_mixer_kernel, nblk=nblk, nchunk=tm // CHUNK, ncast=len(weights)),
        grid=(bsz, nblk),
        in_specs=in_f + in_b + [pl.BlockSpec((N_GATES, LANES), lambda b, j: (0, 0))] + attn_specs
        + w_specs,
        out_specs=[out_f, out_b, pl.BlockSpec((None, tm, ATT_Q), lambda b, j: (b, j, 0))] + wo_specs,
        out_shape=[jax.ShapeDtypeStruct((bsz, s, M_W), F32)] * 2
        + [jax.ShapeDtypeStruct((bsz, s, ATT_Q), BF16)] + w_shapes,
        scratch_shapes=state + state + [pltpu.VMEM((3 * N_HEADS_ATT, BLOCK, 3 * BLOCK), F32)],
        compiler_params=pltpu.CompilerParams(
            dimension_semantics=("arbitrary", "arbitrary"), vmem_limit_bytes=VMEM_LIMIT),
        name="mixer",
    )(qc, kct, vm, gates_t, qc, kct, vm, gates_t, bg_rows, sink, qkva, qkva, qkva, g_attn, *weights)
    return outs[0], outs[1], outs[2], outs[3:]


def _zero_after(a):
    bits = pltpu.bitcast(a, jnp.uint32)
    z = lax.shift_right_logical(lax.shift_right_logical(bits, jnp.uint32(16)), jnp.uint32(16))
    return pltpu.bitcast(z, F32)


def _attn_bias_init(bias_scr):
    nk = 3 * BLOCK
    row = lax.broadcasted_iota(jnp.int32, (BLOCK, nk), 0)
    col = lax.broadcasted_iota(jnp.int32, (BLOCK, nk), 1)
    dist = jnp.abs(col - BLOCK - row)
    distf = dist.astype(F32)
    for var in range(3):
        ok = dist <= WINDOW
        if var == 1:
            ok = ok & (col >= BLOCK)
        elif var == 2:
            ok = ok & (col < 2 * BLOCK)
        for h in range(N_HEADS_ATT):
            slope = 2.0 ** (-8.0 * (h + 1.0) / N_HEADS_ATT)
            bias_scr[var * N_HEADS_ATT + h] = jnp.where(ok, -slope * distf, -jnp.inf)


def _attn_pieces(first, last, sink_ref, q_ref, kvp_ref, kvn_ref, g_ref, o_ref, bias_scr, anchors,
                 *, nsub):
    nk = 3 * BLOCK

    lane_k = lax.broadcasted_iota(jnp.int32, (nk, LANES), 1)
    ones_a = jnp.where(lane_k < HEAD_DIM_ATT, 1.0, 0.0).astype(BF16)
    ones_b = jnp.where(lane_k < HEAD_DIM_ATT, 0.0, 1.0).astype(BF16)
    lo_half_q = lax.broadcasted_iota(jnp.int32, (BLOCK, LANES), 1) < HEAD_DIM_ATT

    def kv_block(idx):
        if idx < 0:
            return kvp_ref[...]
        if idx >= nsub:
            return kvn_ref[...]
        return q_ref[idx * BLOCK:(idx + 1) * BLOCK, ATT_Q:QKVA_W]

    for n in range(nsub):
        rows = slice(n * BLOCK, (n + 1) * BLOCK)
        kv = jnp.concatenate([kv_block(n - 1), kv_block(n), kv_block(n + 1)], axis=0)
        if n == 0:
            var = jnp.where(first, 1, 0)
        elif n == nsub - 1:
            var = jnp.where(last, 2, 0)
        else:
            var = 0
        pieces = []
        for kvh in range(N_KV_HEADS):
            k_st, k_sw = kv[:, 0:LANES], kv[:, LANES:2 * LANES]
            v_st, v_sw = kv[:, 2 * LANES:3 * LANES], kv[:, 3 * LANES:4 * LANES]
            if kvh == 0:
                k_lo, k_hi, v_lo, v_hi = k_st, k_sw, v_st, v_sw
            else:
                k_lo, k_hi, v_lo, v_hi = k_sw, k_st, v_sw, v_st
            kk = jnp.concatenate([k_lo * ones_a, k_hi * ones_b], axis=0)
            vv = jnp.concatenate([
                jnp.concatenate([v_lo * ones_a, ones_a], axis=1),
                jnp.concatenate([v_hi * ones_b, ones_b], axis=1)], axis=0)
            for pair in range(GROUP_SIZE // 2):
                h0 = kvh * GROUP_SIZE + 2 * pair
                qp = q_ref[rows, h0 * HEAD_DIM_ATT:(h0 + 2) * HEAD_DIM_ATT]
                s2 = _dot_nt(qp, kk)
                tie = _zero_after(anchors[-1])[:, 0:1] if anchors else 0.0
                ps, es = [], []
                for t in range(2):
                    logits = s2[:, t * nk:(t + 1) * nk] + bias_scr[var * N_HEADS_ATT + h0 + t]
                    sink = sink_ref[h0 + t] + tie
                    mx = jnp.maximum(jnp.max(logits, axis=-1, keepdims=True), sink)
                    ps.append(jnp.exp(logits - mx).astype(BF16))
                    es.append(jnp.exp(sink - mx))
                res = _dot(jnp.concatenate(ps, axis=1), vv)
                den = res[:, LANES:] + jnp.where(lo_half_q, es[0], es[1])
                pieces.append(res[:, :LANES] / den)
                yield
        att = jnp.concatenate(pieces, axis=1)
        o_ref[rows, :] = _rms(att, g_ref[...]).astype(o_ref.dtype)
        yield


def _outffn_pieces(x_ref, att_ref, hf_ref, hb_ref, om_ref, mod_ref, gm_ref, wo_ref, g2_ref,
                   w1_ref, w2_ref, gf_ref, o_ref, hid_scr, anchors, *, final):
    hs = hf_ref[...] + hb_ref[...]
    parts = []
    for h in range(N_HEADS_M):
        cols = slice(h * HEAD_DIM_M, (h + 1) * HEAD_DIM_M)
        parts.append(_rms(hs[:, cols], gm_ref[:, cols]))
    hm = jax.nn.sigmoid(om_ref[...]) * jnp.concatenate(parts, axis=1)
    mixin = jnp.concatenate([att_ref[...], hm.astype(BF16)], axis=1)
    mix = _dot(mixin, wo_ref[...])
    anchors.append(mix[0:1, 0:LANES])
    x1 = x_ref[...] + mod_ref[2:3, :] * mix
    hff = (_rms(x1, g2_ref[...]) * (1.0 + mod_ref[4:5, :]) + mod_ref[3:4, :]).astype(BF16)
    yield
    for c in range(N_FF_CHUNKS):
        gate = _dot(hff, w1_ref[:, FF_CHUNK * c:FF_CHUNK * (c + 1)])
        up = _dot(hff, w1_ref[:, D_FF + FF_CHUNK * c:D_FF + FF_CHUNK * (c + 1)])
        anchors.append(gate[0:1, 0:LANES])
        hid_scr[:, FF_CHUNK * c:FF_CHUNK * (c + 1)] = (jax.nn.silu(gate) * up).astype(BF16)
        yield
    ff = _dot(hid_scr[...], w2_ref[...])
    anchors.append(ff[0:1, 0:LANES])
    x2 = x1 + mod_ref[5:6, :] * ff
    if final:
        x2 = _rms(x2, gf_ref[...])
    o_ref[...] = x2
    yield


def _attn_kernel(sink_ref, q_ref, kvp_ref, kvn_ref, g_ref, o_ref, bias_scr, *, nt, nsub):
    b = pl.program_id(0)
    j = pl.program_id(1)

    @pl.when((b == 0) & (j == 0))
    def _init():
        _attn_bias_init(bias_scr)

    for _ in _attn_pieces(j == 0, j == nt - 1, sink_ref, q_ref, kvp_ref, kvn_ref, g_ref, o_ref,
                          bias_scr, [], nsub=nsub):
        pass


def _attn_call(sink, qkva, g_attn, tq):
    bsz, s, _ = qkva.shape
    nt = s // tq
    nsub = tq // BLOCK
    nb = s // BLOCK
    kvw = QKVA_W - ATT_Q
    return pl.pallas_call(
        functools.partial(_attn_kernel, nt=nt, nsub=nsub),
        grid=(bsz, nt),
        in_specs=[
            pl.BlockSpec(memory_space=pltpu.SMEM),
            pl.BlockSpec((None, tq, QKVA_W), lambda b, j: (b, j, 0)),
            pl.BlockSpec((None, BLOCK, kvw), lambda b, j: (b, jnp.maximum(j * nsub - 1, 0), 1)),
            pl.BlockSpec((None, BLOCK, kvw), lambda b, j: (b, jnp.minimum((j + 1) * nsub, nb - 1), 1)),
            pl.BlockSpec((1, ATT_Q), lambda b, j: (0, 0)),
        ],
        out_specs=pl.BlockSpec((None, tq, ATT_Q), lambda b, j: (b, j, 0)),
        out_shape=jax.ShapeDtypeStruct((bsz, s, ATT_Q), BF16),
        scratch_shapes=[pltpu.VMEM((3 * N_HEADS_ATT, BLOCK, 3 * BLOCK), F32)],
        compiler_params=pltpu.CompilerParams(
            dimension_semantics=("arbitrary", "arbitrary"), vmem_limit_bytes=VMEM_LIMIT),
        name="attn",
    )(sink, qkva, qkva, qkva, g_attn)


def _outffn_kernel(*refs, final):
    for _ in _outffn_pieces(*refs, [], final=final):
        pass


def _outffn_call(x, att, hf, hb, om, mod, g_m, w_out, g2, w1, w2, g_final, tm, final):
    bsz, s, d = x.shape
    tile = lambda w: pl.BlockSpec((None, tm, w), lambda b, i: (b, i, 0))
    const = lambda shp: pl.BlockSpec(shp, lambda b, i: (0, 0))
    weight = lambda shp: pl.BlockSpec(shp, lambda b, i: (0, 0), pipeline_mode=pl.Buffered(1))
    return pl.pallas_call(
        functools.partial(_outffn_kernel, final=final),
        grid=(bsz, s // tm),
        in_specs=[
            tile(d), tile(ATT_Q), tile(M_W), tile(M_W), tile(M_W),
            pl.BlockSpec((None, N_MOD, d), lambda b, i: (b, 0, 0)),
            const((1, M_W)), weight((ATT_Q + M_W, d)), const((1, d)),
            weight((d, 2 * D_FF)), weight((D_FF, d)), const((1, d)),
        ],
        out_specs=tile(d),
        out_shape=jax.ShapeDtypeStruct((bsz, s, d), F32),
        scratch_shapes=[pltpu.VMEM((tm, D_FF), BF16)],
        compiler_params=pltpu.CompilerParams(
            dimension_semantics=("arbitrary", "arbitrary"), vmem_limit_bytes=VMEM_LIMIT),
        name="outffn",
    )(x, att, hf, hb, om, mod, g_m, w_out, g2, w1, w2, g_final)


def _layer(l, x, c, w_mod, b_mod, g_norm1, w_in, conv_w, conv_b, b_gates, sink,
           g_attn_out, g_mlstm_out, w_out, g_norm2, w_ffn_in, w_ffn_out, g_final, final):
    d = x.shape[-1]
    mod = _mod_call(c, w_mod[l], b_mod[l])

    w_in_p = _cast_call(w_in, l, IN_COLS_PAD)
    qkva, qc, kct, vm, om, gates_t = _inproj_call(
        x, mod, g_norm1[l].reshape(1, d), w_in_p, conv_w[l], conv_b[l].reshape(1, 2 * M_W), tm=512)

    bg_rows = jnp.broadcast_to(b_gates[l][:, None], (N_GATES, LANES))
    hf, hb, att, (w_out_b, w_ffn_in_b, w_ffn_out_b) = _mixer_call(
        qc, kct, vm, gates_t, bg_rows, sink[l], qkva, g_attn_out[l].reshape(1, ATT_Q),
        (w_out, w_ffn_in, w_ffn_out), l, tm=1024)

    return _outffn_call(x, att, hf, hb, om, mod, g_mlstm_out[l].reshape(1, M_W), w_out_b,
                        g_norm2[l].reshape(1, d), w_ffn_in_b, w_ffn_out_b,
                        g_final.reshape(1, d), tm=512, final=final)


def kernel(x, c, w_mod, b_mod, g_norm1, w_in, conv_w, conv_b, b_gates, sink, g_attn_out,
           g_mlstm_out, w_out, g_norm2, w_ffn_in, w_ffn_out, g_final):
    depth = w_mod.shape[0]
    for l in range(depth):
        x = _layer(l, x, c, w_mod, b_mod, g_norm1, w_in, conv_w, conv_b, b_gates, sink, g_attn_out,
                   g_mlstm_out, w_out, g_norm2, w_ffn_in, w_ffn_out, g_final, final=(l == depth - 1))
    return x
```

```python
import functools

import jax
import jax.numpy as jnp
from jax import lax
from jax.experimental import pallas as pl
from jax.experimental.pallas import tpu as pltpu

F32 = jnp.float32
BF16 = jnp.bfloat16

D_MODEL = 1024
EPS = 1e-6
N_HEADS_ATT = 8
N_KV_HEADS = 2
HEAD_DIM_ATT = 64
GROUP_SIZE = N_HEADS_ATT // N_KV_HEADS
WINDOW = 128
BLOCK = 128
N_HEADS_M = 4
HEAD_DIM_M = 128
CHUNK = 128
ATT_Q = N_HEADS_ATT * HEAD_DIM_ATT
ATT_KV = N_KV_HEADS * HEAD_DIM_ATT
M_W = N_HEADS_M * HEAD_DIM_M
N_GATES = 4 * N_HEADS_M
D_FF = 2816
N_MOD = 6

LANES = 128
GATE_PAD = LANES
FF_CHUNK = 256
N_FF_CHUNKS = D_FF // FF_CHUNK
OUT_COL_BLOCK = 256
VMEM_LIMIT = 56 * 1024 * 1024

C_QA = 0
C_KA = ATT_Q
C_VA = ATT_Q + ATT_KV
C_QKM = ATT_Q + 2 * ATT_KV
C_VM = C_QKM + 2 * M_W
C_OM = C_VM + M_W
C_G = C_OM + M_W
IN_COLS_PAD = C_G + GATE_PAD
QKVA_W = ATT_Q + 4 * ATT_KV


def _dot(a, b):
    return jnp.dot(a, b, preferred_element_type=F32)


def _dot_nt(a, b):
    return lax.dot_general(a, b, (((1,), (1,)), ((), ())), preferred_element_type=F32)


def _rms(x, g):
    return x * lax.rsqrt(jnp.mean(x * x, axis=-1, keepdims=True) + EPS) * g


def _alternate(*streams):
    live = list(streams)
    while live:
        for s in list(live):
            if next(s, StopIteration) is StopIteration:
                live.remove(s)


def _cast_kernel(w_ref, o_ref):
    n = w_ref.shape[1]
    o_ref[:, :n] = w_ref[...].astype(o_ref.dtype)
    if o_ref.shape[1] > n:
        o_ref[:, n:] = jnp.zeros((o_ref.shape[0], o_ref.shape[1] - n), o_ref.dtype)


def _cast_call(w, layer, n_out=None, bm=256):
    _, k, n = w.shape
    n_out = n if n_out is None else n_out
    return pl.pallas_call(
        _cast_kernel,
        grid=(k // bm,),
        in_specs=[pl.BlockSpec((None, bm, n), lambda i: (layer, i, 0))],
        out_specs=pl.BlockSpec((bm, n_out), lambda i: (i, 0)),
        out_shape=jax.ShapeDtypeStruct((k, n_out), BF16),
        compiler_params=pltpu.CompilerParams(dimension_semantics=("arbitrary",)),
        name="cast",
    )(w)


def _mod_kernel(c_ref, w_ref, b_ref, o_ref):
    s = jax.nn.silu(c_ref[...]).astype(BF16)
    o_ref[...] = _dot(s, w_ref[...].astype(BF16)) + b_ref[...]


def _mod_call(c, w_mod, b_mod):
    bsz = c.shape[0]
    rows = 8
    cp = jnp.pad(c, ((0, rows - bsz), (0, 0)))
    n = w_mod.shape[1]
    bn = 1024
    out = pl.pallas_call(
        _mod_kernel,
        grid=(n // bn,),
        in_specs=[
            pl.BlockSpec((rows, D_MODEL), lambda i: (0, 0)),
            pl.BlockSpec((D_MODEL, bn), lambda i: (0, i)),
            pl.BlockSpec((1, bn), lambda i: (0, i)),
        ],
        out_specs=pl.BlockSpec((rows, bn), lambda i: (0, i)),
        out_shape=jax.ShapeDtypeStruct((rows, n), F32),
        compiler_params=pltpu.CompilerParams(dimension_semantics=("arbitrary",)),
        name="mod",
    )(cp, w_mod, b_mod.reshape(1, n))
    return out[:bsz].reshape(bsz, N_MOD, D_MODEL)


MXU_COLS = 256
CONV_ROWS = 64


def _inproj_kernel(x_ref, mod_ref, g_ref, w_ref, cw_ref, cb_ref,
                   qkva_ref, qc_ref, kct_ref, vm_ref, om_ref, gt_ref,
                   raw_scr, new_scr, k_scr, carry_scr, *, nblk):
    j = pl.program_id(0)

    @pl.when(j == 0)
    def _init():
        raw_scr[...] = jnp.zeros_like(raw_scr)
        carry_scr[...] = jnp.zeros_like(carry_scr)

    x = x_ref[...]
    h = _rms(x, g_ref[...]) * (1.0 + mod_ref[1:2, :]) + mod_ref[0:1, :]
    hb = h.astype(BF16)
    tm = x.shape[0]
    nchunk = tm // CHUNK
    within = (j + nblk - 1) % nblk

    anchors = []

    def zero_after(a):
        return jnp.concatenate([_zero_after(a)] * (2 * M_W // LANES), axis=1)

    def project():
        order = list(range(C_QKM, C_VM, MXU_COLS)) + list(range(0, C_QKM, MXU_COLS)) + \
            list(range(C_VM, IN_COLS_PAD, MXU_COLS))
        for c0 in order:
            c1 = min(c0 + MXU_COLS, IN_COLS_PAD)
            res = _dot(hb, w_ref[:, c0:c1])
            anchors.append(res[0:1, 0:LANES])
            if c0 < C_KA:
                qkva_ref[:, c0:c1] = (res * (HEAD_DIM_ATT ** -0.5 * LOG2E)).astype(BF16)
            elif c0 < C_QKM:
                ka, va = res[:, :ATT_KV], res[:, ATT_KV:]
                half = HEAD_DIM_ATT
                qkva_ref[:, ATT_Q:ATT_Q + ATT_KV] = ka.astype(BF16)
                qkva_ref[:, ATT_Q + ATT_KV:ATT_Q + 2 * ATT_KV] = pltpu.roll(ka, half, axis=1).astype(BF16)
                qkva_ref[:, ATT_Q + 2 * ATT_KV:ATT_Q + 3 * ATT_KV] = va.astype(BF16)
                qkva_ref[:, ATT_Q + 3 * ATT_KV:ATT_Q + 4 * ATT_KV] = pltpu.roll(va, half, axis=1).astype(BF16)
            elif c0 < C_VM:
                new_scr[:, c0 - C_QKM:c1 - C_QKM] = res
            elif c0 < C_OM:
                vm_ref[:, c0 - C_VM:c1 - C_VM] = res.astype(BF16)
            elif c0 < C_G:
                om_ref[:, c0 - C_OM:c1 - C_OM] = res
            else:
                for g in range(nchunk):
                    blk = res[g * CHUNK:(g + 1) * CHUNK, :].T
                    gt_ref[:, g * CHUNK:(g + 1) * CHUNK] = blk[0:N_GATES, :]
            yield

    def conv():
        R = CONV_ROWS
        row = lax.broadcasted_iota(jnp.int32, (R, 1), 0)
        for p in range(tm // R):
            r0 = p * R
            xg = raw_scr[r0:r0 + R, :]
            if p == 0:
                prev_row = jnp.where(within == 0, 0.0, carry_scr[0:1, :])
            else:
                prev_row = raw_scr[r0 - 1:r0, :]
            if r0 + R == tm:
                next_row = jnp.where(within == nblk - 1, 0.0, new_scr[0:1, :])
            else:
                next_row = raw_scr[r0 + R:r0 + R + 1, :]
            xm1 = jnp.where(row == 0, prev_row, pltpu.roll(xg, 1, axis=0))
            xp1 = jnp.where(row == R - 1, next_row, pltpu.roll(xg, R - 1, axis=0))
            z = zero_after(anchors[-1])
            y =xm1 * (cw_ref[0:1, :] + z) + xg * (cw_ref[1:2, :] + z) + xp1 * (cw_ref[2:3, :] + z) \
                + cb_ref[...]
            y = jax.nn.silu(y)
            qc_ref[r0:r0 + R, :] = (y[:, :M_W] * (HEAD_DIM_M ** -0.5)).astype(BF16)
            k_scr[r0:r0 + R, :] = y[:, M_W:]
            yield
            if (r0 + R) % CHUNK == 0:
                g = r0 // CHUNK
                for hd in range(N_HEADS_M):
                    blk = k_scr[g * CHUNK:(g + 1) * CHUNK, hd * HEAD_DIM_M:(hd + 1) * HEAD_DIM_M]
                    base = (g * N_HEADS_M + hd) * HEAD_DIM_M
                    kct_ref[base:base + HEAD_DIM_M, :] = blk.T.astype(BF16)
                yield

    _alternate(project(), conv())
    carry_scr[0:1, :] = raw_scr[tm - 1:tm, :]
    raw_scr[...] = new_scr[...]


def _inproj_call(x, mod, g1, w_in_p, conv_w, conv_b, tm):
    bsz, s, d = x.shape
    nblk = s // tm
    ntiles = bsz * nblk
    outs = (
        jax.ShapeDtypeStruct((bsz, s, QKVA_W), BF16),
        jax.ShapeDtypeStruct((bsz, s, M_W), BF16),
        jax.ShapeDtypeStruct((bsz, s * N_HEADS_M, HEAD_DIM_M), BF16),
        jax.ShapeDtypeStruct((bsz, s, M_W), BF16),
        jax.ShapeDtypeStruct((bsz, s, M_W), F32),
        jax.ShapeDtypeStruct((bsz, N_GATES, s), F32),
    )

    def cur(j):
        t = jnp.minimum(j, ntiles - 1)
        return t // nblk, t % nblk

    def old(j):
        t = jnp.maximum(j - 1, 0)
        return t // nblk, t % nblk

    cur_tile = lambda w: pl.BlockSpec((None, tm, w), lambda j: (*cur(j), 0))
    old_tile = lambda w: pl.BlockSpec((None, tm, w), lambda j: (*old(j), 0))
    const = lambda shp: pl.BlockSpec(shp, lambda j: (0, 0))
    return pl.pallas_call(
        functools.partial(_inproj_kernel, nblk=nblk),
        grid=(ntiles + 1,),
        in_specs=[
            cur_tile(d),
            pl.BlockSpec((None, N_MOD, d), lambda j: (cur(j)[0], 0, 0)),
            const((1, d)),
            pl.BlockSpec((d, IN_COLS_PAD), lambda j: (0, 0), pipeline_mode=pl.Buffered(1)),
            const((3, 2 * M_W)), const((1, 2 * M_W)),
        ],
        out_specs=[cur_tile(QKVA_W), old_tile(M_W),
                   pl.BlockSpec((None, tm * N_HEADS_M, HEAD_DIM_M), lambda j: (*old(j), 0)),
                   cur_tile(M_W), cur_tile(M_W),
                   pl.BlockSpec((None, N_GATES, tm), lambda j: (cur(j)[0], 0, cur(j)[1]))],
        out_shape=outs,
        scratch_shapes=[pltpu.VMEM((tm, 2 * M_W), F32), pltpu.VMEM((tm, 2 * M_W), F32),
                        pltpu.VMEM((tm, M_W), F32), pltpu.VMEM((8, 2 * M_W), F32)],
        compiler_params=pltpu.CompilerParams(
            dimension_semantics=("arbitrary",), vmem_limit_bytes=VMEM_LIMIT),
        name="inproj",
    )(x, mod, g1, w_in_p, conv_w, conv_b)


def _mlstm_reset(first, c_scr, m_scr):
    @pl.when(first)
    def _init():
        c_scr[...] = jnp.zeros_like(c_scr)
        m_scr[...] = jnp.zeros_like(m_scr)


LOG2E = 1.4426950408889634


def _mlstm_stream(q_ref, kt_ref, v_ref, gt_ref, bgr_ref, h_ref, c_scr, m_scr, *, reverse, nchunk):
    L = CHUNK
    ti = lax.broadcasted_iota(jnp.int32, (L, L), 0)
    si = lax.broadcasted_iota(jnp.int32, (L, L), 1)
    causal = (si >= ti) if reverse else (si <= ti)
    tri_t = jnp.where((ti >= si) if reverse else (ti <= si), 1.0, 0.0).astype(BF16)
    lane = lax.broadcasted_iota(jnp.int32, (1, L), 1)
    last = 0 if reverse else L - 1
    i_off, f_off = (2 * N_HEADS_M, 3 * N_HEADS_M) if reverse else (0, N_HEADS_M)
    ones = jnp.ones((L, HEAD_DIM_M), BF16)
    order = list(range(nchunk - 1, -1, -1) if reverse else range(nchunk))
    rows_of = lambda g: slice(g * L, (g + 1) * L)

    def split3(a):
        hi = a.astype(BF16)
        r1 = a - hi.astype(F32)
        mid = r1.astype(BF16)
        return hi, mid, (r1 - mid.astype(F32)).astype(BF16)

    gates = {}
    pad = jnp.zeros((L - N_GATES, L), F32)
    for g in order:
        gr = gt_ref[:, rows_of(g)] + bgr_ref[...]
        parts = _dot(jnp.concatenate(split3(jax.nn.log_sigmoid(gr)), axis=0), tri_t)
        bcr = (parts[0:N_GATES] + parts[N_GATES:2 * N_GATES] + parts[2 * N_GATES:3 * N_GATES]) * LOG2E
        gates[g] = (gr * LOG2E, bcr, jnp.concatenate([-bcr, pad], axis=0).T)
        yield

    def head(g, h):
        rows = rows_of(g)
        cols = slice(h * HEAD_DIM_M, (h + 1) * HEAD_DIM_M)
        gct, bct, nbcum = gates[g]
        r_row = gct[i_off + h:i_off + h + 1, :] - bct[f_off + h:f_off + h + 1, :]
        btot = jnp.sum(jnp.where(lane == last, bct[f_off + h:f_off + h + 1, :], 0.0),
                       axis=1, keepdims=True)
        rmax = jnp.max(r_row, axis=1, keepdims=True)
        a_max = btot + rmax
        m_prev = m_scr[h:h + 1, 0:1]
        m_new = jnp.maximum(btot + m_prev, a_max)
        m_scr[h:h + 1, :] = jnp.broadcast_to(m_new, (1, LANES))
        decay = jnp.exp2(btot + m_prev - m_new)
        w_row = jnp.exp2(r_row - rmax) * jnp.exp2(a_max - m_new)
        kt_rows = slice((g * N_HEADS_M + h) * HEAD_DIM_M, (g * N_HEADS_M + h + 1) * HEAD_DIM_M)
        cm = jnp.max(jnp.where(causal, r_row, -jnp.inf), axis=1, keepdims=True)
        yield
        q = q_ref[rows, cols]
        s = _dot(q, kt_ref[kt_rows, :])
        u = jnp.maximum(cm, m_prev)
        ub = jnp.broadcast_to(u, (L, L))
        dm = jnp.exp2(jnp.where(causal, r_row - ub, -jnp.inf))
        qi = q * jnp.exp2(m_prev - ub).astype(BF16)
        floor = jnp.exp2(jnp.broadcast_to(nbcum[:, f_off + h:f_off + h + 1], (L, L)) - ub)
        yield
        vaug = jnp.concatenate([v_ref[rows, cols], ones], axis=1)
        lhs = jnp.concatenate([(s * dm).astype(BF16), qi], axis=1)
        rhs = jnp.concatenate([vaug, c_scr[h].astype(BF16)], axis=0)
        out = _dot(lhs, rhs)
        upd = _dot(kt_ref[kt_rows, :] * w_row.astype(BF16), vaug)
        yield
        c_scr[h] = decay * c_scr[h] + upd
        h_ref[rows, cols] = out[:, :HEAD_DIM_M] / jnp.maximum(jnp.abs(out[:, HEAD_DIM_M:]), floor)
        yield

    n_stage, lag = 4, 2
    heads = {}
    for slot in range(lag * (nchunk - 1) + n_stage):
        for i, g in enumerate(order):
            stage = slot - lag * i
            if 0 <= stage < n_stage:
                for h in range(N_HEADS_M):
                    if stage == 0:
                        heads[g, h] = head(g, h)
                    next(heads[g, h])
                    yield


def _mixer_kernel(qf_ref, kf_ref, vf_ref, gtf_ref, qb_ref, kb_ref, vb_ref, gtb_ref, bgr_ref,
                  sink_ref, qa_ref, kvp_ref, kvn_ref, ga_ref, *rest, nblk, nchunk, ncast):
    w_refs, rest = rest[:ncast], rest[ncast:]
    hf_ref, hb_ref, att_ref = rest[:3]
    wo_refs, (cf_scr, mf_scr, cb_scr, mb_scr, bias_scr) = rest[3:3 + ncast], rest[3 + ncast:]
    b = pl.program_id(0)
    j = pl.program_id(1)

    @pl.when((b == 0) & (j == 0))
    def _init():
        _attn_bias_init(bias_scr)

    _mlstm_reset(j == 0, cf_scr, mf_scr)
    _mlstm_reset(j == 0, cb_scr, mb_scr)
    for w_ref, wo_ref in zip(w_refs, wo_refs):
        wo_ref[...] = w_ref[...].astype(wo_ref.dtype)
    fwd = _mlstm_stream(qf_ref, kf_ref, vf_ref, gtf_ref, bgr_ref, hf_ref, cf_scr, mf_scr,
                        reverse=False, nchunk=nchunk)
    bwd = _mlstm_stream(qb_ref, kb_ref, vb_ref, gtb_ref, bgr_ref, hb_ref, cb_scr, mb_scr,
                        reverse=True, nchunk=nchunk)
    att = _attn_pieces(j == 0, j == nblk - 1, sink_ref, qa_ref, kvp_ref, kvn_ref, ga_ref, att_ref,
                       bias_scr, [], nsub=nchunk)
    _alternate(fwd, bwd)
    for _ in att:
        pass


def _mixer_call(qc, kct, vm, gates_t, bg_rows, sink, qkva, g_attn, weights, layer, tm):
    bsz, s, _ = qc.shape
    nblk = s // tm
    nsteps = bsz * nblk
    nsub = tm // BLOCK
    nb = s // BLOCK
    kvw = QKVA_W - ATT_Q
    attn_specs = [
        pl.BlockSpec(memory_space=pltpu.SMEM),
        pl.BlockSpec((None, tm, QKVA_W), lambda b, j: (b, j, 0)),
        pl.BlockSpec((None, BLOCK, kvw), lambda b, j: (b, jnp.maximum(j * nsub - 1, 0), 1)),
        pl.BlockSpec((None, BLOCK, kvw), lambda b, j: (b, jnp.minimum((j + 1) * nsub, nb - 1), 1)),
        pl.BlockSpec((1, ATT_Q), lambda b, j: (0, 0)),
    ]
    w_specs, w_shapes = [], []
    for w in weights:
        _, k, n = w.shape
        rows = k // nsteps
        assert rows * nsteps == k and rows % 16 == 0
        w_specs.append(pl.BlockSpec((None, rows, n), lambda b, j: (layer, b * nblk + j, 0)))
        w_shapes.append(jax.ShapeDtypeStruct((k, n), BF16))
    wo_specs = [pl.BlockSpec((sp.block_shape[1], sp.block_shape[2]), lambda b, j: (b * nblk + j, 0))
                for sp in w_specs]

    def specs(pos):
        tile = lambda w: pl.BlockSpec((None, tm, w), lambda b, j: (b, pos(j), 0))
        kt_spec = pl.BlockSpec((None, tm * N_HEADS_M, HEAD_DIM_M), lambda b, j: (b, pos(j), 0))
        gt_spec = pl.BlockSpec((None, N_GATES, tm), lambda b, j: (b, 0, pos(j)))
        return [tile(M_W), kt_spec, tile(M_W), gt_spec], tile(M_W)

    in_f, out_f = specs(lambda j: j)
    in_b, out_b = specs(lambda j: nblk - 1 - j)
    state = [pltpu.VMEM((N_HEADS_M, HEAD_DIM_M, 2 * HEAD_DIM_M), F32), pltpu.VMEM((8, LANES), F32)]
    outs = pl.pallas_call(
        functools.partial(_mixer_kernel, nblk=nblk, nchunk=tm // CHUNK, ncast=len(weights)),
        grid=(bsz, nblk),
        in_specs=in_f + in_b + [pl.BlockSpec((N_GATES, LANES), lambda b, j: (0, 0))] + attn_specs
        + w_specs,
        out_specs=[out_f, out_b, pl.BlockSpec((None, tm, ATT_Q), lambda b, j: (b, j, 0))] + wo_specs,
        out_shape=[jax.ShapeDtypeStruct((bsz, s, M_W), F32)] * 2
        + [jax.ShapeDtypeStruct((bsz, s, ATT_Q), BF16)] + w_shapes,
        scratch_shapes=state + state + [pltpu.VMEM((3 * N_HEADS_ATT, BLOCK, 3 * BLOCK), F32)],
        compiler_params=pltpu.CompilerParams(
            dimension_semantics=("arbitrary", "arbitrary"), vmem_limit_bytes=VMEM_LIMIT),
        name="mixer",
    )(qc, kct, vm, gates_t, qc, kct, vm, gates_t, bg_rows, sink, qkva, qkva, qkva, g_attn, *weights)
    return outs[0], outs[1], outs[2], outs[3:]


def _zero_after(a):
    bits = pltpu.bitcast(a, jnp.uint32)
    z = lax.shift_right_logical(lax.shift_right_logical(bits, jnp.uint32(16)), jnp.uint32(16))
    return pltpu.bitcast(z, F32)


def _attn_bias_init(bias_scr):
    nk = 3 * BLOCK
    row = lax.broadcasted_iota(jnp.int32, (BLOCK, nk), 0)
    col = lax.broadcasted_iota(jnp.int32, (BLOCK, nk), 1)
    dist = jnp.abs(col - BLOCK - row)
    distf = dist.astype(F32)
    for var in range(3):
        ok = dist <= WINDOW
        if var == 1:
            ok = ok & (col >= BLOCK)
        elif var == 2:
            ok = ok & (col < 2 * BLOCK)
        for h in range(N_HEADS_ATT):
            slope = 2.0 ** (-8.0 * (h + 1.0) / N_HEADS_ATT)
            bias_scr[var * N_HEADS_ATT + h] = jnp.where(ok, (-slope * LOG2E) * distf, -jnp.inf)


def _attn_pieces(first, last, sink_ref, q_ref, kvp_ref, kvn_ref, g_ref, o_ref, bias_scr, anchors,
                 *, nsub):
    nk = 3 * BLOCK

    lane_k = lax.broadcasted_iota(jnp.int32, (nk, LANES), 1)
    ones_a = jnp.where(lane_k < HEAD_DIM_ATT, 1.0, 0.0).astype(BF16)
    ones_b = jnp.where(lane_k < HEAD_DIM_ATT, 0.0, 1.0).astype(BF16)
    lo_half_q = lax.broadcasted_iota(jnp.int32, (BLOCK, LANES), 1) < HEAD_DIM_ATT

    def kv_block(idx):
        if idx < 0:
            return kvp_ref[...]
        if idx >= nsub:
            return kvn_ref[...]
        return q_ref[idx * BLOCK:(idx + 1) * BLOCK, ATT_Q:QKVA_W]

    for n in range(nsub):
        rows = slice(n * BLOCK, (n + 1) * BLOCK)
        kv = jnp.concatenate([kv_block(n - 1), kv_block(n), kv_block(n + 1)], axis=0)
        if n == 0:
            var = jnp.where(first, 1, 0)
        elif n == nsub - 1:
            var = jnp.where(last, 2, 0)
        else:
            var = 0
        pieces = []
        for kvh in range(N_KV_HEADS):
            k_st, k_sw = kv[:, 0:LANES], kv[:, LANES:2 * LANES]
            v_st, v_sw = kv[:, 2 * LANES:3 * LANES], kv[:, 3 * LANES:4 * LANES]
            if kvh == 0:
                k_lo, k_hi, v_lo, v_hi = k_st, k_sw, v_st, v_sw
            else:
                k_lo, k_hi, v_lo, v_hi = k_sw, k_st, v_sw, v_st
            kk = jnp.concatenate([k_lo * ones_a, k_hi * ones_b], axis=0)
            vv = jnp.concatenate([
                jnp.concatenate([v_lo * ones_a, ones_a], axis=1),
                jnp.concatenate([v_hi * ones_b, ones_b], axis=1)], axis=0)
            for pair in range(GROUP_SIZE // 2):
                h0 = kvh * GROUP_SIZE + 2 * pair
                qp = q_ref[rows, h0 * HEAD_DIM_ATT:(h0 + 2) * HEAD_DIM_ATT]
                s2 = _dot_nt(qp, kk)
                tie = _zero_after(anchors[-1])[:, 0:1] if anchors else 0.0
                ps, es = [], []
                for t in range(2):
                    logits = s2[:, t * nk:(t + 1) * nk] + bias_scr[var * N_HEADS_ATT + h0 + t]
                    sink = sink_ref[h0 + t] * LOG2E + tie
                    mx = jnp.maximum(jnp.max(logits, axis=-1, keepdims=True), sink)
                    ps.append(jnp.exp2(logits - mx).astype(BF16))
                    es.append(jnp.exp2(sink - mx))
                res = _dot(jnp.concatenate(ps, axis=1), vv)
                den = res[:, LANES:] + jnp.where(lo_half_q, es[0], es[1])
                pieces.append(res[:, :LANES] / den)
                yield
        att = jnp.concatenate(pieces, axis=1)
        o_ref[rows, :] = _rms(att, g_ref[...]).astype(o_ref.dtype)
        yield


def _outffn_pieces(x_ref, att_ref, hf_ref, hb_ref, om_ref, mod_ref, gm_ref, wo_ref, g2_ref,
                   w1_ref, w2_ref, gf_ref, o_ref, hid_scr, anchors, *, final):
    hs = hf_ref[...] + hb_ref[...]
    parts = []
    for h in range(N_HEADS_M):
        cols = slice(h * HEAD_DIM_M, (h + 1) * HEAD_DIM_M)
        parts.append(_rms(hs[:, cols], gm_ref[:, cols]))
    hm = jax.nn.sigmoid(om_ref[...]) * jnp.concatenate(parts, axis=1)
    mixin = jnp.concatenate([att_ref[...], hm.astype(BF16)], axis=1)
    mix = _dot(mixin, wo_ref[...])
    anchors.append(mix[0:1, 0:LANES])
    x1 = x_ref[...] + mod_ref[2:3, :] * mix
    hff = (_rms(x1, g2_ref[...]) * (1.0 + mod_ref[4:5, :]) + mod_ref[3:4, :]).astype(BF16)
    yield
    for c in range(N_FF_CHUNKS):
        gate = _dot(hff, w1_ref[:, FF_CHUNK * c:FF_CHUNK * (c + 1)])
        up = _dot(hff, w1_ref[:, D_FF + FF_CHUNK * c:D_FF + FF_CHUNK * (c + 1)])
        anchors.append(gate[0:1, 0:LANES])
        hid_scr[:, FF_CHUNK * c:FF_CHUNK * (c + 1)] = (jax.nn.silu(gate) * up).astype(BF16)
        yield
    ff = _dot(hid_scr[...], w2_ref[...])
    anchors.append(ff[0:1, 0:LANES])
    x2 = x1 + mod_ref[5:6, :] * ff
    if final:
        x2 = _rms(x2, gf_ref[...])
    o_ref[...] = x2
    yield


def _outffn_kernel(*refs, final):
    for _ in _outffn_pieces(*refs, [], final=final):
        pass


def _outffn_call(x, att, hf, hb, om, mod, g_m, w_out, g2, w1, w2, g_final, tm, final):
    bsz, s, d = x.shape
    tile = lambda w: pl.BlockSpec((None, tm, w), lambda b, i: (b, i, 0))
    const = lambda shp: pl.BlockSpec(shp, lambda b, i: (0, 0))
    weight = lambda shp: pl.BlockSpec(shp, lambda b, i: (0, 0), pipeline_mode=pl.Buffered(1))
    return pl.pallas_call(
        functools.partial(_outffn_kernel, final=final),
        grid=(bsz, s // tm),
        in_specs=[
            tile(d), tile(ATT_Q), tile(M_W), tile(M_W), tile(M_W),
            pl.BlockSpec((None, N_MOD, d), lambda b, i: (b, 0, 0)),
            const((1, M_W)), weight((ATT_Q + M_W, d)), const((1, d)),
            weight((d, 2 * D_FF)), weight((D_FF, d)), const((1, d)),
        ],
        out_specs=tile(d),
        out_shape=jax.ShapeDtypeStruct((bsz, s, d), F32),
        scratch_shapes=[pltpu.VMEM((tm, D_FF), BF16)],
        compiler_params=pltpu.CompilerParams(
            dimension_semantics=("arbitrary", "arbitrary"), vmem_limit_bytes=VMEM_LIMIT),
        name="outffn",
    )(x, att, hf, hb, om, mod, g_m, w_out, g2, w1, w2, g_final)


def _layer(l, x, c, w_mod, b_mod, g_norm1, w_in, conv_w, conv_b, b_gates, sink,
           g_attn_out, g_mlstm_out, w_out, g_norm2, w_ffn_in, w_ffn_out, g_final, final):
    d = x.shape[-1]
    mod = _mod_call(c, w_mod[l], b_mod[l])

    w_in_p = _cast_call(w_in, l, IN_COLS_PAD)
    qkva, qc, kct, vm, om, gates_t = _inproj_call(
        x, mod, g_norm1[l].reshape(1, d), w_in_p, conv_w[l], conv_b[l].reshape(1, 2 * M_W), tm=512)

    bg_rows = jnp.broadcast_to(b_gates[l][:, None], (N_GATES, LANES))
    hf, hb, att, (w_out_b, w_ffn_in_b, w_ffn_out_b) = _mixer_call(
        qc, kct, vm, gates_t, bg_rows, sink[l], qkva, g_attn_out[l].reshape(1, ATT_Q),
        (w_out, w_ffn_in, w_ffn_out), l, tm=1024)

    return _outffn_call(x, att, hf, hb, om, mod, g_mlstm_out[l].reshape(1, M_W), w_out_b,
                        g_norm2[l].reshape(1, d), w_ffn_in_b, w_ffn_out_b,
                        g_final.reshape(1, d), tm=512, final=final)


def kernel(x, c, w_mod, b_mod, g_norm1, w_in, conv_w, conv_b, b_gates, sink, g_attn_out,
           g_mlstm_out, w_out, g_norm2, w_ffn_in, w_ffn_out, g_final):
    depth = w_mod.shape[0]
    for l in range(depth):
        x = _layer(l, x, c, w_mod, b_mod, g_norm1, w_in, conv_w, conv_b, b_gates, sink, g_attn_out,
                   g_mlstm_out, w_out, g_norm2, w_ffn_in, w_ffn_out, g_final, final=(l == depth - 1))
    return x
```

```python
import functools

import jax
import jax.numpy as jnp
from jax import lax
from jax.experimental import pallas as pl
from jax.experimental.pallas import tpu as pltpu

F32 = jnp.float32
BF16 = jnp.bfloat16

D_MODEL = 1024
EPS = 1e-6
N_HEADS_ATT = 8
N_KV_HEADS = 2
HEAD_DIM_ATT = 64
GROUP_SIZE = N_HEADS_ATT // N_KV_HEADS
WINDOW = 128
BLOCK = 128
N_HEADS_M = 4
HEAD_DIM_M = 128
CHUNK = 128
ATT_Q = N_HEADS_ATT * HEAD_DIM_ATT
ATT_KV = N_KV_HEADS * HEAD_DIM_ATT
M_W = N_HEADS_M * HEAD_DIM_M
N_GATES = 4 * N_HEADS_M
D_FF = 2816
N_MOD = 6

LANES = 128
GATE_PAD = LANES
FF_CHUNK = 256
N_FF_CHUNKS = D_FF // FF_CHUNK
OUT_COL_BLOCK = 256
VMEM_LIMIT = 56 * 1024 * 1024

C_QA = 0
C_KA = ATT_Q
C_VA = ATT_Q + ATT_KV
C_QKM = ATT_Q + 2 * ATT_KV
C_VM = C_QKM + 2 * M_W
C_OM = C_VM + M_W
C_G = C_OM + M_W
IN_COLS_PAD = C_G + GATE_PAD
QKVA_W = ATT_Q + 4 * ATT_KV


def _dot(a, b):
    return jnp.dot(a, b, preferred_element_type=F32)


def _dot_nt(a, b):
    return lax.dot_general(a, b, (((1,), (1,)), ((), ())), preferred_element_type=F32)


def _rms(x, g):
    return x * lax.rsqrt(jnp.mean(x * x, axis=-1, keepdims=True) + EPS) * g


def _alternate(*streams):
    live = list(streams)
    while live:
        for s in list(live):
            if next(s, StopIteration) is StopIteration:
                live.remove(s)


def _cast_t_kernel(wt_ref, o_ref):
    n = wt_ref.shape[0]
    full = n // LANES * LANES
    for r in range(0, full, LANES):
        o_ref[:, r:r + LANES] = wt_ref[r:r + LANES, :].T.astype(o_ref.dtype)
    if o_ref.shape[1] > full:
        tail = jnp.concatenate(
            [wt_ref[full:n, :], jnp.zeros((full + LANES - n, wt_ref.shape[1]), wt_ref.dtype)], axis=0)
        o_ref[:, full:full + LANES] = tail.T.astype(o_ref.dtype)


def _cast_t_call(w, layer, n_out, bk=256):
    _, k, n = w.shape
    assert n_out - n < LANES and n_out % LANES == 0
    return pl.pallas_call(
        _cast_t_kernel,
        grid=(k // bk,),
        in_specs=[pl.BlockSpec((None, n, bk), lambda i: (layer, 0, i))],
        out_specs=pl.BlockSpec((bk, n_out), lambda i: (i, 0)),
        out_shape=jax.ShapeDtypeStruct((k, n_out), BF16),
        compiler_params=pltpu.CompilerParams(
            dimension_semantics=("arbitrary",), vmem_limit_bytes=VMEM_LIMIT),
        name="cast",
    )(jnp.swapaxes(w, 1, 2))


def _mod_kernel(c_ref, w_ref, b_ref, o_ref):
    s = jax.nn.silu(c_ref[...]).astype(BF16)
    o_ref[...] = _dot(s, w_ref[...].astype(BF16)) + b_ref[...]


def _mod_call(c, w_mod, b_mod):
    bsz = c.shape[0]
    rows = 8
    cp = jnp.pad(c, ((0, rows - bsz), (0, 0)))
    n = w_mod.shape[1]
    bn = 1024
    out = pl.pallas_call(
        _mod_kernel,
        grid=(n // bn,),
        in_specs=[
            pl.BlockSpec((rows, D_MODEL), lambda i: (0, 0)),
            pl.BlockSpec((D_MODEL, bn), lambda i: (0, i)),
            pl.BlockSpec((1, bn), lambda i: (0, i)),
        ],
        out_specs=pl.BlockSpec((rows, bn), lambda i: (0, i)),
        out_shape=jax.ShapeDtypeStruct((rows, n), F32),
        compiler_params=pltpu.CompilerParams(dimension_semantics=("arbitrary",)),
        name="mod",
    )(cp, w_mod, b_mod.reshape(1, n))
    return out[:bsz].reshape(bsz, N_MOD, D_MODEL)


MXU_COLS = 256
CONV_ROWS = 64


def _inproj_kernel(x_ref, mod_ref, g_ref, w_ref, cw_ref, cb_ref,
                   qkva_ref, qc_ref, kct_ref, vm_ref, om_ref, gt_ref,
                   raw_scr, new_scr, k_scr, carry_scr, *, nblk):
    j = pl.program_id(0)

    @pl.when(j == 0)
    def _init():
        raw_scr[...] = jnp.zeros_like(raw_scr)
        carry_scr[...] = jnp.zeros_like(carry_scr)

    x = x_ref[...]
    h = _rms(x, g_ref[...]) * (1.0 + mod_ref[1:2, :]) + mod_ref[0:1, :]
    hb = h.astype(BF16)
    tm = x.shape[0]
    nchunk = tm // CHUNK
    within = (j + nblk - 1) % nblk

    anchors = []

    def zero_after(a):
        return jnp.concatenate([_zero_after(a)] * (2 * M_W // LANES), axis=1)

    def project():
        order = list(range(C_QKM, C_VM, MXU_COLS)) + list(range(0, C_QKM, MXU_COLS)) + \
            list(range(C_VM, IN_COLS_PAD, MXU_COLS))
        for c0 in order:
            c1 = min(c0 + MXU_COLS, IN_COLS_PAD)
            res = _dot(hb, w_ref[:, c0:c1])
            anchors.append(res[0:1, 0:LANES])
            if c0 < C_KA:
                qkva_ref[:, c0:c1] = (res * (HEAD_DIM_ATT ** -0.5 * LOG2E)).astype(BF16)
            elif c0 < C_QKM:
                ka, va = res[:, :ATT_KV], res[:, ATT_KV:]
                half = HEAD_DIM_ATT
                qkva_ref[:, ATT_Q:ATT_Q + ATT_KV] = ka.astype(BF16)
                qkva_ref[:, ATT_Q + ATT_KV:ATT_Q + 2 * ATT_KV] = pltpu.roll(ka, half, axis=1).astype(BF16)
                qkva_ref[:, ATT_Q + 2 * ATT_KV:ATT_Q + 3 * ATT_KV] = va.astype(BF16)
                qkva_ref[:, ATT_Q + 3 * ATT_KV:ATT_Q + 4 * ATT_KV] = pltpu.roll(va, half, axis=1).astype(BF16)
            elif c0 < C_VM:
                new_scr[:, c0 - C_QKM:c1 - C_QKM] = res
            elif c0 < C_OM:
                vm_ref[:, c0 - C_VM:c1 - C_VM] = res.astype(BF16)
            elif c0 < C_G:
                om_ref[:, c0 - C_OM:c1 - C_OM] = res
            else:
                for g in range(nchunk):
                    blk = res[g * CHUNK:(g + 1) * CHUNK, :].T
                    gt_ref[:, g * CHUNK:(g + 1) * CHUNK] = blk[0:N_GATES, :]
            yield

    def conv():
        R = CONV_ROWS
        row = lax.broadcasted_iota(jnp.int32, (R, 1), 0)
        for p in range(tm // R):
            r0 = p * R
            xg = raw_scr[r0:r0 + R, :]
            if p == 0:
                prev_row = jnp.where(within == 0, 0.0, carry_scr[0:1, :])
            else:
                prev_row = raw_scr[r0 - 1:r0, :]
            if r0 + R == tm:
                next_row = jnp.where(within == nblk - 1, 0.0, new_scr[0:1, :])
            else:
                next_row = raw_scr[r0 + R:r0 + R + 1, :]
            xm1 = jnp.where(row == 0, prev_row, pltpu.roll(xg, 1, axis=0))
            xp1 = jnp.where(row == R - 1, next_row, pltpu.roll(xg, R - 1, axis=0))
            z = zero_after(anchors[-1])
            y =xm1 * (cw_ref[0:1, :] + z) + xg * (cw_ref[1:2, :] + z) + xp1 * (cw_ref[2:3, :] + z) \
                + cb_ref[...]
            y = jax.nn.silu(y)
            qc_ref[r0:r0 + R, :] = (y[:, :M_W] * (HEAD_DIM_M ** -0.5)).astype(BF16)
            k_scr[r0:r0 + R, :] = y[:, M_W:]
            yield
            if (r0 + R) % CHUNK == 0:
                g = r0 // CHUNK
                for hd in range(N_HEADS_M):
                    blk = k_scr[g * CHUNK:(g + 1) * CHUNK, hd * HEAD_DIM_M:(hd + 1) * HEAD_DIM_M]
                    base = (g * N_HEADS_M + hd) * HEAD_DIM_M
                    kct_ref[base:base + HEAD_DIM_M, :] = blk.T.astype(BF16)
                yield

    _alternate(project(), conv())
    carry_scr[0:1, :] = raw_scr[tm - 1:tm, :]
    raw_scr[...] = new_scr[...]


def _inproj_call(x, mod, g1, w_in_p, conv_w, conv_b, tm):
    bsz, s, d = x.shape
    nblk = s // tm
    ntiles = bsz * nblk
    outs = (
        jax.ShapeDtypeStruct((bsz, s, QKVA_W), BF16),
        jax.ShapeDtypeStruct((bsz, s, M_W), BF16),
        jax.ShapeDtypeStruct((bsz, s * N_HEADS_M, HEAD_DIM_M), BF16),
        jax.ShapeDtypeStruct((bsz, s, M_W), BF16),
        jax.ShapeDtypeStruct((bsz, s, M_W), F32),
        jax.ShapeDtypeStruct((bsz, N_GATES, s), F32),
    )

    def cur(j):
        t = jnp.minimum(j, ntiles - 1)
        return t // nblk, t % nblk

    def old(j):
        t = jnp.maximum(j - 1, 0)
        return t // nblk, t % nblk

    cur_tile = lambda w: pl.BlockSpec((None, tm, w), lambda j: (*cur(j), 0))
    old_tile = lambda w: pl.BlockSpec((None, tm, w), lambda j: (*old(j), 0))
    const = lambda shp: pl.BlockSpec(shp, lambda j: (0, 0))
    return pl.pallas_call(
        functools.partial(_inproj_kernel, nblk=nblk),
        grid=(ntiles + 1,),
        in_specs=[
            cur_tile(d),
            pl.BlockSpec((None, N_MOD, d), lambda j: (cur(j)[0], 0, 0)),
            const((1, d)),
            pl.BlockSpec((d, IN_COLS_PAD), lambda j: (0, 0), pipeline_mode=pl.Buffered(1)),
            const((3, 2 * M_W)), const((1, 2 * M_W)),
        ],
        out_specs=[cur_tile(QKVA_W), old_tile(M_W),
                   pl.BlockSpec((None, tm * N_HEADS_M, HEAD_DIM_M), lambda j: (*old(j), 0)),
                   cur_tile(M_W), cur_tile(M_W),
                   pl.BlockSpec((None, N_GATES, tm), lambda j: (cur(j)[0], 0, cur(j)[1]))],
        out_shape=outs,
        scratch_shapes=[pltpu.VMEM((tm, 2 * M_W), F32), pltpu.VMEM((tm, 2 * M_W), F32),
                        pltpu.VMEM((tm, M_W), F32), pltpu.VMEM((8, 2 * M_W), F32)],
        compiler_params=pltpu.CompilerParams(
            dimension_semantics=("arbitrary",), vmem_limit_bytes=VMEM_LIMIT),
        name="inproj",
    )(x, mod, g1, w_in_p, conv_w, conv_b)


def _mlstm_reset(first, c_scr, m_scr):
    @pl.when(first)
    def _init():
        c_scr[...] = jnp.zeros_like(c_scr)
        m_scr[...] = jnp.zeros_like(m_scr)


LOG2E = 1.4426950408889634


def _mlstm_stream(q_ref, kt_ref, v_ref, gt_ref, bgr_ref, h_ref, c_scr, m_scr, *, reverse, nchunk):
    L = CHUNK
    ti = lax.broadcasted_iota(jnp.int32, (L, L), 0)
    si = lax.broadcasted_iota(jnp.int32, (L, L), 1)
    causal = (si >= ti) if reverse else (si <= ti)
    tri_t = jnp.where((ti >= si) if reverse else (ti <= si), 1.0, 0.0).astype(BF16)
    lane = lax.broadcasted_iota(jnp.int32, (1, L), 1)
    last = 0 if reverse else L - 1
    i_off, f_off = (2 * N_HEADS_M, 3 * N_HEADS_M) if reverse else (0, N_HEADS_M)
    ones = jnp.ones((L, HEAD_DIM_M), BF16)
    order = list(range(nchunk - 1, -1, -1) if reverse else range(nchunk))
    rows_of = lambda g: slice(g * L, (g + 1) * L)

    def split3(a):
        hi = a.astype(BF16)
        r1 = a - hi.astype(F32)
        mid = r1.astype(BF16)
        return hi, mid, (r1 - mid.astype(F32)).astype(BF16)

    gates = {}
    pad = jnp.zeros((L - N_GATES, L), F32)
    for g in order:
        gr = gt_ref[:, rows_of(g)] + bgr_ref[...]
        parts = _dot(jnp.concatenate(split3(jax.nn.log_sigmoid(gr)), axis=0), tri_t)
        bcr = (parts[0:N_GATES] + parts[N_GATES:2 * N_GATES] + parts[2 * N_GATES:3 * N_GATES]) * LOG2E
        gates[g] = (gr * LOG2E, bcr, jnp.concatenate([-bcr, pad], axis=0).T)
        yield

    def head(g, h):
        rows = rows_of(g)
        cols = slice(h * HEAD_DIM_M, (h + 1) * HEAD_DIM_M)
        gct, bct, nbcum = gates[g]
        r_row = gct[i_off + h:i_off + h + 1, :] - bct[f_off + h:f_off + h + 1, :]
        btot = jnp.sum(jnp.where(lane == last, bct[f_off + h:f_off + h + 1, :], 0.0),
                       axis=1, keepdims=True)
        rmax = jnp.max(r_row, axis=1, keepdims=True)
        a_max = btot + rmax
        m_prev = m_scr[h:h + 1, 0:1]
        m_new = jnp.maximum(btot + m_prev, a_max)
        m_scr[h:h + 1, :] = jnp.broadcast_to(m_new, (1, LANES))
        decay = jnp.exp2(btot + m_prev - m_new)
        w_row = jnp.exp2(r_row - rmax) * jnp.exp2(a_max - m_new)
        kt_rows = slice((g * N_HEADS_M + h) * HEAD_DIM_M, (g * N_HEADS_M + h + 1) * HEAD_DIM_M)
        cm = jnp.max(jnp.where(causal, r_row, -jnp.inf), axis=1, keepdims=True)
        yield
        q = q_ref[rows, cols]
        s = _dot(q, kt_ref[kt_rows, :])
        u = jnp.maximum(cm, m_prev)
        ub = jnp.broadcast_to(u, (L, L))
        dm = jnp.exp2(jnp.where(causal, r_row - ub, -jnp.inf))
        qi = q * jnp.exp2(m_prev - ub).astype(BF16)
        floor = jnp.exp2(jnp.broadcast_to(nbcum[:, f_off + h:f_off + h + 1], (L, L)) - ub)
        yield
        vaug = jnp.concatenate([v_ref[rows, cols], ones], axis=1)
        lhs = jnp.concatenate([(s * dm).astype(BF16), qi], axis=1)
        rhs = jnp.concatenate([vaug, c_scr[h].astype(BF16)], axis=0)
        out = _dot(lhs, rhs)
        upd = _dot(kt_ref[kt_rows, :] * w_row.astype(BF16), vaug)
        yield
        c_scr[h] = decay * c_scr[h] + upd
        h_ref[rows, cols] = out[:, :HEAD_DIM_M] / jnp.maximum(jnp.abs(out[:, HEAD_DIM_M:]), floor)
        yield

    n_stage, lag = 4, 2
    heads = {}
    for slot in range(lag * (nchunk - 1) + n_stage):
        for i, g in enumerate(order):
            stage = slot - lag * i
            if 0 <= stage < n_stage:
                for h in range(N_HEADS_M):
                    if stage == 0:
                        heads[g, h] = head(g, h)
                    next(heads[g, h])
                    yield


def _mixer_kernel(qf_ref, kf_ref, vf_ref, gtf_ref, qb_ref, kb_ref, vb_ref, gtb_ref, bgr_ref,
                  sink_ref, qa_ref, kvp_ref, kvn_ref, ga_ref, *rest, nblk, nchunk, ncast):
    w_refs, rest = rest[:ncast], rest[ncast:]
    hf_ref, hb_ref, att_ref = rest[:3]
    wo_refs, (cf_scr, mf_scr, cb_scr, mb_scr, bias_scr) = rest[3:3 + ncast], rest[3 + ncast:]
    b = pl.program_id(0)
    j = pl.program_id(1)

    @pl.when((b == 0) & (j == 0))
    def _init():
        _attn_bias_init(bias_scr)

    _mlstm_reset(j == 0, cf_scr, mf_scr)
    _mlstm_reset(j == 0, cb_scr, mb_scr)
    for w_ref, wo_ref in zip(w_refs, wo_refs):
        wo_ref[...] = w_ref[...].astype(wo_ref.dtype)
    fwd = _mlstm_stream(qf_ref, kf_ref, vf_ref, gtf_ref, bgr_ref, hf_ref, cf_scr, mf_scr,
                        reverse=False, nchunk=nchunk)
    bwd = _mlstm_stream(qb_ref, kb_ref, vb_ref, gtb_ref, bgr_ref, hb_ref, cb_scr, mb_scr,
                        reverse=True, nchunk=nchunk)
    att = _attn_pieces(j == 0, j == nblk - 1, sink_ref, qa_ref, kvp_ref, kvn_ref, ga_ref, att_ref,
                       bias_scr, [], nsub=nchunk)
    _alternate(fwd, bwd)
    for _ in att:
        pass


def _mixer_call(qc, kct, vm, gates_t, bg_rows, sink, qkva, g_attn, weights, layer, tm):
    bsz, s, _ = qc.shape
    nblk = s // tm
    nsteps = bsz * nblk
    nsub = tm // BLOCK
    nb = s // BLOCK
    kvw = QKVA_W - ATT_Q
    attn_specs = [
        pl.BlockSpec(memory_space=pltpu.SMEM),
        pl.BlockSpec((None, tm, QKVA_W), lambda b, j: (b, j, 0)),
        pl.BlockSpec((None, BLOCK, kvw), lambda b, j: (b, jnp.maximum(j * nsub - 1, 0), 1)),
        pl.BlockSpec((None, BLOCK, kvw), lambda b, j: (b, jnp.minimum((j + 1) * nsub, nb - 1), 1)),
        pl.BlockSpec((1, ATT_Q), lambda b, j: (0, 0)),
    ]
    w_specs, w_shapes = [], []
    for w in weights:
        _, k, n = w.shape
        rows = k // nsteps
        assert rows * nsteps == k and rows % 16 == 0
        w_specs.append(pl.BlockSpec((None, rows, n), lambda b, j: (layer, b * nblk + j, 0)))
        w_shapes.append(jax.ShapeDtypeStruct((k, n), BF16))
    wo_specs = [pl.BlockSpec((sp.block_shape[1], sp.block_shape[2]), lambda b, j: (b * nblk + j, 0))
                for sp in w_specs]

    def specs(pos):
        tile = lambda w: pl.BlockSpec((None, tm, w), lambda b, j: (b, pos(j), 0))
        kt_spec = pl.BlockSpec((None, tm * N_HEADS_M, HEAD_DIM_M), lambda b, j: (b, pos(j), 0))
        gt_spec = pl.BlockSpec((None, N_GATES, tm), lambda b, j: (b, 0, pos(j)))
        return [tile(M_W), kt_spec, tile(M_W), gt_spec], tile(M_W)

    in_f, out_f = specs(lambda j: j)
    in_b, out_b = specs(lambda j: nblk - 1 - j)
    state = [pltpu.VMEM((N_HEADS_M, HEAD_DIM_M, 2 * HEAD_DIM_M), F32), pltpu.VMEM((8, LANES), F32)]
    outs = pl.pallas_call(
        functools.partial(_mixer_kernel, nblk=nblk, nchunk=tm // CHUNK, ncast=len(weights)),
        grid=(bsz, nblk),
        in_specs=in_f + in_b + [pl.BlockSpec((N_GATES, LANES), lambda b, j: (0, 0))] + attn_specs
        + w_specs,
        out_specs=[out_f, out_b, pl.BlockSpec((None, tm, ATT_Q), lambda b, j: (b, j, 0))] + wo_specs,
        out_shape=[jax.ShapeDtypeStruct((bsz, s, M_W), F32)] * 2
        + [jax.ShapeDtypeStruct((bsz, s, ATT_Q), BF16)] + w_shapes,
        scratch_shapes=state + state + [pltpu.VMEM((3 * N_HEADS_ATT, BLOCK, 3 * BLOCK), F32)],
        compiler_params=pltpu.CompilerParams(
            dimension_semantics=("arbitrary", "arbitrary"), vmem_limit_bytes=VMEM_LIMIT),
        name="mixer",
    )(qc, kct, vm, gates_t, qc, kct, vm, gates_t, bg_rows, sink, qkva, qkva, qkva, g_attn, *weights)
    return outs[0], outs[1], outs[2], outs[3:]


def _zero_after(a):
    bits = pltpu.bitcast(a, jnp.uint32)
    z = lax.shift_right_logical(lax.shift_right_logical(bits, jnp.uint32(16)), jnp.uint32(16))
    return pltpu.bitcast(z, F32)


def _attn_bias_init(bias_scr):
    nk = 3 * BLOCK
    row = lax.broadcasted_iota(jnp.int32, (BLOCK, nk), 0)
    col = lax.broadcasted_iota(jnp.int32, (BLOCK, nk), 1)
    dist = jnp.abs(col - BLOCK - row)
    distf = dist.astype(F32)
    for var in range(3):
        ok = dist <= WINDOW
        if var == 1:
            ok = ok & (col >= BLOCK)
        elif var == 2:
            ok = ok & (col < 2 * BLOCK)
        for h in range(N_HEADS_ATT):
            slope = 2.0 ** (-8.0 * (h + 1.0) / N_HEADS_ATT)
            bias_scr[var * N_HEADS_ATT + h] = jnp.where(ok, (-slope * LOG2E) * distf, -jnp.inf)


def _attn_pieces(first, last, sink_ref, q_ref, kvp_ref, kvn_ref, g_ref, o_ref, bias_scr, anchors,
                 *, nsub):
    nk = 3 * BLOCK

    lane_k = lax.broadcasted_iota(jnp.int32, (nk, LANES), 1)
    ones_a = jnp.where(lane_k < HEAD_DIM_ATT, 1.0, 0.0).astype(BF16)
    ones_b = jnp.where(lane_k < HEAD_DIM_ATT, 0.0, 1.0).astype(BF16)
    lo_half_q = lax.broadcasted_iota(jnp.int32, (BLOCK, LANES), 1) < HEAD_DIM_ATT

    def kv_block(idx):
        if idx < 0:
            return kvp_ref[...]
        if idx >= nsub:
            return kvn_ref[...]
        return q_ref[idx * BLOCK:(idx + 1) * BLOCK, ATT_Q:QKVA_W]

    for n in range(nsub):
        rows = slice(n * BLOCK, (n + 1) * BLOCK)
        kv = jnp.concatenate([kv_block(n - 1), kv_block(n), kv_block(n + 1)], axis=0)
        if n == 0:
            var = jnp.where(first, 1, 0)
        elif n == nsub - 1:
            var = jnp.where(last, 2, 0)
        else:
            var = 0
        pieces = []
        for kvh in range(N_KV_HEADS):
            k_st, k_sw = kv[:, 0:LANES], kv[:, LANES:2 * LANES]
            v_st, v_sw = kv[:, 2 * LANES:3 * LANES], kv[:, 3 * LANES:4 * LANES]
            if kvh == 0:
                k_lo, k_hi, v_lo, v_hi = k_st, k_sw, v_st, v_sw
            else:
                k_lo, k_hi, v_lo, v_hi = k_sw, k_st, v_sw, v_st
            kk = jnp.concatenate([k_lo * ones_a, k_hi * ones_b], axis=0)
            vv = jnp.concatenate([
                jnp.concatenate([v_lo * ones_a, ones_a], axis=1),
                jnp.concatenate([v_hi * ones_b, ones_b], axis=1)], axis=0)
            for pair in range(GROUP_SIZE // 2):
                h0 = kvh * GROUP_SIZE + 2 * pair
                qp = q_ref[rows, h0 * HEAD_DIM_ATT:(h0 + 2) * HEAD_DIM_ATT]
                s2 = _dot_nt(qp, kk)
                tie = _zero_after(anchors[-1])[:, 0:1] if anchors else 0.0
                ps, es = [], []
                for t in range(2):
                    logits = s2[:, t * nk:(t + 1) * nk] + bias_scr[var * N_HEADS_ATT + h0 + t]
                    sink = sink_ref[h0 + t] * LOG2E + tie
                    mx = jnp.maximum(jnp.max(logits, axis=-1, keepdims=True), sink)
                    ps.append(jnp.exp2(logits - mx).astype(BF16))
                    es.append(jnp.exp2(sink - mx))
                res = _dot(jnp.concatenate(ps, axis=1), vv)
                den = res[:, LANES:] + jnp.where(lo_half_q, es[0], es[1])
                pieces.append(res[:, :LANES] / den)
                yield
        att = jnp.concatenate(pieces, axis=1)
        o_ref[rows, :] = _rms(att, g_ref[...]).astype(o_ref.dtype)
        yield


def _outffn_pieces(x_ref, att_ref, hf_ref, hb_ref, om_ref, mod_ref, gm_ref, wo_ref, g2_ref,
                   w1_ref, w2_ref, gf_ref, o_ref, hid_scr, anchors, *, final):
    hs = hf_ref[...] + hb_ref[...]
    parts = []
    for h in range(N_HEADS_M):
        cols = slice(h * HEAD_DIM_M, (h + 1) * HEAD_DIM_M)
        parts.append(_rms(hs[:, cols], gm_ref[:, cols]))
    hm = jax.nn.sigmoid(om_ref[...]) * jnp.concatenate(parts, axis=1)
    mixin = jnp.concatenate([att_ref[...], hm.astype(BF16)], axis=1)
    mix = _dot(mixin, wo_ref[...])
    anchors.append(mix[0:1, 0:LANES])
    x1 = x_ref[...] + mod_ref[2:3, :] * mix
    hff = (_rms(x1, g2_ref[...]) * (1.0 + mod_ref[4:5, :]) + mod_ref[3:4, :]).astype(BF16)
    yield
    for c in range(N_FF_CHUNKS):
        gate = _dot(hff, w1_ref[:, FF_CHUNK * c:FF_CHUNK * (c + 1)])
        up = _dot(hff, w1_ref[:, D_FF + FF_CHUNK * c:D_FF + FF_CHUNK * (c + 1)])
        anchors.append(gate[0:1, 0:LANES])
        hid_scr[:, FF_CHUNK * c:FF_CHUNK * (c + 1)] = (jax.nn.silu(gate) * up).astype(BF16)
        yield
    ff = _dot(hid_scr[...], w2_ref[...])
    anchors.append(ff[0:1, 0:LANES])
    x2 = x1 + mod_ref[5:6, :] * ff
    if final:
        x2 = _rms(x2, gf_ref[...])
    o_ref[...] = x2
    yield


def _outffn_kernel(*refs, final):
    for _ in _outffn_pieces(*refs, [], final=final):
        pass


def _outffn_call(x, att, hf, hb, om, mod, g_m, w_out, g2, w1, w2, g_final, tm, final):
    bsz, s, d = x.shape
    tile = lambda w: pl.BlockSpec((None, tm, w), lambda b, i: (b, i, 0))
    const = lambda shp: pl.BlockSpec(shp, lambda b, i: (0, 0))
    weight = lambda shp: pl.BlockSpec(shp, lambda b, i: (0, 0), pipeline_mode=pl.Buffered(1))
    return pl.pallas_call(
        functools.partial(_outffn_kernel, final=final),
        grid=(bsz, s // tm),
        in_specs=[
            tile(d), tile(ATT_Q), tile(M_W), tile(M_W), tile(M_W),
            pl.BlockSpec((None, N_MOD, d), lambda b, i: (b, 0, 0)),
            const((1, M_W)), weight((ATT_Q + M_W, d)), const((1, d)),
            weight((d, 2 * D_FF)), weight((D_FF, d)), const((1, d)),
        ],
        out_specs=tile(d),
        out_shape=jax.ShapeDtypeStruct((bsz, s, d), F32),
        scratch_shapes=[pltpu.VMEM((tm, D_FF), BF16)],
        compiler_params=pltpu.CompilerParams(
            dimension_semantics=("arbitrary", "arbitrary"), vmem_limit_bytes=VMEM_LIMIT),
        name="outffn",
    )(x, att, hf, hb, om, mod, g_m, w_out, g2, w1, w2, g_final)


def _layer(l, x, c, w_mod, b_mod, g_norm1, w_in, conv_w, conv_b, b_gates, sink,
           g_attn_out, g_mlstm_out, w_out, g_norm2, w_ffn_in, w_ffn_out, g_final, final):
    d = x.shape[-1]
    mod = _mod_call(c, w_mod[l], b_mod[l])

    w_in_p = _cast_t_call(w_in, l, IN_COLS_PAD)
    qkva, qc, kct, vm, om, gates_t = _inproj_call(
        x, mod, g_norm1[l].reshape(1, d), w_in_p, conv_w[l], conv_b[l].reshape(1, 2 * M_W), tm=512)

    bg_rows = jnp.broadcast_to(b_gates[l][:, None], (N_GATES, LANES))
    hf, hb, att, (w_out_b, w_ffn_in_b, w_ffn_out_b) = _mixer_call(
        qc, kct, vm, gates_t, bg_rows, sink[l], qkva, g_attn_out[l].reshape(1, ATT_Q),
        (w_out, w_ffn_in, w_ffn_out), l, tm=1024)

    return _outffn_call(x, att, hf, hb, om, mod, g_mlstm_out[l].reshape(1, M_W), w_out_b,
                        g_norm2[l].reshape(1, d), w_ffn_in_b, w_ffn_out_b,
                        g_final.reshape(1, d), tm=512, final=final)


def kernel(x, c, w_mod, b_mod, g_norm1, w_in, conv_w, conv_b, b_gates, sink, g_attn_out,
           g_mlstm_out, w_out, g_norm2, w_ffn_in, w_ffn_out, g_final):
    depth = w_mod.shape[0]
    for l in range(depth):
        x = _layer(l, x, c, w_mod, b_mod, g_norm1, w_in, conv_w, conv_b, b_gates, sink, g_attn_out,
                   g_mlstm_out, w_out, g_norm2, w_ffn_in, w_ffn_out, g_final, final=(l == depth - 1))
    return x
```

```python
import functools

import jax
import jax.numpy as jnp
from jax import lax
from jax.experimental import pallas as pl
from jax.experimental.pallas import tpu as pltpu

F32 = jnp.float32
BF16 = jnp.bfloat16

D_MODEL = 1024
EPS = 1e-6
N_HEADS_ATT = 8
N_KV_HEADS = 2
HEAD_DIM_ATT = 64
GROUP_SIZE = N_HEADS_ATT // N_KV_HEADS
WINDOW = 128
BLOCK = 128
N_HEADS_M = 4
HEAD_DIM_M = 128
CHUNK = 128
ATT_Q = N_HEADS_ATT * HEAD_DIM_ATT
ATT_KV = N_KV_HEADS * HEAD_DIM_ATT
M_W = N_HEADS_M * HEAD_DIM_M
N_GATES = 4 * N_HEADS_M
D_FF = 2816
N_MOD = 6

LANES = 128
GATE_PAD = LANES
FF_CHUNK = 256
N_FF_CHUNKS = D_FF // FF_CHUNK
OUT_COL_BLOCK = 256
VMEM_LIMIT = 56 * 1024 * 1024

C_QA = 0
C_KA = ATT_Q
C_VA = ATT_Q + ATT_KV
C_QKM = ATT_Q + 2 * ATT_KV
C_VM = C_QKM + 2 * M_W
C_OM = C_VM + M_W
C_G = C_OM + M_W
IN_COLS_PAD = C_G + GATE_PAD
QKVA_W = ATT_Q + 4 * ATT_KV


def _dot(a, b):
    return jnp.dot(a, b, preferred_element_type=F32)


def _dot_nt(a, b):
    return lax.dot_general(a, b, (((1,), (1,)), ((), ())), preferred_element_type=F32)


def _rms(x, g):
    return x * lax.rsqrt(jnp.mean(x * x, axis=-1, keepdims=True) + EPS) * g


def _alternate(*streams):
    live = list(streams)
    while live:
        for s in list(live):
            if next(s, StopIteration) is StopIteration:
                live.remove(s)


def _cast_t_kernel(wt_ref, o_ref):
    n = wt_ref.shape[0]
    full = n // LANES * LANES
    for r in range(0, full, LANES):
        o_ref[:, r:r + LANES] = wt_ref[r:r + LANES, :].T.astype(o_ref.dtype)
    if o_ref.shape[1] > full:
        tail = jnp.concatenate(
            [wt_ref[full:n, :], jnp.zeros((full + LANES - n, wt_ref.shape[1]), wt_ref.dtype)], axis=0)
        o_ref[:, full:full + LANES] = tail.T.astype(o_ref.dtype)


def _cast_t_call(w, layer, n_out, bk=256):
    _, k, n = w.shape
    assert n_out - n < LANES and n_out % LANES == 0
    return pl.pallas_call(
        _cast_t_kernel,
        grid=(k // bk,),
        in_specs=[pl.BlockSpec((None, n, bk), lambda i: (layer, 0, i))],
        out_specs=pl.BlockSpec((bk, n_out), lambda i: (i, 0)),
        out_shape=jax.ShapeDtypeStruct((k, n_out), BF16),
        compiler_params=pltpu.CompilerParams(
            dimension_semantics=("arbitrary",), vmem_limit_bytes=VMEM_LIMIT),
        name="cast",
    )(jnp.swapaxes(w, 1, 2))


def _mod_kernel(c_ref, w_ref, b_ref, o_ref):
    s = jax.nn.silu(c_ref[...]).astype(BF16)
    o_ref[...] = _dot(s, w_ref[...].astype(BF16)) + b_ref[...]


def _mod_call(c, w_mod, b_mod):
    bsz = c.shape[0]
    rows = 8
    cp = jnp.pad(c, ((0, rows - bsz), (0, 0)))
    n = w_mod.shape[1]
    bn = 1024
    out = pl.pallas_call(
        _mod_kernel,
        grid=(n // bn,),
        in_specs=[
            pl.BlockSpec((rows, D_MODEL), lambda i: (0, 0)),
            pl.BlockSpec((D_MODEL, bn), lambda i: (0, i)),
            pl.BlockSpec((1, bn), lambda i: (0, i)),
        ],
        out_specs=pl.BlockSpec((rows, bn), lambda i: (0, i)),
        out_shape=jax.ShapeDtypeStruct((rows, n), F32),
        compiler_params=pltpu.CompilerParams(dimension_semantics=("arbitrary",)),
        name="mod",
    )(cp, w_mod, b_mod.reshape(1, n))
    return out[:bsz].reshape(bsz, N_MOD, D_MODEL)


MXU_COLS = 256
CONV_ROWS = 64


def _inproj_kernel(x_ref, mod_ref, g_ref, w_ref, cw_ref, cb_ref,
                   qkva_ref, qc_ref, kct_ref, vm_ref, om_ref, gt_ref,
                   raw_scr, new_scr, k_scr, carry_scr, *, nblk):
    j = pl.program_id(0)

    @pl.when(j == 0)
    def _init():
        raw_scr[...] = jnp.zeros_like(raw_scr)
        carry_scr[...] = jnp.zeros_like(carry_scr)

    x = x_ref[...]
    h = _rms(x, g_ref[...] * (1.0 + mod_ref[1:2, :])) + mod_ref[0:1, :]
    hb = h.astype(BF16)
    tm = x.shape[0]
    nchunk = tm // CHUNK
    within = (j + nblk - 1) % nblk

    anchors = []

    def zero_after(a):
        return jnp.concatenate([_zero_after(a)] * (2 * M_W // LANES), axis=1)

    def project():
        order = list(range(C_QKM, C_VM, MXU_COLS)) + list(range(0, C_QKM, MXU_COLS)) + \
            list(range(C_VM, IN_COLS_PAD, MXU_COLS))
        for c0 in order:
            c1 = min(c0 + MXU_COLS, IN_COLS_PAD)
            res = _dot(hb, w_ref[:, c0:c1])
            anchors.append(res[0:1, 0:LANES])
            if c0 < C_KA:
                qkva_ref[:, c0:c1] = (res * (HEAD_DIM_ATT ** -0.5 * LOG2E)).astype(BF16)
            elif c0 < C_QKM:
                ka, va = res[:, :ATT_KV], res[:, ATT_KV:]
                half = HEAD_DIM_ATT
                qkva_ref[:, ATT_Q:ATT_Q + ATT_KV] = ka.astype(BF16)
                qkva_ref[:, ATT_Q + ATT_KV:ATT_Q + 2 * ATT_KV] = pltpu.roll(ka, half, axis=1).astype(BF16)
                qkva_ref[:, ATT_Q + 2 * ATT_KV:ATT_Q + 3 * ATT_KV] = va.astype(BF16)
                qkva_ref[:, ATT_Q + 3 * ATT_KV:ATT_Q + 4 * ATT_KV] = pltpu.roll(va, half, axis=1).astype(BF16)
            elif c0 < C_VM:
                new_scr[:, c0 - C_QKM:c1 - C_QKM] = res
            elif c0 < C_OM:
                vm_ref[:, c0 - C_VM:c1 - C_VM] = res.astype(BF16)
            elif c0 < C_G:
                om_ref[:, c0 - C_OM:c1 - C_OM] = res
            else:
                for g in range(nchunk):
                    blk = res[g * CHUNK:(g + 1) * CHUNK, :].T
                    gt_ref[:, g * CHUNK:(g + 1) * CHUNK] = blk[0:N_GATES, :]
            yield

    def conv():
        R = CONV_ROWS
        row = lax.broadcasted_iota(jnp.int32, (R, 1), 0)
        for p in range(tm // R):
            r0 = p * R
            xg = raw_scr[r0:r0 + R, :]
            if p == 0:
                prev_row = jnp.where(within == 0, 0.0, carry_scr[0:1, :])
            else:
                prev_row = raw_scr[r0 - 1:r0, :]
            if r0 + R == tm:
                next_row = jnp.where(within == nblk - 1, 0.0, new_scr[0:1, :])
            else:
                next_row = raw_scr[r0 + R:r0 + R + 1, :]
            xm1 = jnp.where(row == 0, prev_row, pltpu.roll(xg, 1, axis=0))
            xp1 = jnp.where(row == R - 1, next_row, pltpu.roll(xg, R - 1, axis=0))
            z = zero_after(anchors[-1])
            y =xm1 * (cw_ref[0:1, :] + z) + xg * (cw_ref[1:2, :] + z) + xp1 * (cw_ref[2:3, :] + z) \
                + cb_ref[...]
            y = jax.nn.silu(y)
            qc_ref[r0:r0 + R, :] = (y[:, :M_W] * (HEAD_DIM_M ** -0.5)).astype(BF16)
            k_scr[r0:r0 + R, :] = y[:, M_W:]
            yield
            if (r0 + R) % CHUNK == 0:
                g = r0 // CHUNK
                for hd in range(N_HEADS_M):
                    blk = k_scr[g * CHUNK:(g + 1) * CHUNK, hd * HEAD_DIM_M:(hd + 1) * HEAD_DIM_M]
                    base = (g * N_HEADS_M + hd) * HEAD_DIM_M
                    kct_ref[base:base + HEAD_DIM_M, :] = blk.T.astype(BF16)
                yield

    _alternate(project(), conv())
    carry_scr[0:1, :] = raw_scr[tm - 1:tm, :]
    raw_scr[...] = new_scr[...]


def _inproj_call(x, mod, g1, w_in_p, conv_w, conv_b, tm):
    bsz, s, d = x.shape
    nblk = s // tm
    ntiles = bsz * nblk
    outs = (
        jax.ShapeDtypeStruct((bsz, s, QKVA_W), BF16),
        jax.ShapeDtypeStruct((bsz, s, M_W), BF16),
        jax.ShapeDtypeStruct((bsz, s * N_HEADS_M, HEAD_DIM_M), BF16),
        jax.ShapeDtypeStruct((bsz, s, M_W), BF16),
        jax.ShapeDtypeStruct((bsz, s, M_W), F32),
        jax.ShapeDtypeStruct((bsz, N_GATES, s), F32),
    )

    def cur(j):
        t = jnp.minimum(j, ntiles - 1)
        return t // nblk, t % nblk

    def old(j):
        t = jnp.maximum(j - 1, 0)
        return t // nblk, t % nblk

    cur_tile = lambda w: pl.BlockSpec((None, tm, w), lambda j: (*cur(j), 0))
    old_tile = lambda w: pl.BlockSpec((None, tm, w), lambda j: (*old(j), 0))
    const = lambda shp: pl.BlockSpec(shp, lambda j: (0, 0))
    return pl.pallas_call(
        functools.partial(_inproj_kernel, nblk=nblk),
        grid=(ntiles + 1,),
        in_specs=[
            cur_tile(d),
            pl.BlockSpec((None, N_MOD, d), lambda j: (cur(j)[0], 0, 0)),
            const((1, d)),
            pl.BlockSpec((d, IN_COLS_PAD), lambda j: (0, 0), pipeline_mode=pl.Buffered(1)),
            const((3, 2 * M_W)), const((1, 2 * M_W)),
        ],
        out_specs=[cur_tile(QKVA_W), old_tile(M_W),
                   pl.BlockSpec((None, tm * N_HEADS_M, HEAD_DIM_M), lambda j: (*old(j), 0)),
                   cur_tile(M_W), cur_tile(M_W),
                   pl.BlockSpec((None, N_GATES, tm), lambda j: (cur(j)[0], 0, cur(j)[1]))],
        out_shape=outs,
        scratch_shapes=[pltpu.VMEM((tm, 2 * M_W), F32), pltpu.VMEM((tm, 2 * M_W), F32),
                        pltpu.VMEM((tm, M_W), F32), pltpu.VMEM((8, 2 * M_W), F32)],
        compiler_params=pltpu.CompilerParams(
            dimension_semantics=("arbitrary",), vmem_limit_bytes=VMEM_LIMIT),
        name="inproj",
    )(x, mod, g1, w_in_p, conv_w, conv_b)


def _mlstm_reset(first, c_scr, m_scr):
    @pl.when(first)
    def _init():
        c_scr[...] = jnp.zeros_like(c_scr)
        m_scr[...] = jnp.zeros_like(m_scr)


LOG2E = 1.4426950408889634


def _mlstm_stream(q_ref, kt_ref, v_ref, gt_ref, bgr_ref, h_ref, c_scr, m_scr, *, reverse, nchunk):
    L = CHUNK
    ti = lax.broadcasted_iota(jnp.int32, (L, L), 0)
    si = lax.broadcasted_iota(jnp.int32, (L, L), 1)
    causal = (si >= ti) if reverse else (si <= ti)
    tri_t = jnp.where((ti >= si) if reverse else (ti <= si), 1.0, 0.0).astype(BF16)
    lane = lax.broadcasted_iota(jnp.int32, (1, L), 1)
    last = 0 if reverse else L - 1
    i_off, f_off = (2 * N_HEADS_M, 3 * N_HEADS_M) if reverse else (0, N_HEADS_M)
    ones = jnp.ones((L, HEAD_DIM_M), BF16)
    order = list(range(nchunk - 1, -1, -1) if reverse else range(nchunk))
    rows_of = lambda g: slice(g * L, (g + 1) * L)

    def split3(a):
        hi = a.astype(BF16)
        r1 = a - hi.astype(F32)
        mid = r1.astype(BF16)
        return hi, mid, (r1 - mid.astype(F32)).astype(BF16)

    gates = {}
    pad = jnp.zeros((L - N_GATES, L), F32)
    for g in order:
        gr = gt_ref[:, rows_of(g)] + bgr_ref[...]
        parts = _dot(jnp.concatenate(split3(jax.nn.log_sigmoid(gr)), axis=0), tri_t)
        bcr = (parts[0:N_GATES] + parts[N_GATES:2 * N_GATES] + parts[2 * N_GATES:3 * N_GATES]) * LOG2E
        gates[g] = (gr * LOG2E, bcr, jnp.concatenate([-bcr, pad], axis=0).T)
        yield

    def head(g, h):
        rows = rows_of(g)
        cols = slice(h * HEAD_DIM_M, (h + 1) * HEAD_DIM_M)
        gct, bct, nbcum = gates[g]
        r_row = gct[i_off + h:i_off + h + 1, :] - bct[f_off + h:f_off + h + 1, :]
        btot = jnp.sum(jnp.where(lane == last, bct[f_off + h:f_off + h + 1, :], 0.0),
                       axis=1, keepdims=True)
        rmax = jnp.max(r_row, axis=1, keepdims=True)
        a_max = btot + rmax
        m_prev = m_scr[h:h + 1, 0:1]
        m_new = jnp.maximum(btot + m_prev, a_max)
        m_scr[h:h + 1, :] = jnp.broadcast_to(m_new, (1, LANES))
        decay = jnp.exp2(btot + m_prev - m_new)
        w_row = jnp.exp2(r_row - rmax) * jnp.exp2(a_max - m_new)
        kt_rows = slice((g * N_HEADS_M + h) * HEAD_DIM_M, (g * N_HEADS_M + h + 1) * HEAD_DIM_M)
        cm = jnp.max(jnp.where(causal, r_row, -jnp.inf), axis=1, keepdims=True)
        yield
        q = q_ref[rows, cols]
        s = _dot(q, kt_ref[kt_rows, :])
        u = jnp.maximum(cm, m_prev)
        ub = jnp.broadcast_to(u, (L, L))
        dm = jnp.exp2(jnp.where(causal, r_row - ub, -jnp.inf))
        qi = q * jnp.exp2(m_prev - ub).astype(BF16)
        floor = jnp.exp2(jnp.broadcast_to(nbcum[:, f_off + h:f_off + h + 1], (L, L)) - ub)
        yield
        vaug = jnp.concatenate([v_ref[rows, cols], ones], axis=1)
        lhs = jnp.concatenate([(s * dm).astype(BF16), qi], axis=1)
        rhs = jnp.concatenate([vaug, c_scr[h].astype(BF16)], axis=0)
        out = _dot(lhs, rhs)
        upd = _dot(kt_ref[kt_rows, :] * w_row.astype(BF16), vaug)
        yield
        c_scr[h] = decay * c_scr[h] + upd
        h_ref[rows, cols] = out[:, :HEAD_DIM_M] / jnp.maximum(jnp.abs(out[:, HEAD_DIM_M:]), floor)
        yield

    n_stage, lag = 4, 2
    heads = {}
    for slot in range(lag * (nchunk - 1) + n_stage):
        for i, g in enumerate(order):
            stage = slot - lag * i
            if 0 <= stage < n_stage:
                for h in range(N_HEADS_M):
                    if stage == 0:
                        heads[g, h] = head(g, h)
                    next(heads[g, h])
                    yield


def _mixer_kernel(qf_ref, kf_ref, vf_ref, gtf_ref, qb_ref, kb_ref, vb_ref, gtb_ref, bgr_ref,
                  sink_ref, qa_ref, kvp_ref, kvn_ref, ga_ref, *rest, nblk, nchunk, ncast):
    w_refs, rest = rest[:ncast], rest[ncast:]
    hf_ref, hb_ref, att_ref = rest[:3]
    wo_refs, (cf_scr, mf_scr, cb_scr, mb_scr, bias_scr) = rest[3:3 + ncast], rest[3 + ncast:]
    b = pl.program_id(0)
    j = pl.program_id(1)

    @pl.when((b == 0) & (j == 0))
    def _init():
        _attn_bias_init(bias_scr)

    _mlstm_reset(j == 0, cf_scr, mf_scr)
    _mlstm_reset(j == 0, cb_scr, mb_scr)
    for w_ref, wo_ref in zip(w_refs, wo_refs):
        wo_ref[...] = w_ref[...].astype(wo_ref.dtype)
    fwd = _mlstm_stream(qf_ref, kf_ref, vf_ref, gtf_ref, bgr_ref, hf_ref, cf_scr, mf_scr,
                        reverse=False, nchunk=nchunk)
    bwd = _mlstm_stream(qb_ref, kb_ref, vb_ref, gtb_ref, bgr_ref, hb_ref, cb_scr, mb_scr,
                        reverse=True, nchunk=nchunk)
    att = _attn_pieces(j == 0, j == nblk - 1, sink_ref, qa_ref, kvp_ref, kvn_ref, ga_ref, att_ref,
                       bias_scr, [], nsub=nchunk)
    _alternate(fwd, bwd)
    for _ in att:
        pass


def _mixer_call(qc, kct, vm, gates_t, bg_rows, sink, qkva, g_attn, weights, layer, tm):
    bsz, s, _ = qc.shape
    nblk = s // tm
    nsteps = bsz * nblk
    nsub = tm // BLOCK
    nb = s // BLOCK
    kvw = QKVA_W - ATT_Q
    attn_specs = [
        pl.BlockSpec(memory_space=pltpu.SMEM),
        pl.BlockSpec((None, tm, QKVA_W), lambda b, j: (b, j, 0)),
        pl.BlockSpec((None, BLOCK, kvw), lambda b, j: (b, jnp.maximum(j * nsub - 1, 0), 1)),
        pl.BlockSpec((None, BLOCK, kvw), lambda b, j: (b, jnp.minimum((j + 1) * nsub, nb - 1), 1)),
        pl.BlockSpec((1, ATT_Q), lambda b, j: (0, 0)),
    ]
    w_specs, w_shapes = [], []
    for w in weights:
        _, k, n = w.shape
        rows = k // nsteps
        assert rows * nsteps == k and rows % 16 == 0
        w_specs.append(pl.BlockSpec((None, rows, n), lambda b, j: (layer, b * nblk + j, 0)))
        w_shapes.append(jax.ShapeDtypeStruct((k, n), BF16))
    wo_specs = [pl.BlockSpec((sp.block_shape[1], sp.block_shape[2]), lambda b, j: (b * nblk + j, 0))
                for sp in w_specs]

    def specs(pos):
        tile = lambda w: pl.BlockSpec((None, tm, w), lambda b, j: (b, pos(j), 0))
        kt_spec = pl.BlockSpec((None, tm * N_HEADS_M, HEAD_DIM_M), lambda b, j: (b, pos(j), 0))
        gt_spec = pl.BlockSpec((None, N_GATES, tm), lambda b, j: (b, 0, pos(j)))
        return [tile(M_W), kt_spec, tile(M_W), gt_spec], tile(M_W)

    in_f, out_f = specs(lambda j: j)
    in_b, out_b = specs(lambda j: nblk - 1 - j)
    state = [pltpu.VMEM((N_HEADS_M, HEAD_DIM_M, 2 * HEAD_DIM_M), F32), pltpu.VMEM((8, LANES), F32)]
    outs = pl.pallas_call(
        functools.partial(_mixer_kernel, nblk=nblk, nchunk=tm // CHUNK, ncast=len(weights)),
        grid=(bsz, nblk),
        in_specs=in_f + in_b + [pl.BlockSpec((N_GATES, LANES), lambda b, j: (0, 0))] + attn_specs
        + w_specs,
        out_specs=[out_f, out_b, pl.BlockSpec((None, tm, ATT_Q), lambda b, j: (b, j, 0))] + wo_specs,
        out_shape=[jax.ShapeDtypeStruct((bsz, s, M_W), F32)] * 2
        + [jax.ShapeDtypeStruct((bsz, s, ATT_Q), BF16)] + w_shapes,
        scratch_shapes=state + state + [pltpu.VMEM((3 * N_HEADS_ATT, BLOCK, 3 * BLOCK), F32)],
        compiler_params=pltpu.CompilerParams(
            dimension_semantics=("arbitrary", "arbitrary"), vmem_limit_bytes=VMEM_LIMIT),
        name="mixer",
    )(qc, kct, vm, gates_t, qc, kct, vm, gates_t, bg_rows, sink, qkva, qkva, qkva, g_attn, *weights)
    return outs[0], outs[1], outs[2], outs[3:]


def _zero_after(a):
    bits = pltpu.bitcast(a, jnp.uint32)
    z = lax.shift_right_logical(lax.shift_right_logical(bits, jnp.uint32(16)), jnp.uint32(16))
    return pltpu.bitcast(z, F32)


def _attn_bias_init(bias_scr):
    nk = 3 * BLOCK
    row = lax.broadcasted_iota(jnp.int32, (BLOCK, nk), 0)
    col = lax.broadcasted_iota(jnp.int32, (BLOCK, nk), 1)
    dist = jnp.abs(col - BLOCK - row)
    distf = dist.astype(F32)
    for var in range(3):
        ok = dist <= WINDOW
        if var == 1:
            ok = ok & (col >= BLOCK)
        elif var == 2:
            ok = ok & (col < 2 * BLOCK)
        for h in range(N_HEADS_ATT):
            slope = 2.0 ** (-8.0 * (h + 1.0) / N_HEADS_ATT)
            bias_scr[var * N_HEADS_ATT + h] = jnp.where(ok, (-slope * LOG2E) * distf, -jnp.inf)


def _attn_pieces(first, last, sink_ref, q_ref, kvp_ref, kvn_ref, g_ref, o_ref, bias_scr, anchors,
                 *, nsub):
    nk = 3 * BLOCK

    lane_k = lax.broadcasted_iota(jnp.int32, (nk, LANES), 1)
    ones_a = jnp.where(lane_k < HEAD_DIM_ATT, 1.0, 0.0).astype(BF16)
    ones_b = jnp.where(lane_k < HEAD_DIM_ATT, 0.0, 1.0).astype(BF16)
    lo_half_q = lax.broadcasted_iota(jnp.int32, (BLOCK, LANES), 1) < HEAD_DIM_ATT

    def kv_block(idx):
        if idx < 0:
            return kvp_ref[...]
        if idx >= nsub:
            return kvn_ref[...]
        return q_ref[idx * BLOCK:(idx + 1) * BLOCK, ATT_Q:QKVA_W]

    for n in range(nsub):
        rows = slice(n * BLOCK, (n + 1) * BLOCK)
        kv = jnp.concatenate([kv_block(n - 1), kv_block(n), kv_block(n + 1)], axis=0)
        if n == 0:
            var = jnp.where(first, 1, 0)
        elif n == nsub - 1:
            var = jnp.where(last, 2, 0)
        else:
            var = 0
        pieces = []
        for kvh in range(N_KV_HEADS):
            k_st, k_sw = kv[:, 0:LANES], kv[:, LANES:2 * LANES]
            v_st, v_sw = kv[:, 2 * LANES:3 * LANES], kv[:, 3 * LANES:4 * LANES]
            if kvh == 0:
                k_lo, k_hi, v_lo, v_hi = k_st, k_sw, v_st, v_sw
            else:
                k_lo, k_hi, v_lo, v_hi = k_sw, k_st, v_sw, v_st
            kk = jnp.concatenate([k_lo * ones_a, k_hi * ones_b], axis=0)
            vv = jnp.concatenate([
                jnp.concatenate([v_lo * ones_a, ones_a], axis=1),
                jnp.concatenate([v_hi * ones_b, ones_b], axis=1)], axis=0)
            for pair in range(GROUP_SIZE // 2):
                h0 = kvh * GROUP_SIZE + 2 * pair
                qp = q_ref[rows, h0 * HEAD_DIM_ATT:(h0 + 2) * HEAD_DIM_ATT]
                s2 = _dot_nt(qp, kk)
                tie = _zero_after(anchors[-1])[:, 0:1] if anchors else 0.0
                ps, es = [], []
                for t in range(2):
                    logits = s2[:, t * nk:(t + 1) * nk] + bias_scr[var * N_HEADS_ATT + h0 + t]
                    sink = sink_ref[h0 + t] * LOG2E + tie
                    mx = jnp.maximum(jnp.max(logits, axis=-1, keepdims=True), sink)
                    ps.append(jnp.exp2(logits - mx).astype(BF16))
                    es.append(jnp.exp2(sink - mx))
                res = _dot(jnp.concatenate(ps, axis=1), vv)
                den = res[:, LANES:] + jnp.where(lo_half_q, es[0], es[1])
                pieces.append(res[:, :LANES] / den)
                yield
        att = jnp.concatenate(pieces, axis=1)
        o_ref[rows, :] = _rms(att, g_ref[...]).astype(o_ref.dtype)
        yield


def _outffn_pieces(x_ref, att_ref, hf_ref, hb_ref, om_ref, mod_ref, gm_ref, wo_ref, g2_ref,
                   w1_ref, w2_ref, gf_ref, o_ref, hid_scr, anchors, *, final):
    hs = hf_ref[...] + hb_ref[...]
    parts = []
    for h in range(N_HEADS_M):
        cols = slice(h * HEAD_DIM_M, (h + 1) * HEAD_DIM_M)
        parts.append(_rms(hs[:, cols], gm_ref[:, cols]))
    hm = jax.nn.sigmoid(om_ref[...]) * jnp.concatenate(parts, axis=1)
    mixin = jnp.concatenate([att_ref[...], hm.astype(BF16)], axis=1)
    mix = _dot(mixin, wo_ref[...])
    anchors.append(mix[0:1, 0:LANES])
    x1 = x_ref[...] + mod_ref[2:3, :] * mix
    hff = (_rms(x1, g2_ref[...] * (1.0 + mod_ref[4:5, :])) + mod_ref[3:4, :]).astype(BF16)
    yield
    for c in range(N_FF_CHUNKS):
        gate = _dot(hff, w1_ref[:, FF_CHUNK * c:FF_CHUNK * (c + 1)])
        up = _dot(hff, w1_ref[:, D_FF + FF_CHUNK * c:D_FF + FF_CHUNK * (c + 1)])
        anchors.append(gate[0:1, 0:LANES])
        hid_scr[:, FF_CHUNK * c:FF_CHUNK * (c + 1)] = (jax.nn.silu(gate) * up).astype(BF16)
        yield
    ff = _dot(hid_scr[...], w2_ref[...])
    anchors.append(ff[0:1, 0:LANES])
    x2 = x1 + mod_ref[5:6, :] * ff
    if final:
        x2 = _rms(x2, gf_ref[...])
    o_ref[...] = x2
    yield


def _outffn_kernel(*refs, final):
    for _ in _outffn_pieces(*refs, [], final=final):
        pass


def _outffn_call(x, att, hf, hb, om, mod, g_m, w_out, g2, w1, w2, g_final, tm, final):
    bsz, s, d = x.shape
    tile = lambda w: pl.BlockSpec((None, tm, w), lambda b, i: (b, i, 0))
    const = lambda shp: pl.BlockSpec(shp, lambda b, i: (0, 0))
    weight = lambda shp: pl.BlockSpec(shp, lambda b, i: (0, 0), pipeline_mode=pl.Buffered(1))
    return pl.pallas_call(
        functools.partial(_outffn_kernel, final=final),
        grid=(bsz, s // tm),
        in_specs=[
            tile(d), tile(ATT_Q), tile(M_W), tile(M_W), tile(M_W),
            pl.BlockSpec((None, N_MOD, d), lambda b, i: (b, 0, 0)),
            const((1, M_W)), weight((ATT_Q + M_W, d)), const((1, d)),
            weight((d, 2 * D_FF)), weight((D_FF, d)), const((1, d)),
        ],
        out_specs=tile(d),
        out_shape=jax.ShapeDtypeStruct((bsz, s, d), F32),
        scratch_shapes=[pltpu.VMEM((tm, D_FF), BF16)],
        compiler_params=pltpu.CompilerParams(
            dimension_semantics=("arbitrary", "arbitrary"), vmem_limit_bytes=VMEM_LIMIT),
        name="outffn",
    )(x, att, hf, hb, om, mod, g_m, w_out, g2, w1, w2, g_final)


def _layer(l, x, c, w_mod, b_mod, g_norm1, w_in, conv_w, conv_b, b_gates, sink,
           g_attn_out, g_mlstm_out, w_out, g_norm2, w_ffn_in, w_ffn_out, g_final, final):
    d = x.shape[-1]
    mod = _mod_call(c, w_mod[l], b_mod[l])

    w_in_p = _cast_t_call(w_in, l, IN_COLS_PAD)
    qkva, qc, kct, vm, om, gates_t = _inproj_call(
        x, mod, g_norm1[l].reshape(1, d), w_in_p, conv_w[l], conv_b[l].reshape(1, 2 * M_W), tm=512)

    bg_rows = jnp.broadcast_to(b_gates[l][:, None], (N_GATES, LANES))
    hf, hb, att, (w_out_b, w_ffn_in_b, w_ffn_out_b) = _mixer_call(
        qc, kct, vm, gates_t, bg_rows, sink[l], qkva, g_attn_out[l].reshape(1, ATT_Q),
        (w_out, w_ffn_in, w_ffn_out), l, tm=1024)

    return _outffn_call(x, att, hf, hb, om, mod, g_mlstm_out[l].reshape(1, M_W), w_out_b,
                        g_norm2[l].reshape(1, d), w_ffn_in_b, w_ffn_out_b,
                        g_final.reshape(1, d), tm=512, final=final)


def kernel(x, c, w_mod, b_mod, g_norm1, w_in, conv_w, conv_b, b_gates, sink, g_attn_out,
           g_mlstm_out, w_out, g_norm2, w_ffn_in, w_ffn_out, g_final):
    depth = w_mod.shape[0]
    for l in range(depth):
        x = _layer(l, x, c, w_mod, b_mod, g_norm1, w_in, conv_w, conv_b, b_gates, sink, g_attn_out,
                   g_mlstm_out, w_out, g_norm2, w_ffn_in, w_ffn_out, g_final, final=(l == depth - 1))
    return x
```

```python
import functools

import jax
import jax.numpy as jnp
from jax import lax
from jax.experimental import pallas as pl
from jax.experimental.pallas import tpu as pltpu

F32 = jnp.float32
BF16 = jnp.bfloat16

D_MODEL = 1024
EPS = 1e-6
N_HEADS_ATT = 8
N_KV_HEADS = 2
HEAD_DIM_ATT = 64
GROUP_SIZE = N_HEADS_ATT // N_KV_HEADS
WINDOW = 128
BLOCK = 128
N_HEADS_M = 4
HEAD_DIM_M = 128
CHUNK = 128
ATT_Q = N_HEADS_ATT * HEAD_DIM_ATT
ATT_KV = N_KV_HEADS * HEAD_DIM_ATT
M_W = N_HEADS_M * HEAD_DIM_M
N_GATES = 4 * N_HEADS_M
D_FF = 2816
N_MOD = 6

LANES = 128
GATE_PAD = LANES
FF_CHUNK = 256
N_FF_CHUNKS = D_FF // FF_CHUNK
OUT_COL_BLOCK = 256
VMEM_LIMIT = 56 * 1024 * 1024

C_QA = 0
C_KA = ATT_Q
C_VA = ATT_Q + ATT_KV
C_QKM = ATT_Q + 2 * ATT_KV
C_VM = C_QKM + 2 * M_W
C_OM = C_VM + M_W
C_G = C_OM + M_W
IN_COLS_PAD = C_G + GATE_PAD
QKVA_W = ATT_Q + 4 * ATT_KV


def _dot(a, b):
    return jnp.dot(a, b, preferred_element_type=F32)


def _dot_nt(a, b):
    return lax.dot_general(a, b, (((1,), (1,)), ((), ())), preferred_element_type=F32)


def _rms(x, g):
    return x * lax.rsqrt(jnp.mean(x * x, axis=-1, keepdims=True) + EPS) * g


def _alternate(*streams):
    live = list(streams)
    while live:
        for s in list(live):
            if next(s, StopIteration) is StopIteration:
                live.remove(s)


def _cast_t_kernel(wt_ref, o_ref):
    n = wt_ref.shape[0]
    full = n // LANES * LANES
    for r in range(0, full, LANES):
        o_ref[:, r:r + LANES] = wt_ref[r:r + LANES, :].T.astype(o_ref.dtype)
    if o_ref.shape[1] > full:
        tail = jnp.concatenate(
            [wt_ref[full:n, :], jnp.zeros((full + LANES - n, wt_ref.shape[1]), wt_ref.dtype)], axis=0)
        o_ref[:, full:full + LANES] = tail.T.astype(o_ref.dtype)


def _cast_t_call(w, layer, n_out, bk=256):
    _, k, n = w.shape
    assert n_out - n < LANES and n_out % LANES == 0
    return pl.pallas_call(
        _cast_t_kernel,
        grid=(k // bk,),
        in_specs=[pl.BlockSpec((None, n, bk), lambda i: (layer, 0, i))],
        out_specs=pl.BlockSpec((bk, n_out), lambda i: (i, 0)),
        out_shape=jax.ShapeDtypeStruct((k, n_out), BF16),
        compiler_params=pltpu.CompilerParams(
            dimension_semantics=("arbitrary",), vmem_limit_bytes=VMEM_LIMIT),
        name="cast",
    )(jnp.swapaxes(w, 1, 2))


def _mod_kernel(c_ref, w_ref, b_ref, o_ref):
    s = jax.nn.silu(c_ref[...]).astype(BF16)
    o_ref[...] = _dot(s, w_ref[...].astype(BF16)) + b_ref[...]


def _mod_call(c, w_mod, b_mod):
    bsz = c.shape[0]
    rows = 8
    cp = jnp.pad(c, ((0, rows - bsz), (0, 0)))
    n = w_mod.shape[1]
    bn = 1024
    out = pl.pallas_call(
        _mod_kernel,
        grid=(n // bn,),
        in_specs=[
            pl.BlockSpec((rows, D_MODEL), lambda i: (0, 0)),
            pl.BlockSpec((D_MODEL, bn), lambda i: (0, i)),
            pl.BlockSpec((1, bn), lambda i: (0, i)),
        ],
        out_specs=pl.BlockSpec((rows, bn), lambda i: (0, i)),
        out_shape=jax.ShapeDtypeStruct((rows, n), F32),
        compiler_params=pltpu.CompilerParams(dimension_semantics=("arbitrary",)),
        name="mod",
    )(cp, w_mod, b_mod.reshape(1, n))
    return out[:bsz].reshape(bsz, N_MOD, D_MODEL)


MXU_COLS = 256
CONV_ROWS = 64


def _inproj_kernel(x_ref, mod_ref, g_ref, w_ref, cw_ref, cb_ref,
                   qkva_ref, qc_ref, kct_ref, vm_ref, om_ref, gt_ref,
                   raw_scr, new_scr, k_scr, carry_scr, *, nblk):
    j = pl.program_id(0)

    @pl.when(j == 0)
    def _init():
        raw_scr[...] = jnp.zeros_like(raw_scr)
        carry_scr[...] = jnp.zeros_like(carry_scr)

    x = x_ref[...]
    h = _rms(x, g_ref[...] * (1.0 + mod_ref[1:2, :])) + mod_ref[0:1, :]
    hb = h.astype(BF16)
    tm = x.shape[0]
    nchunk = tm // CHUNK
    within = (j + nblk - 1) % nblk

    anchors = []

    def zero_after(a):
        return jnp.concatenate([_zero_after(a)] * (2 * M_W // LANES), axis=1)

    def project():
        order = list(range(C_QKM, C_VM, MXU_COLS)) + list(range(0, C_QKM, MXU_COLS)) + \
            list(range(C_VM, IN_COLS_PAD, MXU_COLS))
        for c0 in order:
            c1 = min(c0 + MXU_COLS, IN_COLS_PAD)
            res = _dot(hb, w_ref[:, c0:c1])
            anchors.append(res[tm - 1:tm, c1 - c0 - LANES:c1 - c0])
            if c0 < C_KA:
                qkva_ref[:, c0:c1] = (res * (HEAD_DIM_ATT ** -0.5 * LOG2E)).astype(BF16)
            elif c0 < C_QKM:
                ka, va = res[:, :ATT_KV], res[:, ATT_KV:]
                half = HEAD_DIM_ATT
                qkva_ref[:, ATT_Q:ATT_Q + ATT_KV] = ka.astype(BF16)
                qkva_ref[:, ATT_Q + ATT_KV:ATT_Q + 2 * ATT_KV] = pltpu.roll(ka, half, axis=1).astype(BF16)
                qkva_ref[:, ATT_Q + 2 * ATT_KV:ATT_Q + 3 * ATT_KV] = va.astype(BF16)
                qkva_ref[:, ATT_Q + 3 * ATT_KV:ATT_Q + 4 * ATT_KV] = pltpu.roll(va, half, axis=1).astype(BF16)
            elif c0 < C_VM:
                new_scr[:, c0 - C_QKM:c1 - C_QKM] = res
            elif c0 < C_OM:
                vm_ref[:, c0 - C_VM:c1 - C_VM] = res.astype(BF16)
            elif c0 < C_G:
                om_ref[:, c0 - C_OM:c1 - C_OM] = res
            else:
                for g in range(nchunk):
                    blk = res[g * CHUNK:(g + 1) * CHUNK, :].T
                    gt_ref[:, g * CHUNK:(g + 1) * CHUNK] = blk[0:N_GATES, :]
            yield

    def conv():
        R = CONV_ROWS
        row = lax.broadcasted_iota(jnp.int32, (R, 1), 0)
        for p in range(tm // R):
            r0 = p * R
            xg = raw_scr[r0:r0 + R, :]
            if p == 0:
                prev_row = jnp.where(within == 0, 0.0, carry_scr[0:1, :])
            else:
                prev_row = raw_scr[r0 - 1:r0, :]
            if r0 + R == tm:
                next_row = jnp.where(within == nblk - 1, 0.0, new_scr[0:1, :])
            else:
                next_row = raw_scr[r0 + R:r0 + R + 1, :]
            xm1 = jnp.where(row == 0, prev_row, pltpu.roll(xg, 1, axis=0))
            xp1 = jnp.where(row == R - 1, next_row, pltpu.roll(xg, R - 1, axis=0))
            z = zero_after(anchors[-1])
            y =xm1 * (cw_ref[0:1, :] + z) + xg * (cw_ref[1:2, :] + z) + xp1 * (cw_ref[2:3, :] + z) \
                + cb_ref[...]
            y = jax.nn.silu(y)
            qc_ref[r0:r0 + R, :] = (y[:, :M_W] * (HEAD_DIM_M ** -0.5)).astype(BF16)
            k_scr[r0:r0 + R, :] = y[:, M_W:]
            yield
            if (r0 + R) % CHUNK == 0:
                g = r0 // CHUNK
                for hd in range(N_HEADS_M):
                    blk = k_scr[g * CHUNK:(g + 1) * CHUNK, hd * HEAD_DIM_M:(hd + 1) * HEAD_DIM_M]
                    base = (g * N_HEADS_M + hd) * HEAD_DIM_M
                    kct_ref[base:base + HEAD_DIM_M, :] = blk.T.astype(BF16)
                yield

    proj, cv = project(), conv()
    n_proj = -(-IN_COLS_PAD // MXU_COLS)
    n_conv = tm // CONV_ROWS + nchunk
    done = 0
    for k in range(n_proj):
        next(proj)
        while done * n_proj < (k + 1) * n_conv:
            next(cv)
            done += 1
    carry_scr[0:1, :] = raw_scr[tm - 1:tm, :]
    raw_scr[...] = new_scr[...]


def _inproj_call(x, mod, g1, w_in_p, conv_w, conv_b, tm):
    bsz, s, d = x.shape
    nblk = s // tm
    ntiles = bsz * nblk
    outs = (
        jax.ShapeDtypeStruct((bsz, s, QKVA_W), BF16),
        jax.ShapeDtypeStruct((bsz, s, M_W), BF16),
        jax.ShapeDtypeStruct((bsz, s * N_HEADS_M, HEAD_DIM_M), BF16),
        jax.ShapeDtypeStruct((bsz, s, M_W), BF16),
        jax.ShapeDtypeStruct((bsz, s, M_W), F32),
        jax.ShapeDtypeStruct((bsz, N_GATES, s), F32),
    )

    def cur(j):
        t = jnp.minimum(j, ntiles - 1)
        return t // nblk, t % nblk

    def old(j):
        t = jnp.maximum(j - 1, 0)
        return t // nblk, t % nblk

    cur_tile = lambda w: pl.BlockSpec((None, tm, w), lambda j: (*cur(j), 0))
    old_tile = lambda w: pl.BlockSpec((None, tm, w), lambda j: (*old(j), 0))
    const = lambda shp: pl.BlockSpec(shp, lambda j: (0, 0))
    return pl.pallas_call(
        functools.partial(_inproj_kernel, nblk=nblk),
        grid=(ntiles + 1,),
        in_specs=[
            cur_tile(d),
            pl.BlockSpec((None, N_MOD, d), lambda j: (cur(j)[0], 0, 0)),
            const((1, d)),
            pl.BlockSpec((d, IN_COLS_PAD), lambda j: (0, 0), pipeline_mode=pl.Buffered(1)),
            const((3, 2 * M_W)), const((1, 2 * M_W)),
        ],
        out_specs=[cur_tile(QKVA_W), old_tile(M_W),
                   pl.BlockSpec((None, tm * N_HEADS_M, HEAD_DIM_M), lambda j: (*old(j), 0)),
                   cur_tile(M_W), cur_tile(M_W),
                   pl.BlockSpec((None, N_GATES, tm), lambda j: (cur(j)[0], 0, cur(j)[1]))],
        out_shape=outs,
        scratch_shapes=[pltpu.VMEM((tm, 2 * M_W), F32), pltpu.VMEM((tm, 2 * M_W), F32),
                        pltpu.VMEM((tm, M_W), F32), pltpu.VMEM((8, 2 * M_W), F32)],
        compiler_params=pltpu.CompilerParams(
            dimension_semantics=("arbitrary",), vmem_limit_bytes=VMEM_LIMIT),
        name="inproj",
    )(x, mod, g1, w_in_p, conv_w, conv_b)


def _mlstm_reset(first, c_scr, m_scr):
    @pl.when(first)
    def _init():
        c_scr[...] = jnp.zeros_like(c_scr)
        m_scr[...] = jnp.zeros_like(m_scr)


LOG2E = 1.4426950408889634


def _mlstm_stream(q_ref, kt_ref, v_ref, gt_ref, bgr_ref, h_ref, c_scr, m_scr, *, reverse, nchunk):
    L = CHUNK
    ti = lax.broadcasted_iota(jnp.int32, (L, L), 0)
    si = lax.broadcasted_iota(jnp.int32, (L, L), 1)
    causal = (si >= ti) if reverse else (si <= ti)
    tri_t = jnp.where((ti >= si) if reverse else (ti <= si), 1.0, 0.0).astype(BF16)
    lane = lax.broadcasted_iota(jnp.int32, (1, L), 1)
    last = 0 if reverse else L - 1
    i_off, f_off = (2 * N_HEADS_M, 3 * N_HEADS_M) if reverse else (0, N_HEADS_M)
    ones = jnp.ones((L, HEAD_DIM_M), BF16)
    order = list(range(nchunk - 1, -1, -1) if reverse else range(nchunk))
    rows_of = lambda g: slice(g * L, (g + 1) * L)

    def split3(a):
        hi = a.astype(BF16)
        r1 = a - hi.astype(F32)
        mid = r1.astype(BF16)
        return hi, mid, (r1 - mid.astype(F32)).astype(BF16)

    gates = {}
    pad = jnp.zeros((L - N_GATES, L), F32)
    for g in order:
        gr = gt_ref[:, rows_of(g)] + bgr_ref[...]
        parts = _dot(jnp.concatenate(split3(jax.nn.log_sigmoid(gr)), axis=0), tri_t)
        bcr = (parts[0:N_GATES] + parts[N_GATES:2 * N_GATES] + parts[2 * N_GATES:3 * N_GATES]) * LOG2E
        gates[g] = (gr * LOG2E, bcr, jnp.concatenate([-bcr, pad], axis=0).T)
        yield

    def head(g, h):
        rows = rows_of(g)
        cols = slice(h * HEAD_DIM_M, (h + 1) * HEAD_DIM_M)
        gct, bct, nbcum = gates[g]
        r_row = gct[i_off + h:i_off + h + 1, :] - bct[f_off + h:f_off + h + 1, :]
        btot = jnp.sum(jnp.where(lane == last, bct[f_off + h:f_off + h + 1, :], 0.0),
                       axis=1, keepdims=True)
        rmax = jnp.max(r_row, axis=1, keepdims=True)
        a_max = btot + rmax
        m_prev = m_scr[h:h + 1, 0:1]
        m_new = jnp.maximum(btot + m_prev, a_max)
        m_scr[h:h + 1, :] = jnp.broadcast_to(m_new, (1, LANES))
        decay = jnp.exp2(btot + m_prev - m_new)
        w_row = jnp.exp2(r_row - rmax) * jnp.exp2(a_max - m_new)
        kt_rows = slice((g * N_HEADS_M + h) * HEAD_DIM_M, (g * N_HEADS_M + h + 1) * HEAD_DIM_M)
        cm = jnp.max(jnp.where(causal, r_row, -jnp.inf), axis=1, keepdims=True)
        yield
        q = q_ref[rows, cols]
        s = _dot(q, kt_ref[kt_rows, :])
        u = jnp.maximum(cm, m_prev)
        ub = jnp.broadcast_to(u, (L, L))
        dm = jnp.exp2(jnp.where(causal, r_row - ub, -jnp.inf))
        qi = q * jnp.exp2(m_prev - ub).astype(BF16)
        floor = jnp.exp2(jnp.broadcast_to(nbcum[:, f_off + h:f_off + h + 1], (L, L)) - ub)
        yield
        vaug = jnp.concatenate([v_ref[rows, cols], ones], axis=1)
        lhs = jnp.concatenate([(s * dm).astype(BF16), qi], axis=1)
        rhs = jnp.concatenate([vaug, c_scr[h].astype(BF16)], axis=0)
        out = _dot(lhs, rhs)
        upd = _dot(kt_ref[kt_rows, :] * w_row.astype(BF16), vaug)
        yield
        c_scr[h] = decay * c_scr[h] + upd
        h_ref[rows, cols] = out[:, :HEAD_DIM_M] / jnp.maximum(jnp.abs(out[:, HEAD_DIM_M:]), floor)
        yield

    n_stage, lag = 4, 2
    heads = {}
    for slot in range(lag * (nchunk - 1) + n_stage):
        for i, g in enumerate(order):
            stage = slot - lag * i
            if 0 <= stage < n_stage:
                for h in range(N_HEADS_M):
                    if stage == 0:
                        heads[g, h] = head(g, h)
                    next(heads[g, h])
                    yield


def _mixer_kernel(qf_ref, kf_ref, vf_ref, gtf_ref, qb_ref, kb_ref, vb_ref, gtb_ref, bgr_ref,
                  sink_ref, qa_ref, kvp_ref, kvn_ref, ga_ref, *rest, nblk, nchunk, ncast):
    w_refs, rest = rest[:ncast], rest[ncast:]
    hf_ref, hb_ref, att_ref = rest[:3]
    wo_refs, (cf_scr, mf_scr, cb_scr, mb_scr, bias_scr) = rest[3:3 + ncast], rest[3 + ncast:]
    b = pl.program_id(0)
    j = pl.program_id(1)

    @pl.when((b == 0) & (j == 0))
    def _init():
        _attn_bias_init(bias_scr)

    _mlstm_reset(j == 0, cf_scr, mf_scr)
    _mlstm_reset(j == 0, cb_scr, mb_scr)
    for w_ref, wo_ref in zip(w_refs, wo_refs):
        wo_ref[...] = w_ref[...].astype(wo_ref.dtype)
    fwd = _mlstm_stream(qf_ref, kf_ref, vf_ref, gtf_ref, bgr_ref, hf_ref, cf_scr, mf_scr,
                        reverse=False, nchunk=nchunk)
    bwd = _mlstm_stream(qb_ref, kb_ref, vb_ref, gtb_ref, bgr_ref, hb_ref, cb_scr, mb_scr,
                        reverse=True, nchunk=nchunk)
    att = _attn_pieces(j == 0, j == nblk - 1, sink_ref, qa_ref, kvp_ref, kvn_ref, ga_ref, att_ref,
                       bias_scr, [], nsub=nchunk)
    _alternate(fwd, bwd)
    for _ in att:
        pass


def _mixer_call(qc, kct, vm, gates_t, bg_rows, sink, qkva, g_attn, weights, layer, tm):
    bsz, s, _ = qc.shape
    nblk = s // tm
    nsteps = bsz * nblk
    nsub = tm // BLOCK
    nb = s // BLOCK
    kvw = QKVA_W - ATT_Q
    attn_specs = [
        pl.BlockSpec(memory_space=pltpu.SMEM),
        pl.BlockSpec((None, tm, QKVA_W), lambda b, j: (b, j, 0)),
        pl.BlockSpec((None, BLOCK, kvw), lambda b, j: (b, jnp.maximum(j * nsub - 1, 0), 1)),
        pl.BlockSpec((None, BLOCK, kvw), lambda b, j: (b, jnp.minimum((j + 1) * nsub, nb - 1), 1)),
        pl.BlockSpec((1, ATT_Q), lambda b, j: (0, 0)),
    ]
    w_specs, w_shapes = [], []
    for w in weights:
        _, k, n = w.shape
        rows = k // nsteps
        assert rows * nsteps == k and rows % 16 == 0
        w_specs.append(pl.BlockSpec((None, rows, n), lambda b, j: (layer, b * nblk + j, 0)))
        w_shapes.append(jax.ShapeDtypeStruct((k, n), BF16))
    wo_specs = [pl.BlockSpec((sp.block_shape[1], sp.block_shape[2]), lambda b, j: (b * nblk + j, 0))
                for sp in w_specs]

    def specs(pos):
        tile = lambda w: pl.BlockSpec((None, tm, w), lambda b, j: (b, pos(j), 0))
        kt_spec = pl.BlockSpec((None, tm * N_HEADS_M, HEAD_DIM_M), lambda b, j: (b, pos(j), 0))
        gt_spec = pl.BlockSpec((None, N_GATES, tm), lambda b, j: (b, 0, pos(j)))
        return [tile(M_W), kt_spec, tile(M_W), gt_spec], tile(M_W)

    in_f, out_f = specs(lambda j: j)
    in_b, out_b = specs(lambda j: nblk - 1 - j)
    state = [pltpu.VMEM((N_HEADS_M, HEAD_DIM_M, 2 * HEAD_DIM_M), F32), pltpu.VMEM((8, LANES), F32)]
    outs = pl.pallas_call(
        functools.partial(_mixer_kernel, nblk=nblk, nchunk=tm // CHUNK, ncast=len(weights)),
        grid=(bsz, nblk),
        in_specs=in_f + in_b + [pl.BlockSpec((N_GATES, LANES), lambda b, j: (0, 0))] + attn_specs
        + w_specs,
        out_specs=[out_f, out_b, pl.BlockSpec((None, tm, ATT_Q), lambda b, j: (b, j, 0))] + wo_specs,
        out_shape=[jax.ShapeDtypeStruct((bsz, s, M_W), F32)] * 2
        + [jax.ShapeDtypeStruct((bsz, s, ATT_Q), BF16)] + w_shapes,
        scratch_shapes=state + state + [pltpu.VMEM((3 * N_HEADS_ATT, BLOCK, 3 * BLOCK), F32)],
        compiler_params=pltpu.CompilerParams(
            dimension_semantics=("arbitrary", "arbitrary"), vmem_limit_bytes=VMEM_LIMIT),
        name="mixer",
    )(qc, kct, vm, gates_t, qc, kct, vm, gates_t, bg_rows, sink, qkva, qkva, qkva, g_attn, *weights)
    return outs[0], outs[1], outs[2], outs[3:]


def _zero_after(a):
    bits = pltpu.bitcast(a, jnp.uint32)
    z = lax.shift_right_logical(lax.shift_right_logical(bits, jnp.uint32(16)), jnp.uint32(16))
    return pltpu.bitcast(z, F32)


def _attn_bias_init(bias_scr):
    nk = 3 * BLOCK
    row = lax.broadcasted_iota(jnp.int32, (BLOCK, nk), 0)
    col = lax.broadcasted_iota(jnp.int32, (BLOCK, nk), 1)
    dist = jnp.abs(col - BLOCK - row)
    distf = dist.astype(F32)
    for var in range(3):
        ok = dist <= WINDOW
        if var == 1:
            ok = ok & (col >= BLOCK)
        elif var == 2:
            ok = ok & (col < 2 * BLOCK)
        for h in range(N_HEADS_ATT):
            slope = 2.0 ** (-8.0 * (h + 1.0) / N_HEADS_ATT)
            bias_scr[var * N_HEADS_ATT + h] = jnp.where(ok, (-slope * LOG2E) * distf, -jnp.inf)


def _attn_pieces(first, last, sink_ref, q_ref, kvp_ref, kvn_ref, g_ref, o_ref, bias_scr, anchors,
                 *, nsub):
    nk = 3 * BLOCK

    lane_k = lax.broadcasted_iota(jnp.int32, (nk, LANES), 1)
    ones_a = jnp.where(lane_k < HEAD_DIM_ATT, 1.0, 0.0).astype(BF16)
    ones_b = jnp.where(lane_k < HEAD_DIM_ATT, 0.0, 1.0).astype(BF16)
    lo_half_q = lax.broadcasted_iota(jnp.int32, (BLOCK, LANES), 1) < HEAD_DIM_ATT

    def kv_block(idx):
        if idx < 0:
            return kvp_ref[...]
        if idx >= nsub:
            return kvn_ref[...]
        return q_ref[idx * BLOCK:(idx + 1) * BLOCK, ATT_Q:QKVA_W]

    for n in range(nsub):
        rows = slice(n * BLOCK, (n + 1) * BLOCK)
        kv = jnp.concatenate([kv_block(n - 1), kv_block(n), kv_block(n + 1)], axis=0)
        if n == 0:
            var = jnp.where(first, 1, 0)
        elif n == nsub - 1:
            var = jnp.where(last, 2, 0)
        else:
            var = 0
        pieces = []
        for kvh in range(N_KV_HEADS):
            k_st, k_sw = kv[:, 0:LANES], kv[:, LANES:2 * LANES]
            v_st, v_sw = kv[:, 2 * LANES:3 * LANES], kv[:, 3 * LANES:4 * LANES]
            if kvh == 0:
                k_lo, k_hi, v_lo, v_hi = k_st, k_sw, v_st, v_sw
            else:
                k_lo, k_hi, v_lo, v_hi = k_sw, k_st, v_sw, v_st
            kk = jnp.concatenate([k_lo * ones_a, k_hi * ones_b], axis=0)
            vv = jnp.concatenate([
                jnp.concatenate([v_lo * ones_a, ones_a], axis=1),
                jnp.concatenate([v_hi * ones_b, ones_b], axis=1)], axis=0)
            for pair in range(GROUP_SIZE // 2):
                h0 = kvh * GROUP_SIZE + 2 * pair
                qp = q_ref[rows, h0 * HEAD_DIM_ATT:(h0 + 2) * HEAD_DIM_ATT]
                s2 = _dot_nt(qp, kk)
                tie = _zero_after(anchors[-1])[:, 0:1] if anchors else 0.0
                ps, es = [], []
                for t in range(2):
                    logits = s2[:, t * nk:(t + 1) * nk] + bias_scr[var * N_HEADS_ATT + h0 + t]
                    sink = sink_ref[h0 + t] * LOG2E + tie
                    mx = jnp.maximum(jnp.max(logits, axis=-1, keepdims=True), sink)
                    ps.append(jnp.exp2(logits - mx).astype(BF16))
                    es.append(jnp.exp2(sink - mx))
                res = _dot(jnp.concatenate(ps, axis=1), vv)
                den = res[:, LANES:] + jnp.where(lo_half_q, es[0], es[1])
                pieces.append(res[:, :LANES] / den)
                yield
        att = jnp.concatenate(pieces, axis=1)
        o_ref[rows, :] = _rms(att, g_ref[...]).astype(o_ref.dtype)
        yield


def _outffn_pieces(x_ref, att_ref, hf_ref, hb_ref, om_ref, mod_ref, gm_ref, wo_ref, g2_ref,
                   w1_ref, w2_ref, gf_ref, o_ref, hid_scr, anchors, *, final):
    hs = hf_ref[...] + hb_ref[...]
    parts = []
    for h in range(N_HEADS_M):
        cols = slice(h * HEAD_DIM_M, (h + 1) * HEAD_DIM_M)
        parts.append(_rms(hs[:, cols], gm_ref[:, cols]))
    hm = jax.nn.sigmoid(om_ref[...]) * jnp.concatenate(parts, axis=1)
    mixin = jnp.concatenate([att_ref[...], hm.astype(BF16)], axis=1)
    mix = _dot(mixin, wo_ref[...])
    anchors.append(mix[0:1, 0:LANES])
    x1 = x_ref[...] + mod_ref[2:3, :] * mix
    hff = (_rms(x1, g2_ref[...] * (1.0 + mod_ref[4:5, :])) + mod_ref[3:4, :]).astype(BF16)
    yield
    for c in range(N_FF_CHUNKS):
        gate = _dot(hff, w1_ref[:, FF_CHUNK * c:FF_CHUNK * (c + 1)])
        up = _dot(hff, w1_ref[:, D_FF + FF_CHUNK * c:D_FF + FF_CHUNK * (c + 1)])
        anchors.append(gate[0:1, 0:LANES])
        hid_scr[:, FF_CHUNK * c:FF_CHUNK * (c + 1)] = (jax.nn.silu(gate) * up).astype(BF16)
        yield
    ff = _dot(hid_scr[...], w2_ref[...])
    anchors.append(ff[0:1, 0:LANES])
    x2 = x1 + mod_ref[5:6, :] * ff
    if final:
        x2 = _rms(x2, gf_ref[...])
    o_ref[...] = x2
    yield


def _outffn_kernel(*refs, final):
    for _ in _outffn_pieces(*refs, [], final=final):
        pass


def _outffn_call(x, att, hf, hb, om, mod, g_m, w_out, g2, w1, w2, g_final, tm, final):
    bsz, s, d = x.shape
    tile = lambda w: pl.BlockSpec((None, tm, w), lambda b, i: (b, i, 0))
    const = lambda shp: pl.BlockSpec(shp, lambda b, i: (0, 0))
    weight = lambda shp: pl.BlockSpec(shp, lambda b, i: (0, 0), pipeline_mode=pl.Buffered(1))
    return pl.pallas_call(
        functools.partial(_outffn_kernel, final=final),
        grid=(bsz, s // tm),
        in_specs=[
            tile(d), tile(ATT_Q), tile(M_W), tile(M_W), tile(M_W),
            pl.BlockSpec((None, N_MOD, d), lambda b, i: (b, 0, 0)),
            const((1, M_W)), weight((ATT_Q + M_W, d)), const((1, d)),
            weight((d, 2 * D_FF)), weight((D_FF, d)), const((1, d)),
        ],
        out_specs=tile(d),
        out_shape=jax.ShapeDtypeStruct((bsz, s, d), F32),
        scratch_shapes=[pltpu.VMEM((tm, D_FF), BF16)],
        compiler_params=pltpu.CompilerParams(
            dimension_semantics=("arbitrary", "arbitrary"), vmem_limit_bytes=VMEM_LIMIT),
        name="outffn",
    )(x, att, hf, hb, om, mod, g_m, w_out, g2, w1, w2, g_final)


def _layer(l, x, c, w_mod, b_mod, g_norm1, w_in, conv_w, conv_b, b_gates, sink,
           g_attn_out, g_mlstm_out, w_out, g_norm2, w_ffn_in, w_ffn_out, g_final, final):
    d = x.shape[-1]
    mod = _mod_call(c, w_mod[l], b_mod[l])

    w_in_p = _cast_t_call(w_in, l, IN_COLS_PAD)
    qkva, qc, kct, vm, om, gates_t = _inproj_call(
        x, mod, g_norm1[l].reshape(1, d), w_in_p, conv_w[l], conv_b[l].reshape(1, 2 * M_W), tm=512)

    bg_rows = jnp.broadcast_to(b_gates[l][:, None], (N_GATES, LANES))
    hf, hb, att, (w_out_b, w_ffn_in_b, w_ffn_out_b) = _mixer_call(
        qc, kct, vm, gates_t, bg_rows, sink[l], qkva, g_attn_out[l].reshape(1, ATT_Q),
        (w_out, w_ffn_in, w_ffn_out), l, tm=1024)

    return _outffn_call(x, att, hf, hb, om, mod, g_mlstm_out[l].reshape(1, M_W), w_out_b,
                        g_norm2[l].reshape(1, d), w_ffn_in_b, w_ffn_out_b,
                        g_final.reshape(1, d), tm=512, final=final)


def kernel(x, c, w_mod, b_mod, g_norm1, w_in, conv_w, conv_b, b_gates, sink, g_attn_out,
           g_mlstm_out, w_out, g_norm2, w_ffn_in, w_ffn_out, g_final):
    depth = w_mod.shape[0]
    for l in range(depth):
        x = _layer(l, x, c, w_mod, b_mod, g_norm1, w_in, conv_w, conv_b, b_gates, sink, g_attn_out,
                   g_mlstm_out, w_out, g_norm2, w_ffn_in, w_ffn_out, g_final, final=(l == depth - 1))
    return x
```

```python
import functools

import jax
import jax.numpy as jnp
from jax import lax
from jax.experimental import pallas as pl
from jax.experimental.pallas import tpu as pltpu

F32 = jnp.float32
BF16 = jnp.bfloat16

D_MODEL = 1024
EPS = 1e-6
N_HEADS_ATT = 8
N_KV_HEADS = 2
HEAD_DIM_ATT = 64
GROUP_SIZE = N_HEADS_ATT // N_KV_HEADS
WINDOW = 128
BLOCK = 128
N_HEADS_M = 4
HEAD_DIM_M = 128
CHUNK = 128
ATT_Q = N_HEADS_ATT * HEAD_DIM_ATT
ATT_KV = N_KV_HEADS * HEAD_DIM_ATT
M_W = N_HEADS_M * HEAD_DIM_M
N_GATES = 4 * N_HEADS_M
D_FF = 2816
N_MOD = 6

LANES = 128
GATE_PAD = LANES
FF_CHUNK = 256
N_FF_CHUNKS = D_FF // FF_CHUNK
OUT_COL_BLOCK = 256
VMEM_LIMIT = 56 * 1024 * 1024

C_QA = 0
C_KA = ATT_Q
C_VA = ATT_Q + ATT_KV
C_QKM = ATT_Q + 2 * ATT_KV
C_VM = C_QKM + 2 * M_W
C_OM = C_VM + M_W
C_G = C_OM + M_W
IN_COLS_PAD = C_G + GATE_PAD
QKVA_W = ATT_Q + 4 * ATT_KV


def _dot(a, b):
    return jnp.dot(a, b, preferred_element_type=F32)


def _dot_nt(a, b):
    return lax.dot_general(a, b, (((1,), (1,)), ((), ())), preferred_element_type=F32)


def _rms(x, g):
    return x * lax.rsqrt(jnp.mean(x * x, axis=-1, keepdims=True) + EPS) * g


def _mod(mod_ref, k):
    return mod_ref[:, k * D_MODEL:(k + 1) * D_MODEL]


def _alternate(*streams):
    live = list(streams)
    while live:
        for s in list(live):
            if next(s, StopIteration) is StopIteration:
                live.remove(s)


def _cast_t_kernel(wt_ref, o_ref):
    n = wt_ref.shape[0]
    full = n // LANES * LANES
    for r in range(0, full, LANES):
        o_ref[:, r:r + LANES] = wt_ref[r:r + LANES, :].T.astype(o_ref.dtype)
    if o_ref.shape[1] > full:
        tail = jnp.concatenate(
            [wt_ref[full:n, :], jnp.zeros((full + LANES - n, wt_ref.shape[1]), wt_ref.dtype)], axis=0)
        o_ref[:, full:full + LANES] = tail.T.astype(o_ref.dtype)


def _cast_t_call(w, layer, n_out, bk=256):
    _, k, n = w.shape
    assert n_out - n < LANES and n_out % LANES == 0
    return pl.pallas_call(
        _cast_t_kernel,
        grid=(k // bk,),
        in_specs=[pl.BlockSpec((None, n, bk), lambda i: (layer, 0, i))],
        out_specs=pl.BlockSpec((bk, n_out), lambda i: (i, 0)),
        out_shape=jax.ShapeDtypeStruct((k, n_out), BF16),
        compiler_params=pltpu.CompilerParams(
            dimension_semantics=("arbitrary",), vmem_limit_bytes=VMEM_LIMIT),
        name="cast",
    )(jnp.swapaxes(w, 1, 2))


SUBLANES = 8


def _mod_kernel(c_ref, w_ref, b_ref, o_ref):
    bsz, d = c_ref.shape
    c = jnp.concatenate([c_ref[...], jnp.zeros((SUBLANES - bsz, d), F32)], axis=0)
    res = _dot(jax.nn.silu(c).astype(BF16), w_ref[...].astype(BF16)) + b_ref[...]
    for b in range(bsz):
        o_ref[b] = res[b:b + 1, :]


def _mod_call(c, w_mod, b_mod):
    bsz, d = c.shape
    assert bsz <= SUBLANES
    n = w_mod.shape[1]
    bn = 1024
    return pl.pallas_call(
        _mod_kernel,
        grid=(n // bn,),
        in_specs=[
            pl.BlockSpec((bsz, d), lambda i: (0, 0)),
            pl.BlockSpec((d, bn), lambda i: (0, i)),
            pl.BlockSpec((1, bn), lambda i: (0, i)),
        ],
        out_specs=pl.BlockSpec((bsz, 1, bn), lambda i: (0, 0, i)),
        out_shape=jax.ShapeDtypeStruct((bsz, 1, n), F32),
        compiler_params=pltpu.CompilerParams(dimension_semantics=("arbitrary",)),
        name="mod",
    )(c, w_mod, b_mod.reshape(1, n))


MXU_COLS = 256
CONV_ROWS = 64


def _inproj_kernel(x_ref, mod_ref, g_ref, w_ref, cw_ref, cb_ref,
                   qkva_ref, qc_ref, kct_ref, vm_ref, om_ref, gt_ref,
                   raw_scr, new_scr, k_scr, carry_scr, *, nblk):
    j = pl.program_id(0)

    @pl.when(j == 0)
    def _init():
        raw_scr[...] = jnp.zeros_like(raw_scr)
        carry_scr[...] = jnp.zeros_like(carry_scr)

    x = x_ref[...]
    h = _rms(x, g_ref[...] * (1.0 + _mod(mod_ref, 1))) + _mod(mod_ref, 0)
    hb = h.astype(BF16)
    tm = x.shape[0]
    nchunk = tm // CHUNK
    within = (j + nblk - 1) % nblk

    anchors = []

    def zero_after(a):
        return jnp.concatenate([_zero_after(a)] * (2 * M_W // LANES), axis=1)

    def project():
        order = list(range(C_QKM, C_VM, MXU_COLS)) + list(range(0, C_QKM, MXU_COLS)) + \
            list(range(C_VM, IN_COLS_PAD, MXU_COLS))
        for c0 in order:
            c1 = min(c0 + MXU_COLS, IN_COLS_PAD)
            res = _dot(hb, w_ref[:, c0:c1])
            anchors.append(res[tm - 1:tm, c1 - c0 - LANES:c1 - c0])
            if c0 < C_KA:
                qkva_ref[:, c0:c1] = (res * (HEAD_DIM_ATT ** -0.5 * LOG2E)).astype(BF16)
            elif c0 < C_QKM:
                ka, va = res[:, :ATT_KV], res[:, ATT_KV:]
                half = HEAD_DIM_ATT
                qkva_ref[:, ATT_Q:ATT_Q + ATT_KV] = ka.astype(BF16)
                qkva_ref[:, ATT_Q + ATT_KV:ATT_Q + 2 * ATT_KV] = pltpu.roll(ka, half, axis=1).astype(BF16)
                qkva_ref[:, ATT_Q + 2 * ATT_KV:ATT_Q + 3 * ATT_KV] = va.astype(BF16)
                qkva_ref[:, ATT_Q + 3 * ATT_KV:ATT_Q + 4 * ATT_KV] = pltpu.roll(va, half, axis=1).astype(BF16)
            elif c0 < C_VM:
                new_scr[:, c0 - C_QKM:c1 - C_QKM] = res
            elif c0 < C_OM:
                vm_ref[:, c0 - C_VM:c1 - C_VM] = res.astype(BF16)
            elif c0 < C_G:
                om_ref[:, c0 - C_OM:c1 - C_OM] = res
            else:
                for g in range(nchunk):
                    blk = res[g * CHUNK:(g + 1) * CHUNK, :].T
                    gt_ref[:, g * CHUNK:(g + 1) * CHUNK] = blk[0:N_GATES, :]
            yield

    def conv():
        R = CONV_ROWS
        row = lax.broadcasted_iota(jnp.int32, (R, 1), 0)
        for p in range(tm // R):
            r0 = p * R
            xg = raw_scr[r0:r0 + R, :]
            if p == 0:
                prev_row = jnp.where(within == 0, 0.0, carry_scr[0:1, :])
            else:
                prev_row = raw_scr[r0 - 1:r0, :]
            if r0 + R == tm:
                next_row = jnp.where(within == nblk - 1, 0.0, new_scr[0:1, :])
            else:
                next_row = raw_scr[r0 + R:r0 + R + 1, :]
            xm1 = jnp.where(row == 0, prev_row, pltpu.roll(xg, 1, axis=0))
            xp1 = jnp.where(row == R - 1, next_row, pltpu.roll(xg, R - 1, axis=0))
            z = zero_after(anchors[-1])
            y =xm1 * (cw_ref[0:1, :] + z) + xg * (cw_ref[1:2, :] + z) + xp1 * (cw_ref[2:3, :] + z) \
                + cb_ref[...]
            y = jax.nn.silu(y)
            qc_ref[r0:r0 + R, :] = (y[:, :M_W] * (HEAD_DIM_M ** -0.5)).astype(BF16)
            k_scr[r0:r0 + R, :] = y[:, M_W:]
            yield
            if (r0 + R) % CHUNK == 0:
                g = r0 // CHUNK
                for hd in range(N_HEADS_M):
                    blk = k_scr[g * CHUNK:(g + 1) * CHUNK, hd * HEAD_DIM_M:(hd + 1) * HEAD_DIM_M]
                    base = (g * N_HEADS_M + hd) * HEAD_DIM_M
                    kct_ref[base:base + HEAD_DIM_M, :] = blk.T.astype(BF16)
                yield

    proj, cv = project(), conv()
    n_proj = -(-IN_COLS_PAD // MXU_COLS)
    n_conv = tm // CONV_ROWS + nchunk
    done = 0
    for k in range(n_proj):
        next(proj)
        while done * n_proj < (k + 1) * n_conv:
            next(cv)
            done += 1
    carry_scr[0:1, :] = raw_scr[tm - 1:tm, :]
    raw_scr[...] = new_scr[...]


def _inproj_call(x, mod, g1, w_in_p, conv_w, conv_b, tm):
    bsz, s, d = x.shape
    nblk = s // tm
    ntiles = bsz * nblk
    outs = (
        jax.ShapeDtypeStruct((bsz, s, QKVA_W), BF16),
        jax.ShapeDtypeStruct((bsz, s, M_W), BF16),
        jax.ShapeDtypeStruct((bsz, s * N_HEADS_M, HEAD_DIM_M), BF16),
        jax.ShapeDtypeStruct((bsz, s, M_W), BF16),
        jax.ShapeDtypeStruct((bsz, s, M_W), F32),
        jax.ShapeDtypeStruct((bsz, N_GATES, s), F32),
    )

    def cur(j):
        t = jnp.minimum(j, ntiles - 1)
        return t // nblk, t % nblk

    def old(j):
        t = jnp.maximum(j - 1, 0)
        return t // nblk, t % nblk

    cur_tile = lambda w: pl.BlockSpec((None, tm, w), lambda j: (*cur(j), 0))
    old_tile = lambda w: pl.BlockSpec((None, tm, w), lambda j: (*old(j), 0))
    const = lambda shp: pl.BlockSpec(shp, lambda j: (0, 0))
    return pl.pallas_call(
        functools.partial(_inproj_kernel, nblk=nblk),
        grid=(ntiles + 1,),
        in_specs=[
            cur_tile(d),
            pl.BlockSpec((None, 1, N_MOD * d), lambda j: (cur(j)[0], 0, 0)),
            const((1, d)),
            pl.BlockSpec((d, IN_COLS_PAD), lambda j: (0, 0), pipeline_mode=pl.Buffered(1)),
            const((3, 2 * M_W)), const((1, 2 * M_W)),
        ],
        out_specs=[cur_tile(QKVA_W), old_tile(M_W),
                   pl.BlockSpec((None, tm * N_HEADS_M, HEAD_DIM_M), lambda j: (*old(j), 0)),
                   cur_tile(M_W), cur_tile(M_W),
                   pl.BlockSpec((None, N_GATES, tm), lambda j: (cur(j)[0], 0, cur(j)[1]))],
        out_shape=outs,
        scratch_shapes=[pltpu.VMEM((tm, 2 * M_W), F32), pltpu.VMEM((tm, 2 * M_W), F32),
                        pltpu.VMEM((tm, M_W), F32), pltpu.VMEM((8, 2 * M_W), F32)],
        compiler_params=pltpu.CompilerParams(
            dimension_semantics=("arbitrary",), vmem_limit_bytes=VMEM_LIMIT),
        name="inproj",
    )(x, mod, g1, w_in_p, conv_w, conv_b)


def _mlstm_reset(first, c_scr, m_scr):
    @pl.when(first)
    def _init():
        c_scr[...] = jnp.zeros_like(c_scr)
        m_scr[...] = jnp.zeros_like(m_scr)


LOG2E = 1.4426950408889634


def _mlstm_stream(q_ref, kt_ref, v_ref, gt_ref, bgr_ref, h_ref, c_scr, m_scr, *, reverse, nchunk):
    L = CHUNK
    ti = lax.broadcasted_iota(jnp.int32, (L, L), 0)
    si = lax.broadcasted_iota(jnp.int32, (L, L), 1)
    causal = (si >= ti) if reverse else (si <= ti)
    tri_t = jnp.where((ti >= si) if reverse else (ti <= si), 1.0, 0.0).astype(BF16)
    lane = lax.broadcasted_iota(jnp.int32, (1, L), 1)
    last = 0 if reverse else L - 1
    i_off, f_off = (2 * N_HEADS_M, 3 * N_HEADS_M) if reverse else (0, N_HEADS_M)
    ones = jnp.ones((L, HEAD_DIM_M), BF16)
    order = list(range(nchunk - 1, -1, -1) if reverse else range(nchunk))
    rows_of = lambda g: slice(g * L, (g + 1) * L)

    def split3(a):
        hi = a.astype(BF16)
        r1 = a - hi.astype(F32)
        mid = r1.astype(BF16)
        return hi, mid, (r1 - mid.astype(F32)).astype(BF16)

    gates = {}
    pad = jnp.zeros((L - N_GATES, L), F32)
    for g in order:
        gr = gt_ref[:, rows_of(g)] + bgr_ref[...]
        parts = _dot(jnp.concatenate(split3(jax.nn.log_sigmoid(gr)), axis=0), tri_t)
        bcr = (parts[0:N_GATES] + parts[N_GATES:2 * N_GATES] + parts[2 * N_GATES:3 * N_GATES]) * LOG2E
        gates[g] = (gr * LOG2E, bcr, jnp.concatenate([-bcr, pad], axis=0).T)
        yield

    def head(g, h):
        rows = rows_of(g)
        cols = slice(h * HEAD_DIM_M, (h + 1) * HEAD_DIM_M)
        gct, bct, nbcum = gates[g]
        r_row = gct[i_off + h:i_off + h + 1, :] - bct[f_off + h:f_off + h + 1, :]
        btot = jnp.sum(jnp.where(lane == last, bct[f_off + h:f_off + h + 1, :], 0.0),
                       axis=1, keepdims=True)
        rmax = jnp.max(r_row, axis=1, keepdims=True)
        a_max = btot + rmax
        m_prev = m_scr[h:h + 1, 0:1]
        m_new = jnp.maximum(btot + m_prev, a_max)
        m_scr[h:h + 1, :] = jnp.broadcast_to(m_new, (1, LANES))
        decay = jnp.exp2(btot + m_prev - m_new)
        w_row = jnp.exp2(r_row - rmax) * jnp.exp2(a_max - m_new)
        kt_rows = slice((g * N_HEADS_M + h) * HEAD_DIM_M, (g * N_HEADS_M + h + 1) * HEAD_DIM_M)
        cm = jnp.max(jnp.where(causal, r_row, -jnp.inf), axis=1, keepdims=True)
        yield
        q = q_ref[rows, cols]
        s = _dot(q, kt_ref[kt_rows, :])
        u = jnp.maximum(cm, m_prev)
        ub = jnp.broadcast_to(u, (L, L))
        dm = jnp.exp2(jnp.where(causal, r_row - ub, -jnp.inf))
        qi = q * jnp.exp2(m_prev - ub).astype(BF16)
        floor = jnp.exp2(jnp.broadcast_to(nbcum[:, f_off + h:f_off + h + 1], (L, L)) - ub)
        yield
        vaug = jnp.concatenate([v_ref[rows, cols], ones], axis=1)
        lhs = jnp.concatenate([(s * dm).astype(BF16), qi], axis=1)
        rhs = jnp.concatenate([vaug, c_scr[h].astype(BF16)], axis=0)
        out = _dot(lhs, rhs)
        upd = _dot(kt_ref[kt_rows, :] * w_row.astype(BF16), vaug)
        yield
        c_scr[h] = decay * c_scr[h] + upd
        h_ref[rows, cols] = out[:, :HEAD_DIM_M] / jnp.maximum(jnp.abs(out[:, HEAD_DIM_M:]), floor)
        yield

    n_stage, lag = 4, 2
    heads = {}
    for slot in range(lag * (nchunk - 1) + n_stage):
        for i, g in enumerate(order):
            stage = slot - lag * i
            if 0 <= stage < n_stage:
                for h in range(N_HEADS_M):
                    if stage == 0:
                        heads[g, h] = head(g, h)
                    next(heads[g, h])
                    yield


def _mixer_kernel(qf_ref, kf_ref, vf_ref, gtf_ref, qb_ref, kb_ref, vb_ref, gtb_ref, bgr_ref,
                  sink_ref, qa_ref, kvp_ref, kvn_ref, ga_ref, *rest, nblk, nchunk, ncast):
    w_refs, rest = rest[:ncast], rest[ncast:]
    hf_ref, hb_ref, att_ref = rest[:3]
    wo_refs, (cf_scr, mf_scr, cb_scr, mb_scr, bias_scr) = rest[3:3 + ncast], rest[3 + ncast:]
    b = pl.program_id(0)
    j = pl.program_id(1)

    @pl.when((b == 0) & (j == 0))
    def _init():
        _attn_bias_init(bias_scr)

    _mlstm_reset(j == 0, cf_scr, mf_scr)
    _mlstm_reset(j == 0, cb_scr, mb_scr)
    for w_ref, wo_ref in zip(w_refs, wo_refs):
        wo_ref[...] = w_ref[...].astype(wo_ref.dtype)
    fwd = _mlstm_stream(qf_ref, kf_ref, vf_ref, gtf_ref, bgr_ref, hf_ref, cf_scr, mf_scr,
                        reverse=False, nchunk=nchunk)
    bwd = _mlstm_stream(qb_ref, kb_ref, vb_ref, gtb_ref, bgr_ref, hb_ref, cb_scr, mb_scr,
                        reverse=True, nchunk=nchunk)
    att = _attn_pieces(j == 0, j == nblk - 1, sink_ref, qa_ref, kvp_ref, kvn_ref, ga_ref, att_ref,
                       bias_scr, [], nsub=nchunk)
    _alternate(fwd, bwd)
    for _ in att:
        pass


def _mixer_call(qc, kct, vm, gates_t, bg_rows, sink, qkva, g_attn, weights, layer, tm):
    bsz, s, _ = qc.shape
    nblk = s // tm
    nsteps = bsz * nblk
    nsub = tm // BLOCK
    nb = s // BLOCK
    kvw = QKVA_W - ATT_Q
    attn_specs = [
        pl.BlockSpec(memory_space=pltpu.SMEM),
        pl.BlockSpec((None, tm, QKVA_W), lambda b, j: (b, j, 0)),
        pl.BlockSpec((None, BLOCK, kvw), lambda b, j: (b, jnp.maximum(j * nsub - 1, 0), 1)),
        pl.BlockSpec((None, BLOCK, kvw), lambda b, j: (b, jnp.minimum((j + 1) * nsub, nb - 1), 1)),
        pl.BlockSpec((1, ATT_Q), lambda b, j: (0, 0)),
    ]
    w_specs, w_shapes = [], []
    for w in weights:
        _, k, n = w.shape
        rows = k // nsteps
        assert rows * nsteps == k and rows % 16 == 0
        w_specs.append(pl.BlockSpec((None, rows, n), lambda b, j: (layer, b * nblk + j, 0)))
        w_shapes.append(jax.ShapeDtypeStruct((k, n), BF16))
    wo_specs = [pl.BlockSpec((sp.block_shape[1], sp.block_shape[2]), lambda b, j: (b * nblk + j, 0))
                for sp in w_specs]

    def specs(pos):
        tile = lambda w: pl.BlockSpec((None, tm, w), lambda b, j: (b, pos(j), 0))
        kt_spec = pl.BlockSpec((None, tm * N_HEADS_M, HEAD_DIM_M), lambda b, j: (b, pos(j), 0))
        gt_spec = pl.BlockSpec((None, N_GATES, tm), lambda b, j: (b, 0, pos(j)))
        return [tile(M_W), kt_spec, tile(M_W), gt_spec], tile(M_W)

    in_f, out_f = specs(lambda j: j)
    in_b, out_b = specs(lambda j: nblk - 1 - j)
    state = [pltpu.VMEM((N_HEADS_M, HEAD_DIM_M, 2 * HEAD_DIM_M), F32), pltpu.VMEM((8, LANES), F32)]
    outs = pl.pallas_call(
        functools.partial(_mixer_kernel, nblk=nblk, nchunk=tm // CHUNK, ncast=len(weights)),
        grid=(bsz, nblk),
        in_specs=in_f + in_b + [pl.BlockSpec((N_GATES, LANES), lambda b, j: (0, 0))] + attn_specs
        + w_specs,
        out_specs=[out_f, out_b, pl.BlockSpec((None, tm, ATT_Q), lambda b, j: (b, j, 0))] + wo_specs,
        out_shape=[jax.ShapeDtypeStruct((bsz, s, M_W), F32)] * 2
        + [jax.ShapeDtypeStruct((bsz, s, ATT_Q), BF16)] + w_shapes,
        scratch_shapes=state + state + [pltpu.VMEM((3 * N_HEADS_ATT, BLOCK, 3 * BLOCK), F32)],
        compiler_params=pltpu.CompilerParams(
            dimension_semantics=("arbitrary", "arbitrary"), vmem_limit_bytes=VMEM_LIMIT),
        name="mixer",
    )(qc, kct, vm, gates_t, qc, kct, vm, gates_t, bg_rows, sink, qkva, qkva, qkva, g_attn, *weights)
    return outs[0], outs[1], outs[2], outs[3:]


def _zero_after(a):
    bits = pltpu.bitcast(a, jnp.uint32)
    z = lax.shift_right_logical(lax.shift_right_logical(bits, jnp.uint32(16)), jnp.uint32(16))
    return pltpu.bitcast(z, F32)


def _attn_bias_init(bias_scr):
    nk = 3 * BLOCK
    row = lax.broadcasted_iota(jnp.int32, (BLOCK, nk), 0)
    col = lax.broadcasted_iota(jnp.int32, (BLOCK, nk), 1)
    dist = jnp.abs(col - BLOCK - row)
    distf = dist.astype(F32)
    for var in range(3):
        ok = dist <= WINDOW
        if var == 1:
            ok = ok & (col >= BLOCK)
        elif var == 2:
            ok = ok & (col < 2 * BLOCK)
        for h in range(N_HEADS_ATT):
            slope = 2.0 ** (-8.0 * (h + 1.0) / N_HEADS_ATT)
            bias_scr[var * N_HEADS_ATT + h] = jnp.where(ok, (-slope * LOG2E) * distf, -jnp.inf)


def _attn_pieces(first, last, sink_ref, q_ref, kvp_ref, kvn_ref, g_ref, o_ref, bias_scr, anchors,
                 *, nsub):
    nk = 3 * BLOCK

    lane_k = lax.broadcasted_iota(jnp.int32, (nk, LANES), 1)
    ones_a = jnp.where(lane_k < HEAD_DIM_ATT, 1.0, 0.0).astype(BF16)
    ones_b = jnp.where(lane_k < HEAD_DIM_ATT, 0.0, 1.0).astype(BF16)
    lo_half_q = lax.broadcasted_iota(jnp.int32, (BLOCK, LANES), 1) < HEAD_DIM_ATT

    def kv_block(idx):
        if idx < 0:
            return kvp_ref[...]
        if idx >= nsub:
            return kvn_ref[...]
        return q_ref[idx * BLOCK:(idx + 1) * BLOCK, ATT_Q:QKVA_W]

    for n in range(nsub):
        rows = slice(n * BLOCK, (n + 1) * BLOCK)
        kv = jnp.concatenate([kv_block(n - 1), kv_block(n), kv_block(n + 1)], axis=0)
        if n == 0:
            var = jnp.where(first, 1, 0)
        elif n == nsub - 1:
            var = jnp.where(last, 2, 0)
        else:
            var = 0
        pieces = []
        for kvh in range(N_KV_HEADS):
            k_st, k_sw = kv[:, 0:LANES], kv[:, LANES:2 * LANES]
            v_st, v_sw = kv[:, 2 * LANES:3 * LANES], kv[:, 3 * LANES:4 * LANES]
            if kvh == 0:
                k_lo, k_hi, v_lo, v_hi = k_st, k_sw, v_st, v_sw
            else:
                k_lo, k_hi, v_lo, v_hi = k_sw, k_st, v_sw, v_st
            kk = jnp.concatenate([k_lo * ones_a, k_hi * ones_b], axis=0)
            vv = jnp.concatenate([
                jnp.concatenate([v_lo * ones_a, ones_a], axis=1),
                jnp.concatenate([v_hi * ones_b, ones_b], axis=1)], axis=0)
            for pair in range(GROUP_SIZE // 2):
                h0 = kvh * GROUP_SIZE + 2 * pair
                qp = q_ref[rows, h0 * HEAD_DIM_ATT:(h0 + 2) * HEAD_DIM_ATT]
                s2 = _dot_nt(qp, kk)
                tie = _zero_after(anchors[-1])[:, 0:1] if anchors else 0.0
                ps, es = [], []
                for t in range(2):
                    logits = s2[:, t * nk:(t + 1) * nk] + bias_scr[var * N_HEADS_ATT + h0 + t]
                    sink = sink_ref[h0 + t] * LOG2E + tie
                    mx = jnp.maximum(jnp.max(logits, axis=-1, keepdims=True), sink)
                    ps.append(jnp.exp2(logits - mx).astype(BF16))
                    es.append(jnp.exp2(sink - mx))
                res = _dot(jnp.concatenate(ps, axis=1), vv)
                den = res[:, LANES:] + jnp.where(lo_half_q, es[0], es[1])
                pieces.append(res[:, :LANES] / den)
                yield
        att = jnp.concatenate(pieces, axis=1)
        o_ref[rows, :] = _rms(att, g_ref[...]).astype(o_ref.dtype)
        yield


def _outffn_pieces(x_ref, att_ref, hf_ref, hb_ref, om_ref, mod_ref, gm_ref, wo_ref, g2_ref,
                   w1_ref, w2_ref, gf_ref, o_ref, hid_scr, anchors, *, final):
    hs = hf_ref[...] + hb_ref[...]
    parts = []
    for h in range(N_HEADS_M):
        cols = slice(h * HEAD_DIM_M, (h + 1) * HEAD_DIM_M)
        parts.append(_rms(hs[:, cols], gm_ref[:, cols]))
    hm = jax.nn.sigmoid(om_ref[...]) * jnp.concatenate(parts, axis=1)
    mixin = jnp.concatenate([att_ref[...], hm.astype(BF16)], axis=1)
    mix = _dot(mixin, wo_ref[...])
    anchors.append(mix[0:1, 0:LANES])
    x1 = x_ref[...] + _mod(mod_ref, 2) * mix
    hff = (_rms(x1, g2_ref[...] * (1.0 + _mod(mod_ref, 4))) + _mod(mod_ref, 3)).astype(BF16)
    yield
    for c in range(N_FF_CHUNKS):
        gate = _dot(hff, w1_ref[:, FF_CHUNK * c:FF_CHUNK * (c + 1)])
        up = _dot(hff, w1_ref[:, D_FF + FF_CHUNK * c:D_FF + FF_CHUNK * (c + 1)])
        anchors.append(gate[0:1, 0:LANES])
        hid_scr[:, FF_CHUNK * c:FF_CHUNK * (c + 1)] = (jax.nn.silu(gate) * up).astype(BF16)
        yield
    ff = _dot(hid_scr[...], w2_ref[...])
    anchors.append(ff[0:1, 0:LANES])
    x2 = x1 + _mod(mod_ref, 5) * ff
    if final:
        x2 = _rms(x2, gf_ref[...])
    o_ref[...] = x2
    yield


def _outffn_kernel(*refs, final):
    for _ in _outffn_pieces(*refs, [], final=final):
        pass


def _outffn_call(x, att, hf, hb, om, mod, g_m, w_out, g2, w1, w2, g_final, tm, final):
    bsz, s, d = x.shape
    tile = lambda w: pl.BlockSpec((None, tm, w), lambda b, i: (b, i, 0))
    const = lambda shp: pl.BlockSpec(shp, lambda b, i: (0, 0))
    weight = lambda shp: pl.BlockSpec(shp, lambda b, i: (0, 0), pipeline_mode=pl.Buffered(1))
    return pl.pallas_call(
        functools.partial(_outffn_kernel, final=final),
        grid=(bsz, s // tm),
        in_specs=[
            tile(d), tile(ATT_Q), tile(M_W), tile(M_W), tile(M_W),
            pl.BlockSpec((None, 1, N_MOD * d), lambda b, i: (b, 0, 0)),
            const((1, M_W)), weight((ATT_Q + M_W, d)), const((1, d)),
            weight((d, 2 * D_FF)), weight((D_FF, d)), const((1, d)),
        ],
        out_specs=tile(d),
        out_shape=jax.ShapeDtypeStruct((bsz, s, d), F32),
        scratch_shapes=[pltpu.VMEM((tm, D_FF), BF16)],
        compiler_params=pltpu.CompilerParams(
            dimension_semantics=("arbitrary", "arbitrary"), vmem_limit_bytes=VMEM_LIMIT),
        name="outffn",
    )(x, att, hf, hb, om, mod, g_m, w_out, g2, w1, w2, g_final)


def _layer(l, x, c, w_mod, b_mod, g_norm1, w_in, conv_w, conv_b, b_gates, sink,
           g_attn_out, g_mlstm_out, w_out, g_norm2, w_ffn_in, w_ffn_out, g_final, final):
    d = x.shape[-1]
    mod = _mod_call(c, w_mod[l], b_mod[l])

    w_in_p = _cast_t_call(w_in, l, IN_COLS_PAD)
    qkva, qc, kct, vm, om, gates_t = _inproj_call(
        x, mod, g_norm1[l].reshape(1, d), w_in_p, conv_w[l], conv_b[l].reshape(1, 2 * M_W), tm=512)

    bg_rows = jnp.broadcast_to(b_gates[l][:, None], (N_GATES, LANES))
    hf, hb, att, (w_out_b, w_ffn_in_b, w_ffn_out_b) = _mixer_call(
        qc, kct, vm, gates_t, bg_rows, sink[l], qkva, g_attn_out[l].reshape(1, ATT_Q),
        (w_out, w_ffn_in, w_ffn_out), l, tm=1024)

    return _outffn_call(x, att, hf, hb, om, mod, g_mlstm_out[l].reshape(1, M_W), w_out_b,
                        g_norm2[l].reshape(1, d), w_ffn_in_b, w_ffn_out_b,
                        g_final.reshape(1, d), tm=512, final=final)


def kernel(x, c, w_mod, b_mod, g_norm1, w_in, conv_w, conv_b, b_gates, sink, g_attn_out,
           g_mlstm_out, w_out, g_norm2, w_ffn_in, w_ffn_out, g_final):
    depth = w_mod.shape[0]
    for l in range(depth):
        x = _layer(l, x, c, w_mod, b_mod, g_norm1, w_in, conv_w, conv_b, b_gates, sink, g_attn_out,
                   g_mlstm_out, w_out, g_norm2, w_ffn_in, w_ffn_out, g_final, final=(l == depth - 1))
    return x
```

```python
import functools

import jax
import jax.numpy as jnp
from jax import lax
from jax.experimental import pallas as pl
from jax.experimental.pallas import tpu as pltpu

F32 = jnp.float32
BF16 = jnp.bfloat16

D_MODEL = 1024
EPS = 1e-6
N_HEADS_ATT = 8
N_KV_HEADS = 2
HEAD_DIM_ATT = 64
GROUP_SIZE = N_HEADS_ATT // N_KV_HEADS
WINDOW = 128
BLOCK = 128
N_HEADS_M = 4
HEAD_DIM_M = 128
CHUNK = 128
ATT_Q = N_HEADS_ATT * HEAD_DIM_ATT
ATT_KV = N_KV_HEADS * HEAD_DIM_ATT
M_W = N_HEADS_M * HEAD_DIM_M
N_GATES = 4 * N_HEADS_M
D_FF = 2816
N_MOD = 6

LANES = 128
GATE_PAD = LANES
FF_CHUNK = 256
N_FF_CHUNKS = D_FF // FF_CHUNK
OUT_COL_BLOCK = 256
VMEM_LIMIT = 56 * 1024 * 1024

C_QA = 0
C_KA = ATT_Q
C_VA = ATT_Q + ATT_KV
C_QKM = ATT_Q + 2 * ATT_KV
C_VM = C_QKM + 2 * M_W
C_OM = C_VM + M_W
C_G = C_OM + M_W
IN_COLS_PAD = C_G + GATE_PAD
QKVA_W = ATT_Q + 4 * ATT_KV


def _dot(a, b):
    return jnp.dot(a, b, preferred_element_type=F32)


def _dot_nt(a, b):
    return lax.dot_general(a, b, (((1,), (1,)), ((), ())), preferred_element_type=F32)


def _rms(x, g):
    return x * lax.rsqrt(jnp.mean(x * x, axis=-1, keepdims=True) + EPS) * g


def _mod(mod_ref, k):
    return mod_ref[:, k * D_MODEL:(k + 1) * D_MODEL]


def _alternate(*streams):
    live = list(streams)
    while live:
        for s in list(live):
            if next(s, StopIteration) is StopIteration:
                live.remove(s)


def _cast_t_kernel(wt_ref, o_ref):
    n = wt_ref.shape[0]
    full = n // LANES * LANES
    for r in range(0, full, LANES):
        o_ref[:, r:r + LANES] = wt_ref[r:r + LANES, :].T.astype(o_ref.dtype)
    if o_ref.shape[1] > full:
        tail = jnp.concatenate(
            [wt_ref[full:n, :], jnp.zeros((full + LANES - n, wt_ref.shape[1]), wt_ref.dtype)], axis=0)
        o_ref[:, full:full + LANES] = tail.T.astype(o_ref.dtype)


def _cast_t_call(w, layer, n_out, bk=256):
    _, k, n = w.shape
    assert n_out - n < LANES and n_out % LANES == 0
    return pl.pallas_call(
        _cast_t_kernel,
        grid=(k // bk,),
        in_specs=[pl.BlockSpec((None, n, bk), lambda i: (layer, 0, i))],
        out_specs=pl.BlockSpec((bk, n_out), lambda i: (i, 0)),
        out_shape=jax.ShapeDtypeStruct((k, n_out), BF16),
        compiler_params=pltpu.CompilerParams(
            dimension_semantics=("arbitrary",), vmem_limit_bytes=VMEM_LIMIT),
        name="cast",
    )(jnp.swapaxes(w, 1, 2))


SUBLANES = 8


def _mod_kernel(c_ref, w_ref, b_ref, o_ref):
    bsz, d = c_ref.shape
    c = jnp.concatenate([c_ref[...], jnp.zeros((SUBLANES - bsz, d), F32)], axis=0)
    res = _dot(jax.nn.silu(c).astype(BF16), w_ref[...].astype(BF16)) + b_ref[...]
    for b in range(bsz):
        o_ref[b] = res[b:b + 1, :]


def _mod_call(c, w_mod, b_mod):
    bsz, d = c.shape
    assert bsz <= SUBLANES
    n = w_mod.shape[1]
    bn = 1024
    return pl.pallas_call(
        _mod_kernel,
        grid=(n // bn,),
        in_specs=[
            pl.BlockSpec((bsz, d), lambda i: (0, 0)),
            pl.BlockSpec((d, bn), lambda i: (0, i)),
            pl.BlockSpec((1, bn), lambda i: (0, i)),
        ],
        out_specs=pl.BlockSpec((bsz, 1, bn), lambda i: (0, 0, i)),
        out_shape=jax.ShapeDtypeStruct((bsz, 1, n), F32),
        compiler_params=pltpu.CompilerParams(dimension_semantics=("arbitrary",)),
        name="mod",
    )(c, w_mod, b_mod.reshape(1, n))


MXU_COLS = 256
CONV_ROWS = 64


def _inproj_kernel(x_ref, mod_ref, g_ref, w_ref, cw_ref, cb_ref,
                   qkva_ref, qc_ref, kct_ref, vm_ref, om_ref, gt_ref,
                   raw_scr, new_scr, k_scr, carry_scr, *, nblk):
    j = pl.program_id(0)

    @pl.when(j == 0)
    def _init():
        raw_scr[...] = jnp.zeros_like(raw_scr)
        carry_scr[...] = jnp.zeros_like(carry_scr)

    x = x_ref[...]
    h = _rms(x, g_ref[...] * (1.0 + _mod(mod_ref, 1))) + _mod(mod_ref, 0)
    hb = h.astype(BF16)
    tm = x.shape[0]
    nchunk = tm // CHUNK
    within = (j + nblk - 1) % nblk

    anchors = []

    def zero_after(a):
        return jnp.concatenate([_zero_after(a)] * (2 * M_W // LANES), axis=1)

    def project():
        order = list(range(C_QKM, C_VM, MXU_COLS)) + list(range(0, C_QKM, MXU_COLS)) + \
            list(range(C_VM, IN_COLS_PAD, MXU_COLS))
        for c0 in order:
            c1 = min(c0 + MXU_COLS, IN_COLS_PAD)
            res = _dot(hb, w_ref[:, c0:c1])
            anchors.append(res[tm - 1:tm, c1 - c0 - LANES:c1 - c0])
            if c0 < C_KA:
                qkva_ref[:, c0:c1] = (res * (HEAD_DIM_ATT ** -0.5 * LOG2E)).astype(BF16)
            elif c0 < C_QKM:
                ka, va = res[:, :ATT_KV], res[:, ATT_KV:]
                half = HEAD_DIM_ATT
                qkva_ref[:, ATT_Q:ATT_Q + ATT_KV] = ka.astype(BF16)
                qkva_ref[:, ATT_Q + ATT_KV:ATT_Q + 2 * ATT_KV] = pltpu.roll(ka, half, axis=1).astype(BF16)
                qkva_ref[:, ATT_Q + 2 * ATT_KV:ATT_Q + 3 * ATT_KV] = va.astype(BF16)
                qkva_ref[:, ATT_Q + 3 * ATT_KV:ATT_Q + 4 * ATT_KV] = pltpu.roll(va, half, axis=1).astype(BF16)
            elif c0 < C_VM:
                new_scr[:, c0 - C_QKM:c1 - C_QKM] = res
            elif c0 < C_OM:
                vm_ref[:, c0 - C_VM:c1 - C_VM] = res.astype(BF16)
            elif c0 < C_G:
                om_ref[:, c0 - C_OM:c1 - C_OM] = res
            else:
                for g in range(nchunk):
                    blk = res[g * CHUNK:(g + 1) * CHUNK, :].T
                    gt_ref[:, g * CHUNK:(g + 1) * CHUNK] = blk[0:N_GATES, :]
            yield

    def conv():
        R = CONV_ROWS
        row = lax.broadcasted_iota(jnp.int32, (R, 1), 0)
        for p in range(tm // R):
            r0 = p * R
            xg = raw_scr[r0:r0 + R, :]
            if p == 0:
                prev_row = jnp.where(within == 0, 0.0, carry_scr[0:1, :])
            else:
                prev_row = raw_scr[r0 - 1:r0, :]
            if r0 + R == tm:
                next_row = jnp.where(within == nblk - 1, 0.0, new_scr[0:1, :])
            else:
                next_row = raw_scr[r0 + R:r0 + R + 1, :]
            xm1 = jnp.where(row == 0, prev_row, pltpu.roll(xg, 1, axis=0))
            xp1 = jnp.where(row == R - 1, next_row, pltpu.roll(xg, R - 1, axis=0))
            z = zero_after(anchors[-1])
            y =xm1 * (cw_ref[0:1, :] + z) + xg * (cw_ref[1:2, :] + z) + xp1 * (cw_ref[2:3, :] + z) \
                + cb_ref[...]
            y = jax.nn.silu(y)
            qc_ref[r0:r0 + R, :] = (y[:, :M_W] * (HEAD_DIM_M ** -0.5)).astype(BF16)
            k_scr[r0:r0 + R, :] = y[:, M_W:]
            yield
            if (r0 + R) % CHUNK == 0:
                g = r0 // CHUNK
                for hd in range(N_HEADS_M):
                    blk = k_scr[g * CHUNK:(g + 1) * CHUNK, hd * HEAD_DIM_M:(hd + 1) * HEAD_DIM_M]
                    base = (g * N_HEADS_M + hd) * HEAD_DIM_M
                    kct_ref[base:base + HEAD_DIM_M, :] = blk.T.astype(BF16)
                yield

    proj, cv = project(), conv()
    n_proj = -(-IN_COLS_PAD // MXU_COLS)
    n_conv = tm // CONV_ROWS + nchunk
    done = 0
    for k in range(n_proj):
        next(proj)
        while done * n_proj < (k + 1) * n_conv:
            next(cv)
            done += 1
    carry_scr[0:1, :] = raw_scr[tm - 1:tm, :]
    raw_scr[...] = new_scr[...]


def _inproj_call(x, mod, g1, w_in_p, conv_w, conv_b, tm):
    bsz, s, d = x.shape
    nblk = s // tm
    ntiles = bsz * nblk
    outs = (
        jax.ShapeDtypeStruct((bsz, s, QKVA_W), BF16),
        jax.ShapeDtypeStruct((bsz, s, M_W), BF16),
        jax.ShapeDtypeStruct((bsz, s * N_HEADS_M, HEAD_DIM_M), BF16),
        jax.ShapeDtypeStruct((bsz, s, M_W), BF16),
        jax.ShapeDtypeStruct((bsz, s, M_W), F32),
        jax.ShapeDtypeStruct((bsz, N_GATES, s), F32),
    )

    def cur(j):
        t = jnp.minimum(j, ntiles - 1)
        return t // nblk, t % nblk

    def old(j):
        t = jnp.maximum(j - 1, 0)
        return t // nblk, t % nblk

    cur_tile = lambda w: pl.BlockSpec((None, tm, w), lambda j: (*cur(j), 0))
    old_tile = lambda w: pl.BlockSpec((None, tm, w), lambda j: (*old(j), 0))
    const = lambda shp: pl.BlockSpec(shp, lambda j: (0, 0))
    return pl.pallas_call(
        functools.partial(_inproj_kernel, nblk=nblk),
        grid=(ntiles + 1,),
        in_specs=[
            cur_tile(d),
            pl.BlockSpec((None, 1, N_MOD * d), lambda j: (cur(j)[0], 0, 0)),
            const((1, d)),
            pl.BlockSpec((d, IN_COLS_PAD), lambda j: (0, 0), pipeline_mode=pl.Buffered(1)),
            const((3, 2 * M_W)), const((1, 2 * M_W)),
        ],
        out_specs=[cur_tile(QKVA_W), old_tile(M_W),
                   pl.BlockSpec((None, tm * N_HEADS_M, HEAD_DIM_M), lambda j: (*old(j), 0)),
                   cur_tile(M_W), cur_tile(M_W),
                   pl.BlockSpec((None, N_GATES, tm), lambda j: (cur(j)[0], 0, cur(j)[1]))],
        out_shape=outs,
        scratch_shapes=[pltpu.VMEM((tm, 2 * M_W), F32), pltpu.VMEM((tm, 2 * M_W), F32),
                        pltpu.VMEM((tm, M_W), F32), pltpu.VMEM((8, 2 * M_W), F32)],
        compiler_params=pltpu.CompilerParams(
            dimension_semantics=("arbitrary",), vmem_limit_bytes=VMEM_LIMIT),
        name="inproj",
    )(x, mod, g1, w_in_p, conv_w, conv_b)


def _mlstm_reset(first, c_scr, m_scr):
    @pl.when(first)
    def _init():
        c_scr[...] = jnp.zeros_like(c_scr)
        m_scr[...] = jnp.zeros_like(m_scr)


LOG2E = 1.4426950408889634


def _mlstm_stream(q_ref, kt_ref, v_ref, gt_ref, bgr_ref, h_ref, c_scr, m_scr, *, reverse, nchunk):
    L = CHUNK
    ti = lax.broadcasted_iota(jnp.int32, (L, L), 0)
    si = lax.broadcasted_iota(jnp.int32, (L, L), 1)
    causal = (si >= ti) if reverse else (si <= ti)
    tri_t = jnp.where((ti >= si) if reverse else (ti <= si), 1.0, 0.0).astype(BF16)
    lane = lax.broadcasted_iota(jnp.int32, (1, L), 1)
    last = 0 if reverse else L - 1
    i_off, f_off = (2 * N_HEADS_M, 3 * N_HEADS_M) if reverse else (0, N_HEADS_M)
    ones = jnp.ones((L, HEAD_DIM_M), BF16)
    order = list(range(nchunk - 1, -1, -1) if reverse else range(nchunk))
    rows_of = lambda g: slice(g * L, (g + 1) * L)

    def split3(a):
        hi = a.astype(BF16)
        r1 = a - hi.astype(F32)
        mid = r1.astype(BF16)
        return hi, mid, (r1 - mid.astype(F32)).astype(BF16)

    gates = {}
    pad = jnp.zeros((L - N_GATES, L), F32)
    for g in order:
        gr = gt_ref[:, rows_of(g)] + bgr_ref[...]
        parts = _dot(jnp.concatenate(split3(jax.nn.log_sigmoid(gr)), axis=0), tri_t)
        bcr = (parts[0:N_GATES] + parts[N_GATES:2 * N_GATES] + parts[2 * N_GATES:3 * N_GATES]) * LOG2E
        gates[g] = (gr * LOG2E, bcr, jnp.concatenate([-bcr, pad], axis=0).T)
        yield

    def head(g, h):
        rows = rows_of(g)
        cols = slice(h * HEAD_DIM_M, (h + 1) * HEAD_DIM_M)
        gct, bct, nbcum = gates[g]
        r_row = gct[i_off + h:i_off + h + 1, :] - bct[f_off + h:f_off + h + 1, :]
        btot = jnp.sum(jnp.where(lane == last, bct[f_off + h:f_off + h + 1, :], 0.0),
                       axis=1, keepdims=True)
        rmax = jnp.max(r_row, axis=1, keepdims=True)
        a_max = btot + rmax
        m_prev = m_scr[h:h + 1, 0:1]
        m_new = jnp.maximum(btot + m_prev, a_max)
        m_scr[h:h + 1, :] = jnp.broadcast_to(m_new, (1, LANES))
        decay = jnp.exp2(btot + m_prev - m_new)
        w_row = jnp.exp2(r_row - rmax) * jnp.exp2(a_max - m_new)
        kt_rows = slice((g * N_HEADS_M + h) * HEAD_DIM_M, (g * N_HEADS_M + h + 1) * HEAD_DIM_M)
        cm = jnp.max(jnp.where(causal, r_row, -jnp.inf), axis=1, keepdims=True)
        yield
        q = q_ref[rows, cols]
        s = _dot(q, kt_ref[kt_rows, :])
        u = jnp.maximum(cm, m_prev)
        ub = jnp.broadcast_to(u, (L, L))
        dm = jnp.exp2(jnp.where(causal, r_row - ub, -jnp.inf))
        qi = q * jnp.exp2(m_prev - ub).astype(BF16)
        floor = jnp.exp2(jnp.broadcast_to(nbcum[:, f_off + h:f_off + h + 1], (L, L)) - ub)
        yield
        vaug = jnp.concatenate([v_ref[rows, cols], ones], axis=1)
        lhs = jnp.concatenate([(s * dm).astype(BF16), qi], axis=1)
        rhs = jnp.concatenate([vaug, c_scr[h].astype(BF16)], axis=0)
        out = _dot(lhs, rhs)
        upd = _dot(kt_ref[kt_rows, :] * w_row.astype(BF16), vaug)
        yield
        c_scr[h] = decay * c_scr[h] + upd
        hout = out[:, :HEAD_DIM_M] / jnp.maximum(jnp.abs(out[:, HEAD_DIM_M:]), floor)
        h_ref[rows, cols] = hout.astype(h_ref.dtype)
        yield

    n_stage, lag = 4, 2
    heads = {}
    for slot in range(lag * (nchunk - 1) + n_stage):
        for i, g in enumerate(order):
            stage = slot - lag * i
            if 0 <= stage < n_stage:
                for h in range(N_HEADS_M):
                    if stage == 0:
                        heads[g, h] = head(g, h)
                    next(heads[g, h])
                    yield


def _mixer_kernel(qf_ref, kf_ref, vf_ref, gtf_ref, qb_ref, kb_ref, vb_ref, gtb_ref, bgr_ref,
                  sink_ref, qa_ref, kvp_ref, kvn_ref, ga_ref, *rest, nblk, nchunk, ncast):
    w_refs, rest = rest[:ncast], rest[ncast:]
    hf_ref, hb_ref, att_ref = rest[:3]
    wo_refs, (cf_scr, mf_scr, cb_scr, mb_scr, bias_scr) = rest[3:3 + ncast], rest[3 + ncast:]
    b = pl.program_id(0)
    j = pl.program_id(1)

    @pl.when((b == 0) & (j == 0))
    def _init():
        _attn_bias_init(bias_scr)

    _mlstm_reset(j == 0, cf_scr, mf_scr)
    _mlstm_reset(j == 0, cb_scr, mb_scr)
    for w_ref, wo_ref in zip(w_refs, wo_refs):
        wo_ref[...] = w_ref[...].astype(wo_ref.dtype)
    fwd = _mlstm_stream(qf_ref, kf_ref, vf_ref, gtf_ref, bgr_ref, hf_ref, cf_scr, mf_scr,
                        reverse=False, nchunk=nchunk)
    bwd = _mlstm_stream(qb_ref, kb_ref, vb_ref, gtb_ref, bgr_ref, hb_ref, cb_scr, mb_scr,
                        reverse=True, nchunk=nchunk)
    att = _attn_pieces(j == 0, j == nblk - 1, sink_ref, qa_ref, kvp_ref, kvn_ref, ga_ref, att_ref,
                       bias_scr, [], nsub=nchunk)
    _alternate(fwd, bwd)
    for _ in att:
        pass


def _mixer_call(qc, kct, vm, gates_t, bg_rows, sink, qkva, g_attn, weights, layer, tm):
    bsz, s, _ = qc.shape
    nblk = s // tm
    nsteps = bsz * nblk
    nsub = tm // BLOCK
    nb = s // BLOCK
    kvw = QKVA_W - ATT_Q
    attn_specs = [
        pl.BlockSpec(memory_space=pltpu.SMEM),
        pl.BlockSpec((None, tm, QKVA_W), lambda b, j: (b, j, 0)),
        pl.BlockSpec((None, BLOCK, kvw), lambda b, j: (b, jnp.maximum(j * nsub - 1, 0), 1)),
        pl.BlockSpec((None, BLOCK, kvw), lambda b, j: (b, jnp.minimum((j + 1) * nsub, nb - 1), 1)),
        pl.BlockSpec((1, ATT_Q), lambda b, j: (0, 0)),
    ]
    w_specs, w_shapes = [], []
    for w in weights:
        _, k, n = w.shape
        rows = k // nsteps
        assert rows * nsteps == k and rows % 16 == 0
        w_specs.append(pl.BlockSpec((None, rows, n), lambda b, j: (layer, b * nblk + j, 0)))
        w_shapes.append(jax.ShapeDtypeStruct((k, n), BF16))
    wo_specs = [pl.BlockSpec((sp.block_shape[1], sp.block_shape[2]), lambda b, j: (b * nblk + j, 0))
                for sp in w_specs]

    def specs(pos):
        tile = lambda w: pl.BlockSpec((None, tm, w), lambda b, j: (b, pos(j), 0))
        kt_spec = pl.BlockSpec((None, tm * N_HEADS_M, HEAD_DIM_M), lambda b, j: (b, pos(j), 0))
        gt_spec = pl.BlockSpec((None, N_GATES, tm), lambda b, j: (b, 0, pos(j)))
        return [tile(M_W), kt_spec, tile(M_W), gt_spec], tile(M_W)

    in_f, out_f = specs(lambda j: j)
    in_b, out_b = specs(lambda j: nblk - 1 - j)
    state = [pltpu.VMEM((N_HEADS_M, HEAD_DIM_M, 2 * HEAD_DIM_M), F32), pltpu.VMEM((8, LANES), F32)]
    outs = pl.pallas_call(
        functools.partial(_mixer_kernel, nblk=nblk, nchunk=tm // CHUNK, ncast=len(weights)),
        grid=(bsz, nblk),
        in_specs=in_f + in_b + [pl.BlockSpec((N_GATES, LANES), lambda b, j: (0, 0))] + attn_specs
        + w_specs,
        out_specs=[out_f, out_b, pl.BlockSpec((None, tm, ATT_Q), lambda b, j: (b, j, 0))] + wo_specs,
        out_shape=[jax.ShapeDtypeStruct((bsz, s, M_W), BF16)] * 2
        + [jax.ShapeDtypeStruct((bsz, s, ATT_Q), BF16)] + w_shapes,
        scratch_shapes=state + state + [pltpu.VMEM((3 * N_HEADS_ATT, BLOCK, 3 * BLOCK), F32)],
        compiler_params=pltpu.CompilerParams(
            dimension_semantics=("arbitrary", "arbitrary"), vmem_limit_bytes=VMEM_LIMIT),
        name="mixer",
    )(qc, kct, vm, gates_t, qc, kct, vm, gates_t, bg_rows, sink, qkva, qkva, qkva, g_attn, *weights)
    return outs[0], outs[1], outs[2], outs[3:]


def _zero_after(a):
    bits = pltpu.bitcast(a, jnp.uint32)
    z = lax.shift_right_logical(lax.shift_right_logical(bits, jnp.uint32(16)), jnp.uint32(16))
    return pltpu.bitcast(z, F32)


def _attn_bias_init(bias_scr):
    nk = 3 * BLOCK
    row = lax.broadcasted_iota(jnp.int32, (BLOCK, nk), 0)
    col = lax.broadcasted_iota(jnp.int32, (BLOCK, nk), 1)
    dist = jnp.abs(col - BLOCK - row)
    distf = dist.astype(F32)
    for var in range(3):
        ok = dist <= WINDOW
        if var == 1:
            ok = ok & (col >= BLOCK)
        elif var == 2:
            ok = ok & (col < 2 * BLOCK)
        for h in range(N_HEADS_ATT):
            slope = 2.0 ** (-8.0 * (h + 1.0) / N_HEADS_ATT)
            bias_scr[var * N_HEADS_ATT + h] = jnp.where(ok, (-slope * LOG2E) * distf, -jnp.inf)


def _attn_pieces(first, last, sink_ref, q_ref, kvp_ref, kvn_ref, g_ref, o_ref, bias_scr, anchors,
                 *, nsub):
    nk = 3 * BLOCK

    lane_k = lax.broadcasted_iota(jnp.int32, (nk, LANES), 1)
    ones_a = jnp.where(lane_k < HEAD_DIM_ATT, 1.0, 0.0).astype(BF16)
    ones_b = jnp.where(lane_k < HEAD_DIM_ATT, 0.0, 1.0).astype(BF16)
    lo_half_q = lax.broadcasted_iota(jnp.int32, (BLOCK, LANES), 1) < HEAD_DIM_ATT

    def kv_block(idx):
        if idx < 0:
            return kvp_ref[...]
        if idx >= nsub:
            return kvn_ref[...]
        return q_ref[idx * BLOCK:(idx + 1) * BLOCK, ATT_Q:QKVA_W]

    for n in range(nsub):
        rows = slice(n * BLOCK, (n + 1) * BLOCK)
        kv = jnp.concatenate([kv_block(n - 1), kv_block(n), kv_block(n + 1)], axis=0)
        if n == 0:
            var = jnp.where(first, 1, 0)
        elif n == nsub - 1:
            var = jnp.where(last, 2, 0)
        else:
            var = 0
        pieces = []
        for kvh in range(N_KV_HEADS):
            k_st, k_sw = kv[:, 0:LANES], kv[:, LANES:2 * LANES]
            v_st, v_sw = kv[:, 2 * LANES:3 * LANES], kv[:, 3 * LANES:4 * LANES]
            if kvh == 0:
                k_lo, k_hi, v_lo, v_hi = k_st, k_sw, v_st, v_sw
            else:
                k_lo, k_hi, v_lo, v_hi = k_sw, k_st, v_sw, v_st
            kk = jnp.concatenate([k_lo * ones_a, k_hi * ones_b], axis=0)
            vv = jnp.concatenate([
                jnp.concatenate([v_lo * ones_a, ones_a], axis=1),
                jnp.concatenate([v_hi * ones_b, ones_b], axis=1)], axis=0)
            for pair in range(GROUP_SIZE // 2):
                h0 = kvh * GROUP_SIZE + 2 * pair
                qp = q_ref[rows, h0 * HEAD_DIM_ATT:(h0 + 2) * HEAD_DIM_ATT]
                s2 = _dot_nt(qp, kk)
                tie = _zero_after(anchors[-1])[:, 0:1] if anchors else 0.0
                ps, es = [], []
                for t in range(2):
                    logits = s2[:, t * nk:(t + 1) * nk] + bias_scr[var * N_HEADS_ATT + h0 + t]
                    sink = sink_ref[h0 + t] * LOG2E + tie
                    mx = jnp.maximum(jnp.max(logits, axis=-1, keepdims=True), sink)
                    ps.append(jnp.exp2(logits - mx).astype(BF16))
                    es.append(jnp.exp2(sink - mx))
                res = _dot(jnp.concatenate(ps, axis=1), vv)
                den = res[:, LANES:] + jnp.where(lo_half_q, es[0], es[1])
                pieces.append(res[:, :LANES] / den)
                yield
        att = jnp.concatenate(pieces, axis=1)
        o_ref[rows, :] = _rms(att, g_ref[...]).astype(o_ref.dtype)
        yield


def _outffn_pieces(x_ref, att_ref, hf_ref, hb_ref, om_ref, mod_ref, gm_ref, wo_ref, g2_ref,
                   w1_ref, w2_ref, gf_ref, o_ref, hid_scr, anchors, *, final):
    hs = hf_ref[...].astype(F32) + hb_ref[...].astype(F32)
    parts = []
    for h in range(N_HEADS_M):
        cols = slice(h * HEAD_DIM_M, (h + 1) * HEAD_DIM_M)
        parts.append(_rms(hs[:, cols], gm_ref[:, cols]))
    hm = jax.nn.sigmoid(om_ref[...]) * jnp.concatenate(parts, axis=1)
    mixin = jnp.concatenate([att_ref[...], hm.astype(BF16)], axis=1)
    mix = _dot(mixin, wo_ref[...])
    anchors.append(mix[0:1, 0:LANES])
    x1 = x_ref[...] + _mod(mod_ref, 2) * mix
    hff = (_rms(x1, g2_ref[...] * (1.0 + _mod(mod_ref, 4))) + _mod(mod_ref, 3)).astype(BF16)
    yield
    for c in range(N_FF_CHUNKS):
        gate = _dot(hff, w1_ref[:, FF_CHUNK * c:FF_CHUNK * (c + 1)])
        up = _dot(hff, w1_ref[:, D_FF + FF_CHUNK * c:D_FF + FF_CHUNK * (c + 1)])
        anchors.append(gate[0:1, 0:LANES])
        hid_scr[:, FF_CHUNK * c:FF_CHUNK * (c + 1)] = (jax.nn.silu(gate) * up).astype(BF16)
        yield
    ff = _dot(hid_scr[...], w2_ref[...])
    anchors.append(ff[0:1, 0:LANES])
    x2 = x1 + _mod(mod_ref, 5) * ff
    if final:
        x2 = _rms(x2, gf_ref[...])
    o_ref[...] = x2
    yield


def _outffn_kernel(*refs, final):
    for _ in _outffn_pieces(*refs, [], final=final):
        pass


def _outffn_call(x, att, hf, hb, om, mod, g_m, w_out, g2, w1, w2, g_final, tm, final):
    bsz, s, d = x.shape
    tile = lambda w: pl.BlockSpec((None, tm, w), lambda b, i: (b, i, 0))
    const = lambda shp: pl.BlockSpec(shp, lambda b, i: (0, 0))
    weight = lambda shp: pl.BlockSpec(shp, lambda b, i: (0, 0), pipeline_mode=pl.Buffered(1))
    return pl.pallas_call(
        functools.partial(_outffn_kernel, final=final),
        grid=(bsz, s // tm),
        in_specs=[
            tile(d), tile(ATT_Q), tile(M_W), tile(M_W), tile(M_W),
            pl.BlockSpec((None, 1, N_MOD * d), lambda b, i: (b, 0, 0)),
            const((1, M_W)), weight((ATT_Q + M_W, d)), const((1, d)),
            weight((d, 2 * D_FF)), weight((D_FF, d)), const((1, d)),
        ],
        out_specs=tile(d),
        out_shape=jax.ShapeDtypeStruct((bsz, s, d), F32),
        scratch_shapes=[pltpu.VMEM((tm, D_FF), BF16)],
        compiler_params=pltpu.CompilerParams(
            dimension_semantics=("arbitrary", "arbitrary"), vmem_limit_bytes=VMEM_LIMIT),
        name="outffn",
    )(x, att, hf, hb, om, mod, g_m, w_out, g2, w1, w2, g_final)


def _layer(l, x, c, w_mod, b_mod, g_norm1, w_in, conv_w, conv_b, b_gates, sink,
           g_attn_out, g_mlstm_out, w_out, g_norm2, w_ffn_in, w_ffn_out, g_final, final):
    d = x.shape[-1]
    mod = _mod_call(c, w_mod[l], b_mod[l])

    w_in_p = _cast_t_call(w_in, l, IN_COLS_PAD)
    qkva, qc, kct, vm, om, gates_t = _inproj_call(
        x, mod, g_norm1[l].reshape(1, d), w_in_p, conv_w[l], conv_b[l].reshape(1, 2 * M_W), tm=512)

    bg_rows = jnp.broadcast_to(b_gates[l][:, None], (N_GATES, LANES))
    hf, hb, att, (w_out_b, w_ffn_in_b, w_ffn_out_b) = _mixer_call(
        qc, kct, vm, gates_t, bg_rows, sink[l], qkva, g_attn_out[l].reshape(1, ATT_Q),
        (w_out, w_ffn_in, w_ffn_out), l, tm=1024)

    return _outffn_call(x, att, hf, hb, om, mod, g_mlstm_out[l].reshape(1, M_W), w_out_b,
                        g_norm2[l].reshape(1, d), w_ffn_in_b, w_ffn_out_b,
                        g_final.reshape(1, d), tm=512, final=final)


def kernel(x, c, w_mod, b_mod, g_norm1, w_in, conv_w, conv_b, b_gates, sink, g_attn_out,
           g_mlstm_out, w_out, g_norm2, w_ffn_in, w_ffn_out, g_final):
    depth = w_mod.shape[0]
    for l in range(depth):
        x = _layer(l, x, c, w_mod, b_mod, g_norm1, w_in, conv_w, conv_b, b_gates, sink, g_attn_out,
                   g_mlstm_out, w_out, g_norm2, w_ffn_in, w_ffn_out, g_final, final=(l == depth - 1))
    return x
```

```python
import functools

import jax
import jax.numpy as jnp
from jax import lax
from jax.experimental import pallas as pl
from jax.experimental.pallas import tpu as pltpu

F32 = jnp.float32
BF16 = jnp.bfloat16

D_MODEL = 1024
EPS = 1e-6
N_HEADS_ATT = 8
N_KV_HEADS = 2
HEAD_DIM_ATT = 64
GROUP_SIZE = N_HEADS_ATT // N_KV_HEADS
WINDOW = 128
BLOCK = 128
N_HEADS_M = 4
HEAD_DIM_M = 128
CHUNK = 128
ATT_Q = N_HEADS_ATT * HEAD_DIM_ATT
ATT_KV = N_KV_HEADS * HEAD_DIM_ATT
M_W = N_HEADS_M * HEAD_DIM_M
N_GATES = 4 * N_HEADS_M
D_FF = 2816
N_MOD = 6

LANES = 128
GATE_PAD = LANES
FF_CHUNK = 256
N_FF_CHUNKS = D_FF // FF_CHUNK
VMEM_LIMIT = 56 * 1024 * 1024

C_QA = 0
C_KA = ATT_Q
C_VA = ATT_Q + ATT_KV
C_QKM = ATT_Q + 2 * ATT_KV
C_VM = C_QKM + 2 * M_W
C_OM = C_VM + M_W
C_G = C_OM + M_W
IN_COLS_PAD = C_G + GATE_PAD
QKVA_W = ATT_Q + 4 * ATT_KV


def _dot(a, b):
    return jnp.dot(a, b, preferred_element_type=F32)


def _dot_nt(a, b):
    return lax.dot_general(a, b, (((1,), (1,)), ((), ())), preferred_element_type=F32)


def _rms(x, g):
    return x * lax.rsqrt(jnp.mean(x * x, axis=-1, keepdims=True) + EPS) * g


def _mod(mod_ref, k):
    return mod_ref[:, k * D_MODEL:(k + 1) * D_MODEL]


def _alternate(*streams):
    live = list(streams)
    while live:
        for s in list(live):
            if next(s, StopIteration) is StopIteration:
                live.remove(s)


def _cast_t_kernel(wt_ref, o_ref):
    n = wt_ref.shape[0]
    full = n // LANES * LANES
    for r in range(0, full, LANES):
        o_ref[:, r:r + LANES] = wt_ref[r:r + LANES, :].T.astype(o_ref.dtype)
    if o_ref.shape[1] > full:
        tail = jnp.concatenate(
            [wt_ref[full:n, :], jnp.zeros((full + LANES - n, wt_ref.shape[1]), wt_ref.dtype)], axis=0)
        o_ref[:, full:full + LANES] = tail.T.astype(o_ref.dtype)


def _cast_t_call(w, layer, n_out, bk=256):
    _, k, n = w.shape
    assert n_out - n < LANES and n_out % LANES == 0
    return pl.pallas_call(
        _cast_t_kernel,
        grid=(k // bk,),
        in_specs=[pl.BlockSpec((None, n, bk), lambda i: (layer, 0, i))],
        out_specs=pl.BlockSpec((bk, n_out), lambda i: (i, 0)),
        out_shape=jax.ShapeDtypeStruct((k, n_out), BF16),
        compiler_params=pltpu.CompilerParams(
            dimension_semantics=("arbitrary",), vmem_limit_bytes=VMEM_LIMIT),
        name="cast",
    )(jnp.swapaxes(w, 1, 2))


SUBLANES = 8


def _mod_kernel(c_ref, w_ref, b_ref, o_ref):
    bsz, d = c_ref.shape
    c = jnp.concatenate([c_ref[...], jnp.zeros((SUBLANES - bsz, d), F32)], axis=0)
    res = _dot(jax.nn.silu(c).astype(BF16), w_ref[...].astype(BF16)) + b_ref[...]
    for b in range(bsz):
        o_ref[b] = res[b:b + 1, :]


def _mod_call(c, w_mod, b_mod):
    bsz, d = c.shape
    assert bsz <= SUBLANES
    n = w_mod.shape[1]
    bn = 1024
    return pl.pallas_call(
        _mod_kernel,
        grid=(n // bn,),
        in_specs=[
            pl.BlockSpec((bsz, d), lambda i: (0, 0)),
            pl.BlockSpec((d, bn), lambda i: (0, i)),
            pl.BlockSpec((1, bn), lambda i: (0, i)),
        ],
        out_specs=pl.BlockSpec((bsz, 1, bn), lambda i: (0, 0, i)),
        out_shape=jax.ShapeDtypeStruct((bsz, 1, n), F32),
        compiler_params=pltpu.CompilerParams(dimension_semantics=("arbitrary",)),
        name="mod",
    )(c, w_mod, b_mod.reshape(1, n))


MXU_COLS = 256
CONV_ROWS = 64


def _inproj_kernel(x_ref, mod_ref, g_ref, w_ref, cw_ref, cb_ref,
                   qkva_ref, qc_ref, kct_ref, vm_ref, om_ref, gt_ref,
                   raw_scr, new_scr, k_scr, carry_scr, *, nblk):
    j = pl.program_id(0)

    @pl.when(j == 0)
    def _init():
        raw_scr[...] = jnp.zeros_like(raw_scr)
        carry_scr[...] = jnp.zeros_like(carry_scr)

    x = x_ref[...]
    h = _rms(x, g_ref[...] * (1.0 + _mod(mod_ref, 1))) + _mod(mod_ref, 0)
    hb = h.astype(BF16)
    tm = x.shape[0]
    nchunk = tm // CHUNK
    within = (j + nblk - 1) % nblk

    anchors = []

    def zero_after(a):
        return jnp.concatenate([_zero_after(a)] * (2 * M_W // LANES), axis=1)

    def project():
        order = list(range(C_QKM, C_VM, MXU_COLS)) + list(range(0, C_QKM, MXU_COLS)) + \
            list(range(C_VM, IN_COLS_PAD, MXU_COLS))
        for c0 in order:
            c1 = min(c0 + MXU_COLS, IN_COLS_PAD)
            res = _dot(hb, w_ref[:, c0:c1])
            anchors.append(res[tm - 1:tm, c1 - c0 - LANES:c1 - c0])
            if c0 < C_KA:
                qkva_ref[:, c0:c1] = (res * (HEAD_DIM_ATT ** -0.5 * LOG2E)).astype(BF16)
            elif c0 < C_QKM:
                ka, va = res[:, :ATT_KV], res[:, ATT_KV:]
                half = HEAD_DIM_ATT
                qkva_ref[:, ATT_Q:ATT_Q + ATT_KV] = ka.astype(BF16)
                qkva_ref[:, ATT_Q + ATT_KV:ATT_Q + 2 * ATT_KV] = pltpu.roll(ka, half, axis=1).astype(BF16)
                qkva_ref[:, ATT_Q + 2 * ATT_KV:ATT_Q + 3 * ATT_KV] = va.astype(BF16)
                qkva_ref[:, ATT_Q + 3 * ATT_KV:ATT_Q + 4 * ATT_KV] = pltpu.roll(va, half, axis=1).astype(BF16)
            elif c0 < C_VM:
                new_scr[:, c0 - C_QKM:c1 - C_QKM] = res
            elif c0 < C_OM:
                vm_ref[:, c0 - C_VM:c1 - C_VM] = res.astype(BF16)
            elif c0 < C_G:
                om_ref[:, c0 - C_OM:c1 - C_OM] = res
            else:
                for g in range(nchunk):
                    blk = res[g * CHUNK:(g + 1) * CHUNK, :].T
                    gt_ref[:, g * CHUNK:(g + 1) * CHUNK] = blk[0:N_GATES, :]
            yield

    def conv():
        R = CONV_ROWS
        row = lax.broadcasted_iota(jnp.int32, (R, 1), 0)
        for p in range(tm // R):
            r0 = p * R
            xg = raw_scr[r0:r0 + R, :]
            if p == 0:
                prev_row = jnp.where(within == 0, 0.0, carry_scr[0:1, :])
            else:
                prev_row = raw_scr[r0 - 1:r0, :]
            if r0 + R == tm:
                next_row = jnp.where(within == nblk - 1, 0.0, new_scr[0:1, :])
            else:
                next_row = raw_scr[r0 + R:r0 + R + 1, :]
            xm1 = jnp.where(row == 0, prev_row, pltpu.roll(xg, 1, axis=0))
            xp1 = jnp.where(row == R - 1, next_row, pltpu.roll(xg, R - 1, axis=0))
            z = zero_after(anchors[-1])
            y =xm1 * (cw_ref[0:1, :] + z) + xg * (cw_ref[1:2, :] + z) + xp1 * (cw_ref[2:3, :] + z) \
                + cb_ref[...]
            y = jax.nn.silu(y)
            qc_ref[r0:r0 + R, :] = (y[:, :M_W] * (HEAD_DIM_M ** -0.5)).astype(BF16)
            k_scr[r0:r0 + R, :] = y[:, M_W:]
            yield
            if (r0 + R) % CHUNK == 0:
                g = r0 // CHUNK
                for hd in range(N_HEADS_M):
                    blk = k_scr[g * CHUNK:(g + 1) * CHUNK, hd * HEAD_DIM_M:(hd + 1) * HEAD_DIM_M]
                    base = (g * N_HEADS_M + hd) * HEAD_DIM_M
                    kct_ref[base:base + HEAD_DIM_M, :] = blk.T.astype(BF16)
                yield

    proj, cv = project(), conv()
    n_proj = -(-IN_COLS_PAD // MXU_COLS)
    n_conv = tm // CONV_ROWS + nchunk
    done = 0
    for k in range(n_proj):
        next(proj)
        while done * n_proj < (k + 1) * n_conv:
            next(cv)
            done += 1
    carry_scr[0:1, :] = raw_scr[tm - 1:tm, :]
    raw_scr[...] = new_scr[...]


def _inproj_call(x, mod, g1, w_in_p, conv_w, conv_b, tm):
    bsz, s, d = x.shape
    nblk = s // tm
    ntiles = bsz * nblk
    outs = (
        jax.ShapeDtypeStruct((bsz, s, QKVA_W), BF16),
        jax.ShapeDtypeStruct((bsz, s, M_W), BF16),
        jax.ShapeDtypeStruct((bsz, s * N_HEADS_M, HEAD_DIM_M), BF16),
        jax.ShapeDtypeStruct((bsz, s, M_W), BF16),
        jax.ShapeDtypeStruct((bsz, s, M_W), F32),
        jax.ShapeDtypeStruct((bsz, N_GATES, s), F32),
    )

    def cur(j):
        t = jnp.minimum(j, ntiles - 1)
        return t // nblk, t % nblk

    def old(j):
        t = jnp.maximum(j - 1, 0)
        return t // nblk, t % nblk

    cur_tile = lambda w: pl.BlockSpec((None, tm, w), lambda j: (*cur(j), 0))
    old_tile = lambda w: pl.BlockSpec((None, tm, w), lambda j: (*old(j), 0))
    const = lambda shp: pl.BlockSpec(shp, lambda j: (0, 0))
    return pl.pallas_call(
        functools.partial(_inproj_kernel, nblk=nblk),
        grid=(ntiles + 1,),
        in_specs=[
            cur_tile(d),
            pl.BlockSpec((None, 1, N_MOD * d), lambda j: (cur(j)[0], 0, 0)),
            const((1, d)),
            pl.BlockSpec((d, IN_COLS_PAD), lambda j: (0, 0), pipeline_mode=pl.Buffered(1)),
            const((3, 2 * M_W)), const((1, 2 * M_W)),
        ],
        out_specs=[cur_tile(QKVA_W), old_tile(M_W),
                   pl.BlockSpec((None, tm * N_HEADS_M, HEAD_DIM_M), lambda j: (*old(j), 0)),
                   cur_tile(M_W), cur_tile(M_W),
                   pl.BlockSpec((None, N_GATES, tm), lambda j: (cur(j)[0], 0, cur(j)[1]))],
        out_shape=outs,
        scratch_shapes=[pltpu.VMEM((tm, 2 * M_W), F32), pltpu.VMEM((tm, 2 * M_W), F32),
                        pltpu.VMEM((tm, M_W), F32), pltpu.VMEM((8, 2 * M_W), F32)],
        compiler_params=pltpu.CompilerParams(
            dimension_semantics=("arbitrary",), vmem_limit_bytes=VMEM_LIMIT),
        name="inproj",
    )(x, mod, g1, w_in_p, conv_w, conv_b)


def _mlstm_reset(first, c_scr, m_scr):
    @pl.when(first)
    def _init():
        c_scr[...] = jnp.zeros_like(c_scr)
        m_scr[...] = jnp.zeros_like(m_scr)


LOG2E = 1.4426950408889634


def _mlstm_stream(q_ref, kt_ref, v_ref, gt_ref, bgr_ref, h_ref, c_scr, m_scr, *, reverse, nchunk):
    L = CHUNK
    ti = lax.broadcasted_iota(jnp.int32, (L, L), 0)
    si = lax.broadcasted_iota(jnp.int32, (L, L), 1)
    causal = (si >= ti) if reverse else (si <= ti)
    tri_t = jnp.where((ti >= si) if reverse else (ti <= si), 1.0, 0.0).astype(BF16)
    lane = lax.broadcasted_iota(jnp.int32, (1, L), 1)
    last = 0 if reverse else L - 1
    i_off, f_off = (2 * N_HEADS_M, 3 * N_HEADS_M) if reverse else (0, N_HEADS_M)
    ones = jnp.ones((L, HEAD_DIM_M), BF16)
    order = list(range(nchunk - 1, -1, -1) if reverse else range(nchunk))
    rows_of = lambda g: slice(g * L, (g + 1) * L)

    def split3(a):
        hi = a.astype(BF16)
        r1 = a - hi.astype(F32)
        mid = r1.astype(BF16)
        return hi, mid, (r1 - mid.astype(F32)).astype(BF16)

    gates = {}
    pad = jnp.zeros((L - N_GATES, L), F32)
    for g in order:
        gr = gt_ref[:, rows_of(g)] + bgr_ref[...]
        parts = _dot(jnp.concatenate(split3(jax.nn.log_sigmoid(gr)), axis=0), tri_t)
        bcr = (parts[0:N_GATES] + parts[N_GATES:2 * N_GATES] + parts[2 * N_GATES:3 * N_GATES]) * LOG2E
        gates[g] = (gr * LOG2E, bcr, jnp.concatenate([-bcr, pad], axis=0).T)
        yield

    def head(g, h):
        rows = rows_of(g)
        cols = slice(h * HEAD_DIM_M, (h + 1) * HEAD_DIM_M)
        gct, bct, nbcum = gates[g]
        r_row = gct[i_off + h:i_off + h + 1, :] - bct[f_off + h:f_off + h + 1, :]
        btot = jnp.sum(jnp.where(lane == last, bct[f_off + h:f_off + h + 1, :], 0.0),
                       axis=1, keepdims=True)
        rmax = jnp.max(r_row, axis=1, keepdims=True)
        a_max = btot + rmax
        m_prev = m_scr[h:h + 1, 0:1]
        m_new = jnp.maximum(btot + m_prev, a_max)
        m_scr[h:h + 1, :] = jnp.broadcast_to(m_new, (1, LANES))
        decay = jnp.exp2(btot + m_prev - m_new)
        w_row = jnp.exp2(r_row - rmax) * jnp.exp2(a_max - m_new)
        kt_rows = slice((g * N_HEADS_M + h) * HEAD_DIM_M, (g * N_HEADS_M + h + 1) * HEAD_DIM_M)
        cm = jnp.max(jnp.where(causal, r_row, -jnp.inf), axis=1, keepdims=True)
        yield
        q = q_ref[rows, cols]
        s = _dot(q, kt_ref[kt_rows, :])
        u = jnp.maximum(cm, m_prev)
        ub = jnp.broadcast_to(u, (L, L))
        dm = jnp.exp2(jnp.where(causal, r_row - ub, -jnp.inf))
        qi = q * jnp.exp2(m_prev - ub).astype(BF16)
        floor = jnp.exp2(jnp.broadcast_to(nbcum[:, f_off + h:f_off + h + 1], (L, L)) - ub)
        yield
        vaug = jnp.concatenate([v_ref[rows, cols], ones], axis=1)
        lhs = jnp.concatenate([(s * dm).astype(BF16), qi], axis=1)
        rhs = jnp.concatenate([vaug, c_scr[h].astype(BF16)], axis=0)
        out = _dot(lhs, rhs)
        upd = _dot(kt_ref[kt_rows, :] * w_row.astype(BF16), vaug)
        yield
        c_scr[h] = decay * c_scr[h] + upd
        h_ref[rows, cols] = out[:, :HEAD_DIM_M] / jnp.maximum(jnp.abs(out[:, HEAD_DIM_M:]), floor)
        yield

    n_stage, lag = 4, 2
    heads = {}
    for slot in range(lag * (nchunk - 1) + n_stage):
        for i, g in enumerate(order):
            stage = slot - lag * i
            if 0 <= stage < n_stage:
                for h in range(N_HEADS_M):
                    if stage == 0:
                        heads[g, h] = head(g, h)
                    next(heads[g, h])
                    yield


def _mixer_kernel(qf_ref, kf_ref, vf_ref, gtf_ref, qb_ref, kb_ref, vb_ref, gtb_ref, bgr_ref,
                  sink_ref, qa_ref, kvp_ref, kvn_ref, ga_ref, *rest, nblk, nchunk, ncast):
    w_refs, rest = rest[:ncast], rest[ncast:]
    hf_ref, hb_ref, att_ref = rest[:3]
    wo_refs, (cf_scr, mf_scr, cb_scr, mb_scr, bias_scr) = rest[3:3 + ncast], rest[3 + ncast:]
    b = pl.program_id(0)
    j = pl.program_id(1)

    @pl.when((b == 0) & (j == 0))
    def _init():
        _attn_bias_init(bias_scr)

    _mlstm_reset(j == 0, cf_scr, mf_scr)
    _mlstm_reset(j == 0, cb_scr, mb_scr)
    for w_ref, wo_ref in zip(w_refs, wo_refs):
        wo_ref[...] = w_ref[...].astype(wo_ref.dtype)
    fwd = _mlstm_stream(qf_ref, kf_ref, vf_ref, gtf_ref, bgr_ref, hf_ref, cf_scr, mf_scr,
                        reverse=False, nchunk=nchunk)
    bwd = _mlstm_stream(qb_ref, kb_ref, vb_ref, gtb_ref, bgr_ref, hb_ref, cb_scr, mb_scr,
                        reverse=True, nchunk=nchunk)
    att = _attn_pieces(j == 0, j == nblk - 1, sink_ref, qa_ref, kvp_ref, kvn_ref, ga_ref, att_ref,
                       bias_scr, nsub=nchunk)
    _alternate(fwd, bwd)
    for _ in att:
        pass


def _mixer_call(qc, kct, vm, gates_t, bg_rows, sink, qkva, g_attn, weights, layer, tm):
    bsz, s, _ = qc.shape
    nblk = s // tm
    nsteps = bsz * nblk
    nsub = tm // BLOCK
    nb = s // BLOCK
    kvw = QKVA_W - ATT_Q
    attn_specs = [
        pl.BlockSpec(memory_space=pltpu.SMEM),
        pl.BlockSpec((None, tm, QKVA_W), lambda b, j: (b, j, 0)),
        pl.BlockSpec((None, BLOCK, kvw), lambda b, j: (b, jnp.maximum(j * nsub - 1, 0), 1)),
        pl.BlockSpec((None, BLOCK, kvw), lambda b, j: (b, jnp.minimum((j + 1) * nsub, nb - 1), 1)),
        pl.BlockSpec((1, ATT_Q), lambda b, j: (0, 0)),
    ]
    w_specs, w_shapes = [], []
    for w in weights:
        _, k, n = w.shape
        rows = k // nsteps
        assert rows * nsteps == k and rows % 16 == 0
        w_specs.append(pl.BlockSpec((None, rows, n), lambda b, j: (layer, b * nblk + j, 0)))
        w_shapes.append(jax.ShapeDtypeStruct((k, n), BF16))
    wo_specs = [pl.BlockSpec((sp.block_shape[1], sp.block_shape[2]), lambda b, j: (b * nblk + j, 0))
                for sp in w_specs]

    def specs(pos):
        tile = lambda w: pl.BlockSpec((None, tm, w), lambda b, j: (b, pos(j), 0))
        kt_spec = pl.BlockSpec((None, tm * N_HEADS_M, HEAD_DIM_M), lambda b, j: (b, pos(j), 0))
        gt_spec = pl.BlockSpec((None, N_GATES, tm), lambda b, j: (b, 0, pos(j)))
        return [tile(M_W), kt_spec, tile(M_W), gt_spec], tile(M_W)

    in_f, out_f = specs(lambda j: j)
    in_b, out_b = specs(lambda j: nblk - 1 - j)
    state = [pltpu.VMEM((N_HEADS_M, HEAD_DIM_M, 2 * HEAD_DIM_M), F32), pltpu.VMEM((8, LANES), F32)]
    outs = pl.pallas_call(
        functools.partial(_mixer_kernel, nblk=nblk, nchunk=tm // CHUNK, ncast=len(weights)),
        grid=(bsz, nblk),
        in_specs=in_f + in_b + [pl.BlockSpec((N_GATES, LANES), lambda b, j: (0, 0))] + attn_specs
        + w_specs,
        out_specs=[out_f, out_b, pl.BlockSpec((None, tm, ATT_Q), lambda b, j: (b, j, 0))] + wo_specs,
        out_shape=[jax.ShapeDtypeStruct((bsz, s, M_W), F32)] * 2
        + [jax.ShapeDtypeStruct((bsz, s, ATT_Q), BF16)] + w_shapes,
        scratch_shapes=state + state + [pltpu.VMEM((3 * N_HEADS_ATT, BLOCK, 3 * BLOCK), F32)],
        compiler_params=pltpu.CompilerParams(
            dimension_semantics=("arbitrary", "arbitrary"), vmem_limit_bytes=VMEM_LIMIT),
        name="mixer",
    )(qc, kct, vm, gates_t, qc, kct, vm, gates_t, bg_rows, sink, qkva, qkva, qkva, g_attn, *weights)
    return outs[0], outs[1], outs[2], outs[3:]


def _zero_after(a):
    bits = pltpu.bitcast(a, jnp.uint32)
    z = lax.shift_right_logical(lax.shift_right_logical(bits, jnp.uint32(16)), jnp.uint32(16))
    return pltpu.bitcast(z, F32)


def _attn_bias_init(bias_scr):
    nk = 3 * BLOCK
    row = lax.broadcasted_iota(jnp.int32, (BLOCK, nk), 0)
    col = lax.broadcasted_iota(jnp.int32, (BLOCK, nk), 1)
    dist = jnp.abs(col - BLOCK - row)
    distf = dist.astype(F32)
    for var in range(3):
        ok = dist <= WINDOW
        if var == 1:
            ok = ok & (col >= BLOCK)
        elif var == 2:
            ok = ok & (col < 2 * BLOCK)
        for h in range(N_HEADS_ATT):
            slope = 2.0 ** (-8.0 * (h + 1.0) / N_HEADS_ATT)
            bias_scr[var * N_HEADS_ATT + h] = jnp.where(ok, (-slope * LOG2E) * distf, -jnp.inf)


def _attn_pieces(first, last, sink_ref, q_ref, kvp_ref, kvn_ref, g_ref, o_ref, bias_scr, *, nsub):
    nk = 3 * BLOCK

    lane_k = lax.broadcasted_iota(jnp.int32, (nk, LANES), 1)
    ones_a = jnp.where(lane_k < HEAD_DIM_ATT, 1.0, 0.0).astype(BF16)
    ones_b = jnp.where(lane_k < HEAD_DIM_ATT, 0.0, 1.0).astype(BF16)
    lo_half_q = lax.broadcasted_iota(jnp.int32, (BLOCK, LANES), 1) < HEAD_DIM_ATT

    def kv_block(idx):
        if idx < 0:
            return kvp_ref[...]
        if idx >= nsub:
            return kvn_ref[...]
        return q_ref[idx * BLOCK:(idx + 1) * BLOCK, ATT_Q:QKVA_W]

    for n in range(nsub):
        rows = slice(n * BLOCK, (n + 1) * BLOCK)
        kv = jnp.concatenate([kv_block(n - 1), kv_block(n), kv_block(n + 1)], axis=0)
        if n == 0:
            var = jnp.where(first, 1, 0)
        elif n == nsub - 1:
            var = jnp.where(last, 2, 0)
        else:
            var = 0
        pieces = []
        for kvh in range(N_KV_HEADS):
            k_st, k_sw = kv[:, 0:LANES], kv[:, LANES:2 * LANES]
            v_st, v_sw = kv[:, 2 * LANES:3 * LANES], kv[:, 3 * LANES:4 * LANES]
            if kvh == 0:
                k_lo, k_hi, v_lo, v_hi = k_st, k_sw, v_st, v_sw
            else:
                k_lo, k_hi, v_lo, v_hi = k_sw, k_st, v_sw, v_st
            kk = jnp.concatenate([k_lo * ones_a, k_hi * ones_b], axis=0)
            vv = jnp.concatenate([
                jnp.concatenate([v_lo * ones_a, ones_a], axis=1),
                jnp.concatenate([v_hi * ones_b, ones_b], axis=1)], axis=0)
            for pair in range(GROUP_SIZE // 2):
                h0 = kvh * GROUP_SIZE + 2 * pair
                qp = q_ref[rows, h0 * HEAD_DIM_ATT:(h0 + 2) * HEAD_DIM_ATT]
                s2 = _dot_nt(qp, kk)
                ps, es = [], []
                for t in range(2):
                    logits = s2[:, t * nk:(t + 1) * nk] + bias_scr[var * N_HEADS_ATT + h0 + t]
                    sink = sink_ref[h0 + t] * LOG2E
                    mx = jnp.maximum(jnp.max(logits, axis=-1, keepdims=True), sink)
                    ps.append(jnp.exp2(logits - mx).astype(BF16))
                    es.append(jnp.exp2(sink - mx))
                res = _dot(jnp.concatenate(ps, axis=1), vv)
                den = res[:, LANES:] + jnp.where(lo_half_q, es[0], es[1])
                pieces.append(res[:, :LANES] / den)
                yield
        att = jnp.concatenate(pieces, axis=1)
        o_ref[rows, :] = _rms(att, g_ref[...]).astype(o_ref.dtype)
        yield


def _outffn_body(x_ref, att_ref, hf_ref, hb_ref, om_ref, mod_ref, gm_ref, wo_ref, g2_ref,
                 w1_ref, w2_ref, gf_ref, o_ref, hid_scr, *, final):
    hs = hf_ref[...] + hb_ref[...]
    parts = []
    for h in range(N_HEADS_M):
        cols = slice(h * HEAD_DIM_M, (h + 1) * HEAD_DIM_M)
        parts.append(_rms(hs[:, cols], gm_ref[:, cols]))
    hm = jax.nn.sigmoid(om_ref[...]) * jnp.concatenate(parts, axis=1)
    mixin = jnp.concatenate([att_ref[...], hm.astype(BF16)], axis=1)
    x1 = x_ref[...] + _mod(mod_ref, 2) * _dot(mixin, wo_ref[...])
    hff = (_rms(x1, g2_ref[...] * (1.0 + _mod(mod_ref, 4))) + _mod(mod_ref, 3)).astype(BF16)
    for c in range(N_FF_CHUNKS):
        gate = _dot(hff, w1_ref[:, FF_CHUNK * c:FF_CHUNK * (c + 1)])
        up = _dot(hff, w1_ref[:, D_FF + FF_CHUNK * c:D_FF + FF_CHUNK * (c + 1)])
        hid_scr[:, FF_CHUNK * c:FF_CHUNK * (c + 1)] = (jax.nn.silu(gate) * up).astype(BF16)
    x2 = x1 + _mod(mod_ref, 5) * _dot(hid_scr[...], w2_ref[...])
    if final:
        x2 = _rms(x2, gf_ref[...])
    o_ref[...] = x2


def _outffn_kernel(*refs, final):
    _outffn_body(*refs, final=final)


def _outffn_call(x, att, hf, hb, om, mod, g_m, w_out, g2, w1, w2, g_final, tm, final):
    bsz, s, d = x.shape
    tile = lambda w: pl.BlockSpec((None, tm, w), lambda b, i: (b, i, 0))
    const = lambda shp: pl.BlockSpec(shp, lambda b, i: (0, 0))
    weight = lambda shp: pl.BlockSpec(shp, lambda b, i: (0, 0), pipeline_mode=pl.Buffered(1))
    return pl.pallas_call(
        functools.partial(_outffn_kernel, final=final),
        grid=(bsz, s // tm),
        in_specs=[
            tile(d), tile(ATT_Q), tile(M_W), tile(M_W), tile(M_W),
            pl.BlockSpec((None, 1, N_MOD * d), lambda b, i: (b, 0, 0)),
            const((1, M_W)), weight((ATT_Q + M_W, d)), const((1, d)),
            weight((d, 2 * D_FF)), weight((D_FF, d)), const((1, d)),
        ],
        out_specs=tile(d),
        out_shape=jax.ShapeDtypeStruct((bsz, s, d), F32),
        scratch_shapes=[pltpu.VMEM((tm, D_FF), BF16)],
        compiler_params=pltpu.CompilerParams(
            dimension_semantics=("arbitrary", "arbitrary"), vmem_limit_bytes=VMEM_LIMIT),
        name="outffn",
    )(x, att, hf, hb, om, mod, g_m, w_out, g2, w1, w2, g_final)


def _layer(l, x, c, w_mod, b_mod, g_norm1, w_in, conv_w, conv_b, b_gates, sink,
           g_attn_out, g_mlstm_out, w_out, g_norm2, w_ffn_in, w_ffn_out, g_final, final):
    d = x.shape[-1]
    mod = _mod_call(c, w_mod[l], b_mod[l])

    w_in_p = _cast_t_call(w_in, l, IN_COLS_PAD)
    qkva, qc, kct, vm, om, gates_t = _inproj_call(
        x, mod, g_norm1[l].reshape(1, d), w_in_p, conv_w[l], conv_b[l].reshape(1, 2 * M_W), tm=512)

    bg_rows = jnp.broadcast_to(b_gates[l][:, None], (N_GATES, LANES))
    hf, hb, att, (w_out_b, w_ffn_in_b, w_ffn_out_b) = _mixer_call(
        qc, kct, vm, gates_t, bg_rows, sink[l], qkva, g_attn_out[l].reshape(1, ATT_Q),
        (w_out, w_ffn_in, w_ffn_out), l, tm=1024)

    return _outffn_call(x, att, hf, hb, om, mod, g_mlstm_out[l].reshape(1, M_W), w_out_b,
                        g_norm2[l].reshape(1, d), w_ffn_in_b, w_ffn_out_b,
                        g_final.reshape(1, d), tm=512, final=final)


def kernel(x, c, w_mod, b_mod, g_norm1, w_in, conv_w, conv_b, b_gates, sink, g_attn_out,
           g_mlstm_out, w_out, g_norm2, w_ffn_in, w_ffn_out, g_final):
    depth = w_mod.shape[0]
    for l in range(depth):
        x = _layer(l, x, c, w_mod, b_mod, g_norm1, w_in, conv_w, conv_b, b_gates, sink, g_attn_out,
                   g_mlstm_out, w_out, g_norm2, w_ffn_in, w_ffn_out, g_final, final=(l == depth - 1))
    return x
```

```python
import functools

import jax
import jax.numpy as jnp
from jax import lax
from jax.experimental import pallas as pl
from jax.experimental.pallas import tpu as pltpu

F32 = jnp.float32
BF16 = jnp.bfloat16

D_MODEL = 1024
EPS = 1e-6
N_HEADS_ATT = 8
N_KV_HEADS = 2
HEAD_DIM_ATT = 64
GROUP_SIZE = N_HEADS_ATT // N_KV_HEADS
WINDOW = 128
BLOCK = 128
N_HEADS_M = 4
HEAD_DIM_M = 128
CHUNK = 128
ATT_Q = N_HEADS_ATT * HEAD_DIM_ATT
ATT_KV = N_KV_HEADS * HEAD_DIM_ATT
M_W = N_HEADS_M * HEAD_DIM_M
N_GATES = 4 * N_HEADS_M
D_FF = 2816
N_MOD = 6

LANES = 128
GATE_PAD = LANES
FF_CHUNK = 256
N_FF_CHUNKS = D_FF // FF_CHUNK
VMEM_LIMIT = 56 * 1024 * 1024

C_QA = 0
C_KA = ATT_Q
C_VA = ATT_Q + ATT_KV
C_QKM = ATT_Q + 2 * ATT_KV
C_VM = C_QKM + 2 * M_W
C_OM = C_VM + M_W
C_G = C_OM + M_W
IN_COLS_PAD = C_G + GATE_PAD
QKVA_W = ATT_Q + 4 * ATT_KV


def _dot(a, b):
    return jnp.dot(a, b, preferred_element_type=F32)


def _dot_nt(a, b):
    return lax.dot_general(a, b, (((1,), (1,)), ((), ())), preferred_element_type=F32)


def _rms(x, g):
    return x * lax.rsqrt(jnp.mean(x * x, axis=-1, keepdims=True) + EPS) * g


def _mod(mod_ref, k):
    return mod_ref[:, k * D_MODEL:(k + 1) * D_MODEL]


def _alternate(*streams):
    live = list(streams)
    while live:
        for s in list(live):
            if next(s, StopIteration) is StopIteration:
                live.remove(s)


def _cast_t_kernel(wt_ref, o_ref):
    n = wt_ref.shape[0]
    full = n // LANES * LANES
    for r in range(0, full, LANES):
        o_ref[:, r:r + LANES] = wt_ref[r:r + LANES, :].T.astype(o_ref.dtype)
    if o_ref.shape[1] > full:
        tail = jnp.concatenate(
            [wt_ref[full:n, :], jnp.zeros((full + LANES - n, wt_ref.shape[1]), wt_ref.dtype)], axis=0)
        o_ref[:, full:full + LANES] = tail.T.astype(o_ref.dtype)


def _cast_t_call(w, layer, n_out, bk=256):
    _, k, n = w.shape
    assert n_out - n < LANES and n_out % LANES == 0
    return pl.pallas_call(
        _cast_t_kernel,
        grid=(k // bk,),
        in_specs=[pl.BlockSpec((None, n, bk), lambda i: (layer, 0, i))],
        out_specs=pl.BlockSpec((bk, n_out), lambda i: (i, 0)),
        out_shape=jax.ShapeDtypeStruct((k, n_out), BF16),
        compiler_params=pltpu.CompilerParams(
            dimension_semantics=("arbitrary",), vmem_limit_bytes=VMEM_LIMIT),
        name="cast",
    )(jnp.swapaxes(w, 1, 2))


SUBLANES = 8


def _mod_kernel(c_ref, w_ref, b_ref, o_ref):
    bsz, d = c_ref.shape
    c = jnp.concatenate([c_ref[...], jnp.zeros((SUBLANES - bsz, d), F32)], axis=0)
    res = _dot(jax.nn.silu(c).astype(BF16), w_ref[...].astype(BF16)) + b_ref[...]
    for b in range(bsz):
        o_ref[b] = res[b:b + 1, :]


def _mod_call(c, w_mod, b_mod):
    bsz, d = c.shape
    assert bsz <= SUBLANES
    n = w_mod.shape[1]
    bn = 1024
    return pl.pallas_call(
        _mod_kernel,
        grid=(n // bn,),
        in_specs=[
            pl.BlockSpec((bsz, d), lambda i: (0, 0)),
            pl.BlockSpec((d, bn), lambda i: (0, i)),
            pl.BlockSpec((1, bn), lambda i: (0, i)),
        ],
        out_specs=pl.BlockSpec((bsz, 1, bn), lambda i: (0, 0, i)),
        out_shape=jax.ShapeDtypeStruct((bsz, 1, n), F32),
        compiler_params=pltpu.CompilerParams(dimension_semantics=("arbitrary",)),
        name="mod",
    )(c, w_mod, b_mod.reshape(1, n))


MXU_COLS = 256
CONV_ROWS = 64


def _inproj_kernel(x_ref, mod_ref, g_ref, w_ref, cw_ref, cb_ref,
                   qkva_ref, qc_ref, kct_ref, vm_ref, om_ref, gt_ref,
                   raw_scr, new_scr, k_scr, carry_scr, *, nblk):
    j = pl.program_id(0)

    @pl.when(j == 0)
    def _init():
        raw_scr[...] = jnp.zeros_like(raw_scr)
        carry_scr[...] = jnp.zeros_like(carry_scr)

    x = x_ref[...]
    h = _rms(x, g_ref[...] * (1.0 + _mod(mod_ref, 1))) + _mod(mod_ref, 0)
    hb = h.astype(BF16)
    tm = x.shape[0]
    nchunk = tm // CHUNK
    within = (j + nblk - 1) % nblk

    anchors = []

    def zero_after(a):
        return jnp.concatenate([_zero_after(a)] * (2 * M_W // LANES), axis=1)

    def project():
        order = list(range(C_QKM, C_VM, MXU_COLS)) + list(range(0, C_QKM, MXU_COLS)) + \
            list(range(C_VM, IN_COLS_PAD, MXU_COLS))
        for c0 in order:
            c1 = min(c0 + MXU_COLS, IN_COLS_PAD)
            res = _dot(hb, w_ref[:, c0:c1])
            anchors.append(res[tm - 1:tm, c1 - c0 - LANES:c1 - c0])
            if c0 < C_KA:
                qkva_ref[:, c0:c1] = (res * (HEAD_DIM_ATT ** -0.5 * LOG2E)).astype(BF16)
            elif c0 < C_QKM:
                ka, va = res[:, :ATT_KV], res[:, ATT_KV:]
                half = HEAD_DIM_ATT
                qkva_ref[:, ATT_Q:ATT_Q + ATT_KV] = ka.astype(BF16)
                qkva_ref[:, ATT_Q + ATT_KV:ATT_Q + 2 * ATT_KV] = pltpu.roll(ka, half, axis=1).astype(BF16)
                qkva_ref[:, ATT_Q + 2 * ATT_KV:ATT_Q + 3 * ATT_KV] = va.astype(BF16)
                qkva_ref[:, ATT_Q + 3 * ATT_KV:ATT_Q + 4 * ATT_KV] = pltpu.roll(va, half, axis=1).astype(BF16)
            elif c0 < C_VM:
                new_scr[:, c0 - C_QKM:c1 - C_QKM] = res
            elif c0 < C_OM:
                vm_ref[:, c0 - C_VM:c1 - C_VM] = res.astype(BF16)
            elif c0 < C_G:
                om_ref[:, c0 - C_OM:c1 - C_OM] = res
            else:
                for g in range(nchunk):
                    blk = res[g * CHUNK:(g + 1) * CHUNK, :].T
                    gt_ref[:, g * CHUNK:(g + 1) * CHUNK] = blk[0:N_GATES, :]
            yield

    def conv():
        R = CONV_ROWS
        row = lax.broadcasted_iota(jnp.int32, (R, 1), 0)
        for p in range(tm // R):
            r0 = p * R
            xg = raw_scr[r0:r0 + R, :]
            if p == 0:
                prev_row = jnp.where(within == 0, 0.0, carry_scr[0:1, :])
            else:
                prev_row = raw_scr[r0 - 1:r0, :]
            if r0 + R == tm:
                next_row = jnp.where(within == nblk - 1, 0.0, new_scr[0:1, :])
            else:
                next_row = raw_scr[r0 + R:r0 + R + 1, :]
            xm1 = jnp.where(row == 0, prev_row, pltpu.roll(xg, 1, axis=0))
            xp1 = jnp.where(row == R - 1, next_row, pltpu.roll(xg, R - 1, axis=0))
            z = zero_after(anchors[-1])
            y =xm1 * (cw_ref[0:1, :] + z) + xg * (cw_ref[1:2, :] + z) + xp1 * (cw_ref[2:3, :] + z) \
                + cb_ref[...]
            y = jax.nn.silu(y)
            qc_ref[r0:r0 + R, :] = (y[:, :M_W] * (HEAD_DIM_M ** -0.5)).astype(BF16)
            k_scr[r0:r0 + R, :] = y[:, M_W:]
            yield
            if (r0 + R) % CHUNK == 0:
                g = r0 // CHUNK
                for hd in range(N_HEADS_M):
                    blk = k_scr[g * CHUNK:(g + 1) * CHUNK, hd * HEAD_DIM_M:(hd + 1) * HEAD_DIM_M]
                    base = (g * N_HEADS_M + hd) * HEAD_DIM_M
                    kct_ref[base:base + HEAD_DIM_M, :] = blk.T.astype(BF16)
                yield

    proj, cv = project(), conv()
    n_proj = -(-IN_COLS_PAD // MXU_COLS)
    n_conv = tm // CONV_ROWS + nchunk
    done = 0
    next(proj)
    for k in range(1, n_proj):
        next(proj)
        while done * n_proj < k * n_conv:
            next(cv)
            done += 1
    for _ in cv:
        pass
    carry_scr[0:1, :] = raw_scr[tm - 1:tm, :]
    raw_scr[...] = new_scr[...]


def _inproj_call(x, mod, g1, w_in_p, conv_w, conv_b, tm):
    bsz, s, d = x.shape
    nblk = s // tm
    ntiles = bsz * nblk
    outs = (
        jax.ShapeDtypeStruct((bsz, s, QKVA_W), BF16),
        jax.ShapeDtypeStruct((bsz, s, M_W), BF16),
        jax.ShapeDtypeStruct((bsz, s * N_HEADS_M, HEAD_DIM_M), BF16),
        jax.ShapeDtypeStruct((bsz, s, M_W), BF16),
        jax.ShapeDtypeStruct((bsz, s, M_W), F32),
        jax.ShapeDtypeStruct((bsz, N_GATES, s), F32),
    )

    def cur(j):
        t = jnp.minimum(j, ntiles - 1)
        return t // nblk, t % nblk

    def old(j):
        t = jnp.maximum(j - 1, 0)
        return t // nblk, t % nblk

    cur_tile = lambda w: pl.BlockSpec((None, tm, w), lambda j: (*cur(j), 0))
    old_tile = lambda w: pl.BlockSpec((None, tm, w), lambda j: (*old(j), 0))
    const = lambda shp: pl.BlockSpec(shp, lambda j: (0, 0))
    return pl.pallas_call(
        functools.partial(_inproj_kernel, nblk=nblk),
        grid=(ntiles + 1,),
        in_specs=[
            cur_tile(d),
            pl.BlockSpec((None, 1, N_MOD * d), lambda j: (cur(j)[0], 0, 0)),
            const((1, d)),
            pl.BlockSpec((d, IN_COLS_PAD), lambda j: (0, 0), pipeline_mode=pl.Buffered(1)),
            const((3, 2 * M_W)), const((1, 2 * M_W)),
        ],
        out_specs=[cur_tile(QKVA_W), old_tile(M_W),
                   pl.BlockSpec((None, tm * N_HEADS_M, HEAD_DIM_M), lambda j: (*old(j), 0)),
                   cur_tile(M_W), cur_tile(M_W),
                   pl.BlockSpec((None, N_GATES, tm), lambda j: (cur(j)[0], 0, cur(j)[1]))],
        out_shape=outs,
        scratch_shapes=[pltpu.VMEM((tm, 2 * M_W), F32), pltpu.VMEM((tm, 2 * M_W), F32),
                        pltpu.VMEM((tm, M_W), F32), pltpu.VMEM((8, 2 * M_W), F32)],
        compiler_params=pltpu.CompilerParams(
            dimension_semantics=("arbitrary",), vmem_limit_bytes=VMEM_LIMIT),
        name="inproj",
    )(x, mod, g1, w_in_p, conv_w, conv_b)


def _mlstm_reset(first, c_scr, m_scr):
    @pl.when(first)
    def _init():
        c_scr[...] = jnp.zeros_like(c_scr)
        m_scr[...] = jnp.zeros_like(m_scr)


LOG2E = 1.4426950408889634


def _mlstm_stream(q_ref, kt_ref, v_ref, gt_ref, bgr_ref, h_ref, c_scr, m_scr, *, reverse, nchunk):
    L = CHUNK
    ti = lax.broadcasted_iota(jnp.int32, (L, L), 0)
    si = lax.broadcasted_iota(jnp.int32, (L, L), 1)
    causal = (si >= ti) if reverse else (si <= ti)
    tri_t = jnp.where((ti >= si) if reverse else (ti <= si), 1.0, 0.0).astype(BF16)
    lane = lax.broadcasted_iota(jnp.int32, (1, L), 1)
    last = 0 if reverse else L - 1
    i_off, f_off = (2 * N_HEADS_M, 3 * N_HEADS_M) if reverse else (0, N_HEADS_M)
    ones = jnp.ones((L, HEAD_DIM_M), BF16)
    order = list(range(nchunk - 1, -1, -1) if reverse else range(nchunk))
    rows_of = lambda g: slice(g * L, (g + 1) * L)

    def split3(a):
        hi = a.astype(BF16)
        r1 = a - hi.astype(F32)
        mid = r1.astype(BF16)
        return hi, mid, (r1 - mid.astype(F32)).astype(BF16)

    gates = {}
    pad = jnp.zeros((L - N_GATES, L), F32)
    for g in order:
        gr = gt_ref[:, rows_of(g)] + bgr_ref[...]
        parts = _dot(jnp.concatenate(split3(jax.nn.log_sigmoid(gr)), axis=0), tri_t)
        bcr = (parts[0:N_GATES] + parts[N_GATES:2 * N_GATES] + parts[2 * N_GATES:3 * N_GATES]) * LOG2E
        gates[g] = (gr * LOG2E, bcr, jnp.concatenate([-bcr, pad], axis=0).T)
        yield

    def head(g, h):
        rows = rows_of(g)
        cols = slice(h * HEAD_DIM_M, (h + 1) * HEAD_DIM_M)
        gct, bct, nbcum = gates[g]
        r_row = gct[i_off + h:i_off + h + 1, :] - bct[f_off + h:f_off + h + 1, :]
        btot = jnp.sum(jnp.where(lane == last, bct[f_off + h:f_off + h + 1, :], 0.0),
                       axis=1, keepdims=True)
        rmax = jnp.max(r_row, axis=1, keepdims=True)
        a_max = btot + rmax
        m_prev = m_scr[h:h + 1, 0:1]
        m_new = jnp.maximum(btot + m_prev, a_max)
        m_scr[h:h + 1, :] = jnp.broadcast_to(m_new, (1, LANES))
        decay = jnp.exp2(btot + m_prev - m_new)
        w_row = jnp.exp2(r_row - rmax) * jnp.exp2(a_max - m_new)
        kt_rows = slice((g * N_HEADS_M + h) * HEAD_DIM_M, (g * N_HEADS_M + h + 1) * HEAD_DIM_M)
        cm = jnp.max(jnp.where(causal, r_row, -jnp.inf), axis=1, keepdims=True)
        yield
        q = q_ref[rows, cols]
        s = _dot(q, kt_ref[kt_rows, :])
        u = jnp.maximum(cm, m_prev)
        ub = jnp.broadcast_to(u, (L, L))
        dm = jnp.exp2(jnp.where(causal, r_row - ub, -jnp.inf))
        qi = q * jnp.exp2(m_prev - ub).astype(BF16)
        floor = jnp.exp2(jnp.broadcast_to(nbcum[:, f_off + h:f_off + h + 1], (L, L)) - ub)
        yield
        vaug = jnp.concatenate([v_ref[rows, cols], ones], axis=1)
        lhs = jnp.concatenate([(s * dm).astype(BF16), qi], axis=1)
        rhs = jnp.concatenate([vaug, c_scr[h].astype(BF16)], axis=0)
        out = _dot(lhs, rhs)
        upd = _dot(kt_ref[kt_rows, :] * w_row.astype(BF16), vaug)
        yield
        c_scr[h] = decay * c_scr[h] + upd
        h_ref[rows, cols] = out[:, :HEAD_DIM_M] / jnp.maximum(jnp.abs(out[:, HEAD_DIM_M:]), floor)
        yield

    n_stage, lag = 4, 2
    heads = {}
    for slot in range(lag * (nchunk - 1) + n_stage):
        for i, g in enumerate(order):
            stage = slot - lag * i
            if 0 <= stage < n_stage:
                for h in range(N_HEADS_M):
                    if stage == 0:
                        heads[g, h] = head(g, h)
                    next(heads[g, h])
                    yield


def _mixer_kernel(qf_ref, kf_ref, vf_ref, gtf_ref, qb_ref, kb_ref, vb_ref, gtb_ref, bgr_ref,
                  sink_ref, qa_ref, kvp_ref, kvn_ref, ga_ref, *rest, nblk, nchunk, ncast):
    w_refs, rest = rest[:ncast], rest[ncast:]
    hf_ref, hb_ref, att_ref = rest[:3]
    wo_refs, (cf_scr, mf_scr, cb_scr, mb_scr, bias_scr) = rest[3:3 + ncast], rest[3 + ncast:]
    b = pl.program_id(0)
    j = pl.program_id(1)

    @pl.when((b == 0) & (j == 0))
    def _init():
        _attn_bias_init(bias_scr)

    _mlstm_reset(j == 0, cf_scr, mf_scr)
    _mlstm_reset(j == 0, cb_scr, mb_scr)
    for w_ref, wo_ref in zip(w_refs, wo_refs):
        wo_ref[...] = w_ref[...].astype(wo_ref.dtype)
    fwd = _mlstm_stream(qf_ref, kf_ref, vf_ref, gtf_ref, bgr_ref, hf_ref, cf_scr, mf_scr,
                        reverse=False, nchunk=nchunk)
    bwd = _mlstm_stream(qb_ref, kb_ref, vb_ref, gtb_ref, bgr_ref, hb_ref, cb_scr, mb_scr,
                        reverse=True, nchunk=nchunk)
    att = _attn_pieces(j == 0, j == nblk - 1, sink_ref, qa_ref, kvp_ref, kvn_ref, ga_ref, att_ref,
                       bias_scr, nsub=nchunk)
    _alternate(fwd, bwd)
    for _ in att:
        pass


def _mixer_call(qc, kct, vm, gates_t, bg_rows, sink, qkva, g_attn, weights, layer, tm):
    bsz, s, _ = qc.shape
    nblk = s // tm
    nsteps = bsz * nblk
    nsub = tm // BLOCK
    nb = s // BLOCK
    kvw = QKVA_W - ATT_Q
    attn_specs = [
        pl.BlockSpec(memory_space=pltpu.SMEM),
        pl.BlockSpec((None, tm, QKVA_W), lambda b, j: (b, j, 0)),
        pl.BlockSpec((None, BLOCK, kvw), lambda b, j: (b, jnp.maximum(j * nsub - 1, 0), 1)),
        pl.BlockSpec((None, BLOCK, kvw), lambda b, j: (b, jnp.minimum((j + 1) * nsub, nb - 1), 1)),
        pl.BlockSpec((1, ATT_Q), lambda b, j: (0, 0)),
    ]
    w_specs, w_shapes = [], []
    for w in weights:
        _, k, n = w.shape
        rows = k // nsteps
        assert rows * nsteps == k and rows % 16 == 0
        w_specs.append(pl.BlockSpec((None, rows, n), lambda b, j: (layer, b * nblk + j, 0)))
        w_shapes.append(jax.ShapeDtypeStruct((k, n), BF16))
    wo_specs = [pl.BlockSpec((sp.block_shape[1], sp.block_shape[2]), lambda b, j: (b * nblk + j, 0))
                for sp in w_specs]

    def specs(pos):
        tile = lambda w: pl.BlockSpec((None, tm, w), lambda b, j: (b, pos(j), 0))
        kt_spec = pl.BlockSpec((None, tm * N_HEADS_M, HEAD_DIM_M), lambda b, j: (b, pos(j), 0))
        gt_spec = pl.BlockSpec((None, N_GATES, tm), lambda b, j: (b, 0, pos(j)))
        return [tile(M_W), kt_spec, tile(M_W), gt_spec], tile(M_W)

    in_f, out_f = specs(lambda j: j)
    in_b, out_b = specs(lambda j: nblk - 1 - j)
    state = [pltpu.VMEM((N_HEADS_M, HEAD_DIM_M, 2 * HEAD_DIM_M), F32), pltpu.VMEM((8, LANES), F32)]
    outs = pl.pallas_call(
        functools.partial(_mixer_kernel, nblk=nblk, nchunk=tm // CHUNK, ncast=len(weights)),
        grid=(bsz, nblk),
        in_specs=in_f + in_b + [pl.BlockSpec((N_GATES, LANES), lambda b, j: (0, 0))] + attn_specs
        + w_specs,
        out_specs=[out_f, out_b, pl.BlockSpec((None, tm, ATT_Q), lambda b, j: (b, j, 0))] + wo_specs,
        out_shape=[jax.ShapeDtypeStruct((bsz, s, M_W), F32)] * 2
        + [jax.ShapeDtypeStruct((bsz, s, ATT_Q), BF16)] + w_shapes,
        scratch_shapes=state + state + [pltpu.VMEM((3 * N_HEADS_ATT, BLOCK, 3 * BLOCK), F32)],
        compiler_params=pltpu.CompilerParams(
            dimension_semantics=("arbitrary", "arbitrary"), vmem_limit_bytes=VMEM_LIMIT),
        name="mixer",
    )(qc, kct, vm, gates_t, qc, kct, vm, gates_t, bg_rows, sink, qkva, qkva, qkva, g_attn, *weights)
    return outs[0], outs[1], outs[2], outs[3:]


def _zero_after(a):
    bits = pltpu.bitcast(a, jnp.uint32)
    z = lax.shift_right_logical(lax.shift_right_logical(bits, jnp.uint32(16)), jnp.uint32(16))
    return pltpu.bitcast(z, F32)


def _attn_bias_init(bias_scr):
    nk = 3 * BLOCK
    row = lax.broadcasted_iota(jnp.int32, (BLOCK, nk), 0)
    col = lax.broadcasted_iota(jnp.int32, (BLOCK, nk), 1)
    dist = jnp.abs(col - BLOCK - row)
    distf = dist.astype(F32)
    for var in range(3):
        ok = dist <= WINDOW
        if var == 1:
            ok = ok & (col >= BLOCK)
        elif var == 2:
            ok = ok & (col < 2 * BLOCK)
        for h in range(N_HEADS_ATT):
            slope = 2.0 ** (-8.0 * (h + 1.0) / N_HEADS_ATT)
            bias_scr[var * N_HEADS_ATT + h] = jnp.where(ok, (-slope * LOG2E) * distf, -jnp.inf)


def _attn_pieces(first, last, sink_ref, q_ref, kvp_ref, kvn_ref, g_ref, o_ref, bias_scr, *, nsub):
    nk = 3 * BLOCK

    lane_k = lax.broadcasted_iota(jnp.int32, (nk, LANES), 1)
    ones_a = jnp.where(lane_k < HEAD_DIM_ATT, 1.0, 0.0).astype(BF16)
    ones_b = jnp.where(lane_k < HEAD_DIM_ATT, 0.0, 1.0).astype(BF16)
    lo_half_q = lax.broadcasted_iota(jnp.int32, (BLOCK, LANES), 1) < HEAD_DIM_ATT

    def kv_block(idx):
        if idx < 0:
            return kvp_ref[...]
        if idx >= nsub:
            return kvn_ref[...]
        return q_ref[idx * BLOCK:(idx + 1) * BLOCK, ATT_Q:QKVA_W]

    for n in range(nsub):
        rows = slice(n * BLOCK, (n + 1) * BLOCK)
        kv = jnp.concatenate([kv_block(n - 1), kv_block(n), kv_block(n + 1)], axis=0)
        if n == 0:
            var = jnp.where(first, 1, 0)
        elif n == nsub - 1:
            var = jnp.where(last, 2, 0)
        else:
            var = 0
        pieces = []
        for kvh in range(N_KV_HEADS):
            k_st, k_sw = kv[:, 0:LANES], kv[:, LANES:2 * LANES]
            v_st, v_sw = kv[:, 2 * LANES:3 * LANES], kv[:, 3 * LANES:4 * LANES]
            if kvh == 0:
                k_lo, k_hi, v_lo, v_hi = k_st, k_sw, v_st, v_sw
            else:
                k_lo, k_hi, v_lo, v_hi = k_sw, k_st, v_sw, v_st
            kk = jnp.concatenate([k_lo * ones_a, k_hi * ones_b], axis=0)
            vv = jnp.concatenate([
                jnp.concatenate([v_lo * ones_a, ones_a], axis=1),
                jnp.concatenate([v_hi * ones_b, ones_b], axis=1)], axis=0)
            for pair in range(GROUP_SIZE // 2):
                h0 = kvh * GROUP_SIZE + 2 * pair
                qp = q_ref[rows, h0 * HEAD_DIM_ATT:(h0 + 2) * HEAD_DIM_ATT]
                s2 = _dot_nt(qp, kk)
                ps, es = [], []
                for t in range(2):
                    logits = s2[:, t * nk:(t + 1) * nk] + bias_scr[var * N_HEADS_ATT + h0 + t]
                    sink = sink_ref[h0 + t] * LOG2E
                    mx = jnp.maximum(jnp.max(logits, axis=-1, keepdims=True), sink)
                    ps.append(jnp.exp2(logits - mx).astype(BF16))
                    es.append(jnp.exp2(sink - mx))
                res = _dot(jnp.concatenate(ps, axis=1), vv)
                den = res[:, LANES:] + jnp.where(lo_half_q, es[0], es[1])
                pieces.append(res[:, :LANES] / den)
                yield
        att = jnp.concatenate(pieces, axis=1)
        o_ref[rows, :] = _rms(att, g_ref[...]).astype(o_ref.dtype)
        yield


def _outffn_body(x_ref, att_ref, hf_ref, hb_ref, om_ref, mod_ref, gm_ref, wo_ref, g2_ref,
                 w1_ref, w2_ref, gf_ref, o_ref, hid_scr, *, final):
    hs = hf_ref[...] + hb_ref[...]
    parts = []
    for h in range(N_HEADS_M):
        cols = slice(h * HEAD_DIM_M, (h + 1) * HEAD_DIM_M)
        parts.append(_rms(hs[:, cols], gm_ref[:, cols]))
    hm = jax.nn.sigmoid(om_ref[...]) * jnp.concatenate(parts, axis=1)
    mixin = jnp.concatenate([att_ref[...], hm.astype(BF16)], axis=1)
    x1 = x_ref[...] + _mod(mod_ref, 2) * _dot(mixin, wo_ref[...])
    hff = (_rms(x1, g2_ref[...] * (1.0 + _mod(mod_ref, 4))) + _mod(mod_ref, 3)).astype(BF16)
    for c in range(N_FF_CHUNKS):
        gate = _dot(hff, w1_ref[:, FF_CHUNK * c:FF_CHUNK * (c + 1)])
        up = _dot(hff, w1_ref[:, D_FF + FF_CHUNK * c:D_FF + FF_CHUNK * (c + 1)])
        hid_scr[:, FF_CHUNK * c:FF_CHUNK * (c + 1)] = (jax.nn.silu(gate) * up).astype(BF16)
    x2 = x1 + _mod(mod_ref, 5) * _dot(hid_scr[...], w2_ref[...])
    if final:
        x2 = _rms(x2, gf_ref[...])
    o_ref[...] = x2


def _outffn_kernel(*refs, final):
    _outffn_body(*refs, final=final)


def _outffn_call(x, att, hf, hb, om, mod, g_m, w_out, g2, w1, w2, g_final, tm, final):
    bsz, s, d = x.shape
    tile = lambda w: pl.BlockSpec((None, tm, w), lambda b, i: (b, i, 0))
    const = lambda shp: pl.BlockSpec(shp, lambda b, i: (0, 0))
    weight = lambda shp: pl.BlockSpec(shp, lambda b, i: (0, 0), pipeline_mode=pl.Buffered(1))
    return pl.pallas_call(
        functools.partial(_outffn_kernel, final=final),
        grid=(bsz, s // tm),
        in_specs=[
            tile(d), tile(ATT_Q), tile(M_W), tile(M_W), tile(M_W),
            pl.BlockSpec((None, 1, N_MOD * d), lambda b, i: (b, 0, 0)),
            const((1, M_W)), weight((ATT_Q + M_W, d)), const((1, d)),
            weight((d, 2 * D_FF)), weight((D_FF, d)), const((1, d)),
        ],
        out_specs=tile(d),
        out_shape=jax.ShapeDtypeStruct((bsz, s, d), F32),
        scratch_shapes=[pltpu.VMEM((tm, D_FF), BF16)],
        compiler_params=pltpu.CompilerParams(
            dimension_semantics=("arbitrary", "arbitrary"), vmem_limit_bytes=VMEM_LIMIT),
        name="outffn",
    )(x, att, hf, hb, om, mod, g_m, w_out, g2, w1, w2, g_final)


def _layer(l, x, c, w_mod, b_mod, g_norm1, w_in, conv_w, conv_b, b_gates, sink,
           g_attn_out, g_mlstm_out, w_out, g_norm2, w_ffn_in, w_ffn_out, g_final, final):
    d = x.shape[-1]
    mod = _mod_call(c, w_mod[l], b_mod[l])

    w_in_p = _cast_t_call(w_in, l, IN_COLS_PAD)
    qkva, qc, kct, vm, om, gates_t = _inproj_call(
        x, mod, g_norm1[l].reshape(1, d), w_in_p, conv_w[l], conv_b[l].reshape(1, 2 * M_W), tm=512)

    bg_rows = jnp.broadcast_to(b_gates[l][:, None], (N_GATES, LANES))
    hf, hb, att, (w_out_b, w_ffn_in_b, w_ffn_out_b) = _mixer_call(
        qc, kct, vm, gates_t, bg_rows, sink[l], qkva, g_attn_out[l].reshape(1, ATT_Q),
        (w_out, w_ffn_in, w_ffn_out), l, tm=1024)

    return _outffn_call(x, att, hf, hb, om, mod, g_mlstm_out[l].reshape(1, M_W), w_out_b,
                        g_norm2[l].reshape(1, d), w_ffn_in_b, w_ffn_out_b,
                        g_final.reshape(1, d), tm=512, final=final)


def kernel(x, c, w_mod, b_mod, g_norm1, w_in, conv_w, conv_b, b_gates, sink, g_attn_out,
           g_mlstm_out, w_out, g_norm2, w_ffn_in, w_ffn_out, g_final):
    depth = w_mod.shape[0]
    for l in range(depth):
        x = _layer(l, x, c, w_mod, b_mod, g_norm1, w_in, conv_w, conv_b, b_gates, sink, g_attn_out,
                   g_mlstm_out, w_out, g_norm2, w_ffn_in, w_ffn_out, g_final, final=(l == depth - 1))
    return x
```

```python
import functools

import jax
import jax.numpy as jnp
from jax import lax
from jax.experimental import pallas as pl
from jax.experimental.pallas import tpu as pltpu

F32 = jnp.float32
BF16 = jnp.bfloat16

D_MODEL = 1024
EPS = 1e-6
N_HEADS_ATT = 8
N_KV_HEADS = 2
HEAD_DIM_ATT = 64
GROUP_SIZE = N_HEADS_ATT // N_KV_HEADS
WINDOW = 128
BLOCK = 128
N_HEADS_M = 4
HEAD_DIM_M = 128
CHUNK = 128
ATT_Q = N_HEADS_ATT * HEAD_DIM_ATT
ATT_KV = N_KV_HEADS * HEAD_DIM_ATT
M_W = N_HEADS_M * HEAD_DIM_M
N_GATES = 4 * N_HEADS_M
D_FF = 2816
N_MOD = 6

LANES = 128
GATE_PAD = LANES
FF_CHUNK = 256
N_FF_CHUNKS = D_FF // FF_CHUNK
VMEM_LIMIT = 56 * 1024 * 1024

C_QA = 0
C_KA = ATT_Q
C_VA = ATT_Q + ATT_KV
C_QKM = ATT_Q + 2 * ATT_KV
C_VM = C_QKM + 2 * M_W
C_OM = C_VM + M_W
C_G = C_OM + M_W
IN_COLS_PAD = C_G + GATE_PAD
QKVA_W = ATT_Q + 4 * ATT_KV


def _dot(a, b):
    return jnp.dot(a, b, preferred_element_type=F32)


def _dot_nt(a, b):
    return lax.dot_general(a, b, (((1,), (1,)), ((), ())), preferred_element_type=F32)


def _rms(x, g):
    return x * lax.rsqrt(jnp.mean(x * x, axis=-1, keepdims=True) + EPS) * g


def _mod(mod_ref, k):
    return mod_ref[:, k * D_MODEL:(k + 1) * D_MODEL]


def _alternate(*streams):
    live = list(streams)
    while live:
        for s in list(live):
            if next(s, StopIteration) is StopIteration:
                live.remove(s)


def _cast_t_kernel(wt_ref, o_ref):
    n = wt_ref.shape[0]
    full = n // LANES * LANES
    for r in range(0, full, LANES):
        o_ref[:, r:r + LANES] = wt_ref[r:r + LANES, :].T.astype(o_ref.dtype)
    if o_ref.shape[1] > full:
        tail = jnp.concatenate(
            [wt_ref[full:n, :], jnp.zeros((full + LANES - n, wt_ref.shape[1]), wt_ref.dtype)], axis=0)
        o_ref[:, full:full + LANES] = tail.T.astype(o_ref.dtype)


SUBLANES = 8
MOD_STEPS = 8


def _mod_kernel(c_ref, w_ref, b_ref, wt_ref, o_ref, wo_ref):
    bsz, d = c_ref.shape
    c = jnp.concatenate([c_ref[...], jnp.zeros((SUBLANES - bsz, d), F32)], axis=0)
    res = _dot(jax.nn.silu(c).astype(BF16), w_ref[...].astype(BF16)) + b_ref[...]
    for b in range(bsz):
        o_ref[b] = res[b:b + 1, :]
    _cast_t_kernel(wt_ref, wo_ref)


def _mod_call(c, w_mod, b_mod, w_in, layer, n_out):
    bsz, d = c.shape
    assert bsz <= SUBLANES
    n_mod = w_mod.shape[2]
    _, k, n = w_in.shape
    assert n_out - n < LANES and n_out % LANES == 0
    bn, bk = n_mod // MOD_STEPS, k // MOD_STEPS
    assert bn * MOD_STEPS == n_mod and bk * MOD_STEPS == k and bn % LANES == 0 and bk % LANES == 0
    return pl.pallas_call(
        _mod_kernel,
        grid=(MOD_STEPS,),
        in_specs=[
            pl.BlockSpec((bsz, d), lambda i: (0, 0)),
            pl.BlockSpec((None, d, bn), lambda i: (layer, 0, i)),
            pl.BlockSpec((1, bn), lambda i: (0, i)),
            pl.BlockSpec((None, n, bk), lambda i: (layer, 0, i)),
        ],
        out_specs=[pl.BlockSpec((bsz, 1, bn), lambda i: (0, 0, i)),
                   pl.BlockSpec((bk, n_out), lambda i: (i, 0))],
        out_shape=[jax.ShapeDtypeStruct((bsz, 1, n_mod), F32), jax.ShapeDtypeStruct((k, n_out), BF16)],
        compiler_params=pltpu.CompilerParams(
            dimension_semantics=("arbitrary",), vmem_limit_bytes=VMEM_LIMIT),
        name="mod",
    )(c, w_mod, b_mod.reshape(1, n_mod), jnp.swapaxes(w_in, 1, 2))


MXU_COLS = 256
CONV_ROWS = 64


def _inproj_kernel(x_ref, mod_ref, g_ref, w_ref, cw_ref, cb_ref,
                   qkva_ref, qc_ref, kct_ref, vm_ref, om_ref, gt_ref,
                   raw_scr, new_scr, k_scr, carry_scr, *, nblk):
    j = pl.program_id(0)

    @pl.when(j == 0)
    def _init():
        raw_scr[...] = jnp.zeros_like(raw_scr)
        carry_scr[...] = jnp.zeros_like(carry_scr)

    x = x_ref[...]
    h = _rms(x, g_ref[...] * (1.0 + _mod(mod_ref, 1))) + _mod(mod_ref, 0)
    hb = h.astype(BF16)
    tm = x.shape[0]
    nchunk = tm // CHUNK
    within = (j + nblk - 1) % nblk

    anchors = []

    def zero_after(a):
        return jnp.concatenate([_zero_after(a)] * (2 * M_W // LANES), axis=1)

    def project():
        order = list(range(C_QKM, C_VM, MXU_COLS)) + list(range(0, C_QKM, MXU_COLS)) + \
            list(range(C_VM, IN_COLS_PAD, MXU_COLS))
        for c0 in order:
            c1 = min(c0 + MXU_COLS, IN_COLS_PAD)
            res = _dot(hb, w_ref[:, c0:c1])
            anchors.append(res[tm - 1:tm, c1 - c0 - LANES:c1 - c0])
            if c0 < C_KA:
                qkva_ref[:, c0:c1] = (res * (HEAD_DIM_ATT ** -0.5 * LOG2E)).astype(BF16)
            elif c0 < C_QKM:
                ka, va = res[:, :ATT_KV], res[:, ATT_KV:]
                half = HEAD_DIM_ATT
                qkva_ref[:, ATT_Q:ATT_Q + ATT_KV] = ka.astype(BF16)
                qkva_ref[:, ATT_Q + ATT_KV:ATT_Q + 2 * ATT_KV] = pltpu.roll(ka, half, axis=1).astype(BF16)
                qkva_ref[:, ATT_Q + 2 * ATT_KV:ATT_Q + 3 * ATT_KV] = va.astype(BF16)
                qkva_ref[:, ATT_Q + 3 * ATT_KV:ATT_Q + 4 * ATT_KV] = pltpu.roll(va, half, axis=1).astype(BF16)
            elif c0 < C_VM:
                new_scr[:, c0 - C_QKM:c1 - C_QKM] = res
            elif c0 < C_OM:
                vm_ref[:, c0 - C_VM:c1 - C_VM] = res.astype(BF16)
            elif c0 < C_G:
                om_ref[:, c0 - C_OM:c1 - C_OM] = res
            else:
                for g in range(nchunk):
                    blk = res[g * CHUNK:(g + 1) * CHUNK, :].T
                    gt_ref[:, g * CHUNK:(g + 1) * CHUNK] = blk[0:N_GATES, :]
            yield

    def conv():
        R = CONV_ROWS
        row = lax.broadcasted_iota(jnp.int32, (R, 1), 0)
        for p in range(tm // R):
            r0 = p * R
            xg = raw_scr[r0:r0 + R, :]
            if p == 0:
                prev_row = jnp.where(within == 0, 0.0, carry_scr[0:1, :])
            else:
                prev_row = raw_scr[r0 - 1:r0, :]
            if r0 + R == tm:
                next_row = jnp.where(within == nblk - 1, 0.0, new_scr[0:1, :])
            else:
                next_row = raw_scr[r0 + R:r0 + R + 1, :]
            xm1 = jnp.where(row == 0, prev_row, pltpu.roll(xg, 1, axis=0))
            xp1 = jnp.where(row == R - 1, next_row, pltpu.roll(xg, R - 1, axis=0))
            z = zero_after(anchors[-1])
            y =xm1 * (cw_ref[0:1, :] + z) + xg * (cw_ref[1:2, :] + z) + xp1 * (cw_ref[2:3, :] + z) \
                + cb_ref[...]
            y = jax.nn.silu(y)
            qc_ref[r0:r0 + R, :] = (y[:, :M_W] * (HEAD_DIM_M ** -0.5)).astype(BF16)
            k_scr[r0:r0 + R, :] = y[:, M_W:]
            yield
            if (r0 + R) % CHUNK == 0:
                g = r0 // CHUNK
                for hd in range(N_HEADS_M):
                    blk = k_scr[g * CHUNK:(g + 1) * CHUNK, hd * HEAD_DIM_M:(hd + 1) * HEAD_DIM_M]
                    base = (g * N_HEADS_M + hd) * HEAD_DIM_M
                    kct_ref[base:base + HEAD_DIM_M, :] = blk.T.astype(BF16)
                yield

    proj, cv = project(), conv()
    n_proj = -(-IN_COLS_PAD // MXU_COLS)
    n_conv = tm // CONV_ROWS + nchunk
    done = 0
    next(proj)
    for k in range(1, n_proj):
        next(proj)
        while done * n_proj < k * n_conv:
            next(cv)
            done += 1
    for _ in cv:
        pass
    carry_scr[0:1, :] = raw_scr[tm - 1:tm, :]
    raw_scr[...] = new_scr[...]


def _inproj_call(x, mod, g1, w_in_p, conv_w, conv_b, tm):
    bsz, s, d = x.shape
    nblk = s // tm
    ntiles = bsz * nblk
    outs = (
        jax.ShapeDtypeStruct((bsz, s, QKVA_W), BF16),
        jax.ShapeDtypeStruct((bsz, s, M_W), BF16),
        jax.ShapeDtypeStruct((bsz, s * N_HEADS_M, HEAD_DIM_M), BF16),
        jax.ShapeDtypeStruct((bsz, s, M_W), BF16),
        jax.ShapeDtypeStruct((bsz, s, M_W), F32),
        jax.ShapeDtypeStruct((bsz, N_GATES, s), F32),
    )

    def cur(j):
        t = jnp.minimum(j, ntiles - 1)
        return t // nblk, t % nblk

    def old(j):
        t = jnp.maximum(j - 1, 0)
        return t // nblk, t % nblk

    cur_tile = lambda w: pl.BlockSpec((None, tm, w), lambda j: (*cur(j), 0))
    old_tile = lambda w: pl.BlockSpec((None, tm, w), lambda j: (*old(j), 0))
    const = lambda shp: pl.BlockSpec(shp, lambda j: (0, 0))
    return pl.pallas_call(
        functools.partial(_inproj_kernel, nblk=nblk),
        grid=(ntiles + 1,),
        in_specs=[
            cur_tile(d),
            pl.BlockSpec((None, 1, N_MOD * d), lambda j: (cur(j)[0], 0, 0)),
            const((1, d)),
            pl.BlockSpec((d, IN_COLS_PAD), lambda j: (0, 0), pipeline_mode=pl.Buffered(1)),
            const((3, 2 * M_W)), const((1, 2 * M_W)),
        ],
        out_specs=[cur_tile(QKVA_W), old_tile(M_W),
                   pl.BlockSpec((None, tm * N_HEADS_M, HEAD_DIM_M), lambda j: (*old(j), 0)),
                   cur_tile(M_W), cur_tile(M_W),
                   pl.BlockSpec((None, N_GATES, tm), lambda j: (cur(j)[0], 0, cur(j)[1]))],
        out_shape=outs,
        scratch_shapes=[pltpu.VMEM((tm, 2 * M_W), F32), pltpu.VMEM((tm, 2 * M_W), F32),
                        pltpu.VMEM((tm, M_W), F32), pltpu.VMEM((8, 2 * M_W), F32)],
        compiler_params=pltpu.CompilerParams(
            dimension_semantics=("arbitrary",), vmem_limit_bytes=VMEM_LIMIT),
        name="inproj",
    )(x, mod, g1, w_in_p, conv_w, conv_b)


def _mlstm_reset(first, c_scr, m_scr):
    @pl.when(first)
    def _init():
        c_scr[...] = jnp.zeros_like(c_scr)
        m_scr[...] = jnp.zeros_like(m_scr)


LOG2E = 1.4426950408889634


def _mlstm_stream(q_ref, kt_ref, v_ref, gt_ref, bgr_ref, h_ref, c_scr, m_scr, *, reverse, nchunk):
    L = CHUNK
    ti = lax.broadcasted_iota(jnp.int32, (L, L), 0)
    si = lax.broadcasted_iota(jnp.int32, (L, L), 1)
    causal = (si >= ti) if reverse else (si <= ti)
    tri_t = jnp.where((ti >= si) if reverse else (ti <= si), 1.0, 0.0).astype(BF16)
    lane = lax.broadcasted_iota(jnp.int32, (1, L), 1)
    last = 0 if reverse else L - 1
    i_off, f_off = (2 * N_HEADS_M, 3 * N_HEADS_M) if reverse else (0, N_HEADS_M)
    ones = jnp.ones((L, HEAD_DIM_M), BF16)
    order = list(range(nchunk - 1, -1, -1) if reverse else range(nchunk))
    rows_of = lambda g: slice(g * L, (g + 1) * L)

    def split3(a):
        hi = a.astype(BF16)
        r1 = a - hi.astype(F32)
        mid = r1.astype(BF16)
        return hi, mid, (r1 - mid.astype(F32)).astype(BF16)

    gates = {}
    pad = jnp.zeros((L - N_GATES, L), F32)
    for g in order:
        gr = gt_ref[:, rows_of(g)] + bgr_ref[...]
        parts = _dot(jnp.concatenate(split3(jax.nn.log_sigmoid(gr)), axis=0), tri_t)
        bcr = (parts[0:N_GATES] + parts[N_GATES:2 * N_GATES] + parts[2 * N_GATES:3 * N_GATES]) * LOG2E
        gates[g] = (gr * LOG2E, bcr, jnp.concatenate([-bcr, pad], axis=0).T)
        yield

    def head(g, h):
        rows = rows_of(g)
        cols = slice(h * HEAD_DIM_M, (h + 1) * HEAD_DIM_M)
        gct, bct, nbcum = gates[g]
        r_row = gct[i_off + h:i_off + h + 1, :] - bct[f_off + h:f_off + h + 1, :]
        btot = jnp.sum(jnp.where(lane == last, bct[f_off + h:f_off + h + 1, :], 0.0),
                       axis=1, keepdims=True)
        rmax = jnp.max(r_row, axis=1, keepdims=True)
        a_max = btot + rmax
        m_prev = m_scr[h:h + 1, 0:1]
        m_new = jnp.maximum(btot + m_prev, a_max)
        m_scr[h:h + 1, :] = jnp.broadcast_to(m_new, (1, LANES))
        decay = jnp.exp2(btot + m_prev - m_new)
        w_row = jnp.exp2(r_row - rmax) * jnp.exp2(a_max - m_new)
        kt_rows = slice((g * N_HEADS_M + h) * HEAD_DIM_M, (g * N_HEADS_M + h + 1) * HEAD_DIM_M)
        cm = jnp.max(jnp.where(causal, r_row, -jnp.inf), axis=1, keepdims=True)
        yield
        q = q_ref[rows, cols]
        s = _dot(q, kt_ref[kt_rows, :])
        u = jnp.maximum(cm, m_prev)
        ub = jnp.broadcast_to(u, (L, L))
        dm = jnp.exp2(jnp.where(causal, r_row - ub, -jnp.inf))
        qi = q * jnp.exp2(m_prev - ub).astype(BF16)
        floor = jnp.exp2(jnp.broadcast_to(nbcum[:, f_off + h:f_off + h + 1], (L, L)) - ub)
        yield
        vaug = jnp.concatenate([v_ref[rows, cols], ones], axis=1)
        lhs = jnp.concatenate([(s * dm).astype(BF16), qi], axis=1)
        rhs = jnp.concatenate([vaug, c_scr[h].astype(BF16)], axis=0)
        out = _dot(lhs, rhs)
        upd = _dot(kt_ref[kt_rows, :] * w_row.astype(BF16), vaug)
        yield
        c_scr[h] = decay * c_scr[h] + upd
        h_ref[rows, cols] = out[:, :HEAD_DIM_M] / jnp.maximum(jnp.abs(out[:, HEAD_DIM_M:]), floor)
        yield

    n_stage, lag = 4, 2
    heads = {}
    for slot in range(lag * (nchunk - 1) + n_stage):
        for i, g in enumerate(order):
            stage = slot - lag * i
            if 0 <= stage < n_stage:
                for h in range(N_HEADS_M):
                    if stage == 0:
                        heads[g, h] = head(g, h)
                    next(heads[g, h])
                    yield


def _mixer_kernel(qf_ref, kf_ref, vf_ref, gtf_ref, qb_ref, kb_ref, vb_ref, gtb_ref, bgr_ref,
                  sink_ref, qa_ref, kvp_ref, kvn_ref, ga_ref, *rest, nblk, nchunk, ncast):
    w_refs, rest = rest[:ncast], rest[ncast:]
    hf_ref, hb_ref, att_ref = rest[:3]
    wo_refs, (cf_scr, mf_scr, cb_scr, mb_scr, bias_scr) = rest[3:3 + ncast], rest[3 + ncast:]
    b = pl.program_id(0)
    j = pl.program_id(1)

    @pl.when((b == 0) & (j == 0))
    def _init():
        _attn_bias_init(bias_scr)

    _mlstm_reset(j == 0, cf_scr, mf_scr)
    _mlstm_reset(j == 0, cb_scr, mb_scr)
    for w_ref, wo_ref in zip(w_refs, wo_refs):
        wo_ref[...] = w_ref[...].astype(wo_ref.dtype)
    fwd = _mlstm_stream(qf_ref, kf_ref, vf_ref, gtf_ref, bgr_ref, hf_ref, cf_scr, mf_scr,
                        reverse=False, nchunk=nchunk)
    bwd = _mlstm_stream(qb_ref, kb_ref, vb_ref, gtb_ref, bgr_ref, hb_ref, cb_scr, mb_scr,
                        reverse=True, nchunk=nchunk)
    att = _attn_pieces(j == 0, j == nblk - 1, sink_ref, qa_ref, kvp_ref, kvn_ref, ga_ref, att_ref,
                       bias_scr, nsub=nchunk)
    _alternate(fwd, bwd)
    for _ in att:
        pass


def _mixer_call(qc, kct, vm, gates_t, bg_rows, sink, qkva, g_attn, weights, layer, tm):
    bsz, s, _ = qc.shape
    nblk = s // tm
    nsteps = bsz * nblk
    nsub = tm // BLOCK
    nb = s // BLOCK
    kvw = QKVA_W - ATT_Q
    attn_specs = [
        pl.BlockSpec(memory_space=pltpu.SMEM),
        pl.BlockSpec((None, tm, QKVA_W), lambda b, j: (b, j, 0)),
        pl.BlockSpec((None, BLOCK, kvw), lambda b, j: (b, jnp.maximum(j * nsub - 1, 0), 1)),
        pl.BlockSpec((None, BLOCK, kvw), lambda b, j: (b, jnp.minimum((j + 1) * nsub, nb - 1), 1)),
        pl.BlockSpec((1, ATT_Q), lambda b, j: (0, 0)),
    ]
    w_specs, w_shapes = [], []
    for w in weights:
        _, k, n = w.shape
        rows = k // nsteps
        assert rows * nsteps == k and rows % 16 == 0
        w_specs.append(pl.BlockSpec((None, rows, n), lambda b, j: (layer, b * nblk + j, 0)))
        w_shapes.append(jax.ShapeDtypeStruct((k, n), BF16))
    wo_specs = [pl.BlockSpec((sp.block_shape[1], sp.block_shape[2]), lambda b, j: (b * nblk + j, 0))
                for sp in w_specs]

    def specs(pos):
        tile = lambda w: pl.BlockSpec((None, tm, w), lambda b, j: (b, pos(j), 0))
        kt_spec = pl.BlockSpec((None, tm * N_HEADS_M, HEAD_DIM_M), lambda b, j: (b, pos(j), 0))
        gt_spec = pl.BlockSpec((None, N_GATES, tm), lambda b, j: (b, 0, pos(j)))
        return [tile(M_W), kt_spec, tile(M_W), gt_spec], tile(M_W)

    in_f, out_f = specs(lambda j: j)
    in_b, out_b = specs(lambda j: nblk - 1 - j)
    state = [pltpu.VMEM((N_HEADS_M, HEAD_DIM_M, 2 * HEAD_DIM_M), F32), pltpu.VMEM((8, LANES), F32)]
    outs = pl.pallas_call(
        functools.partial(_mixer_kernel, nblk=nblk, nchunk=tm // CHUNK, ncast=len(weights)),
        grid=(bsz, nblk),
        in_specs=in_f + in_b + [pl.BlockSpec((N_GATES, LANES), lambda b, j: (0, 0))] + attn_specs
        + w_specs,
        out_specs=[out_f, out_b, pl.BlockSpec((None, tm, ATT_Q), lambda b, j: (b, j, 0))] + wo_specs,
        out_shape=[jax.ShapeDtypeStruct((bsz, s, M_W), F32)] * 2
        + [jax.ShapeDtypeStruct((bsz, s, ATT_Q), BF16)] + w_shapes,
        scratch_shapes=state + state + [pltpu.VMEM((3 * N_HEADS_ATT, BLOCK, 3 * BLOCK), F32)],
        compiler_params=pltpu.CompilerParams(
            dimension_semantics=("arbitrary", "arbitrary"), vmem_limit_bytes=VMEM_LIMIT),
        name="mixer",
    )(qc, kct, vm, gates_t, qc, kct, vm, gates_t, bg_rows, sink, qkva, qkva, qkva, g_attn, *weights)
    return outs[0], outs[1], outs[2], outs[3:]


def _zero_after(a):
    bits = pltpu.bitcast(a, jnp.uint32)
    z = lax.shift_right_logical(lax.shift_right_logical(bits, jnp.uint32(16)), jnp.uint32(16))
    return pltpu.bitcast(z, F32)


def _attn_bias_init(bias_scr):
    nk = 3 * BLOCK
    row = lax.broadcasted_iota(jnp.int32, (BLOCK, nk), 0)
    col = lax.broadcasted_iota(jnp.int32, (BLOCK, nk), 1)
    dist = jnp.abs(col - BLOCK - row)
    distf = dist.astype(F32)
    for var in range(3):
        ok = dist <= WINDOW
        if var == 1:
            ok = ok & (col >= BLOCK)
        elif var == 2:
            ok = ok & (col < 2 * BLOCK)
        for h in range(N_HEADS_ATT):
            slope = 2.0 ** (-8.0 * (h + 1.0) / N_HEADS_ATT)
            bias_scr[var * N_HEADS_ATT + h] = jnp.where(ok, (-slope * LOG2E) * distf, -jnp.inf)


def _attn_pieces(first, last, sink_ref, q_ref, kvp_ref, kvn_ref, g_ref, o_ref, bias_scr, *, nsub):
    nk = 3 * BLOCK

    lane_k = lax.broadcasted_iota(jnp.int32, (nk, LANES), 1)
    ones_a = jnp.where(lane_k < HEAD_DIM_ATT, 1.0, 0.0).astype(BF16)
    ones_b = jnp.where(lane_k < HEAD_DIM_ATT, 0.0, 1.0).astype(BF16)
    lo_half_q = lax.broadcasted_iota(jnp.int32, (BLOCK, LANES), 1) < HEAD_DIM_ATT

    def kv_block(idx):
        if idx < 0:
            return kvp_ref[...]
        if idx >= nsub:
            return kvn_ref[...]
        return q_ref[idx * BLOCK:(idx + 1) * BLOCK, ATT_Q:QKVA_W]

    for n in range(nsub):
        rows = slice(n * BLOCK, (n + 1) * BLOCK)
        kv = jnp.concatenate([kv_block(n - 1), kv_block(n), kv_block(n + 1)], axis=0)
        if n == 0:
            var = jnp.where(first, 1, 0)
        elif n == nsub - 1:
            var = jnp.where(last, 2, 0)
        else:
            var = 0
        pieces = []
        for kvh in range(N_KV_HEADS):
            k_st, k_sw = kv[:, 0:LANES], kv[:, LANES:2 * LANES]
            v_st, v_sw = kv[:, 2 * LANES:3 * LANES], kv[:, 3 * LANES:4 * LANES]
            if kvh == 0:
                k_lo, k_hi, v_lo, v_hi = k_st, k_sw, v_st, v_sw
            else:
                k_lo, k_hi, v_lo, v_hi = k_sw, k_st, v_sw, v_st
            kk = jnp.concatenate([k_lo * ones_a, k_hi * ones_b], axis=0)
            vv = jnp.concatenate([
                jnp.concatenate([v_lo * ones_a, ones_a], axis=1),
                jnp.concatenate([v_hi * ones_b, ones_b], axis=1)], axis=0)
            for pair in range(GROUP_SIZE // 2):
                h0 = kvh * GROUP_SIZE + 2 * pair
                qp = q_ref[rows, h0 * HEAD_DIM_ATT:(h0 + 2) * HEAD_DIM_ATT]
                s2 = _dot_nt(qp, kk)
                ps, es = [], []
                for t in range(2):
                    logits = s2[:, t * nk:(t + 1) * nk] + bias_scr[var * N_HEADS_ATT + h0 + t]
                    sink = sink_ref[h0 + t] * LOG2E
                    mx = jnp.maximum(jnp.max(logits, axis=-1, keepdims=True), sink)
                    ps.append(jnp.exp2(logits - mx).astype(BF16))
                    es.append(jnp.exp2(sink - mx))
                res = _dot(jnp.concatenate(ps, axis=1), vv)
                den = res[:, LANES:] + jnp.where(lo_half_q, es[0], es[1])
                pieces.append(res[:, :LANES] / den)
                yield
        att = jnp.concatenate(pieces, axis=1)
        o_ref[rows, :] = _rms(att, g_ref[...]).astype(o_ref.dtype)
        yield


def _outffn_body(x_ref, att_ref, hf_ref, hb_ref, om_ref, mod_ref, gm_ref, wo_ref, g2_ref,
                 w1_ref, w2_ref, gf_ref, o_ref, hid_scr, *, final):
    hs = hf_ref[...] + hb_ref[...]
    parts = []
    for h in range(N_HEADS_M):
        cols = slice(h * HEAD_DIM_M, (h + 1) * HEAD_DIM_M)
        parts.append(_rms(hs[:, cols], gm_ref[:, cols]))
    hm = jax.nn.sigmoid(om_ref[...]) * jnp.concatenate(parts, axis=1)
    mixin = jnp.concatenate([att_ref[...], hm.astype(BF16)], axis=1)
    x1 = x_ref[...] + _mod(mod_ref, 2) * _dot(mixin, wo_ref[...])
    hff = (_rms(x1, g2_ref[...] * (1.0 + _mod(mod_ref, 4))) + _mod(mod_ref, 3)).astype(BF16)
    for c in range(N_FF_CHUNKS):
        gate = _dot(hff, w1_ref[:, FF_CHUNK * c:FF_CHUNK * (c + 1)])
        up = _dot(hff, w1_ref[:, D_FF + FF_CHUNK * c:D_FF + FF_CHUNK * (c + 1)])
        hid_scr[:, FF_CHUNK * c:FF_CHUNK * (c + 1)] = (jax.nn.silu(gate) * up).astype(BF16)
    x2 = x1 + _mod(mod_ref, 5) * _dot(hid_scr[...], w2_ref[...])
    if final:
        x2 = _rms(x2, gf_ref[...])
    o_ref[...] = x2


def _outffn_kernel(*refs, final):
    _outffn_body(*refs, final=final)


def _outffn_call(x, att, hf, hb, om, mod, g_m, w_out, g2, w1, w2, g_final, tm, final):
    bsz, s, d = x.shape
    tile = lambda w: pl.BlockSpec((None, tm, w), lambda b, i: (b, i, 0))
    const = lambda shp: pl.BlockSpec(shp, lambda b, i: (0, 0))
    weight = lambda shp: pl.BlockSpec(shp, lambda b, i: (0, 0), pipeline_mode=pl.Buffered(1))
    return pl.pallas_call(
        functools.partial(_outffn_kernel, final=final),
        grid=(bsz, s // tm),
        in_specs=[
            tile(d), tile(ATT_Q), tile(M_W), tile(M_W), tile(M_W),
            pl.BlockSpec((None, 1, N_MOD * d), lambda b, i: (b, 0, 0)),
            const((1, M_W)), weight((ATT_Q + M_W, d)), const((1, d)),
            weight((d, 2 * D_FF)), weight((D_FF, d)), const((1, d)),
        ],
        out_specs=tile(d),
        out_shape=jax.ShapeDtypeStruct((bsz, s, d), F32),
        scratch_shapes=[pltpu.VMEM((tm, D_FF), BF16)],
        compiler_params=pltpu.CompilerParams(
            dimension_semantics=("arbitrary", "arbitrary"), vmem_limit_bytes=VMEM_LIMIT),
        name="outffn",
    )(x, att, hf, hb, om, mod, g_m, w_out, g2, w1, w2, g_final)


def _layer(l, x, c, w_mod, b_mod, g_norm1, w_in, conv_w, conv_b, b_gates, sink,
           g_attn_out, g_mlstm_out, w_out, g_norm2, w_ffn_in, w_ffn_out, g_final, final):
    d = x.shape[-1]
    mod, w_in_p = _mod_call(c, w_mod, b_mod[l], w_in, l, IN_COLS_PAD)
    qkva, qc, kct, vm, om, gates_t = _inproj_call(
        x, mod, g_norm1[l].reshape(1, d), w_in_p, conv_w[l], conv_b[l].reshape(1, 2 * M_W), tm=512)

    bg_rows = jnp.broadcast_to(b_gates[l][:, None], (N_GATES, LANES))
    hf, hb, att, (w_out_b, w_ffn_in_b, w_ffn_out_b) = _mixer_call(
        qc, kct, vm, gates_t, bg_rows, sink[l], qkva, g_attn_out[l].reshape(1, ATT_Q),
        (w_out, w_ffn_in, w_ffn_out), l, tm=1024)

    return _outffn_call(x, att, hf, hb, om, mod, g_mlstm_out[l].reshape(1, M_W), w_out_b,
                        g_norm2[l].reshape(1, d), w_ffn_in_b, w_ffn_out_b,
                        g_final.reshape(1, d), tm=512, final=final)


def kernel(x, c, w_mod, b_mod, g_norm1, w_in, conv_w, conv_b, b_gates, sink, g_attn_out,
           g_mlstm_out, w_out, g_norm2, w_ffn_in, w_ffn_out, g_final):
    depth = w_mod.shape[0]
    for l in range(depth):
        x = _layer(l, x, c, w_mod, b_mod, g_norm1, w_in, conv_w, conv_b, b_gates, sink, g_attn_out,
                   g_mlstm_out, w_out, g_norm2, w_ffn_in, w_ffn_out, g_final, final=(l == depth - 1))
    return x
```

```python
import functools

import jax
import jax.numpy as jnp
from jax import lax
from jax.experimental import pallas as pl
from jax.experimental.pallas import tpu as pltpu

F32 = jnp.float32
BF16 = jnp.bfloat16

D_MODEL = 1024
EPS = 1e-6
N_HEADS_ATT = 8
N_KV_HEADS = 2
HEAD_DIM_ATT = 64
GROUP_SIZE = N_HEADS_ATT // N_KV_HEADS
WINDOW = 128
BLOCK = 128
N_HEADS_M = 4
HEAD_DIM_M = 128
CHUNK = 128
CONV_WIDTH = 3
ATT_Q = N_HEADS_ATT * HEAD_DIM_ATT
ATT_KV = N_KV_HEADS * HEAD_DIM_ATT
M_W = N_HEADS_M * HEAD_DIM_M
N_GATES = 4 * N_HEADS_M
D_FF = 2816
N_MOD = 6

LANES = 128
GATE_PAD = LANES
FF_CHUNK = 256
N_FF_CHUNKS = D_FF // FF_CHUNK
VMEM_LIMIT = 56 * 1024 * 1024

C_QA = 0
C_KA = ATT_Q
C_VA = ATT_Q + ATT_KV
C_QKM = ATT_Q + 2 * ATT_KV
C_VM = C_QKM + 2 * M_W
C_OM = C_VM + M_W
C_G = C_OM + M_W
IN_COLS_PAD = C_G + GATE_PAD
QKVA_W = ATT_Q + 4 * ATT_KV


def _dot(a, b):
    return jnp.dot(a, b, preferred_element_type=F32)


def _dot_nt(a, b):
    return lax.dot_general(a, b, (((1,), (1,)), ((), ())), preferred_element_type=F32)


def _rms(x, g):
    return x * lax.rsqrt(jnp.mean(x * x, axis=-1, keepdims=True) + EPS) * g


def _mod(mod_ref, k):
    return mod_ref[:, k * D_MODEL:(k + 1) * D_MODEL]


def _alternate(*streams):
    live = list(streams)
    while live:
        for s in list(live):
            if next(s, StopIteration) is StopIteration:
                live.remove(s)


def _cast_t_kernel(wt_ref, o_ref):
    n = wt_ref.shape[0]
    full = n // LANES * LANES
    for r in range(0, full, LANES):
        o_ref[:, r:r + LANES] = wt_ref[r:r + LANES, :].T.astype(o_ref.dtype)
    if o_ref.shape[1] > full:
        tail = jnp.concatenate(
            [wt_ref[full:n, :], jnp.zeros((full + LANES - n, wt_ref.shape[1]), wt_ref.dtype)], axis=0)
        o_ref[:, full:full + LANES] = tail.T.astype(o_ref.dtype)


SUBLANES = 8
MOD_STEPS = 8


def _mod_kernel(c_ref, w_ref, b_ref, wt_ref, o_ref, wo_ref):
    bsz, d = c_ref.shape
    c = jnp.concatenate([c_ref[...], jnp.zeros((SUBLANES - bsz, d), F32)], axis=0)
    res = _dot(jax.nn.silu(c).astype(BF16), w_ref[...].astype(BF16)) + b_ref[...]
    for b in range(bsz):
        o_ref[b] = res[b:b + 1, :]
    _cast_t_kernel(wt_ref, wo_ref)


def _mod_call(c, w_mod, b_mod, w_in, layer, n_out):
    bsz, d = c.shape
    assert bsz <= SUBLANES
    n_mod = w_mod.shape[2]
    _, k, n = w_in.shape
    assert n_out - n < LANES and n_out % LANES == 0
    bn, bk = n_mod // MOD_STEPS, k // MOD_STEPS
    assert bn * MOD_STEPS == n_mod and bk * MOD_STEPS == k and bn % LANES == 0 and bk % LANES == 0
    return pl.pallas_call(
        _mod_kernel,
        grid=(MOD_STEPS,),
        in_specs=[
            pl.BlockSpec((bsz, d), lambda i: (0, 0)),
            pl.BlockSpec((None, d, bn), lambda i: (layer, 0, i)),
            pl.BlockSpec((1, bn), lambda i: (0, i)),
            pl.BlockSpec((None, n, bk), lambda i: (layer, 0, i)),
        ],
        out_specs=[pl.BlockSpec((bsz, 1, bn), lambda i: (0, 0, i)),
                   pl.BlockSpec((bk, n_out), lambda i: (i, 0))],
        out_shape=[jax.ShapeDtypeStruct((bsz, 1, n_mod), F32), jax.ShapeDtypeStruct((k, n_out), BF16)],
        compiler_params=pltpu.CompilerParams(
            dimension_semantics=("arbitrary",), vmem_limit_bytes=VMEM_LIMIT),
        name="mod",
    )(c, w_mod, b_mod.reshape(1, n_mod), jnp.swapaxes(w_in, 1, 2))


MXU_COLS = 256
CONV_ROWS = 64


def _inproj_kernel(x_ref, mod_ref, g_ref, w_ref, cw_ref, cb_ref,
                   qkva_ref, qc_ref, kct_ref, vm_ref, om_ref, gt_ref,
                   raw_scr, new_scr, k_scr, carry_scr, *, nblk):
    j = pl.program_id(0)

    @pl.when(j == 0)
    def _init():
        raw_scr[...] = jnp.zeros_like(raw_scr)
        carry_scr[...] = jnp.zeros_like(carry_scr)

    x = x_ref[...]
    h = _rms(x, g_ref[...] * (1.0 + _mod(mod_ref, 1))) + _mod(mod_ref, 0)
    hb = h.astype(BF16)
    tm = x.shape[0]
    nchunk = tm // CHUNK
    within = (j + nblk - 1) % nblk

    anchors = []

    def zero_after(a):
        return jnp.concatenate([_zero_after(a)] * (2 * M_W // LANES), axis=1)

    def project():
        order = list(range(C_QKM, C_VM, MXU_COLS)) + list(range(0, C_QKM, MXU_COLS)) + \
            list(range(C_VM, IN_COLS_PAD, MXU_COLS))
        for c0 in order:
            c1 = min(c0 + MXU_COLS, IN_COLS_PAD)
            res = _dot(hb, w_ref[:, c0:c1])
            anchors.append(res[tm - 1:tm, c1 - c0 - LANES:c1 - c0])
            if c0 < C_KA:
                qkva_ref[:, c0:c1] = (res * (HEAD_DIM_ATT ** -0.5 * LOG2E)).astype(BF16)
            elif c0 < C_QKM:
                ka, va = res[:, :ATT_KV], res[:, ATT_KV:]
                half = HEAD_DIM_ATT
                qkva_ref[:, ATT_Q:ATT_Q + ATT_KV] = ka.astype(BF16)
                qkva_ref[:, ATT_Q + ATT_KV:ATT_Q + 2 * ATT_KV] = pltpu.roll(ka, half, axis=1).astype(BF16)
                qkva_ref[:, ATT_Q + 2 * ATT_KV:ATT_Q + 3 * ATT_KV] = va.astype(BF16)
                qkva_ref[:, ATT_Q + 3 * ATT_KV:ATT_Q + 4 * ATT_KV] = pltpu.roll(va, half, axis=1).astype(BF16)
            elif c0 < C_VM:
                new_scr[:, c0 - C_QKM:c1 - C_QKM] = res
            elif c0 < C_OM:
                vm_ref[:, c0 - C_VM:c1 - C_VM] = res.astype(BF16)
            elif c0 < C_G:
                om_ref[:, c0 - C_OM:c1 - C_OM] = res
            else:
                for g in range(nchunk):
                    blk = res[g * CHUNK:(g + 1) * CHUNK, :].T
                    gt_ref[:, g * CHUNK:(g + 1) * CHUNK] = blk[0:N_GATES, :]
            yield

    def conv():
        R = CONV_ROWS
        row = lax.broadcasted_iota(jnp.int32, (R, 1), 0)
        for p in range(tm // R):
            r0 = p * R
            xg = raw_scr[r0:r0 + R, :]
            if p == 0:
                prev_row = jnp.where(within == 0, 0.0, carry_scr[0:1, :])
            else:
                prev_row = raw_scr[r0 - 1:r0, :]
            if r0 + R == tm:
                next_row = jnp.where(within == nblk - 1, 0.0, new_scr[0:1, :])
            else:
                next_row = raw_scr[r0 + R:r0 + R + 1, :]
            xm1 = jnp.where(row == 0, prev_row, pltpu.roll(xg, 1, axis=0))
            xp1 = jnp.where(row == R - 1, next_row, pltpu.roll(xg, R - 1, axis=0))
            z = zero_after(anchors[-1])
            w0, w1, w2 = (cw_ref[:, t * 2 * M_W:(t + 1) * 2 * M_W] + z for t in range(CONV_WIDTH))
            y = xm1 * w0 + xg * w1 + xp1 * w2 + cb_ref[...]
            y = jax.nn.silu(y)
            qc_ref[r0:r0 + R, :] = (y[:, :M_W] * (HEAD_DIM_M ** -0.5)).astype(BF16)
            k_scr[r0:r0 + R, :] = y[:, M_W:]
            yield
            if (r0 + R) % CHUNK == 0:
                g = r0 // CHUNK
                for hd in range(N_HEADS_M):
                    blk = k_scr[g * CHUNK:(g + 1) * CHUNK, hd * HEAD_DIM_M:(hd + 1) * HEAD_DIM_M]
                    base = (g * N_HEADS_M + hd) * HEAD_DIM_M
                    kct_ref[base:base + HEAD_DIM_M, :] = blk.T.astype(BF16)
                yield

    _alternate(project(), conv())
    carry_scr[0:1, :] = raw_scr[tm - 1:tm, :]
    raw_scr[...] = new_scr[...]


def _inproj_call(x, mod, g1, w_in_p, conv_w, conv_b, tm):
    bsz, s, d = x.shape
    nblk = s // tm
    ntiles = bsz * nblk
    outs = (
        jax.ShapeDtypeStruct((bsz, s, QKVA_W), BF16),
        jax.ShapeDtypeStruct((bsz, s, M_W), BF16),
        jax.ShapeDtypeStruct((bsz, s * N_HEADS_M, HEAD_DIM_M), BF16),
        jax.ShapeDtypeStruct((bsz, s, M_W), BF16),
        jax.ShapeDtypeStruct((bsz, s, M_W), F32),
        jax.ShapeDtypeStruct((bsz, N_GATES, s), F32),
    )

    def cur(j):
        t = jnp.minimum(j, ntiles - 1)
        return t // nblk, t % nblk

    def old(j):
        t = jnp.maximum(j - 1, 0)
        return t // nblk, t % nblk

    cur_tile = lambda w: pl.BlockSpec((None, tm, w), lambda j: (*cur(j), 0))
    old_tile = lambda w: pl.BlockSpec((None, tm, w), lambda j: (*old(j), 0))
    const = lambda shp: pl.BlockSpec(shp, lambda j: (0, 0))
    return pl.pallas_call(
        functools.partial(_inproj_kernel, nblk=nblk),
        grid=(ntiles + 1,),
        in_specs=[
            cur_tile(d),
            pl.BlockSpec((None, 1, N_MOD * d), lambda j: (cur(j)[0], 0, 0)),
            const((1, d)),
            pl.BlockSpec((d, IN_COLS_PAD), lambda j: (0, 0), pipeline_mode=pl.Buffered(1)),
            const((1, CONV_WIDTH * 2 * M_W)), const((1, 2 * M_W)),
        ],
        out_specs=[cur_tile(QKVA_W), old_tile(M_W),
                   pl.BlockSpec((None, tm * N_HEADS_M, HEAD_DIM_M), lambda j: (*old(j), 0)),
                   cur_tile(M_W), cur_tile(M_W),
                   pl.BlockSpec((None, N_GATES, tm), lambda j: (cur(j)[0], 0, cur(j)[1]))],
        out_shape=outs,
        scratch_shapes=[pltpu.VMEM((tm, 2 * M_W), F32), pltpu.VMEM((tm, 2 * M_W), F32),
                        pltpu.VMEM((tm, M_W), F32), pltpu.VMEM((8, 2 * M_W), F32)],
        compiler_params=pltpu.CompilerParams(
            dimension_semantics=("arbitrary",), vmem_limit_bytes=VMEM_LIMIT),
        name="inproj",
    )(x, mod, g1, w_in_p, conv_w, conv_b)


def _mlstm_reset(first, c_scr, m_scr):
    @pl.when(first)
    def _init():
        c_scr[...] = jnp.zeros_like(c_scr)
        m_scr[...] = jnp.zeros_like(m_scr)


LOG2E = 1.4426950408889634


def _mlstm_stream(q_ref, kt_ref, v_ref, gt_ref, bgr_ref, h_ref, c_scr, m_scr, *, reverse, nchunk):
    L = CHUNK
    ti = lax.broadcasted_iota(jnp.int32, (L, L), 0)
    si = lax.broadcasted_iota(jnp.int32, (L, L), 1)
    causal = (si >= ti) if reverse else (si <= ti)
    tri_t = jnp.where((ti >= si) if reverse else (ti <= si), 1.0, 0.0).astype(BF16)
    lane = lax.broadcasted_iota(jnp.int32, (1, L), 1)
    last = 0 if reverse else L - 1
    i_off, f_off = (2 * N_HEADS_M, 3 * N_HEADS_M) if reverse else (0, N_HEADS_M)
    ones = jnp.ones((L, HEAD_DIM_M), BF16)
    order = list(range(nchunk - 1, -1, -1) if reverse else range(nchunk))
    rows_of = lambda g: slice(g * L, (g + 1) * L)

    def split3(a):
        hi = a.astype(BF16)
        r1 = a - hi.astype(F32)
        mid = r1.astype(BF16)
        return hi, mid, (r1 - mid.astype(F32)).astype(BF16)

    gates = {}
    pad = jnp.zeros((L - N_GATES, L), F32)
    for g in order:
        gr = gt_ref[:, rows_of(g)] + bgr_ref[...]
        parts = _dot(jnp.concatenate(split3(jax.nn.log_sigmoid(gr)), axis=0), tri_t)
        bcr = (parts[0:N_GATES] + parts[N_GATES:2 * N_GATES] + parts[2 * N_GATES:3 * N_GATES]) * LOG2E
        gates[g] = (gr * LOG2E, bcr, jnp.concatenate([-bcr, pad], axis=0).T)
        yield

    def head(g, h):
        rows = rows_of(g)
        cols = slice(h * HEAD_DIM_M, (h + 1) * HEAD_DIM_M)
        gct, bct, nbcum = gates[g]
        r_row = gct[i_off + h:i_off + h + 1, :] - bct[f_off + h:f_off + h + 1, :]
        btot = jnp.sum(jnp.where(lane == last, bct[f_off + h:f_off + h + 1, :], 0.0),
                       axis=1, keepdims=True)
        rmax = jnp.max(r_row, axis=1, keepdims=True)
        a_max = btot + rmax
        m_prev = m_scr[h:h + 1, 0:1]
        m_new = jnp.maximum(btot + m_prev, a_max)
        m_scr[h:h + 1, :] = jnp.broadcast_to(m_new, (1, LANES))
        decay = jnp.exp2(btot + m_prev - m_new)
        w_row = jnp.exp2(r_row - rmax) * jnp.exp2(a_max - m_new)
        kt_rows = slice((g * N_HEADS_M + h) * HEAD_DIM_M, (g * N_HEADS_M + h + 1) * HEAD_DIM_M)
        cm = jnp.max(jnp.where(causal, r_row, -jnp.inf), axis=1, keepdims=True)
        yield
        q = q_ref[rows, cols]
        s = _dot(q, kt_ref[kt_rows, :])
        u = jnp.maximum(cm, m_prev)
        ub = jnp.broadcast_to(u, (L, L))
        dm = jnp.exp2(jnp.where(causal, r_row - ub, -jnp.inf))
        qi = q * jnp.exp2(m_prev - ub).astype(BF16)
        floor = jnp.exp2(jnp.broadcast_to(nbcum[:, f_off + h:f_off + h + 1], (L, L)) - ub)
        yield
        vaug = jnp.concatenate([v_ref[rows, cols], ones], axis=1)
        lhs = jnp.concatenate([(s * dm).astype(BF16), qi], axis=1)
        rhs = jnp.concatenate([vaug, c_scr[h].astype(BF16)], axis=0)
        out = _dot(lhs, rhs)
        upd = _dot(kt_ref[kt_rows, :] * w_row.astype(BF16), vaug)
        yield
        c_scr[h] = decay * c_scr[h] + upd
        h_ref[rows, cols] = out[:, :HEAD_DIM_M] / jnp.maximum(jnp.abs(out[:, HEAD_DIM_M:]), floor)
        yield

    n_stage, lag = 4, 2
    heads = {}
    for slot in range(lag * (nchunk - 1) + n_stage):
        for i, g in enumerate(order):
            stage = slot - lag * i
            if 0 <= stage < n_stage:
                for h in range(N_HEADS_M):
                    if stage == 0:
                        heads[g, h] = head(g, h)
                    next(heads[g, h])
                    yield


def _mixer_kernel(qf_ref, kf_ref, vf_ref, gtf_ref, qb_ref, kb_ref, vb_ref, gtb_ref, bgr_ref,
                  sink_ref, qa_ref, kvp_ref, kvn_ref, ga_ref, *rest, nblk, nchunk, ncast):
    w_refs, rest = rest[:ncast], rest[ncast:]
    hf_ref, hb_ref, att_ref = rest[:3]
    wo_refs, (cf_scr, mf_scr, cb_scr, mb_scr, bias_scr) = rest[3:3 + ncast], rest[3 + ncast:]
    b = pl.program_id(0)
    j = pl.program_id(1)

    @pl.when((b == 0) & (j == 0))
    def _init():
        _attn_bias_init(bias_scr)

    _mlstm_reset(j == 0, cf_scr, mf_scr)
    _mlstm_reset(j == 0, cb_scr, mb_scr)
    for w_ref, wo_ref in zip(w_refs, wo_refs):
        wo_ref[...] = w_ref[...].astype(wo_ref.dtype)
    fwd = _mlstm_stream(qf_ref, kf_ref, vf_ref, gtf_ref, bgr_ref, hf_ref, cf_scr, mf_scr,
                        reverse=False, nchunk=nchunk)
    bwd = _mlstm_stream(qb_ref, kb_ref, vb_ref, gtb_ref, bgr_ref, hb_ref, cb_scr, mb_scr,
                        reverse=True, nchunk=nchunk)
    att = _attn_pieces(j == 0, j == nblk - 1, sink_ref, qa_ref, kvp_ref, kvn_ref, ga_ref, att_ref,
                       bias_scr, nsub=nchunk)
    _alternate(fwd, bwd)
    for _ in att:
        pass


def _mixer_call(qc, kct, vm, gates_t, bg_rows, sink, qkva, g_attn, weights, layer, tm):
    bsz, s, _ = qc.shape
    nblk = s // tm
    nsteps = bsz * nblk
    nsub = tm // BLOCK
    nb = s // BLOCK
    kvw = QKVA_W - ATT_Q
    attn_specs = [
        pl.BlockSpec(memory_space=pltpu.SMEM),
        pl.BlockSpec((None, tm, QKVA_W), lambda b, j: (b, j, 0)),
        pl.BlockSpec((None, BLOCK, kvw), lambda b, j: (b, jnp.maximum(j * nsub - 1, 0), 1)),
        pl.BlockSpec((None, BLOCK, kvw), lambda b, j: (b, jnp.minimum((j + 1) * nsub, nb - 1), 1)),
        pl.BlockSpec((1, ATT_Q), lambda b, j: (0, 0)),
    ]
    w_specs, w_shapes = [], []
    for w in weights:
        _, k, n = w.shape
        rows = k // nsteps
        assert rows * nsteps == k and rows % 16 == 0
        w_specs.append(pl.BlockSpec((None, rows, n), lambda b, j: (layer, b * nblk + j, 0)))
        w_shapes.append(jax.ShapeDtypeStruct((k, n), BF16))
    wo_specs = [pl.BlockSpec((sp.block_shape[1], sp.block_shape[2]), lambda b, j: (b * nblk + j, 0))
                for sp in w_specs]

    def specs(pos):
        tile = lambda w: pl.BlockSpec((None, tm, w), lambda b, j: (b, pos(j), 0))
        kt_spec = pl.BlockSpec((None, tm * N_HEADS_M, HEAD_DIM_M), lambda b, j: (b, pos(j), 0))
        gt_spec = pl.BlockSpec((None, N_GATES, tm), lambda b, j: (b, 0, pos(j)))
        return [tile(M_W), kt_spec, tile(M_W), gt_spec], tile(M_W)

    in_f, out_f = specs(lambda j: j)
    in_b, out_b = specs(lambda j: nblk - 1 - j)
    state = [pltpu.VMEM((N_HEADS_M, HEAD_DIM_M, 2 * HEAD_DIM_M), F32), pltpu.VMEM((8, LANES), F32)]
    outs = pl.pallas_call(
        functools.partial(_mixer_kernel, nblk=nblk, nchunk=tm // CHUNK, ncast=len(weights)),
        grid=(bsz, nblk),
        in_specs=in_f + in_b + [pl.BlockSpec((N_GATES, LANES), lambda b, j: (0, 0))] + attn_specs
        + w_specs,
        out_specs=[out_f, out_b, pl.BlockSpec((None, tm, ATT_Q), lambda b, j: (b, j, 0))] + wo_specs,
        out_shape=[jax.ShapeDtypeStruct((bsz, s, M_W), F32)] * 2
        + [jax.ShapeDtypeStruct((bsz, s, ATT_Q), BF16)] + w_shapes,
        scratch_shapes=state + state + [pltpu.VMEM((3 * N_HEADS_ATT, BLOCK, 3 * BLOCK), F32)],
        compiler_params=pltpu.CompilerParams(
            dimension_semantics=("arbitrary", "arbitrary"), vmem_limit_bytes=VMEM_LIMIT),
        name="mixer",
    )(qc, kct, vm, gates_t, qc, kct, vm, gates_t, bg_rows, sink, qkva, qkva, qkva, g_attn, *weights)
    return outs[0], outs[1], outs[2], outs[3:]


def _zero_after(a):
    bits = pltpu.bitcast(a, jnp.uint32)
    z = lax.shift_right_logical(lax.shift_right_logical(bits, jnp.uint32(16)), jnp.uint32(16))
    return pltpu.bitcast(z, F32)


def _attn_bias_init(bias_scr):
    nk = 3 * BLOCK
    row = lax.broadcasted_iota(jnp.int32, (BLOCK, nk), 0)
    col = lax.broadcasted_iota(jnp.int32, (BLOCK, nk), 1)
    dist = jnp.abs(col - BLOCK - row)
    distf = dist.astype(F32)
    for var in range(3):
        ok = dist <= WINDOW
        if var == 1:
            ok = ok & (col >= BLOCK)
        elif var == 2:
            ok = ok & (col < 2 * BLOCK)
        for h in range(N_HEADS_ATT):
            slope = 2.0 ** (-8.0 * (h + 1.0) / N_HEADS_ATT)
            bias_scr[var * N_HEADS_ATT + h] = jnp.where(ok, (-slope * LOG2E) * distf, -jnp.inf)


def _attn_pieces(first, last, sink_ref, q_ref, kvp_ref, kvn_ref, g_ref, o_ref, bias_scr, *, nsub):
    nk = 3 * BLOCK

    lane_k = lax.broadcasted_iota(jnp.int32, (nk, LANES), 1)
    ones_a = jnp.where(lane_k < HEAD_DIM_ATT, 1.0, 0.0).astype(BF16)
    ones_b = jnp.where(lane_k < HEAD_DIM_ATT, 0.0, 1.0).astype(BF16)
    lo_half_q = lax.broadcasted_iota(jnp.int32, (BLOCK, LANES), 1) < HEAD_DIM_ATT

    def kv_block(idx):
        if idx < 0:
            return kvp_ref[...]
        if idx >= nsub:
            return kvn_ref[...]
        return q_ref[idx * BLOCK:(idx + 1) * BLOCK, ATT_Q:QKVA_W]

    for n in range(nsub):
        rows = slice(n * BLOCK, (n + 1) * BLOCK)
        kv = jnp.concatenate([kv_block(n - 1), kv_block(n), kv_block(n + 1)], axis=0)
        if n == 0:
            var = jnp.where(first, 1, 0)
        elif n == nsub - 1:
            var = jnp.where(last, 2, 0)
        else:
            var = 0
        pieces = []
        for kvh in range(N_KV_HEADS):
            k_st, k_sw = kv[:, 0:LANES], kv[:, LANES:2 * LANES]
            v_st, v_sw = kv[:, 2 * LANES:3 * LANES], kv[:, 3 * LANES:4 * LANES]
            if kvh == 0:
                k_lo, k_hi, v_lo, v_hi = k_st, k_sw, v_st, v_sw
            else:
                k_lo, k_hi, v_lo, v_hi = k_sw, k_st, v_sw, v_st
            kk = jnp.concatenate([k_lo * ones_a, k_hi * ones_b], axis=0)
            vv = jnp.concatenate([
                jnp.concatenate([v_lo * ones_a, ones_a], axis=1),
                jnp.concatenate([v_hi * ones_b, ones_b], axis=1)], axis=0)
            for pair in range(GROUP_SIZE // 2):
                h0 = kvh * GROUP_SIZE + 2 * pair
                qp = q_ref[rows, h0 * HEAD_DIM_ATT:(h0 + 2) * HEAD_DIM_ATT]
                s2 = _dot_nt(qp, kk)
                ps, es = [], []
                for t in range(2):
                    logits = s2[:, t * nk:(t + 1) * nk] + bias_scr[var * N_HEADS_ATT + h0 + t]
                    sink = sink_ref[h0 + t] * LOG2E
                    mx = jnp.maximum(jnp.max(logits, axis=-1, keepdims=True), sink)
                    ps.append(jnp.exp2(logits - mx).astype(BF16))
                    es.append(jnp.exp2(sink - mx))
                res = _dot(jnp.concatenate(ps, axis=1), vv)
                den = res[:, LANES:] + jnp.where(lo_half_q, es[0], es[1])
                pieces.append(res[:, :LANES] / den)
                yield
        att = jnp.concatenate(pieces, axis=1)
        o_ref[rows, :] = _rms(att, g_ref[...]).astype(o_ref.dtype)
        yield


def _outffn_body(x_ref, att_ref, hf_ref, hb_ref, om_ref, mod_ref, gm_ref, wo_ref, g2_ref,
                 w1_ref, w2_ref, gf_ref, o_ref, hid_scr, *, final):
    hs = hf_ref[...] + hb_ref[...]
    parts = []
    for h in range(N_HEADS_M):
        cols = slice(h * HEAD_DIM_M, (h + 1) * HEAD_DIM_M)
        parts.append(_rms(hs[:, cols], gm_ref[:, cols]))
    hm = jax.nn.sigmoid(om_ref[...]) * jnp.concatenate(parts, axis=1)
    mixin = jnp.concatenate([att_ref[...], hm.astype(BF16)], axis=1)
    x1 = x_ref[...] + _mod(mod_ref, 2) * _dot(mixin, wo_ref[...])
    hff = (_rms(x1, g2_ref[...] * (1.0 + _mod(mod_ref, 4))) + _mod(mod_ref, 3)).astype(BF16)
    for c in range(N_FF_CHUNKS):
        gate = _dot(hff, w1_ref[:, FF_CHUNK * c:FF_CHUNK * (c + 1)])
        up = _dot(hff, w1_ref[:, D_FF + FF_CHUNK * c:D_FF + FF_CHUNK * (c + 1)])
        hid_scr[:, FF_CHUNK * c:FF_CHUNK * (c + 1)] = (jax.nn.silu(gate) * up).astype(BF16)
    x2 = x1 + _mod(mod_ref, 5) * _dot(hid_scr[...], w2_ref[...])
    if final:
        x2 = _rms(x2, gf_ref[...])
    o_ref[...] = x2


def _outffn_kernel(*refs, final):
    _outffn_body(*refs, final=final)


def _outffn_call(x, att, hf, hb, om, mod, g_m, w_out, g2, w1, w2, g_final, tm, final):
    bsz, s, d = x.shape
    tile = lambda w: pl.BlockSpec((None, tm, w), lambda b, i: (b, i, 0))
    const = lambda shp: pl.BlockSpec(shp, lambda b, i: (0, 0))
    weight = lambda shp: pl.BlockSpec(shp, lambda b, i: (0, 0), pipeline_mode=pl.Buffered(1))
    return pl.pallas_call(
        functools.partial(_outffn_kernel, final=final),
        grid=(bsz, s // tm),
        in_specs=[
            tile(d), tile(ATT_Q), tile(M_W), tile(M_W), tile(M_W),
            pl.BlockSpec((None, 1, N_MOD * d), lambda b, i: (b, 0, 0)),
            const((1, M_W)), weight((ATT_Q + M_W, d)), const((1, d)),
            weight((d, 2 * D_FF)), weight((D_FF, d)), const((1, d)),
        ],
        out_specs=tile(d),
        out_shape=jax.ShapeDtypeStruct((bsz, s, d), F32),
        scratch_shapes=[pltpu.VMEM((tm, D_FF), BF16)],
        compiler_params=pltpu.CompilerParams(
            dimension_semantics=("arbitrary", "arbitrary"), vmem_limit_bytes=VMEM_LIMIT),
        name="outffn",
    )(x, att, hf, hb, om, mod, g_m, w_out, g2, w1, w2, g_final)


def _layer(l, x, c, w_mod, b_mod, g_norm1, w_in, conv_w, conv_b, b_gates, sink,
           g_attn_out, g_mlstm_out, w_out, g_norm2, w_ffn_in, w_ffn_out, g_final, final):
    d = x.shape[-1]
    mod, w_in_p = _mod_call(c, w_mod, b_mod[l], w_in, l, IN_COLS_PAD)
    qkva, qc, kct, vm, om, gates_t = _inproj_call(
        x, mod, g_norm1[l].reshape(1, d), w_in_p, conv_w[l].reshape(1, CONV_WIDTH * 2 * M_W),
        conv_b[l].reshape(1, 2 * M_W), tm=512)

    bg_rows = jnp.broadcast_to(b_gates[l][:, None], (N_GATES, LANES))
    hf, hb, att, (w_out_b, w_ffn_in_b, w_ffn_out_b) = _mixer_call(
        qc, kct, vm, gates_t, bg_rows, sink[l], qkva, g_attn_out[l].reshape(1, ATT_Q),
        (w_out, w_ffn_in, w_ffn_out), l, tm=1024)

    return _outffn_call(x, att, hf, hb, om, mod, g_mlstm_out[l].reshape(1, M_W), w_out_b,
                        g_norm2[l].reshape(1, d), w_ffn_in_b, w_ffn_out_b,
                        g_final.reshape(1, d), tm=512, final=final)


def kernel(x, c, w_mod, b_mod, g_norm1, w_in, conv_w, conv_b, b_gates, sink, g_attn_out,
           g_mlstm_out, w_out, g_norm2, w_ffn_in, w_ffn_out, g_final):
    depth = w_mod.shape[0]
    for l in range(depth):
        x = _layer(l, x, c, w_mod, b_mod, g_norm1, w_in, conv_w, conv_b, b_gates, sink, g_attn_out,
                   g_mlstm_out, w_out, g_norm2, w_ffn_in, w_ffn_out, g_final, final=(l == depth - 1))
    return x
```

```python
import functools

import jax
import jax.numpy as jnp
from jax import lax
from jax.experimental import pallas as pl
from jax.experimental.pallas import tpu as pltpu

F32 = jnp.float32
BF16 = jnp.bfloat16

D_MODEL = 1024
EPS = 1e-6
N_HEADS_ATT = 8
N_KV_HEADS = 2
HEAD_DIM_ATT = 64
GROUP_SIZE = N_HEADS_ATT // N_KV_HEADS
WINDOW = 128
BLOCK = 128
N_HEADS_M = 4
HEAD_DIM_M = 128
CHUNK = 128
CONV_WIDTH = 3
ATT_Q = N_HEADS_ATT * HEAD_DIM_ATT
ATT_KV = N_KV_HEADS * HEAD_DIM_ATT
M_W = N_HEADS_M * HEAD_DIM_M
N_GATES = 4 * N_HEADS_M
D_FF = 2816
N_MOD = 6

LANES = 128
GATE_PAD = LANES
FF_CHUNK = 256
N_FF_CHUNKS = D_FF // FF_CHUNK
VMEM_LIMIT = 56 * 1024 * 1024

C_QA = 0
C_KA = ATT_Q
C_VA = ATT_Q + ATT_KV
C_QKM = ATT_Q + 2 * ATT_KV
C_VM = C_QKM + 2 * M_W
C_OM = C_VM + M_W
C_G = C_OM + M_W
IN_COLS_PAD = C_G + GATE_PAD
QKVA_W = ATT_Q + 4 * ATT_KV


def _dot(a, b):
    return jnp.dot(a, b, preferred_element_type=F32)


def _dot_nt(a, b):
    return lax.dot_general(a, b, (((1,), (1,)), ((), ())), preferred_element_type=F32)


def _rms(x, g):
    return x * lax.rsqrt(jnp.mean(x * x, axis=-1, keepdims=True) + EPS) * g


def _mod(mod_ref, k):
    return mod_ref[:, k * D_MODEL:(k + 1) * D_MODEL]


def _alternate(*streams):
    live = list(streams)
    while live:
        for s in list(live):
            if next(s, StopIteration) is StopIteration:
                live.remove(s)


def _cast_t_kernel(wt_ref, o_ref):
    n = wt_ref.shape[0]
    full = n // LANES * LANES
    for r in range(0, full, LANES):
        o_ref[:, r:r + LANES] = wt_ref[r:r + LANES, :].T.astype(o_ref.dtype)
    if o_ref.shape[1] > full:
        tail = jnp.concatenate(
            [wt_ref[full:n, :], jnp.zeros((full + LANES - n, wt_ref.shape[1]), wt_ref.dtype)], axis=0)
        o_ref[:, full:full + LANES] = tail.T.astype(o_ref.dtype)


SUBLANES = 8
MOD_STEPS = 8


def _mod_kernel(c_ref, w_ref, b_ref, wt_ref, o_ref, wo_ref):
    bsz, d = c_ref.shape
    c = jnp.concatenate([c_ref[...], jnp.zeros((SUBLANES - bsz, d), F32)], axis=0)
    res = _dot(jax.nn.silu(c).astype(BF16), w_ref[...].astype(BF16)) + b_ref[...]
    for b in range(bsz):
        o_ref[b] = res[b:b + 1, :]
    _cast_t_kernel(wt_ref, wo_ref)


def _mod_call(c, w_mod, b_mod, w_in, layer, n_out):
    bsz, d = c.shape
    assert bsz <= SUBLANES
    n_mod = w_mod.shape[2]
    _, k, n = w_in.shape
    assert n_out - n < LANES and n_out % LANES == 0
    bn, bk = n_mod // MOD_STEPS, k // MOD_STEPS
    assert bn * MOD_STEPS == n_mod and bk * MOD_STEPS == k and bn % LANES == 0 and bk % LANES == 0
    return pl.pallas_call(
        _mod_kernel,
        grid=(MOD_STEPS,),
        in_specs=[
            pl.BlockSpec((bsz, d), lambda i: (0, 0)),
            pl.BlockSpec((None, d, bn), lambda i: (layer, 0, i)),
            pl.BlockSpec((1, bn), lambda i: (0, i)),
            pl.BlockSpec((None, n, bk), lambda i: (layer, 0, i)),
        ],
        out_specs=[pl.BlockSpec((bsz, 1, bn), lambda i: (0, 0, i)),
                   pl.BlockSpec((bk, n_out), lambda i: (i, 0))],
        out_shape=[jax.ShapeDtypeStruct((bsz, 1, n_mod), F32), jax.ShapeDtypeStruct((k, n_out), BF16)],
        compiler_params=pltpu.CompilerParams(
            dimension_semantics=("arbitrary",), vmem_limit_bytes=VMEM_LIMIT),
        name="mod",
    )(c, w_mod, b_mod.reshape(1, n_mod), jnp.swapaxes(w_in, 1, 2))


MXU_COLS = 256
CONV_ROWS = 64


def _inproj_kernel(x_ref, mod_ref, g_ref, w_ref, cw_ref, cb_ref,
                   qkva_ref, qc_ref, kct_ref, vm_ref, om_ref, gt_ref,
                   raw_scr, new_scr, k_scr, carry_scr, *, nblk):
    j = pl.program_id(0)

    @pl.when(j == 0)
    def _init():
        raw_scr[...] = jnp.zeros_like(raw_scr)
        carry_scr[...] = jnp.zeros_like(carry_scr)

    x = x_ref[...]
    h = _rms(x, g_ref[...] * (1.0 + _mod(mod_ref, 1))) + _mod(mod_ref, 0)
    hb = h.astype(BF16)
    tm = x.shape[0]
    nchunk = tm // CHUNK
    within = (j + nblk - 1) % nblk

    anchors = []

    def zero_after(a):
        return jnp.concatenate([_zero_after(a)] * (2 * M_W // LANES), axis=1)

    def project():
        order = list(range(C_QKM, C_VM, MXU_COLS)) + list(range(0, C_QKM, MXU_COLS)) + \
            list(range(C_VM, IN_COLS_PAD, MXU_COLS))
        for c0 in order:
            c1 = min(c0 + MXU_COLS, IN_COLS_PAD)
            res = _dot(hb, w_ref[:, c0:c1])
            anchors.append(res[tm - 1:tm, c1 - c0 - LANES:c1 - c0])
            if c0 < C_KA:
                qkva_ref[:, c0:c1] = (res * (HEAD_DIM_ATT ** -0.5 * LOG2E)).astype(BF16)
            elif c0 < C_QKM:
                ka, va = res[:, :ATT_KV], res[:, ATT_KV:]
                half = HEAD_DIM_ATT
                qkva_ref[:, ATT_Q:ATT_Q + ATT_KV] = ka.astype(BF16)
                qkva_ref[:, ATT_Q + ATT_KV:ATT_Q + 2 * ATT_KV] = pltpu.roll(ka, half, axis=1).astype(BF16)
                qkva_ref[:, ATT_Q + 2 * ATT_KV:ATT_Q + 3 * ATT_KV] = va.astype(BF16)
                qkva_ref[:, ATT_Q + 3 * ATT_KV:ATT_Q + 4 * ATT_KV] = pltpu.roll(va, half, axis=1).astype(BF16)
            elif c0 < C_VM:
                new_scr[:, c0 - C_QKM:c1 - C_QKM] = res
            elif c0 < C_OM:
                vm_ref[:, c0 - C_VM:c1 - C_VM] = res.astype(BF16)
            elif c0 < C_G:
                om_ref[:, c0 - C_OM:c1 - C_OM] = res
            else:
                for g in range(nchunk):
                    blk = res[g * CHUNK:(g + 1) * CHUNK, :].T
                    gt_ref[:, g * CHUNK:(g + 1) * CHUNK] = blk[0:N_GATES, :]
            yield

    def conv():
        R = CONV_ROWS
        row = lax.broadcasted_iota(jnp.int32, (R, 1), 0)
        for p in range(tm // R):
            r0 = p * R
            xg = raw_scr[r0:r0 + R, :]
            if p == 0:
                prev_row = jnp.where(within == 0, 0.0, carry_scr[0:1, :])
            else:
                prev_row = raw_scr[r0 - 1:r0, :]
            if r0 + R == tm:
                next_row = jnp.where(within == nblk - 1, 0.0, new_scr[0:1, :])
            else:
                next_row = raw_scr[r0 + R:r0 + R + 1, :]
            xm1 = jnp.where(row == 0, prev_row, pltpu.roll(xg, 1, axis=0))
            xp1 = jnp.where(row == R - 1, next_row, pltpu.roll(xg, R - 1, axis=0))
            z = zero_after(anchors[-1])
            w0, w1, w2 = (cw_ref[:, t * 2 * M_W:(t + 1) * 2 * M_W] + z for t in range(CONV_WIDTH))
            y = xm1 * w0 + xg * w1 + xp1 * w2 + cb_ref[...]
            y = jax.nn.silu(y)
            qc_ref[r0:r0 + R, :] = (y[:, :M_W] * (HEAD_DIM_M ** -0.5)).astype(BF16)
            k_scr[r0:r0 + R, :] = y[:, M_W:]
            yield
            if (r0 + R) % CHUNK == 0:
                g = r0 // CHUNK
                for hd in range(N_HEADS_M):
                    blk = k_scr[g * CHUNK:(g + 1) * CHUNK, hd * HEAD_DIM_M:(hd + 1) * HEAD_DIM_M]
                    base = (g * N_HEADS_M + hd) * HEAD_DIM_M
                    kct_ref[base:base + HEAD_DIM_M, :] = blk.T.astype(BF16)
                yield

    _alternate(project(), conv())
    carry_scr[0:1, :] = raw_scr[tm - 1:tm, :]
    raw_scr[...] = new_scr[...]


def _inproj_call(x, mod, g1, w_in_p, conv_w, conv_b, tm):
    bsz, s, d = x.shape
    nblk = s // tm
    ntiles = bsz * nblk
    outs = (
        jax.ShapeDtypeStruct((bsz, s, QKVA_W), BF16),
        jax.ShapeDtypeStruct((bsz, s, M_W), BF16),
        jax.ShapeDtypeStruct((bsz, s * N_HEADS_M, HEAD_DIM_M), BF16),
        jax.ShapeDtypeStruct((bsz, s, M_W), BF16),
        jax.ShapeDtypeStruct((bsz, s, M_W), F32),
        jax.ShapeDtypeStruct((bsz, N_GATES, s), F32),
    )

    def cur(j):
        t = jnp.minimum(j, ntiles - 1)
        return t // nblk, t % nblk

    def old(j):
        t = jnp.maximum(j - 1, 0)
        return t // nblk, t % nblk

    cur_tile = lambda w: pl.BlockSpec((None, tm, w), lambda j: (*cur(j), 0))
    old_tile = lambda w: pl.BlockSpec((None, tm, w), lambda j: (*old(j), 0))
    const = lambda shp: pl.BlockSpec(shp, lambda j: (0, 0))
    return pl.pallas_call(
        functools.partial(_inproj_kernel, nblk=nblk),
        grid=(ntiles + 1,),
        in_specs=[
            cur_tile(d),
            pl.BlockSpec((None, 1, N_MOD * d), lambda j: (cur(j)[0], 0, 0)),
            const((1, d)),
            pl.BlockSpec((d, IN_COLS_PAD), lambda j: (0, 0), pipeline_mode=pl.Buffered(1)),
            const((1, CONV_WIDTH * 2 * M_W)), const((1, 2 * M_W)),
        ],
        out_specs=[cur_tile(QKVA_W), old_tile(M_W),
                   pl.BlockSpec((None, tm * N_HEADS_M, HEAD_DIM_M), lambda j: (*old(j), 0)),
                   cur_tile(M_W), cur_tile(M_W),
                   pl.BlockSpec((None, N_GATES, tm), lambda j: (cur(j)[0], 0, cur(j)[1]))],
        out_shape=outs,
        scratch_shapes=[pltpu.VMEM((tm, 2 * M_W), F32), pltpu.VMEM((tm, 2 * M_W), F32),
                        pltpu.VMEM((tm, M_W), F32), pltpu.VMEM((8, 2 * M_W), F32)],
        compiler_params=pltpu.CompilerParams(
            dimension_semantics=("arbitrary",), vmem_limit_bytes=VMEM_LIMIT),
        name="inproj",
    )(x, mod, g1, w_in_p, conv_w, conv_b)


def _mlstm_reset(first, c_scr, m_scr):
    @pl.when(first)
    def _init():
        c_scr[...] = jnp.zeros_like(c_scr)
        m_scr[...] = jnp.zeros_like(m_scr)


LOG2E = 1.4426950408889634


def _mlstm_stream(q_ref, kt_ref, v_ref, gt_ref, bgr_ref, h_ref, c_scr, m_scr, *, reverse, nchunk):
    L = CHUNK
    ti = lax.broadcasted_iota(jnp.int32, (L, L), 0)
    si = lax.broadcasted_iota(jnp.int32, (L, L), 1)
    causal = (si >= ti) if reverse else (si <= ti)
    tri_t = jnp.where((ti >= si) if reverse else (ti <= si), 1.0, 0.0).astype(BF16)
    lane = lax.broadcasted_iota(jnp.int32, (1, L), 1)
    last = 0 if reverse else L - 1
    i_off, f_off = (2 * N_HEADS_M, 3 * N_HEADS_M) if reverse else (0, N_HEADS_M)
    ones = jnp.ones((L, HEAD_DIM_M), BF16)
    order = list(range(nchunk - 1, -1, -1) if reverse else range(nchunk))
    rows_of = lambda g: slice(g * L, (g + 1) * L)

    def split3(a):
        hi = a.astype(BF16)
        r1 = a - hi.astype(F32)
        mid = r1.astype(BF16)
        return hi, mid, (r1 - mid.astype(F32)).astype(BF16)

    gates = {}
    pad = jnp.zeros((L - N_GATES, L), F32)
    for g in order:
        gr = gt_ref[:, rows_of(g)] + bgr_ref[...]
        parts = _dot(jnp.concatenate(split3(jax.nn.log_sigmoid(gr)), axis=0), tri_t)
        bcr = (parts[0:N_GATES] + parts[N_GATES:2 * N_GATES] + parts[2 * N_GATES:3 * N_GATES]) * LOG2E
        gates[g] = (gr * LOG2E, bcr, jnp.concatenate([-bcr, pad], axis=0).T)
        yield

    def head(g, h):
        rows = rows_of(g)
        cols = slice(h * HEAD_DIM_M, (h + 1) * HEAD_DIM_M)
        gct, bct, nbcum = gates[g]
        r_row = gct[i_off + h:i_off + h + 1, :] - bct[f_off + h:f_off + h + 1, :]
        btot = jnp.sum(jnp.where(lane == last, bct[f_off + h:f_off + h + 1, :], 0.0),
                       axis=1, keepdims=True)
        rmax = jnp.max(r_row, axis=1, keepdims=True)
        a_max = btot + rmax
        m_prev = m_scr[h:h + 1, 0:1]
        m_new = jnp.maximum(btot + m_prev, a_max)
        m_scr[h:h + 1, :] = jnp.broadcast_to(m_new, (1, LANES))
        decay = jnp.exp2(btot + m_prev - m_new)
        w_row = jnp.exp2(r_row - rmax) * jnp.exp2(a_max - m_new)
        kt_rows = slice((g * N_HEADS_M + h) * HEAD_DIM_M, (g * N_HEADS_M + h + 1) * HEAD_DIM_M)
        cm = jnp.max(jnp.where(causal, r_row, -jnp.inf), axis=1, keepdims=True)
        yield
        q = q_ref[rows, cols]
        s = _dot(q, kt_ref[kt_rows, :])
        u = jnp.maximum(cm, m_prev)
        ub = jnp.broadcast_to(u, (L, L))
        dm = jnp.exp2(jnp.where(causal, r_row - ub, -jnp.inf))
        qi = q * jnp.exp2(m_prev - ub).astype(BF16)
        floor = jnp.exp2(jnp.broadcast_to(nbcum[:, f_off + h:f_off + h + 1], (L, L)) - ub)
        yield
        vaug = jnp.concatenate([v_ref[rows, cols], ones], axis=1)
        lhs = jnp.concatenate([(s * dm).astype(BF16), qi], axis=1)
        rhs = jnp.concatenate([vaug, c_scr[h].astype(BF16)], axis=0)
        out = _dot(lhs, rhs)
        upd = _dot(kt_ref[kt_rows, :] * w_row.astype(BF16), vaug)
        yield
        c_scr[h] = decay * c_scr[h] + upd
        h_ref[rows, cols] = out[:, :HEAD_DIM_M] / jnp.maximum(jnp.abs(out[:, HEAD_DIM_M:]), floor)
        yield

    n_stage, lag = 4, 2
    heads = {}
    for slot in range(lag * (nchunk - 1) + n_stage):
        for i, g in enumerate(order):
            stage = slot - lag * i
            if 0 <= stage < n_stage:
                for h in range(N_HEADS_M):
                    if stage == 0:
                        heads[g, h] = head(g, h)
                    next(heads[g, h])
                    yield


def _mixer_kernel(qf_ref, kf_ref, vf_ref, gtf_ref, qb_ref, kb_ref, vb_ref, gtb_ref, bgr_ref,
                  sink_ref, qa_ref, kvp_ref, kvn_ref, ga_ref, *rest, nblk, nchunk, ncast):
    w_refs, rest = rest[:ncast], rest[ncast:]
    hf_ref, hb_ref, att_ref = rest[:3]
    wo_refs, (cf_scr, mf_scr, cb_scr, mb_scr, bias_scr) = rest[3:3 + ncast], rest[3 + ncast:]
    b = pl.program_id(0)
    j = pl.program_id(1)

    @pl.when((b == 0) & (j == 0))
    def _init():
        _attn_bias_init(bias_scr)

    _mlstm_reset(j == 0, cf_scr, mf_scr)
    _mlstm_reset(j == 0, cb_scr, mb_scr)
    for w_ref, wo_ref in zip(w_refs, wo_refs):
        wo_ref[...] = w_ref[...].astype(wo_ref.dtype)
    fwd = _mlstm_stream(qf_ref, kf_ref, vf_ref, gtf_ref, bgr_ref, hf_ref, cf_scr, mf_scr,
                        reverse=False, nchunk=nchunk)
    bwd = _mlstm_stream(qb_ref, kb_ref, vb_ref, gtb_ref, bgr_ref, hb_ref, cb_scr, mb_scr,
                        reverse=True, nchunk=nchunk)
    att = _attn_pieces(j == 0, j == nblk - 1, sink_ref, qa_ref, kvp_ref, kvn_ref, ga_ref, att_ref,
                       bias_scr, nsub=nchunk)
    _alternate(fwd, bwd)
    for _ in att:
        pass


def _mixer_call(qc, kct, vm, gates_t, bg_rows, sink, qkva, g_attn, weights, layer, tm):
    bsz, s, _ = qc.shape
    nblk = s // tm
    nsteps = bsz * nblk
    nsub = tm // BLOCK
    nb = s // BLOCK
    kvw = QKVA_W - ATT_Q
    attn_specs = [
        pl.BlockSpec(memory_space=pltpu.SMEM),
        pl.BlockSpec((None, tm, QKVA_W), lambda b, j: (b, j, 0)),
        pl.BlockSpec((None, BLOCK, kvw), lambda b, j: (b, jnp.maximum(j * nsub - 1, 0), 1)),
        pl.BlockSpec((None, BLOCK, kvw), lambda b, j: (b, jnp.minimum((j + 1) * nsub, nb - 1), 1)),
        pl.BlockSpec((1, ATT_Q), lambda b, j: (0, 0)),
    ]
    w_specs, w_shapes = [], []
    for w in weights:
        _, k, n = w.shape
        rows = k // nsteps
        assert rows * nsteps == k and rows % 16 == 0
        w_specs.append(pl.BlockSpec((None, rows, n), lambda b, j: (layer, b * nblk + j, 0)))
        w_shapes.append(jax.ShapeDtypeStruct((k, n), BF16))
    wo_specs = [pl.BlockSpec((sp.block_shape[1], sp.block_shape[2]), lambda b, j: (b * nblk + j, 0))
                for sp in w_specs]

    def specs(pos):
        tile = lambda w: pl.BlockSpec((None, tm, w), lambda b, j: (b, pos(j), 0))
        kt_spec = pl.BlockSpec((None, tm * N_HEADS_M, HEAD_DIM_M), lambda b, j: (b, pos(j), 0))
        gt_spec = pl.BlockSpec((None, N_GATES, tm), lambda b, j: (b, 0, pos(j)))
        return [tile(M_W), kt_spec, tile(M_W), gt_spec], tile(M_W)

    in_f, out_f = specs(lambda j: j)
    in_b, out_b = specs(lambda j: nblk - 1 - j)
    state = [pltpu.VMEM((N_HEADS_M, HEAD_DIM_M, 2 * HEAD_DIM_M), F32), pltpu.VMEM((8, LANES), F32)]
    outs = pl.pallas_call(
        functools.partial(_mixer_kernel, nblk=nblk, nchunk=tm // CHUNK, ncast=len(weights)),
        grid=(bsz, nblk),
        in_specs=in_f + in_b + [pl.BlockSpec((N_GATES, LANES), lambda b, j: (0, 0))] + attn_specs
        + w_specs,
        out_specs=[out_f, out_b, pl.BlockSpec((None, tm, ATT_Q), lambda b, j: (b, j, 0))] + wo_specs,
        out_shape=[jax.ShapeDtypeStruct((bsz, s, M_W), F32)] * 2
        + [jax.ShapeDtypeStruct((bsz, s, ATT_Q), BF16)] + w_shapes,
        scratch_shapes=state + state + [pltpu.VMEM((3 * N_HEADS_ATT, BLOCK, 3 * BLOCK), F32)],
        compiler_params=pltpu.CompilerParams(
            dimension_semantics=("arbitrary", "arbitrary"), vmem_limit_bytes=VMEM_LIMIT),
        name="mixer",
    )(qc, kct, vm, gates_t, qc, kct, vm, gates_t, bg_rows, sink, qkva, qkva, qkva, g_attn, *weights)
    return outs[0], outs[1], outs[2], outs[3:]


def _zero_after(a):
    bits = pltpu.bitcast(a, jnp.uint32)
    z = lax.shift_right_logical(lax.shift_right_logical(bits, jnp.uint32(16)), jnp.uint32(16))
    return pltpu.bitcast(z, F32)


def _attn_bias_init(bias_scr):
    nk = 3 * BLOCK
    row = lax.broadcasted_iota(jnp.int32, (BLOCK, nk), 0)
    col = lax.broadcasted_iota(jnp.int32, (BLOCK, nk), 1)
    dist = jnp.abs(col - BLOCK - row)
    distf = dist.astype(F32)
    for var in range(3):
        ok = dist <= WINDOW
        if var == 1:
            ok = ok & (col >= BLOCK)
        elif var == 2:
            ok = ok & (col < 2 * BLOCK)
        for h in range(N_HEADS_ATT):
            slope = 2.0 ** (-8.0 * (h + 1.0) / N_HEADS_ATT)
            bias_scr[var * N_HEADS_ATT + h] = jnp.where(ok, (-slope * LOG2E) * distf, -jnp.inf)


def _attn_pieces(first, last, sink_ref, q_ref, kvp_ref, kvn_ref, g_ref, o_ref, bias_scr, *, nsub):
    nk = 3 * BLOCK

    lane_k = lax.broadcasted_iota(jnp.int32, (nk, LANES), 1)
    ones_a = jnp.where(lane_k < HEAD_DIM_ATT, 1.0, 0.0).astype(BF16)
    ones_b = jnp.where(lane_k < HEAD_DIM_ATT, 0.0, 1.0).astype(BF16)
    lo_half_q = lax.broadcasted_iota(jnp.int32, (BLOCK, LANES), 1) < HEAD_DIM_ATT

    def kv_block(idx):
        if idx < 0:
            return kvp_ref[...]
        if idx >= nsub:
            return kvn_ref[...]
        return q_ref[idx * BLOCK:(idx + 1) * BLOCK, ATT_Q:QKVA_W]

    for n in range(nsub):
        rows = slice(n * BLOCK, (n + 1) * BLOCK)
        kv = jnp.concatenate([kv_block(n - 1), kv_block(n), kv_block(n + 1)], axis=0)
        if n == 0:
            var = jnp.where(first, 1, 0)
        elif n == nsub - 1:
            var = jnp.where(last, 2, 0)
        else:
            var = 0
        pieces = []
        for kvh in range(N_KV_HEADS):
            k_st, k_sw = kv[:, 0:LANES], kv[:, LANES:2 * LANES]
            v_st, v_sw = kv[:, 2 * LANES:3 * LANES], kv[:, 3 * LANES:4 * LANES]
            if kvh == 0:
                k_lo, k_hi, v_lo, v_hi = k_st, k_sw, v_st, v_sw
            else:
                k_lo, k_hi, v_lo, v_hi = k_sw, k_st, v_sw, v_st
            kk = jnp.concatenate([k_lo * ones_a, k_hi * ones_b], axis=0)
            vv = jnp.concatenate([
                jnp.concatenate([v_lo * ones_a, ones_a], axis=1),
                jnp.concatenate([v_hi * ones_b, ones_b], axis=1)], axis=0)
            for pair in range(GROUP_SIZE // 2):
                h0 = kvh * GROUP_SIZE + 2 * pair
                qp = q_ref[rows, h0 * HEAD_DIM_ATT:(h0 + 2) * HEAD_DIM_ATT]
                s2 = _dot_nt(qp, kk)
                ps, es = [], []
                for t in range(2):
                    logits = s2[:, t * nk:(t + 1) * nk] + bias_scr[var * N_HEADS_ATT + h0 + t]
                    sink = sink_ref[h0 + t] * LOG2E
                    mx = jnp.maximum(jnp.max(logits, axis=-1, keepdims=True), sink)
                    ps.append(jnp.exp2(logits - mx).astype(BF16))
                    es.append(jnp.exp2(sink - mx))
                res = _dot(jnp.concatenate(ps, axis=1), vv)
                den = res[:, LANES:] + jnp.where(lo_half_q, es[0], es[1])
                pieces.append(res[:, :LANES] / den)
                yield
        att = jnp.concatenate(pieces, axis=1)
        o_ref[rows, :] = _rms(att, g_ref[...]).astype(o_ref.dtype)
        yield


def _outffn_rows(rows, x_ref, att_ref, hf_ref, hb_ref, om_ref, mod_ref, gm_ref, wo_ref, g2_ref,
                 w1_ref, w2_ref, gf_ref, o_ref, hid_scr, *, final):
    hs = hf_ref[rows, :] + hb_ref[rows, :]
    parts = []
    for h in range(N_HEADS_M):
        cols = slice(h * HEAD_DIM_M, (h + 1) * HEAD_DIM_M)
        parts.append(_rms(hs[:, cols], gm_ref[:, cols]))
    hm = jax.nn.sigmoid(om_ref[rows, :]) * jnp.concatenate(parts, axis=1)
    mixin = jnp.concatenate([att_ref[rows, :], hm.astype(BF16)], axis=1)
    x1 = x_ref[rows, :] + _mod(mod_ref, 2) * _dot(mixin, wo_ref[...])
    hff = (_rms(x1, g2_ref[...] * (1.0 + _mod(mod_ref, 4))) + _mod(mod_ref, 3)).astype(BF16)
    yield
    for c in range(N_FF_CHUNKS):
        gate = _dot(hff, w1_ref[:, FF_CHUNK * c:FF_CHUNK * (c + 1)])
        up = _dot(hff, w1_ref[:, D_FF + FF_CHUNK * c:D_FF + FF_CHUNK * (c + 1)])
        hid_scr[rows, FF_CHUNK * c:FF_CHUNK * (c + 1)] = (jax.nn.silu(gate) * up).astype(BF16)
        yield
    x2 = x1 + _mod(mod_ref, 5) * _dot(hid_scr[rows, :], w2_ref[...])
    if final:
        x2 = _rms(x2, gf_ref[...])
    o_ref[rows, :] = x2
    yield


OUTFFN_ROW_GROUPS = 2


def _outffn_kernel(*refs, final):
    rows = refs[0].shape[0] // OUTFFN_ROW_GROUPS
    pending = [_outffn_rows(slice(i * rows, (i + 1) * rows), *refs, final=final)
               for i in range(OUTFFN_ROW_GROUPS)]
    live = []
    while pending or live:
        if pending:
            live.append(pending.pop(0))
        for s in list(live):
            if next(s, StopIteration) is StopIteration:
                live.remove(s)


def _outffn_call(x, att, hf, hb, om, mod, g_m, w_out, g2, w1, w2, g_final, tm, final):
    bsz, s, d = x.shape
    tile = lambda w: pl.BlockSpec((None, tm, w), lambda b, i: (b, i, 0))
    const = lambda shp: pl.BlockSpec(shp, lambda b, i: (0, 0))
    weight = lambda shp: pl.BlockSpec(shp, lambda b, i: (0, 0), pipeline_mode=pl.Buffered(1))
    return pl.pallas_call(
        functools.partial(_outffn_kernel, final=final),
        grid=(bsz, s // tm),
        in_specs=[
            tile(d), tile(ATT_Q), tile(M_W), tile(M_W), tile(M_W),
            pl.BlockSpec((None, 1, N_MOD * d), lambda b, i: (b, 0, 0)),
            const((1, M_W)), weight((ATT_Q + M_W, d)), const((1, d)),
            weight((d, 2 * D_FF)), weight((D_FF, d)), const((1, d)),
        ],
        out_specs=tile(d),
        out_shape=jax.ShapeDtypeStruct((bsz, s, d), F32),
        scratch_shapes=[pltpu.VMEM((tm, D_FF), BF16)],
        compiler_params=pltpu.CompilerParams(
            dimension_semantics=("arbitrary", "arbitrary"), vmem_limit_bytes=VMEM_LIMIT),
        name="outffn",
    )(x, att, hf, hb, om, mod, g_m, w_out, g2, w1, w2, g_final)


def _layer(l, x, c, w_mod, b_mod, g_norm1, w_in, conv_w, conv_b, b_gates, sink,
           g_attn_out, g_mlstm_out, w_out, g_norm2, w_ffn_in, w_ffn_out, g_final, final):
    d = x.shape[-1]
    mod, w_in_p = _mod_call(c, w_mod, b_mod[l], w_in, l, IN_COLS_PAD)
    qkva, qc, kct, vm, om, gates_t = _inproj_call(
        x, mod, g_norm1[l].reshape(1, d), w_in_p, conv_w[l].reshape(1, CONV_WIDTH * 2 * M_W),
        conv_b[l].reshape(1, 2 * M_W), tm=512)

    bg_rows = jnp.broadcast_to(b_gates[l][:, None], (N_GATES, LANES))
    hf, hb, att, (w_out_b, w_ffn_in_b, w_ffn_out_b) = _mixer_call(
        qc, kct, vm, gates_t, bg_rows, sink[l], qkva, g_attn_out[l].reshape(1, ATT_Q),
        (w_out, w_ffn_in, w_ffn_out), l, tm=1024)

    return _outffn_call(x, att, hf, hb, om, mod, g_mlstm_out[l].reshape(1, M_W), w_out_b,
                        g_norm2[l].reshape(1, d), w_ffn_in_b, w_ffn_out_b,
                        g_final.reshape(1, d), tm=512, final=final)


def kernel(x, c, w_mod, b_mod, g_norm1, w_in, conv_w, conv_b, b_gates, sink, g_attn_out,
           g_mlstm_out, w_out, g_norm2, w_ffn_in, w_ffn_out, g_final):
    depth = w_mod.shape[0]
    for l in range(depth):
        x = _layer(l, x, c, w_mod, b_mod, g_norm1, w_in, conv_w, conv_b, b_gates, sink, g_attn_out,
                   g_mlstm_out, w_out, g_norm2, w_ffn_in, w_ffn_out, g_final, final=(l == depth - 1))
    return x
```

```python
import functools

import jax
import jax.numpy as jnp
from jax import lax
from jax.experimental import pallas as pl
from jax.experimental.pallas import tpu as pltpu

F32 = jnp.float32
BF16 = jnp.bfloat16

D_MODEL = 1024
EPS = 1e-6
N_HEADS_ATT = 8
N_KV_HEADS = 2
HEAD_DIM_ATT = 64
GROUP_SIZE = N_HEADS_ATT // N_KV_HEADS
WINDOW = 128
BLOCK = 128
N_HEADS_M = 4
HEAD_DIM_M = 128
CHUNK = 128
CONV_WIDTH = 3
ATT_Q = N_HEADS_ATT * HEAD_DIM_ATT
ATT_KV = N_KV_HEADS * HEAD_DIM_ATT
M_W = N_HEADS_M * HEAD_DIM_M
N_GATES = 4 * N_HEADS_M
D_FF = 2816
N_MOD = 6

LANES = 128
GATE_PAD = LANES
FF_CHUNK = 256
N_FF_CHUNKS = D_FF // FF_CHUNK
VMEM_LIMIT = 56 * 1024 * 1024

C_QA = 0
C_KA = ATT_Q
C_VA = ATT_Q + ATT_KV
C_QKM = ATT_Q + 2 * ATT_KV
C_VM = C_QKM + 2 * M_W
C_OM = C_VM + M_W
C_G = C_OM + M_W
IN_COLS_PAD = C_G + GATE_PAD
QKVA_W = ATT_Q + 4 * ATT_KV


def _dot(a, b):
    return jnp.dot(a, b, preferred_element_type=F32)


def _dot_nt(a, b):
    return lax.dot_general(a, b, (((1,), (1,)), ((), ())), preferred_element_type=F32)


def _rms(x, g):
    return x * lax.rsqrt(jnp.mean(x * x, axis=-1, keepdims=True) + EPS) * g


def _mod(mod_ref, k):
    return mod_ref[:, k * D_MODEL:(k + 1) * D_MODEL]


def _alternate(*streams):
    live = list(streams)
    while live:
        for s in list(live):
            if next(s, StopIteration) is StopIteration:
                live.remove(s)


def _cast_t_kernel(wt_ref, o_ref):
    n = wt_ref.shape[0]
    full = n // LANES * LANES
    for r in range(0, full, LANES):
        o_ref[:, r:r + LANES] = wt_ref[r:r + LANES, :].T.astype(o_ref.dtype)
    if o_ref.shape[1] > full:
        tail = jnp.concatenate(
            [wt_ref[full:n, :], jnp.zeros((full + LANES - n, wt_ref.shape[1]), wt_ref.dtype)], axis=0)
        o_ref[:, full:full + LANES] = tail.T.astype(o_ref.dtype)


SUBLANES = 8
MOD_STEPS = 8


def _mod_kernel(c_ref, w_ref, b_ref, wt_ref, o_ref, wo_ref):
    bsz, d = c_ref.shape
    c = jnp.concatenate([c_ref[...], jnp.zeros((SUBLANES - bsz, d), F32)], axis=0)
    res = _dot(jax.nn.silu(c).astype(BF16), w_ref[...].astype(BF16)) + b_ref[...]
    for b in range(bsz):
        o_ref[b] = res[b:b + 1, :]
    _cast_t_kernel(wt_ref, wo_ref)


def _mod_call(c, w_mod, b_mod, w_in, layer, n_out):
    bsz, d = c.shape
    assert bsz <= SUBLANES
    n_mod = w_mod.shape[2]
    _, k, n = w_in.shape
    assert n_out - n < LANES and n_out % LANES == 0
    bn, bk = n_mod // MOD_STEPS, k // MOD_STEPS
    assert bn * MOD_STEPS == n_mod and bk * MOD_STEPS == k and bn % LANES == 0 and bk % LANES == 0
    return pl.pallas_call(
        _mod_kernel,
        grid=(MOD_STEPS,),
        in_specs=[
            pl.BlockSpec((bsz, d), lambda i: (0, 0)),
            pl.BlockSpec((None, d, bn), lambda i: (layer, 0, i)),
            pl.BlockSpec((1, bn), lambda i: (0, i)),
            pl.BlockSpec((None, n, bk), lambda i: (layer, 0, i)),
        ],
        out_specs=[pl.BlockSpec((bsz, 1, bn), lambda i: (0, 0, i)),
                   pl.BlockSpec((bk, n_out), lambda i: (i, 0))],
        out_shape=[jax.ShapeDtypeStruct((bsz, 1, n_mod), F32), jax.ShapeDtypeStruct((k, n_out), BF16)],
        compiler_params=pltpu.CompilerParams(
            dimension_semantics=("arbitrary",), vmem_limit_bytes=VMEM_LIMIT),
        name="mod",
    )(c, w_mod, b_mod.reshape(1, n_mod), jnp.swapaxes(w_in, 1, 2))


MXU_COLS = 256
CONV_ROWS = 64


def _inproj_kernel(x_ref, mod_ref, g_ref, w_ref, cw_ref, cb_ref,
                   qkva_ref, qc_ref, kct_ref, vm_ref, om_ref, gt_ref,
                   raw_scr, new_scr, k_scr, carry_scr, *, nblk):
    j = pl.program_id(0)

    @pl.when(j == 0)
    def _init():
        raw_scr[...] = jnp.zeros_like(raw_scr)
        carry_scr[...] = jnp.zeros_like(carry_scr)

    x = x_ref[...]
    h = _rms(x, g_ref[...] * (1.0 + _mod(mod_ref, 1))) + _mod(mod_ref, 0)
    hb = h.astype(BF16)
    tm = x.shape[0]
    nchunk = tm // CHUNK
    within = (j + nblk - 1) % nblk

    anchors = []

    def zero_after(a):
        return jnp.concatenate([_zero_after(a)] * (2 * M_W // LANES), axis=1)

    def project():
        order = list(range(C_QKM, C_VM, MXU_COLS)) + list(range(0, C_QKM, MXU_COLS)) + \
            list(range(C_VM, IN_COLS_PAD, MXU_COLS))
        for c0 in order:
            c1 = min(c0 + MXU_COLS, IN_COLS_PAD)
            res = _dot(hb, w_ref[:, c0:c1])
            anchors.append(res[tm - 1:tm, c1 - c0 - LANES:c1 - c0])
            if c0 < C_KA:
                qkva_ref[:, c0:c1] = (res * (HEAD_DIM_ATT ** -0.5 * LOG2E)).astype(BF16)
            elif c0 < C_QKM:
                ka, va = res[:, :ATT_KV], res[:, ATT_KV:]
                half = HEAD_DIM_ATT
                qkva_ref[:, ATT_Q:ATT_Q + ATT_KV] = ka.astype(BF16)
                qkva_ref[:, ATT_Q + ATT_KV:ATT_Q + 2 * ATT_KV] = pltpu.roll(ka, half, axis=1).astype(BF16)
                qkva_ref[:, ATT_Q + 2 * ATT_KV:ATT_Q + 3 * ATT_KV] = va.astype(BF16)
                qkva_ref[:, ATT_Q + 3 * ATT_KV:ATT_Q + 4 * ATT_KV] = pltpu.roll(va, half, axis=1).astype(BF16)
            elif c0 < C_VM:
                new_scr[:, c0 - C_QKM:c1 - C_QKM] = res
            elif c0 < C_OM:
                vm_ref[:, c0 - C_VM:c1 - C_VM] = res.astype(BF16)
            elif c0 < C_G:
                om_ref[:, c0 - C_OM:c1 - C_OM] = res
            else:
                for g in range(nchunk):
                    blk = res[g * CHUNK:(g + 1) * CHUNK, :].T
                    gt_ref[:, g * CHUNK:(g + 1) * CHUNK] = blk[0:N_GATES, :]
            yield

    def conv():
        R = CONV_ROWS
        row = lax.broadcasted_iota(jnp.int32, (R, 1), 0)
        for p in range(tm // R):
            r0 = p * R
            xg = raw_scr[r0:r0 + R, :]
            if p == 0:
                prev_row = jnp.where(within == 0, 0.0, carry_scr[0:1, :])
            else:
                prev_row = raw_scr[r0 - 1:r0, :]
            if r0 + R == tm:
                next_row = jnp.where(within == nblk - 1, 0.0, new_scr[0:1, :])
            else:
                next_row = raw_scr[r0 + R:r0 + R + 1, :]
            xm1 = jnp.where(row == 0, prev_row, pltpu.roll(xg, 1, axis=0))
            xp1 = jnp.where(row == R - 1, next_row, pltpu.roll(xg, R - 1, axis=0))
            z = zero_after(anchors[-1])
            w0, w1, w2 = (cw_ref[:, t * 2 * M_W:(t + 1) * 2 * M_W] + z for t in range(CONV_WIDTH))
            y = xm1 * w0 + xg * w1 + xp1 * w2 + cb_ref[...]
            y = jax.nn.silu(y)
            qc_ref[r0:r0 + R, :] = (y[:, :M_W] * (HEAD_DIM_M ** -0.5)).astype(BF16)
            k_scr[r0:r0 + R, :] = y[:, M_W:]
            yield
            if (r0 + R) % CHUNK == 0:
                g = r0 // CHUNK
                for hd in range(N_HEADS_M):
                    blk = k_scr[g * CHUNK:(g + 1) * CHUNK, hd * HEAD_DIM_M:(hd + 1) * HEAD_DIM_M]
                    base = (g * N_HEADS_M + hd) * HEAD_DIM_M
                    kct_ref[base:base + HEAD_DIM_M, :] = blk.T.astype(BF16)
                yield

    _alternate(project(), conv())
    carry_scr[0:1, :] = raw_scr[tm - 1:tm, :]
    raw_scr[...] = new_scr[...]


def _inproj_call(x, mod, g1, w_in_p, conv_w, conv_b, tm):
    bsz, s, d = x.shape
    nblk = s // tm
    ntiles = bsz * nblk
    outs = (
        jax.ShapeDtypeStruct((bsz, s, QKVA_W), BF16),
        jax.ShapeDtypeStruct((bsz, s, M_W), BF16),
        jax.ShapeDtypeStruct((bsz, s * N_HEADS_M, HEAD_DIM_M), BF16),
        jax.ShapeDtypeStruct((bsz, s, M_W), BF16),
        jax.ShapeDtypeStruct((bsz, s, M_W), F32),
        jax.ShapeDtypeStruct((bsz, N_GATES, s), F32),
    )

    def cur(j):
        t = jnp.minimum(j, ntiles - 1)
        return t // nblk, t % nblk

    def old(j):
        t = jnp.maximum(j - 1, 0)
        return t // nblk, t % nblk

    cur_tile = lambda w: pl.BlockSpec((None, tm, w), lambda j: (*cur(j), 0))
    old_tile = lambda w: pl.BlockSpec((None, tm, w), lambda j: (*old(j), 0))
    const = lambda shp: pl.BlockSpec(shp, lambda j: (0, 0))
    return pl.pallas_call(
        functools.partial(_inproj_kernel, nblk=nblk),
        grid=(ntiles + 1,),
        in_specs=[
            cur_tile(d),
            pl.BlockSpec((None, 1, N_MOD * d), lambda j: (cur(j)[0], 0, 0)),
            const((1, d)),
            pl.BlockSpec((d, IN_COLS_PAD), lambda j: (0, 0), pipeline_mode=pl.Buffered(1)),
            const((1, CONV_WIDTH * 2 * M_W)), const((1, 2 * M_W)),
        ],
        out_specs=[cur_tile(QKVA_W), old_tile(M_W),
                   pl.BlockSpec((None, tm * N_HEADS_M, HEAD_DIM_M), lambda j: (*old(j), 0)),
                   cur_tile(M_W), cur_tile(M_W),
                   pl.BlockSpec((None, N_GATES, tm), lambda j: (cur(j)[0], 0, cur(j)[1]))],
        out_shape=outs,
        scratch_shapes=[pltpu.VMEM((tm, 2 * M_W), F32), pltpu.VMEM((tm, 2 * M_W), F32),
                        pltpu.VMEM((tm, M_W), F32), pltpu.VMEM((8, 2 * M_W), F32)],
        compiler_params=pltpu.CompilerParams(
            dimension_semantics=("arbitrary",), vmem_limit_bytes=VMEM_LIMIT),
        name="inproj",
    )(x, mod, g1, w_in_p, conv_w, conv_b)


def _mlstm_reset(first, c_scr, m_scr):
    @pl.when(first)
    def _init():
        c_scr[...] = jnp.zeros_like(c_scr)
        m_scr[...] = jnp.zeros_like(m_scr)


LOG2E = 1.4426950408889634


def _mlstm_stream(q_ref, kt_ref, v_ref, gt_ref, bgr_ref, h_ref, c_scr, m_scr, *, reverse, nchunk):
    L = CHUNK
    ti = lax.broadcasted_iota(jnp.int32, (L, L), 0)
    si = lax.broadcasted_iota(jnp.int32, (L, L), 1)
    causal = (si >= ti) if reverse else (si <= ti)
    tri_t = jnp.where((ti >= si) if reverse else (ti <= si), 1.0, 0.0).astype(BF16)
    lane = lax.broadcasted_iota(jnp.int32, (1, L), 1)
    last = 0 if reverse else L - 1
    i_off, f_off = (2 * N_HEADS_M, 3 * N_HEADS_M) if reverse else (0, N_HEADS_M)
    ones = jnp.ones((L, HEAD_DIM_M), BF16)
    order = list(range(nchunk - 1, -1, -1) if reverse else range(nchunk))
    rows_of = lambda g: slice(g * L, (g + 1) * L)

    def split3(a):
        hi = a.astype(BF16)
        r1 = a - hi.astype(F32)
        mid = r1.astype(BF16)
        return hi, mid, (r1 - mid.astype(F32)).astype(BF16)

    gates = {}
    pad = jnp.zeros((L - N_GATES, L), F32)
    for g in order:
        gr = gt_ref[:, rows_of(g)] + bgr_ref[...]
        parts = _dot(jnp.concatenate(split3(jax.nn.log_sigmoid(gr)), axis=0), tri_t)
        bcr = (parts[0:N_GATES] + parts[N_GATES:2 * N_GATES] + parts[2 * N_GATES:3 * N_GATES]) * LOG2E
        gates[g] = (gr * LOG2E, bcr, jnp.concatenate([-bcr, pad], axis=0).T)
        yield

    def head(g, h):
        rows = rows_of(g)
        cols = slice(h * HEAD_DIM_M, (h + 1) * HEAD_DIM_M)
        gct, bct, nbcum = gates[g]
        r_row = gct[i_off + h:i_off + h + 1, :] - bct[f_off + h:f_off + h + 1, :]
        btot = jnp.sum(jnp.where(lane == last, bct[f_off + h:f_off + h + 1, :], 0.0),
                       axis=1, keepdims=True)
        rmax = jnp.max(r_row, axis=1, keepdims=True)
        a_max = btot + rmax
        m_prev = m_scr[h:h + 1, 0:1]
        m_new = jnp.maximum(btot + m_prev, a_max)
        m_scr[h:h + 1, :] = jnp.broadcast_to(m_new, (1, LANES))
        decay = jnp.exp2(btot + m_prev - m_new)
        w_row = jnp.exp2(r_row - rmax) * jnp.exp2(a_max - m_new)
        kt_rows = slice((g * N_HEADS_M + h) * HEAD_DIM_M, (g * N_HEADS_M + h + 1) * HEAD_DIM_M)
        cm = jnp.max(jnp.where(causal, r_row, -jnp.inf), axis=1, keepdims=True)
        yield
        q = q_ref[rows, cols]
        s = _dot(q, kt_ref[kt_rows, :])
        u = jnp.maximum(cm, m_prev)
        ub = jnp.broadcast_to(u, (L, L))
        dm = jnp.exp2(jnp.where(causal, r_row - ub, -jnp.inf))
        qi = q * jnp.exp2(m_prev - ub).astype(BF16)
        floor = jnp.exp2(jnp.broadcast_to(nbcum[:, f_off + h:f_off + h + 1], (L, L)) - ub)
        yield
        vaug = jnp.concatenate([v_ref[rows, cols], ones], axis=1)
        lhs = jnp.concatenate([(s * dm).astype(BF16), qi], axis=1)
        rhs = jnp.concatenate([vaug, c_scr[h].astype(BF16)], axis=0)
        out = _dot(lhs, rhs)
        upd = _dot(kt_ref[kt_rows, :] * w_row.astype(BF16), vaug)
        yield
        c_scr[h] = decay * c_scr[h] + upd
        h_ref[rows, cols] = out[:, :HEAD_DIM_M] / jnp.maximum(jnp.abs(out[:, HEAD_DIM_M:]), floor)
        yield

    n_stage, lag = 4, 4
    heads = {}
    for slot in range(lag * (nchunk - 1) + n_stage):
        for i, g in enumerate(order):
            stage = slot - lag * i
            if 0 <= stage < n_stage:
                for h in range(N_HEADS_M):
                    if stage == 0:
                        heads[g, h] = head(g, h)
                    next(heads[g, h])
                    yield


def _mixer_kernel(qf_ref, kf_ref, vf_ref, gtf_ref, qb_ref, kb_ref, vb_ref, gtb_ref, bgr_ref,
                  sink_ref, qa_ref, kvp_ref, kvn_ref, ga_ref, *rest, nblk, nchunk, ncast):
    w_refs, rest = rest[:ncast], rest[ncast:]
    hf_ref, hb_ref, att_ref = rest[:3]
    wo_refs, (cf_scr, mf_scr, cb_scr, mb_scr, bias_scr) = rest[3:3 + ncast], rest[3 + ncast:]
    b = pl.program_id(0)
    j = pl.program_id(1)

    @pl.when((b == 0) & (j == 0))
    def _init():
        _attn_bias_init(bias_scr)

    _mlstm_reset(j == 0, cf_scr, mf_scr)
    _mlstm_reset(j == 0, cb_scr, mb_scr)
    for w_ref, wo_ref in zip(w_refs, wo_refs):
        wo_ref[...] = w_ref[...].astype(wo_ref.dtype)
    fwd = _mlstm_stream(qf_ref, kf_ref, vf_ref, gtf_ref, bgr_ref, hf_ref, cf_scr, mf_scr,
                        reverse=False, nchunk=nchunk)
    bwd = _mlstm_stream(qb_ref, kb_ref, vb_ref, gtb_ref, bgr_ref, hb_ref, cb_scr, mb_scr,
                        reverse=True, nchunk=nchunk)
    att = _attn_pieces(j == 0, j == nblk - 1, sink_ref, qa_ref, kvp_ref, kvn_ref, ga_ref, att_ref,
                       bias_scr, nsub=nchunk)
    _alternate(fwd, bwd)
    for _ in att:
        pass


def _mixer_call(qc, kct, vm, gates_t, bg_rows, sink, qkva, g_attn, weights, layer, tm):
    bsz, s, _ = qc.shape
    nblk = s // tm
    nsteps = bsz * nblk
    nsub = tm // BLOCK
    nb = s // BLOCK
    kvw = QKVA_W - ATT_Q
    attn_specs = [
        pl.BlockSpec(memory_space=pltpu.SMEM),
        pl.BlockSpec((None, tm, QKVA_W), lambda b, j: (b, j, 0)),
        pl.BlockSpec((None, BLOCK, kvw), lambda b, j: (b, jnp.maximum(j * nsub - 1, 0), 1)),
        pl.BlockSpec((None, BLOCK, kvw), lambda b, j: (b, jnp.minimum((j + 1) * nsub, nb - 1), 1)),
        pl.BlockSpec((1, ATT_Q), lambda b, j: (0, 0)),
    ]
    w_specs, w_shapes = [], []
    for w in weights:
        _, k, n = w.shape
        rows = k // nsteps
        assert rows * nsteps == k and rows % 16 == 0
        w_specs.append(pl.BlockSpec((None, rows, n), lambda b, j: (layer, b * nblk + j, 0)))
        w_shapes.append(jax.ShapeDtypeStruct((k, n), BF16))
    wo_specs = [pl.BlockSpec((sp.block_shape[1], sp.block_shape[2]), lambda b, j: (b * nblk + j, 0))
                for sp in w_specs]

    def specs(pos):
        tile = lambda w: pl.BlockSpec((None, tm, w), lambda b, j: (b, pos(j), 0))
        kt_spec = pl.BlockSpec((None, tm * N_HEADS_M, HEAD_DIM_M), lambda b, j: (b, pos(j), 0))
        gt_spec = pl.BlockSpec((None, N_GATES, tm), lambda b, j: (b, 0, pos(j)))
        return [tile(M_W), kt_spec, tile(M_W), gt_spec], tile(M_W)

    in_f, out_f = specs(lambda j: j)
    in_b, out_b = specs(lambda j: nblk - 1 - j)
    state = [pltpu.VMEM((N_HEADS_M, HEAD_DIM_M, 2 * HEAD_DIM_M), F32), pltpu.VMEM((8, LANES), F32)]
    outs = pl.pallas_call(
        functools.partial(_mixer_kernel, nblk=nblk, nchunk=tm // CHUNK, ncast=len(weights)),
        grid=(bsz, nblk),
        in_specs=in_f + in_b + [pl.BlockSpec((N_GATES, LANES), lambda b, j: (0, 0))] + attn_specs
        + w_specs,
        out_specs=[out_f, out_b, pl.BlockSpec((None, tm, ATT_Q), lambda b, j: (b, j, 0))] + wo_specs,
        out_shape=[jax.ShapeDtypeStruct((bsz, s, M_W), F32)] * 2
        + [jax.ShapeDtypeStruct((bsz, s, ATT_Q), BF16)] + w_shapes,
        scratch_shapes=state + state + [pltpu.VMEM((3 * N_HEADS_ATT, BLOCK, 3 * BLOCK), F32)],
        compiler_params=pltpu.CompilerParams(
            dimension_semantics=("arbitrary", "arbitrary"), vmem_limit_bytes=VMEM_LIMIT),
        name="mixer",
    )(qc, kct, vm, gates_t, qc, kct, vm, gates_t, bg_rows, sink, qkva, qkva, qkva, g_attn, *weights)
    return outs[0], outs[1], outs[2], outs[3:]


def _zero_after(a):
    bits = pltpu.bitcast(a, jnp.uint32)
    z = lax.shift_right_logical(lax.shift_right_logical(bits, jnp.uint32(16)), jnp.uint32(16))
    return pltpu.bitcast(z, F32)


def _attn_bias_init(bias_scr):
    nk = 3 * BLOCK
    row = lax.broadcasted_iota(jnp.int32, (BLOCK, nk), 0)
    col = lax.broadcasted_iota(jnp.int32, (BLOCK, nk), 1)
    dist = jnp.abs(col - BLOCK - row)
    distf = dist.astype(F32)
    for var in range(3):
        ok = dist <= WINDOW
        if var == 1:
            ok = ok & (col >= BLOCK)
        elif var == 2:
            ok = ok & (col < 2 * BLOCK)
        for h in range(N_HEADS_ATT):
            slope = 2.0 ** (-8.0 * (h + 1.0) / N_HEADS_ATT)
            bias_scr[var * N_HEADS_ATT + h] = jnp.where(ok, (-slope * LOG2E) * distf, -jnp.inf)


def _attn_pieces(first, last, sink_ref, q_ref, kvp_ref, kvn_ref, g_ref, o_ref, bias_scr, *, nsub):
    nk = 3 * BLOCK

    lane_k = lax.broadcasted_iota(jnp.int32, (nk, LANES), 1)
    ones_a = jnp.where(lane_k < HEAD_DIM_ATT, 1.0, 0.0).astype(BF16)
    ones_b = jnp.where(lane_k < HEAD_DIM_ATT, 0.0, 1.0).astype(BF16)
    lo_half_q = lax.broadcasted_iota(jnp.int32, (BLOCK, LANES), 1) < HEAD_DIM_ATT

    def kv_block(idx):
        if idx < 0:
            return kvp_ref[...]
        if idx >= nsub:
            return kvn_ref[...]
        return q_ref[idx * BLOCK:(idx + 1) * BLOCK, ATT_Q:QKVA_W]

    for n in range(nsub):
        rows = slice(n * BLOCK, (n + 1) * BLOCK)
        kv = jnp.concatenate([kv_block(n - 1), kv_block(n), kv_block(n + 1)], axis=0)
        if n == 0:
            var = jnp.where(first, 1, 0)
        elif n == nsub - 1:
            var = jnp.where(last, 2, 0)
        else:
            var = 0
        pieces = []
        for kvh in range(N_KV_HEADS):
            k_st, k_sw = kv[:, 0:LANES], kv[:, LANES:2 * LANES]
            v_st, v_sw = kv[:, 2 * LANES:3 * LANES], kv[:, 3 * LANES:4 * LANES]
            if kvh == 0:
                k_lo, k_hi, v_lo, v_hi = k_st, k_sw, v_st, v_sw
            else:
                k_lo, k_hi, v_lo, v_hi = k_sw, k_st, v_sw, v_st
            kk = jnp.concatenate([k_lo * ones_a, k_hi * ones_b], axis=0)
            vv = jnp.concatenate([
                jnp.concatenate([v_lo * ones_a, ones_a], axis=1),
                jnp.concatenate([v_hi * ones_b, ones_b], axis=1)], axis=0)
            for pair in range(GROUP_SIZE // 2):
                h0 = kvh * GROUP_SIZE + 2 * pair
                qp = q_ref[rows, h0 * HEAD_DIM_ATT:(h0 + 2) * HEAD_DIM_ATT]
                s2 = _dot_nt(qp, kk)
                ps, es = [], []
                for t in range(2):
                    logits = s2[:, t * nk:(t + 1) * nk] + bias_scr[var * N_HEADS_ATT + h0 + t]
                    sink = sink_ref[h0 + t] * LOG2E
                    mx = jnp.maximum(jnp.max(logits, axis=-1, keepdims=True), sink)
                    ps.append(jnp.exp2(logits - mx).astype(BF16))
                    es.append(jnp.exp2(sink - mx))
                res = _dot(jnp.concatenate(ps, axis=1), vv)
                den = res[:, LANES:] + jnp.where(lo_half_q, es[0], es[1])
                pieces.append(res[:, :LANES] / den)
                yield
        att = jnp.concatenate(pieces, axis=1)
        o_ref[rows, :] = _rms(att, g_ref[...]).astype(o_ref.dtype)
        yield


def _outffn_body(x_ref, att_ref, hf_ref, hb_ref, om_ref, mod_ref, gm_ref, wo_ref, g2_ref,
                 w1_ref, w2_ref, gf_ref, o_ref, hid_scr, *, final):
    hs = hf_ref[...] + hb_ref[...]
    parts = []
    for h in range(N_HEADS_M):
        cols = slice(h * HEAD_DIM_M, (h + 1) * HEAD_DIM_M)
        parts.append(_rms(hs[:, cols], gm_ref[:, cols]))
    hm = jax.nn.sigmoid(om_ref[...]) * jnp.concatenate(parts, axis=1)
    mixin = jnp.concatenate([att_ref[...], hm.astype(BF16)], axis=1)
    x1 = x_ref[...] + _mod(mod_ref, 2) * _dot(mixin, wo_ref[...])
    hff = (_rms(x1, g2_ref[...] * (1.0 + _mod(mod_ref, 4))) + _mod(mod_ref, 3)).astype(BF16)
    for c in range(N_FF_CHUNKS):
        gate = _dot(hff, w1_ref[:, FF_CHUNK * c:FF_CHUNK * (c + 1)])
        up = _dot(hff, w1_ref[:, D_FF + FF_CHUNK * c:D_FF + FF_CHUNK * (c + 1)])
        hid_scr[:, FF_CHUNK * c:FF_CHUNK * (c + 1)] = (jax.nn.silu(gate) * up).astype(BF16)
    x2 = x1 + _mod(mod_ref, 5) * _dot(hid_scr[...], w2_ref[...])
    if final:
        x2 = _rms(x2, gf_ref[...])
    o_ref[...] = x2


def _outffn_kernel(*refs, final):
    _outffn_body(*refs, final=final)


def _outffn_call(x, att, hf, hb, om, mod, g_m, w_out, g2, w1, w2, g_final, tm, final):
    bsz, s, d = x.shape
    tile = lambda w: pl.BlockSpec((None, tm, w), lambda b, i: (b, i, 0))
    const = lambda shp: pl.BlockSpec(shp, lambda b, i: (0, 0))
    weight = lambda shp: pl.BlockSpec(shp, lambda b, i: (0, 0), pipeline_mode=pl.Buffered(1))
    return pl.pallas_call(
        functools.partial(_outffn_kernel, final=final),
        grid=(bsz, s // tm),
        in_specs=[
            tile(d), tile(ATT_Q), tile(M_W), tile(M_W), tile(M_W),
            pl.BlockSpec((None, 1, N_MOD * d), lambda b, i: (b, 0, 0)),
            const((1, M_W)), weight((ATT_Q + M_W, d)), const((1, d)),
            weight((d, 2 * D_FF)), weight((D_FF, d)), const((1, d)),
        ],
        out_specs=tile(d),
        out_shape=jax.ShapeDtypeStruct((bsz, s, d), F32),
        scratch_shapes=[pltpu.VMEM((tm, D_FF), BF16)],
        compiler_params=pltpu.CompilerParams(
            dimension_semantics=("arbitrary", "arbitrary"), vmem_limit_bytes=VMEM_LIMIT),
        name="outffn",
    )(x, att, hf, hb, om, mod, g_m, w_out, g2, w1, w2, g_final)


def _layer(l, x, c, w_mod, b_mod, g_norm1, w_in, conv_w, conv_b, b_gates, sink,
           g_attn_out, g_mlstm_out, w_out, g_norm2, w_ffn_in, w_ffn_out, g_final, final):
    d = x.shape[-1]
    mod, w_in_p = _mod_call(c, w_mod, b_mod[l], w_in, l, IN_COLS_PAD)
    qkva, qc, kct, vm, om, gates_t = _inproj_call(
        x, mod, g_norm1[l].reshape(1, d), w_in_p, conv_w[l].reshape(1, CONV_WIDTH * 2 * M_W),
        conv_b[l].reshape(1, 2 * M_W), tm=512)

    bg_rows = jnp.broadcast_to(b_gates[l][:, None], (N_GATES, LANES))
    hf, hb, att, (w_out_b, w_ffn_in_b, w_ffn_out_b) = _mixer_call(
        qc, kct, vm, gates_t, bg_rows, sink[l], qkva, g_attn_out[l].reshape(1, ATT_Q),
        (w_out, w_ffn_in, w_ffn_out), l, tm=1024)

    return _outffn_call(x, att, hf, hb, om, mod, g_mlstm_out[l].reshape(1, M_W), w_out_b,
                        g_norm2[l].reshape(1, d), w_ffn_in_b, w_ffn_out_b,
                        g_final.reshape(1, d), tm=512, final=final)


def kernel(x, c, w_mod, b_mod, g_norm1, w_in, conv_w, conv_b, b_gates, sink, g_attn_out,
           g_mlstm_out, w_out, g_norm2, w_ffn_in, w_ffn_out, g_final):
    depth = w_mod.shape[0]
    for l in range(depth):
        x = _layer(l, x, c, w_mod, b_mod, g_norm1, w_in, conv_w, conv_b, b_gates, sink, g_attn_out,
                   g_mlstm_out, w_out, g_norm2, w_ffn_in, w_ffn_out, g_final, final=(l == depth - 1))
    return x
```

```python
import functools

import jax
import jax.numpy as jnp
from jax import lax
from jax.experimental import pallas as pl
from jax.experimental.pallas import tpu as pltpu

F32 = jnp.float32
BF16 = jnp.bfloat16

D_MODEL = 1024
EPS = 1e-6
N_HEADS_ATT = 8
N_KV_HEADS = 2
HEAD_DIM_ATT = 64
GROUP_SIZE = N_HEADS_ATT // N_KV_HEADS
WINDOW = 128
BLOCK = 128
N_HEADS_M = 4
HEAD_DIM_M = 128
CHUNK = 128
CONV_WIDTH = 3
ATT_Q = N_HEADS_ATT * HEAD_DIM_ATT
ATT_KV = N_KV_HEADS * HEAD_DIM_ATT
M_W = N_HEADS_M * HEAD_DIM_M
N_GATES = 4 * N_HEADS_M
D_FF = 2816
N_MOD = 6

LANES = 128
GATE_PAD = LANES
FF_CHUNK = 256
N_FF_CHUNKS = D_FF // FF_CHUNK
VMEM_LIMIT = 56 * 1024 * 1024

C_QA = 0
C_KA = ATT_Q
C_VA = ATT_Q + ATT_KV
C_QKM = ATT_Q + 2 * ATT_KV
C_VM = C_QKM + 2 * M_W
C_OM = C_VM + M_W
C_G = C_OM + M_W
IN_COLS_PAD = C_G + GATE_PAD
QKVA_W = ATT_Q + 4 * ATT_KV


def _dot(a, b):
    return jnp.dot(a, b, preferred_element_type=F32)


def _dot_nt(a, b):
    return lax.dot_general(a, b, (((1,), (1,)), ((), ())), preferred_element_type=F32)


def _rms(x, g):
    return x * lax.rsqrt(jnp.mean(x * x, axis=-1, keepdims=True) + EPS) * g


def _mod(mod_ref, k):
    return mod_ref[:, k * D_MODEL:(k + 1) * D_MODEL]


def _alternate(*streams):
    live = list(streams)
    while live:
        for s in list(live):
            if next(s, StopIteration) is StopIteration:
                live.remove(s)


def _cast_t_kernel(wt_ref, o_ref):
    n = wt_ref.shape[0]
    full = n // LANES * LANES
    for r in range(0, full, LANES):
        o_ref[:, r:r + LANES] = wt_ref[r:r + LANES, :].T.astype(o_ref.dtype)
    if o_ref.shape[1] > full:
        tail = jnp.concatenate(
            [wt_ref[full:n, :], jnp.zeros((full + LANES - n, wt_ref.shape[1]), wt_ref.dtype)], axis=0)
        o_ref[:, full:full + LANES] = tail.T.astype(o_ref.dtype)


SUBLANES = 8
MOD_STEPS = 8


def _mod_kernel(c_ref, w_ref, b_ref, wt_ref, o_ref, wo_ref):
    bsz, d = c_ref.shape
    c = jnp.concatenate([c_ref[...], jnp.zeros((SUBLANES - bsz, d), F32)], axis=0)
    res = _dot(jax.nn.silu(c).astype(BF16), w_ref[...].astype(BF16)) + b_ref[...]
    for b in range(bsz):
        o_ref[b] = res[b:b + 1, :]
    _cast_t_kernel(wt_ref, wo_ref)


def _mod_call(c, w_mod, b_mod, w_in, layer, n_out):
    bsz, d = c.shape
    assert bsz <= SUBLANES
    n_mod = w_mod.shape[2]
    _, k, n = w_in.shape
    assert n_out - n < LANES and n_out % LANES == 0
    bn, bk = n_mod // MOD_STEPS, k // MOD_STEPS
    assert bn * MOD_STEPS == n_mod and bk * MOD_STEPS == k and bn % LANES == 0 and bk % LANES == 0
    return pl.pallas_call(
        _mod_kernel,
        grid=(MOD_STEPS,),
        in_specs=[
            pl.BlockSpec((bsz, d), lambda i: (0, 0)),
            pl.BlockSpec((None, d, bn), lambda i: (layer, 0, i)),
            pl.BlockSpec((1, bn), lambda i: (0, i)),
            pl.BlockSpec((None, n, bk), lambda i: (layer, 0, i)),
        ],
        out_specs=[pl.BlockSpec((bsz, 1, bn), lambda i: (0, 0, i)),
                   pl.BlockSpec((bk, n_out), lambda i: (i, 0))],
        out_shape=[jax.ShapeDtypeStruct((bsz, 1, n_mod), F32), jax.ShapeDtypeStruct((k, n_out), BF16)],
        compiler_params=pltpu.CompilerParams(
            dimension_semantics=("arbitrary",), vmem_limit_bytes=VMEM_LIMIT),
        name="mod",
    )(c, w_mod, b_mod.reshape(1, n_mod), jnp.swapaxes(w_in, 1, 2))


MXU_COLS = 256
CONV_ROWS = 64


def _inproj_kernel(x_ref, mod_ref, g_ref, w_ref, cw_ref, cb_ref,
                   qkva_ref, qc_ref, kct_ref, vm_ref, om_ref, gt_ref,
                   raw_scr, new_scr, k_scr, carry_scr, *, nblk):
    j = pl.program_id(0)

    @pl.when(j == 0)
    def _init():
        raw_scr[...] = jnp.zeros_like(raw_scr)
        carry_scr[...] = jnp.zeros_like(carry_scr)

    x = x_ref[...]
    h = _rms(x, g_ref[...] * (1.0 + _mod(mod_ref, 1))) + _mod(mod_ref, 0)
    hb = h.astype(BF16)
    tm = x.shape[0]
    nchunk = tm // CHUNK
    within = (j + nblk - 1) % nblk

    anchors = []

    def zero_after(a):
        return jnp.concatenate([_zero_after(a)] * (2 * M_W // LANES), axis=1)

    def project():
        order = list(range(C_QKM, C_VM, MXU_COLS)) + list(range(0, C_QKM, MXU_COLS)) + \
            list(range(C_VM, IN_COLS_PAD, MXU_COLS))
        for c0 in order:
            c1 = min(c0 + MXU_COLS, IN_COLS_PAD)
            res = _dot(hb, w_ref[:, c0:c1])
            anchors.append(res[tm - 1:tm, c1 - c0 - LANES:c1 - c0])
            if c0 < C_KA:
                qkva_ref[:, c0:c1] = (res * (HEAD_DIM_ATT ** -0.5 * LOG2E)).astype(BF16)
            elif c0 < C_QKM:
                ka, va = res[:, :ATT_KV], res[:, ATT_KV:]
                half = HEAD_DIM_ATT
                qkva_ref[:, ATT_Q:ATT_Q + ATT_KV] = ka.astype(BF16)
                qkva_ref[:, ATT_Q + ATT_KV:ATT_Q + 2 * ATT_KV] = pltpu.roll(ka, half, axis=1).astype(BF16)
                qkva_ref[:, ATT_Q + 2 * ATT_KV:ATT_Q + 3 * ATT_KV] = va.astype(BF16)
                qkva_ref[:, ATT_Q + 3 * ATT_KV:ATT_Q + 4 * ATT_KV] = pltpu.roll(va, half, axis=1).astype(BF16)
            elif c0 < C_VM:
                new_scr[:, c0 - C_QKM:c1 - C_QKM] = res
            elif c0 < C_OM:
                vm_ref[:, c0 - C_VM:c1 - C_VM] = res.astype(BF16)
            elif c0 < C_G:
                om_ref[:, c0 - C_OM:c1 - C_OM] = res
            else:
                for g in range(nchunk):
                    blk = res[g * CHUNK:(g + 1) * CHUNK, :].T
                    gt_ref[:, g * CHUNK:(g + 1) * CHUNK] = blk[0:N_GATES, :]
            yield

    def conv():
        R = CONV_ROWS
        row = lax.broadcasted_iota(jnp.int32, (R, 1), 0)
        for p in range(tm // R):
            r0 = p * R
            xg = raw_scr[r0:r0 + R, :]
            if p == 0:
                prev_row = jnp.where(within == 0, 0.0, carry_scr[0:1, :])
            else:
                prev_row = raw_scr[r0 - 1:r0, :]
            if r0 + R == tm:
                next_row = jnp.where(within == nblk - 1, 0.0, new_scr[0:1, :])
            else:
                next_row = raw_scr[r0 + R:r0 + R + 1, :]
            xm1 = jnp.where(row == 0, prev_row, pltpu.roll(xg, 1, axis=0))
            xp1 = jnp.where(row == R - 1, next_row, pltpu.roll(xg, R - 1, axis=0))
            z = zero_after(anchors[-1])
            w0, w1, w2 = (cw_ref[:, t * 2 * M_W:(t + 1) * 2 * M_W] + z for t in range(CONV_WIDTH))
            y = xm1 * w0 + xg * w1 + xp1 * w2 + cb_ref[...]
            y = jax.nn.silu(y)
            qc_ref[r0:r0 + R, :] = (y[:, :M_W] * (HEAD_DIM_M ** -0.5)).astype(BF16)
            k_scr[r0:r0 + R, :] = y[:, M_W:]
            yield
            if (r0 + R) % CHUNK == 0:
                g = r0 // CHUNK
                for hd in range(N_HEADS_M):
                    blk = k_scr[g * CHUNK:(g + 1) * CHUNK, hd * HEAD_DIM_M:(hd + 1) * HEAD_DIM_M]
                    base = (g * N_HEADS_M + hd) * HEAD_DIM_M
                    kct_ref[base:base + HEAD_DIM_M, :] = blk.T.astype(BF16)
                yield

    _alternate(project(), conv())
    carry_scr[0:1, :] = raw_scr[tm - 1:tm, :]
    raw_scr[...] = new_scr[...]


def _inproj_call(x, mod, g1, w_in_p, conv_w, conv_b, tm):
    bsz, s, d = x.shape
    nblk = s // tm
    ntiles = bsz * nblk
    outs = (
        jax.ShapeDtypeStruct((bsz, s, QKVA_W), BF16),
        jax.ShapeDtypeStruct((bsz, s, M_W), BF16),
        jax.ShapeDtypeStruct((bsz, s * N_HEADS_M, HEAD_DIM_M), BF16),
        jax.ShapeDtypeStruct((bsz, s, M_W), BF16),
        jax.ShapeDtypeStruct((bsz, s, M_W), F32),
        jax.ShapeDtypeStruct((bsz, N_GATES, s), F32),
    )

    def cur(j):
        t = jnp.minimum(j, ntiles - 1)
        return t // nblk, t % nblk

    def old(j):
        t = jnp.maximum(j - 1, 0)
        return t // nblk, t % nblk

    cur_tile = lambda w: pl.BlockSpec((None, tm, w), lambda j: (*cur(j), 0))
    old_tile = lambda w: pl.BlockSpec((None, tm, w), lambda j: (*old(j), 0))
    const = lambda shp: pl.BlockSpec(shp, lambda j: (0, 0))
    return pl.pallas_call(
        functools.partial(_inproj_kernel, nblk=nblk),
        grid=(ntiles + 1,),
        in_specs=[
            cur_tile(d),
            pl.BlockSpec((None, 1, N_MOD * d), lambda j: (cur(j)[0], 0, 0)),
            const((1, d)),
            pl.BlockSpec((d, IN_COLS_PAD), lambda j: (0, 0), pipeline_mode=pl.Buffered(1)),
            const((1, CONV_WIDTH * 2 * M_W)), const((1, 2 * M_W)),
        ],
        out_specs=[cur_tile(QKVA_W), old_tile(M_W),
                   pl.BlockSpec((None, tm * N_HEADS_M, HEAD_DIM_M), lambda j: (*old(j), 0)),
                   cur_tile(M_W), cur_tile(M_W),
                   pl.BlockSpec((None, N_GATES, tm), lambda j: (cur(j)[0], 0, cur(j)[1]))],
        out_shape=outs,
        scratch_shapes=[pltpu.VMEM((tm, 2 * M_W), F32), pltpu.VMEM((tm, 2 * M_W), F32),
                        pltpu.VMEM((tm, M_W), F32), pltpu.VMEM((8, 2 * M_W), F32)],
        compiler_params=pltpu.CompilerParams(
            dimension_semantics=("arbitrary",), vmem_limit_bytes=VMEM_LIMIT),
        name="inproj",
    )(x, mod, g1, w_in_p, conv_w, conv_b)


def _mlstm_reset(first, c_scr, m_scr):
    @pl.when(first)
    def _init():
        c_scr[...] = jnp.zeros_like(c_scr)
        m_scr[...] = jnp.zeros_like(m_scr)


LOG2E = 1.4426950408889634


def _mlstm_stream(q_ref, kt_ref, v_ref, gt_ref, bgr_ref, h_ref, c_scr, m_scr, *, reverse, nchunk):
    L = CHUNK
    ti = lax.broadcasted_iota(jnp.int32, (L, L), 0)
    si = lax.broadcasted_iota(jnp.int32, (L, L), 1)
    causal = (si >= ti) if reverse else (si <= ti)
    tri_t = jnp.where((ti >= si) if reverse else (ti <= si), 1.0, 0.0).astype(BF16)
    lane = lax.broadcasted_iota(jnp.int32, (1, L), 1)
    last = 0 if reverse else L - 1
    i_off, f_off = (2 * N_HEADS_M, 3 * N_HEADS_M) if reverse else (0, N_HEADS_M)
    ones = jnp.ones((L, HEAD_DIM_M), BF16)
    order = list(range(nchunk - 1, -1, -1) if reverse else range(nchunk))
    rows_of = lambda g: slice(g * L, (g + 1) * L)

    def split3(a):
        hi = a.astype(BF16)
        r1 = a - hi.astype(F32)
        mid = r1.astype(BF16)
        return hi, mid, (r1 - mid.astype(F32)).astype(BF16)

    gates = {}
    pad = jnp.zeros((L - N_GATES, L), F32)
    for g in order:
        gr = gt_ref[:, rows_of(g)] + bgr_ref[...]
        parts = _dot(jnp.concatenate(split3(jax.nn.log_sigmoid(gr)), axis=0), tri_t)
        bcr = (parts[0:N_GATES] + parts[N_GATES:2 * N_GATES] + parts[2 * N_GATES:3 * N_GATES]) * LOG2E
        gates[g] = (gr * LOG2E, bcr, jnp.concatenate([-bcr, pad], axis=0).T)
        yield

    def head(g, h):
        rows = rows_of(g)
        cols = slice(h * HEAD_DIM_M, (h + 1) * HEAD_DIM_M)
        gct, bct, nbcum = gates[g]
        r_row = gct[i_off + h:i_off + h + 1, :] - bct[f_off + h:f_off + h + 1, :]
        btot = jnp.sum(jnp.where(lane == last, bct[f_off + h:f_off + h + 1, :], 0.0),
                       axis=1, keepdims=True)
        rmax = jnp.max(r_row, axis=1, keepdims=True)
        a_max = btot + rmax
        m_prev = m_scr[h:h + 1, 0:1]
        m_new = jnp.maximum(btot + m_prev, a_max)
        m_scr[h:h + 1, :] = jnp.broadcast_to(m_new, (1, LANES))
        decay = jnp.exp2(btot + m_prev - m_new)
        w_row = jnp.exp2(r_row - rmax) * jnp.exp2(a_max - m_new)
        kt_rows = slice((g * N_HEADS_M + h) * HEAD_DIM_M, (g * N_HEADS_M + h + 1) * HEAD_DIM_M)
        cm = jnp.max(jnp.where(causal, r_row, -jnp.inf), axis=1, keepdims=True)
        yield
        q = q_ref[rows, cols]
        s = _dot(q, kt_ref[kt_rows, :])
        u = jnp.maximum(cm, m_prev)
        ub = jnp.broadcast_to(u, (L, L))
        dm = jnp.exp2(jnp.where(causal, r_row - ub, -jnp.inf))
        qi = q * jnp.exp2(m_prev - ub).astype(BF16)
        floor = jnp.exp2(jnp.broadcast_to(nbcum[:, f_off + h:f_off + h + 1], (L, L)) - ub)
        yield
        vaug = jnp.concatenate([v_ref[rows, cols], ones], axis=1)
        lhs = jnp.concatenate([(s * dm).astype(BF16), qi], axis=1)
        rhs = jnp.concatenate([vaug, c_scr[h].astype(BF16)], axis=0)
        out = _dot(lhs, rhs)
        upd = _dot(kt_ref[kt_rows, :] * w_row.astype(BF16), vaug)
        yield
        c_scr[h] = decay * c_scr[h] + upd
        h_ref[rows, cols] = out[:, :HEAD_DIM_M] / jnp.maximum(jnp.abs(out[:, HEAD_DIM_M:]), floor)
        yield

    n_stage = 4
    for g in order:
        heads = [head(g, h) for h in range(N_HEADS_M)]
        for _ in range(n_stage):
            for stream in heads:
                next(stream)
                yield


def _mixer_kernel(qf_ref, kf_ref, vf_ref, gtf_ref, qb_ref, kb_ref, vb_ref, gtb_ref, bgr_ref,
                  sink_ref, qa_ref, kvp_ref, kvn_ref, ga_ref, *rest, nblk, nchunk, ncast):
    w_refs, rest = rest[:ncast], rest[ncast:]
    hf_ref, hb_ref, att_ref = rest[:3]
    wo_refs, (cf_scr, mf_scr, cb_scr, mb_scr, bias_scr) = rest[3:3 + ncast], rest[3 + ncast:]
    b = pl.program_id(0)
    j = pl.program_id(1)

    @pl.when((b == 0) & (j == 0))
    def _init():
        _attn_bias_init(bias_scr)

    _mlstm_reset(j == 0, cf_scr, mf_scr)
    _mlstm_reset(j == 0, cb_scr, mb_scr)
    for w_ref, wo_ref in zip(w_refs, wo_refs):
        wo_ref[...] = w_ref[...].astype(wo_ref.dtype)
    fwd = _mlstm_stream(qf_ref, kf_ref, vf_ref, gtf_ref, bgr_ref, hf_ref, cf_scr, mf_scr,
                        reverse=False, nchunk=nchunk)
    bwd = _mlstm_stream(qb_ref, kb_ref, vb_ref, gtb_ref, bgr_ref, hb_ref, cb_scr, mb_scr,
                        reverse=True, nchunk=nchunk)
    att = _attn_pieces(j == 0, j == nblk - 1, sink_ref, qa_ref, kvp_ref, kvn_ref, ga_ref, att_ref,
                       bias_scr, nsub=nchunk)
    _alternate(fwd, bwd)
    for _ in att:
        pass


def _mixer_call(qc, kct, vm, gates_t, bg_rows, sink, qkva, g_attn, weights, layer, tm):
    bsz, s, _ = qc.shape
    nblk = s // tm
    nsteps = bsz * nblk
    nsub = tm // BLOCK
    nb = s // BLOCK
    kvw = QKVA_W - ATT_Q
    attn_specs = [
        pl.BlockSpec(memory_space=pltpu.SMEM),
        pl.BlockSpec((None, tm, QKVA_W), lambda b, j: (b, j, 0)),
        pl.BlockSpec((None, BLOCK, kvw), lambda b, j: (b, jnp.maximum(j * nsub - 1, 0), 1)),
        pl.BlockSpec((None, BLOCK, kvw), lambda b, j: (b, jnp.minimum((j + 1) * nsub, nb - 1), 1)),
        pl.BlockSpec((1, ATT_Q), lambda b, j: (0, 0)),
    ]
    w_specs, w_shapes = [], []
    for w in weights:
        _, k, n = w.shape
        rows = k // nsteps
        assert rows * nsteps == k and rows % 16 == 0
        w_specs.append(pl.BlockSpec((None, rows, n), lambda b, j: (layer, b * nblk + j, 0)))
        w_shapes.append(jax.ShapeDtypeStruct((k, n), BF16))
    wo_specs = [pl.BlockSpec((sp.block_shape[1], sp.block_shape[2]), lambda b, j: (b * nblk + j, 0))
                for sp in w_specs]

    def specs(pos):
        tile = lambda w: pl.BlockSpec((None, tm, w), lambda b, j: (b, pos(j), 0))
        kt_spec = pl.BlockSpec((None, tm * N_HEADS_M, HEAD_DIM_M), lambda b, j: (b, pos(j), 0))
        gt_spec = pl.BlockSpec((None, N_GATES, tm), lambda b, j: (b, 0, pos(j)))
        return [tile(M_W), kt_spec, tile(M_W), gt_spec], tile(M_W)

    in_f, out_f = specs(lambda j: j)
    in_b, out_b = specs(lambda j: nblk - 1 - j)
    state = [pltpu.VMEM((N_HEADS_M, HEAD_DIM_M, 2 * HEAD_DIM_M), F32), pltpu.VMEM((8, LANES), F32)]
    outs = pl.pallas_call(
        functools.partial(_mixer_kernel, nblk=nblk, nchunk=tm // CHUNK, ncast=len(weights)),
        grid=(bsz, nblk),
        in_specs=in_f + in_b + [pl.BlockSpec((N_GATES, LANES), lambda b, j: (0, 0))] + attn_specs
        + w_specs,
        out_specs=[out_f, out_b, pl.BlockSpec((None, tm, ATT_Q), lambda b, j: (b, j, 0))] + wo_specs,
        out_shape=[jax.ShapeDtypeStruct((bsz, s, M_W), F32)] * 2
        + [jax.ShapeDtypeStruct((bsz, s, ATT_Q), BF16)] + w_shapes,
        scratch_shapes=state + state + [pltpu.VMEM((3 * N_HEADS_ATT, BLOCK, 3 * BLOCK), F32)],
        compiler_params=pltpu.CompilerParams(
            dimension_semantics=("arbitrary", "arbitrary"), vmem_limit_bytes=VMEM_LIMIT),
        name="mixer",
    )(qc, kct, vm, gates_t, qc, kct, vm, gates_t, bg_rows, sink, qkva, qkva, qkva, g_attn, *weights)
    return outs[0], outs[1], outs[2], outs[3:]


def _zero_after(a):
    bits = pltpu.bitcast(a, jnp.uint32)
    z = lax.shift_right_logical(lax.shift_right_logical(bits, jnp.uint32(16)), jnp.uint32(16))
    return pltpu.bitcast(z, F32)


def _attn_bias_init(bias_scr):
    nk = 3 * BLOCK
    row = lax.broadcasted_iota(jnp.int32, (BLOCK, nk), 0)
    col = lax.broadcasted_iota(jnp.int32, (BLOCK, nk), 1)
    dist = jnp.abs(col - BLOCK - row)
    distf = dist.astype(F32)
    for var in range(3):
        ok = dist <= WINDOW
        if var == 1:
            ok = ok & (col >= BLOCK)
        elif var == 2:
            ok = ok & (col < 2 * BLOCK)
        for h in range(N_HEADS_ATT):
            slope = 2.0 ** (-8.0 * (h + 1.0) / N_HEADS_ATT)
            bias_scr[var * N_HEADS_ATT + h] = jnp.where(ok, (-slope * LOG2E) * distf, -jnp.inf)


def _attn_pieces(first, last, sink_ref, q_ref, kvp_ref, kvn_ref, g_ref, o_ref, bias_scr, *, nsub):
    nk = 3 * BLOCK

    lane_k = lax.broadcasted_iota(jnp.int32, (nk, LANES), 1)
    ones_a = jnp.where(lane_k < HEAD_DIM_ATT, 1.0, 0.0).astype(BF16)
    ones_b = jnp.where(lane_k < HEAD_DIM_ATT, 0.0, 1.0).astype(BF16)
    lo_half_q = lax.broadcasted_iota(jnp.int32, (BLOCK, LANES), 1) < HEAD_DIM_ATT

    def kv_block(idx):
        if idx < 0:
            return kvp_ref[...]
        if idx >= nsub:
            return kvn_ref[...]
        return q_ref[idx * BLOCK:(idx + 1) * BLOCK, ATT_Q:QKVA_W]

    for n in range(nsub):
        rows = slice(n * BLOCK, (n + 1) * BLOCK)
        kv = jnp.concatenate([kv_block(n - 1), kv_block(n), kv_block(n + 1)], axis=0)
        if n == 0:
            var = jnp.where(first, 1, 0)
        elif n == nsub - 1:
            var = jnp.where(last, 2, 0)
        else:
            var = 0
        pieces = []
        for kvh in range(N_KV_HEADS):
            k_st, k_sw = kv[:, 0:LANES], kv[:, LANES:2 * LANES]
            v_st, v_sw = kv[:, 2 * LANES:3 * LANES], kv[:, 3 * LANES:4 * LANES]
            if kvh == 0:
                k_lo, k_hi, v_lo, v_hi = k_st, k_sw, v_st, v_sw
            else:
                k_lo, k_hi, v_lo, v_hi = k_sw, k_st, v_sw, v_st
            kk = jnp.concatenate([k_lo * ones_a, k_hi * ones_b], axis=0)
            vv = jnp.concatenate([
                jnp.concatenate([v_lo * ones_a, ones_a], axis=1),
                jnp.concatenate([v_hi * ones_b, ones_b], axis=1)], axis=0)
            for pair in range(GROUP_SIZE // 2):
                h0 = kvh * GROUP_SIZE + 2 * pair
                qp = q_ref[rows, h0 * HEAD_DIM_ATT:(h0 + 2) * HEAD_DIM_ATT]
                s2 = _dot_nt(qp, kk)
                ps, es = [], []
                for t in range(2):
                    logits = s2[:, t * nk:(t + 1) * nk] + bias_scr[var * N_HEADS_ATT + h0 + t]
                    sink = sink_ref[h0 + t] * LOG2E
                    mx = jnp.maximum(jnp.max(logits, axis=-1, keepdims=True), sink)
                    ps.append(jnp.exp2(logits - mx).astype(BF16))
                    es.append(jnp.exp2(sink - mx))
                res = _dot(jnp.concatenate(ps, axis=1), vv)
                den = res[:, LANES:] + jnp.where(lo_half_q, es[0], es[1])
                pieces.append(res[:, :LANES] / den)
                yield
        att = jnp.concatenate(pieces, axis=1)
        o_ref[rows, :] = _rms(att, g_ref[...]).astype(o_ref.dtype)
        yield


def _outffn_body(x_ref, att_ref, hf_ref, hb_ref, om_ref, mod_ref, gm_ref, wo_ref, g2_ref,
                 w1_ref, w2_ref, gf_ref, o_ref, hid_scr, *, final):
    hs = hf_ref[...] + hb_ref[...]
    parts = []
    for h in range(N_HEADS_M):
        cols = slice(h * HEAD_DIM_M, (h + 1) * HEAD_DIM_M)
        parts.append(_rms(hs[:, cols], gm_ref[:, cols]))
    hm = jax.nn.sigmoid(om_ref[...]) * jnp.concatenate(parts, axis=1)
    mixin = jnp.concatenate([att_ref[...], hm.astype(BF16)], axis=1)
    x1 = x_ref[...] + _mod(mod_ref, 2) * _dot(mixin, wo_ref[...])
    hff = (_rms(x1, g2_ref[...] * (1.0 + _mod(mod_ref, 4))) + _mod(mod_ref, 3)).astype(BF16)
    for c in range(N_FF_CHUNKS):
        gate = _dot(hff, w1_ref[:, FF_CHUNK * c:FF_CHUNK * (c + 1)])
        up = _dot(hff, w1_ref[:, D_FF + FF_CHUNK * c:D_FF + FF_CHUNK * (c + 1)])
        hid_scr[:, FF_CHUNK * c:FF_CHUNK * (c + 1)] = (jax.nn.silu(gate) * up).astype(BF16)
    x2 = x1 + _mod(mod_ref, 5) * _dot(hid_scr[...], w2_ref[...])
    if final:
        x2 = _rms(x2, gf_ref[...])
    o_ref[...] = x2


def _outffn_kernel(*refs, final):
    _outffn_body(*refs, final=final)


def _outffn_call(x, att, hf, hb, om, mod, g_m, w_out, g2, w1, w2, g_final, tm, final):
    bsz, s, d = x.shape
    tile = lambda w: pl.BlockSpec((None, tm, w), lambda b, i: (b, i, 0))
    const = lambda shp: pl.BlockSpec(shp, lambda b, i: (0, 0))
    weight = lambda shp: pl.BlockSpec(shp, lambda b, i: (0, 0), pipeline_mode=pl.Buffered(1))
    return pl.pallas_call(
        functools.partial(_outffn_kernel, final=final),
        grid=(bsz, s // tm),
        in_specs=[
            tile(d), tile(ATT_Q), tile(M_W), tile(M_W), tile(M_W),
            pl.BlockSpec((None, 1, N_MOD * d), lambda b, i: (b, 0, 0)),
            const((1, M_W)), weight((ATT_Q + M_W, d)), const((1, d)),
            weight((d, 2 * D_FF)), weight((D_FF, d)), const((1, d)),
        ],
        out_specs=tile(d),
        out_shape=jax.ShapeDtypeStruct((bsz, s, d), F32),
        scratch_shapes=[pltpu.VMEM((tm, D_FF), BF16)],
        compiler_params=pltpu.CompilerParams(
            dimension_semantics=("arbitrary", "arbitrary"), vmem_limit_bytes=VMEM_LIMIT),
        name="outffn",
    )(x, att, hf, hb, om, mod, g_m, w_out, g2, w1, w2, g_final)


def _layer(l, x, c, w_mod, b_mod, g_norm1, w_in, conv_w, conv_b, b_gates, sink,
           g_attn_out, g_mlstm_out, w_out, g_norm2, w_ffn_in, w_ffn_out, g_final, final):
    d = x.shape[-1]
    mod, w_in_p = _mod_call(c, w_mod, b_mod[l], w_in, l, IN_COLS_PAD)
    qkva, qc, kct, vm, om, gates_t = _inproj_call(
        x, mod, g_norm1[l].reshape(1, d), w_in_p, conv_w[l].reshape(1, CONV_WIDTH * 2 * M_W),
        conv_b[l].reshape(1, 2 * M_W), tm=512)

    bg_rows = jnp.broadcast_to(b_gates[l][:, None], (N_GATES, LANES))
    hf, hb, att, (w_out_b, w_ffn_in_b, w_ffn_out_b) = _mixer_call(
        qc, kct, vm, gates_t, bg_rows, sink[l], qkva, g_attn_out[l].reshape(1, ATT_Q),
        (w_out, w_ffn_in, w_ffn_out), l, tm=1024)

    return _outffn_call(x, att, hf, hb, om, mod, g_mlstm_out[l].reshape(1, M_W), w_out_b,
                        g_norm2[l].reshape(1, d), w_ffn_in_b, w_ffn_out_b,
                        g_final.reshape(1, d), tm=512, final=final)


def kernel(x, c, w_mod, b_mod, g_norm1, w_in, conv_w, conv_b, b_gates, sink, g_attn_out,
           g_mlstm_out, w_out, g_norm2, w_ffn_in, w_ffn_out, g_final):
    depth = w_mod.shape[0]
    for l in range(depth):
        x = _layer(l, x, c, w_mod, b_mod, g_norm1, w_in, conv_w, conv_b, b_gates, sink, g_attn_out,
                   g_mlstm_out, w_out, g_norm2, w_ffn_in, w_ffn_out, g_final, final=(l == depth - 1))
    return x
```

```python
import functools

import jax
import jax.numpy as jnp
from jax import lax
from jax.experimental import pallas as pl
from jax.experimental.pallas import tpu as pltpu

F32 = jnp.float32
BF16 = jnp.bfloat16

D_MODEL = 1024
EPS = 1e-6
N_HEADS_ATT = 8
N_KV_HEADS = 2
HEAD_DIM_ATT = 64
GROUP_SIZE = N_HEADS_ATT // N_KV_HEADS
WINDOW = 128
BLOCK = 128
N_HEADS_M = 4
HEAD_DIM_M = 128
CHUNK = 128
CONV_WIDTH = 3
ATT_Q = N_HEADS_ATT * HEAD_DIM_ATT
ATT_KV = N_KV_HEADS * HEAD_DIM_ATT
M_W = N_HEADS_M * HEAD_DIM_M
N_GATES = 4 * N_HEADS_M
D_FF = 2816
N_MOD = 6

LANES = 128
GATE_PAD = LANES
FF_CHUNK = 256
N_FF_CHUNKS = D_FF // FF_CHUNK
VMEM_LIMIT = 56 * 1024 * 1024

C_QA = 0
C_KA = ATT_Q
C_VA = ATT_Q + ATT_KV
C_QKM = ATT_Q + 2 * ATT_KV
C_VM = C_QKM + 2 * M_W
C_OM = C_VM + M_W
C_G = C_OM + M_W
IN_COLS_PAD = C_G + GATE_PAD
QKVA_W = ATT_Q + 4 * ATT_KV


def _dot(a, b):
    return jnp.dot(a, b, preferred_element_type=F32)


def _dot_nt(a, b):
    return lax.dot_general(a, b, (((1,), (1,)), ((), ())), preferred_element_type=F32)


def _rms(x, g):
    return x * lax.rsqrt(jnp.mean(x * x, axis=-1, keepdims=True) + EPS) * g


def _mod(mod_ref, k):
    return mod_ref[:, k * D_MODEL:(k + 1) * D_MODEL]


def _alternate(*streams):
    live = list(streams)
    while live:
        for s in list(live):
            if next(s, StopIteration) is StopIteration:
                live.remove(s)


def _cast_t_kernel(wt_ref, o_ref):
    n = wt_ref.shape[0]
    full = n // LANES * LANES
    for r in range(0, full, LANES):
        o_ref[:, r:r + LANES] = wt_ref[r:r + LANES, :].T.astype(o_ref.dtype)
    if o_ref.shape[1] > full:
        tail = jnp.concatenate(
            [wt_ref[full:n, :], jnp.zeros((full + LANES - n, wt_ref.shape[1]), wt_ref.dtype)], axis=0)
        o_ref[:, full:full + LANES] = tail.T.astype(o_ref.dtype)


SUBLANES = 8
MOD_STEPS = 8


def _mod_kernel(c_ref, w_ref, b_ref, wt_ref, o_ref, wo_ref):
    bsz, d = c_ref.shape
    c = jnp.concatenate([c_ref[...], jnp.zeros((SUBLANES - bsz, d), F32)], axis=0)
    res = _dot(jax.nn.silu(c).astype(BF16), w_ref[...].astype(BF16)) + b_ref[...]
    for b in range(bsz):
        o_ref[b] = res[b:b + 1, :]
    _cast_t_kernel(wt_ref, wo_ref)


def _mod_call(c, w_mod, b_mod, w_in, layer, n_out):
    bsz, d = c.shape
    assert bsz <= SUBLANES
    n_mod = w_mod.shape[2]
    _, k, n = w_in.shape
    assert n_out - n < LANES and n_out % LANES == 0
    bn, bk = n_mod // MOD_STEPS, k // MOD_STEPS
    assert bn * MOD_STEPS == n_mod and bk * MOD_STEPS == k and bn % LANES == 0 and bk % LANES == 0
    return pl.pallas_call(
        _mod_kernel,
        grid=(MOD_STEPS,),
        in_specs=[
            pl.BlockSpec((bsz, d), lambda i: (0, 0)),
            pl.BlockSpec((None, d, bn), lambda i: (layer, 0, i)),
            pl.BlockSpec((1, bn), lambda i: (0, i)),
            pl.BlockSpec((None, n, bk), lambda i: (layer, 0, i)),
        ],
        out_specs=[pl.BlockSpec((bsz, 1, bn), lambda i: (0, 0, i)),
                   pl.BlockSpec((bk, n_out), lambda i: (i, 0))],
        out_shape=[jax.ShapeDtypeStruct((bsz, 1, n_mod), F32), jax.ShapeDtypeStruct((k, n_out), BF16)],
        compiler_params=pltpu.CompilerParams(
            dimension_semantics=("arbitrary",), vmem_limit_bytes=VMEM_LIMIT // 2),
        name="mod",
    )(c, w_mod, b_mod.reshape(1, n_mod), jnp.swapaxes(w_in, 1, 2))


MXU_COLS = 256
CONV_ROWS = 64


def _inproj_kernel(x_ref, mod_ref, g_ref, w_ref, cw_ref, cb_ref,
                   qkva_ref, qc_ref, kct_ref, vm_ref, om_ref, gt_ref,
                   raw_scr, new_scr, k_scr, carry_scr, *, nblk):
    j = pl.program_id(0)

    @pl.when(j == 0)
    def _init():
        raw_scr[...] = jnp.zeros_like(raw_scr)
        carry_scr[...] = jnp.zeros_like(carry_scr)

    x = x_ref[...]
    h = _rms(x, g_ref[...] * (1.0 + _mod(mod_ref, 1))) + _mod(mod_ref, 0)
    hb = h.astype(BF16)
    tm = x.shape[0]
    nchunk = tm // CHUNK
    within = (j + nblk - 1) % nblk

    anchors = []

    def zero_after(a):
        return jnp.concatenate([_zero_after(a)] * (2 * M_W // LANES), axis=1)

    def project():
        order = list(range(C_QKM, C_VM, MXU_COLS)) + list(range(0, C_QKM, MXU_COLS)) + \
            list(range(C_VM, IN_COLS_PAD, MXU_COLS))
        for c0 in order:
            c1 = min(c0 + MXU_COLS, IN_COLS_PAD)
            res = _dot(hb, w_ref[:, c0:c1])
            anchors.append(res[tm - 1:tm, c1 - c0 - LANES:c1 - c0])
            if c0 < C_KA:
                qkva_ref[:, c0:c1] = (res * (HEAD_DIM_ATT ** -0.5 * LOG2E)).astype(BF16)
            elif c0 < C_QKM:
                ka, va = res[:, :ATT_KV], res[:, ATT_KV:]
                half = HEAD_DIM_ATT
                qkva_ref[:, ATT_Q:ATT_Q + ATT_KV] = ka.astype(BF16)
                qkva_ref[:, ATT_Q + ATT_KV:ATT_Q + 2 * ATT_KV] = pltpu.roll(ka, half, axis=1).astype(BF16)
                qkva_ref[:, ATT_Q + 2 * ATT_KV:ATT_Q + 3 * ATT_KV] = va.astype(BF16)
                qkva_ref[:, ATT_Q + 3 * ATT_KV:ATT_Q + 4 * ATT_KV] = pltpu.roll(va, half, axis=1).astype(BF16)
            elif c0 < C_VM:
                new_scr[:, c0 - C_QKM:c1 - C_QKM] = res
            elif c0 < C_OM:
                vm_ref[:, c0 - C_VM:c1 - C_VM] = res.astype(BF16)
            elif c0 < C_G:
                om_ref[:, c0 - C_OM:c1 - C_OM] = res
            else:
                for g in range(nchunk):
                    blk = res[g * CHUNK:(g + 1) * CHUNK, :].T
                    gt_ref[:, g * CHUNK:(g + 1) * CHUNK] = blk[0:N_GATES, :]
            yield

    def conv():
        R = CONV_ROWS
        row = lax.broadcasted_iota(jnp.int32, (R, 1), 0)
        for p in range(tm // R):
            r0 = p * R
            xg = raw_scr[r0:r0 + R, :]
            if p == 0:
                prev_row = jnp.where(within == 0, 0.0, carry_scr[0:1, :])
            else:
                prev_row = raw_scr[r0 - 1:r0, :]
            if r0 + R == tm:
                next_row = jnp.where(within == nblk - 1, 0.0, new_scr[0:1, :])
            else:
                next_row = raw_scr[r0 + R:r0 + R + 1, :]
            xm1 = jnp.where(row == 0, prev_row, pltpu.roll(xg, 1, axis=0))
            xp1 = jnp.where(row == R - 1, next_row, pltpu.roll(xg, R - 1, axis=0))
            z = zero_after(anchors[-1])
            w0, w1, w2 = (cw_ref[:, t * 2 * M_W:(t + 1) * 2 * M_W] + z for t in range(CONV_WIDTH))
            y = xm1 * w0 + xg * w1 + xp1 * w2 + cb_ref[...]
            y = jax.nn.silu(y)
            qc_ref[r0:r0 + R, :] = (y[:, :M_W] * (HEAD_DIM_M ** -0.5)).astype(BF16)
            k_scr[r0:r0 + R, :] = y[:, M_W:]
            yield
            if (r0 + R) % CHUNK == 0:
                g = r0 // CHUNK
                for hd in range(N_HEADS_M):
                    blk = k_scr[g * CHUNK:(g + 1) * CHUNK, hd * HEAD_DIM_M:(hd + 1) * HEAD_DIM_M]
                    base = (g * N_HEADS_M + hd) * HEAD_DIM_M
                    kct_ref[base:base + HEAD_DIM_M, :] = blk.T.astype(BF16)
                yield

    _alternate(project(), conv())
    carry_scr[0:1, :] = raw_scr[tm - 1:tm, :]
    raw_scr[...] = new_scr[...]


def _inproj_call(x, mod, g1, w_in_p, conv_w, conv_b, tm):
    bsz, s, d = x.shape
    nblk = s // tm
    ntiles = bsz * nblk
    outs = (
        jax.ShapeDtypeStruct((bsz, s, QKVA_W), BF16),
        jax.ShapeDtypeStruct((bsz, s, M_W), BF16),
        jax.ShapeDtypeStruct((bsz, s * N_HEADS_M, HEAD_DIM_M), BF16),
        jax.ShapeDtypeStruct((bsz, s, M_W), BF16),
        jax.ShapeDtypeStruct((bsz, s, M_W), F32),
        jax.ShapeDtypeStruct((bsz, N_GATES, s), F32),
    )

    def cur(j):
        t = jnp.minimum(j, ntiles - 1)
        return t // nblk, t % nblk

    def old(j):
        t = jnp.maximum(j - 1, 0)
        return t // nblk, t % nblk

    cur_tile = lambda w: pl.BlockSpec((None, tm, w), lambda j: (*cur(j), 0))
    old_tile = lambda w: pl.BlockSpec((None, tm, w), lambda j: (*old(j), 0))
    const = lambda shp: pl.BlockSpec(shp, lambda j: (0, 0))
    return pl.pallas_call(
        functools.partial(_inproj_kernel, nblk=nblk),
        grid=(ntiles + 1,),
        in_specs=[
            cur_tile(d),
            pl.BlockSpec((None, 1, N_MOD * d), lambda j: (cur(j)[0], 0, 0)),
            const((1, d)),
            pl.BlockSpec((d, IN_COLS_PAD), lambda j: (0, 0), pipeline_mode=pl.Buffered(1)),
            const((1, CONV_WIDTH * 2 * M_W)), const((1, 2 * M_W)),
        ],
        out_specs=[cur_tile(QKVA_W), old_tile(M_W),
                   pl.BlockSpec((None, tm * N_HEADS_M, HEAD_DIM_M), lambda j: (*old(j), 0)),
                   cur_tile(M_W), cur_tile(M_W),
                   pl.BlockSpec((None, N_GATES, tm), lambda j: (cur(j)[0], 0, cur(j)[1]))],
        out_shape=outs,
        scratch_shapes=[pltpu.VMEM((tm, 2 * M_W), F32), pltpu.VMEM((tm, 2 * M_W), F32),
                        pltpu.VMEM((tm, M_W), F32), pltpu.VMEM((8, 2 * M_W), F32)],
        compiler_params=pltpu.CompilerParams(
            dimension_semantics=("arbitrary",), vmem_limit_bytes=VMEM_LIMIT * 3 // 4),
        name="inproj",
    )(x, mod, g1, w_in_p, conv_w, conv_b)


def _mlstm_reset(first, c_scr, m_scr):
    @pl.when(first)
    def _init():
        c_scr[...] = jnp.zeros_like(c_scr)
        m_scr[...] = jnp.zeros_like(m_scr)


LOG2E = 1.4426950408889634


def _mlstm_stream(q_ref, kt_ref, v_ref, gt_ref, bgr_ref, h_ref, c_scr, m_scr, *, reverse, nchunk):
    L = CHUNK
    ti = lax.broadcasted_iota(jnp.int32, (L, L), 0)
    si = lax.broadcasted_iota(jnp.int32, (L, L), 1)
    causal = (si >= ti) if reverse else (si <= ti)
    tri_t = jnp.where((ti >= si) if reverse else (ti <= si), 1.0, 0.0).astype(BF16)
    lane = lax.broadcasted_iota(jnp.int32, (1, L), 1)
    last = 0 if reverse else L - 1
    i_off, f_off = (2 * N_HEADS_M, 3 * N_HEADS_M) if reverse else (0, N_HEADS_M)
    ones = jnp.ones((L, HEAD_DIM_M), BF16)
    order = list(range(nchunk - 1, -1, -1) if reverse else range(nchunk))
    rows_of = lambda g: slice(g * L, (g + 1) * L)

    def split3(a):
        hi = a.astype(BF16)
        r1 = a - hi.astype(F32)
        mid = r1.astype(BF16)
        return hi, mid, (r1 - mid.astype(F32)).astype(BF16)

    gates = {}
    pad = jnp.zeros((L - N_GATES, L), F32)
    for g in order:
        gr = gt_ref[:, rows_of(g)] + bgr_ref[...]
        parts = _dot(jnp.concatenate(split3(jax.nn.log_sigmoid(gr)), axis=0), tri_t)
        bcr = (parts[0:N_GATES] + parts[N_GATES:2 * N_GATES] + parts[2 * N_GATES:3 * N_GATES]) * LOG2E
        gates[g] = (gr * LOG2E, bcr, jnp.concatenate([-bcr, pad], axis=0).T)
        yield

    def head(g, h):
        rows = rows_of(g)
        cols = slice(h * HEAD_DIM_M, (h + 1) * HEAD_DIM_M)
        gct, bct, nbcum = gates[g]
        r_row = gct[i_off + h:i_off + h + 1, :] - bct[f_off + h:f_off + h + 1, :]
        btot = jnp.sum(jnp.where(lane == last, bct[f_off + h:f_off + h + 1, :], 0.0),
                       axis=1, keepdims=True)
        rmax = jnp.max(r_row, axis=1, keepdims=True)
        a_max = btot + rmax
        m_prev = m_scr[h:h + 1, 0:1]
        m_new = jnp.maximum(btot + m_prev, a_max)
        m_scr[h:h + 1, :] = jnp.broadcast_to(m_new, (1, LANES))
        decay = jnp.exp2(btot + m_prev - m_new)
        w_row = jnp.exp2(r_row - rmax) * jnp.exp2(a_max - m_new)
        kt_rows = slice((g * N_HEADS_M + h) * HEAD_DIM_M, (g * N_HEADS_M + h + 1) * HEAD_DIM_M)
        cm = jnp.max(jnp.where(causal, r_row, -jnp.inf), axis=1, keepdims=True)
        yield
        q = q_ref[rows, cols]
        s = _dot(q, kt_ref[kt_rows, :])
        u = jnp.maximum(cm, m_prev)
        ub = jnp.broadcast_to(u, (L, L))
        dm = jnp.exp2(jnp.where(causal, r_row - ub, -jnp.inf))
        qi = q * jnp.exp2(m_prev - ub).astype(BF16)
        floor = jnp.exp2(jnp.broadcast_to(nbcum[:, f_off + h:f_off + h + 1], (L, L)) - ub)
        yield
        vaug = jnp.concatenate([v_ref[rows, cols], ones], axis=1)
        lhs = jnp.concatenate([(s * dm).astype(BF16), qi], axis=1)
        rhs = jnp.concatenate([vaug, c_scr[h].astype(BF16)], axis=0)
        out = _dot(lhs, rhs)
        upd = _dot(kt_ref[kt_rows, :] * w_row.astype(BF16), vaug)
        yield
        c_scr[h] = decay * c_scr[h] + upd
        h_ref[rows, cols] = out[:, :HEAD_DIM_M] / jnp.maximum(jnp.abs(out[:, HEAD_DIM_M:]), floor)
        yield

    n_stage = 4
    for g in order:
        heads = [head(g, h) for h in range(N_HEADS_M)]
        for _ in range(n_stage):
            for stream in heads:
                next(stream)
                yield


def _mixer_kernel(qf_ref, kf_ref, vf_ref, gtf_ref, qb_ref, kb_ref, vb_ref, gtb_ref, bgr_ref,
                  sink_ref, qa_ref, kvp_ref, kvn_ref, ga_ref, *rest, nblk, nchunk, ncast):
    w_refs, rest = rest[:ncast], rest[ncast:]
    hf_ref, hb_ref, att_ref = rest[:3]
    wo_refs, (cf_scr, mf_scr, cb_scr, mb_scr, bias_scr) = rest[3:3 + ncast], rest[3 + ncast:]
    b = pl.program_id(0)
    j = pl.program_id(1)

    @pl.when((b == 0) & (j == 0))
    def _init():
        _attn_bias_init(bias_scr)

    _mlstm_reset(j == 0, cf_scr, mf_scr)
    _mlstm_reset(j == 0, cb_scr, mb_scr)
    for w_ref, wo_ref in zip(w_refs, wo_refs):
        wo_ref[...] = w_ref[...].astype(wo_ref.dtype)
    fwd = _mlstm_stream(qf_ref, kf_ref, vf_ref, gtf_ref, bgr_ref, hf_ref, cf_scr, mf_scr,
                        reverse=False, nchunk=nchunk)
    bwd = _mlstm_stream(qb_ref, kb_ref, vb_ref, gtb_ref, bgr_ref, hb_ref, cb_scr, mb_scr,
                        reverse=True, nchunk=nchunk)
    att = _attn_pieces(j == 0, j == nblk - 1, sink_ref, qa_ref, kvp_ref, kvn_ref, ga_ref, att_ref,
                       bias_scr, nsub=nchunk)
    _alternate(fwd, bwd)
    for _ in att:
        pass


def _mixer_call(qc, kct, vm, gates_t, bg_rows, sink, qkva, g_attn, weights, layer, tm):
    bsz, s, _ = qc.shape
    nblk = s // tm
    nsteps = bsz * nblk
    nsub = tm // BLOCK
    nb = s // BLOCK
    kvw = QKVA_W - ATT_Q
    attn_specs = [
        pl.BlockSpec(memory_space=pltpu.SMEM),
        pl.BlockSpec((None, tm, QKVA_W), lambda b, j: (b, j, 0)),
        pl.BlockSpec((None, BLOCK, kvw), lambda b, j: (b, jnp.maximum(j * nsub - 1, 0), 1)),
        pl.BlockSpec((None, BLOCK, kvw), lambda b, j: (b, jnp.minimum((j + 1) * nsub, nb - 1), 1)),
        pl.BlockSpec((1, ATT_Q), lambda b, j: (0, 0)),
    ]
    w_specs, w_shapes = [], []
    for w in weights:
        _, k, n = w.shape
        rows = k // nsteps
        assert rows * nsteps == k and rows % 16 == 0
        w_specs.append(pl.BlockSpec((None, rows, n), lambda b, j: (layer, b * nblk + j, 0)))
        w_shapes.append(jax.ShapeDtypeStruct((k, n), BF16))
    wo_specs = [pl.BlockSpec((sp.block_shape[1], sp.block_shape[2]), lambda b, j: (b * nblk + j, 0))
                for sp in w_specs]

    def specs(pos):
        tile = lambda w: pl.BlockSpec((None, tm, w), lambda b, j: (b, pos(j), 0))
        kt_spec = pl.BlockSpec((None, tm * N_HEADS_M, HEAD_DIM_M), lambda b, j: (b, pos(j), 0))
        gt_spec = pl.BlockSpec((None, N_GATES, tm), lambda b, j: (b, 0, pos(j)))
        return [tile(M_W), kt_spec, tile(M_W), gt_spec], tile(M_W)

    in_f, out_f = specs(lambda j: j)
    in_b, out_b = specs(lambda j: nblk - 1 - j)
    state = [pltpu.VMEM((N_HEADS_M, HEAD_DIM_M, 2 * HEAD_DIM_M), F32), pltpu.VMEM((8, LANES), F32)]
    outs = pl.pallas_call(
        functools.partial(_mixer_kernel, nblk=nblk, nchunk=tm // CHUNK, ncast=len(weights)),
        grid=(bsz, nblk),
        in_specs=in_f + in_b + [pl.BlockSpec((N_GATES, LANES), lambda b, j: (0, 0))] + attn_specs
        + w_specs,
        out_specs=[out_f, out_b, pl.BlockSpec((None, tm, ATT_Q), lambda b, j: (b, j, 0))] + wo_specs,
        out_shape=[jax.ShapeDtypeStruct((bsz, s, M_W), F32)] * 2
        + [jax.ShapeDtypeStruct((bsz, s, ATT_Q), BF16)] + w_shapes,
        scratch_shapes=state + state + [pltpu.VMEM((3 * N_HEADS_ATT, BLOCK, 3 * BLOCK), F32)],
        compiler_params=pltpu.CompilerParams(
            dimension_semantics=("arbitrary", "arbitrary"), vmem_limit_bytes=VMEM_LIMIT),
        name="mixer",
    )(qc, kct, vm, gates_t, qc, kct, vm, gates_t, bg_rows, sink, qkva, qkva, qkva, g_attn, *weights)
    return outs[0], outs[1], outs[2], outs[3:]


def _zero_after(a):
    bits = pltpu.bitcast(a, jnp.uint32)
    z = lax.shift_right_logical(lax.shift_right_logical(bits, jnp.uint32(16)), jnp.uint32(16))
    return pltpu.bitcast(z, F32)


def _attn_bias_init(bias_scr):
    nk = 3 * BLOCK
    row = lax.broadcasted_iota(jnp.int32, (BLOCK, nk), 0)
    col = lax.broadcasted_iota(jnp.int32, (BLOCK, nk), 1)
    dist = jnp.abs(col - BLOCK - row)
    distf = dist.astype(F32)
    for var in range(3):
        ok = dist <= WINDOW
        if var == 1:
            ok = ok & (col >= BLOCK)
        elif var == 2:
            ok = ok & (col < 2 * BLOCK)
        for h in range(N_HEADS_ATT):
            slope = 2.0 ** (-8.0 * (h + 1.0) / N_HEADS_ATT)
            bias_scr[var * N_HEADS_ATT + h] = jnp.where(ok, (-slope * LOG2E) * distf, -jnp.inf)


def _attn_pieces(first, last, sink_ref, q_ref, kvp_ref, kvn_ref, g_ref, o_ref, bias_scr, *, nsub):
    nk = 3 * BLOCK

    lane_k = lax.broadcasted_iota(jnp.int32, (nk, LANES), 1)
    ones_a = jnp.where(lane_k < HEAD_DIM_ATT, 1.0, 0.0).astype(BF16)
    ones_b = jnp.where(lane_k < HEAD_DIM_ATT, 0.0, 1.0).astype(BF16)
    lo_half_q = lax.broadcasted_iota(jnp.int32, (BLOCK, LANES), 1) < HEAD_DIM_ATT

    def kv_block(idx):
        if idx < 0:
            return kvp_ref[...]
        if idx >= nsub:
            return kvn_ref[...]
        return q_ref[idx * BLOCK:(idx + 1) * BLOCK, ATT_Q:QKVA_W]

    for n in range(nsub):
        rows = slice(n * BLOCK, (n + 1) * BLOCK)
        kv = jnp.concatenate([kv_block(n - 1), kv_block(n), kv_block(n + 1)], axis=0)
        if n == 0:
            var = jnp.where(first, 1, 0)
        elif n == nsub - 1:
            var = jnp.where(last, 2, 0)
        else:
            var = 0
        pieces = []
        for kvh in range(N_KV_HEADS):
            k_st, k_sw = kv[:, 0:LANES], kv[:, LANES:2 * LANES]
            v_st, v_sw = kv[:, 2 * LANES:3 * LANES], kv[:, 3 * LANES:4 * LANES]
            if kvh == 0:
                k_lo, k_hi, v_lo, v_hi = k_st, k_sw, v_st, v_sw
            else:
                k_lo, k_hi, v_lo, v_hi = k_sw, k_st, v_sw, v_st
            kk = jnp.concatenate([k_lo * ones_a, k_hi * ones_b], axis=0)
            vv = jnp.concatenate([
                jnp.concatenate([v_lo * ones_a, ones_a], axis=1),
                jnp.concatenate([v_hi * ones_b, ones_b], axis=1)], axis=0)
            for pair in range(GROUP_SIZE // 2):
                h0 = kvh * GROUP_SIZE + 2 * pair
                qp = q_ref[rows, h0 * HEAD_DIM_ATT:(h0 + 2) * HEAD_DIM_ATT]
                s2 = _dot_nt(qp, kk)
                ps, es = [], []
                for t in range(2):
                    logits = s2[:, t * nk:(t + 1) * nk] + bias_scr[var * N_HEADS_ATT + h0 + t]
                    sink = sink_ref[h0 + t] * LOG2E
                    mx = jnp.maximum(jnp.max(logits, axis=-1, keepdims=True), sink)
                    ps.append(jnp.exp2(logits - mx).astype(BF16))
                    es.append(jnp.exp2(sink - mx))
                res = _dot(jnp.concatenate(ps, axis=1), vv)
                den = res[:, LANES:] + jnp.where(lo_half_q, es[0], es[1])
                pieces.append(res[:, :LANES] / den)
                yield
        att = jnp.concatenate(pieces, axis=1)
        o_ref[rows, :] = _rms(att, g_ref[...]).astype(o_ref.dtype)
        yield


def _outffn_body(x_ref, att_ref, hf_ref, hb_ref, om_ref, mod_ref, gm_ref, wo_ref, g2_ref,
                 w1_ref, w2_ref, gf_ref, o_ref, hid_scr, *, final):
    hs = hf_ref[...] + hb_ref[...]
    parts = []
    for h in range(N_HEADS_M):
        cols = slice(h * HEAD_DIM_M, (h + 1) * HEAD_DIM_M)
        parts.append(_rms(hs[:, cols], gm_ref[:, cols]))
    hm = jax.nn.sigmoid(om_ref[...]) * jnp.concatenate(parts, axis=1)
    mixin = jnp.concatenate([att_ref[...], hm.astype(BF16)], axis=1)
    x1 = x_ref[...] + _mod(mod_ref, 2) * _dot(mixin, wo_ref[...])
    hff = (_rms(x1, g2_ref[...] * (1.0 + _mod(mod_ref, 4))) + _mod(mod_ref, 3)).astype(BF16)
    for c in range(N_FF_CHUNKS):
        gate = _dot(hff, w1_ref[:, FF_CHUNK * c:FF_CHUNK * (c + 1)])
        up = _dot(hff, w1_ref[:, D_FF + FF_CHUNK * c:D_FF + FF_CHUNK * (c + 1)])
        hid_scr[:, FF_CHUNK * c:FF_CHUNK * (c + 1)] = (jax.nn.silu(gate) * up).astype(BF16)
    x2 = x1 + _mod(mod_ref, 5) * _dot(hid_scr[...], w2_ref[...])
    if final:
        x2 = _rms(x2, gf_ref[...])
    o_ref[...] = x2


def _outffn_kernel(*refs, final):
    _outffn_body(*refs, final=final)


def _outffn_call(x, att, hf, hb, om, mod, g_m, w_out, g2, w1, w2, g_final, tm, final):
    bsz, s, d = x.shape
    tile = lambda w: pl.BlockSpec((None, tm, w), lambda b, i: (b, i, 0))
    const = lambda shp: pl.BlockSpec(shp, lambda b, i: (0, 0))
    weight = lambda shp: pl.BlockSpec(shp, lambda b, i: (0, 0), pipeline_mode=pl.Buffered(1))
    return pl.pallas_call(
        functools.partial(_outffn_kernel, final=final),
        grid=(bsz, s // tm),
        in_specs=[
            tile(d), tile(ATT_Q), tile(M_W), tile(M_W), tile(M_W),
            pl.BlockSpec((None, 1, N_MOD * d), lambda b, i: (b, 0, 0)),
            const((1, M_W)), weight((ATT_Q + M_W, d)), const((1, d)),
            weight((d, 2 * D_FF)), weight((D_FF, d)), const((1, d)),
        ],
        out_specs=tile(d),
        out_shape=jax.ShapeDtypeStruct((bsz, s, d), F32),
        scratch_shapes=[pltpu.VMEM((tm, D_FF), BF16)],
        compiler_params=pltpu.CompilerParams(
            dimension_semantics=("arbitrary", "arbitrary"), vmem_limit_bytes=VMEM_LIMIT),
        name="outffn",
    )(x, att, hf, hb, om, mod, g_m, w_out, g2, w1, w2, g_final)


def _layer(l, x, c, w_mod, b_mod, g_norm1, w_in, conv_w, conv_b, b_gates, sink,
           g_attn_out, g_mlstm_out, w_out, g_norm2, w_ffn_in, w_ffn_out, g_final, final):
    d = x.shape[-1]
    mod, w_in_p = _mod_call(c, w_mod, b_mod[l], w_in, l, IN_COLS_PAD)
    qkva, qc, kct, vm, om, gates_t = _inproj_call(
        x, mod, g_norm1[l].reshape(1, d), w_in_p, conv_w[l].reshape(1, CONV_WIDTH * 2 * M_W),
        conv_b[l].reshape(1, 2 * M_W), tm=512)

    bg_rows = jnp.broadcast_to(b_gates[l][:, None], (N_GATES, LANES))
    hf, hb, att, (w_out_b, w_ffn_in_b, w_ffn_out_b) = _mixer_call(
        qc, kct, vm, gates_t, bg_rows, sink[l], qkva, g_attn_out[l].reshape(1, ATT_Q),
        (w_out, w_ffn_in, w_ffn_out), l, tm=1024)

    return _outffn_call(x, att, hf, hb, om, mod, g_mlstm_out[l].reshape(1, M_W), w_out_b,
                        g_norm2[l].reshape(1, d), w_ffn_in_b, w_ffn_out_b,
                        g_final.reshape(1, d), tm=512, final=final)


def kernel(x, c, w_mod, b_mod, g_norm1, w_in, conv_w, conv_b, b_gates, sink, g_attn_out,
           g_mlstm_out, w_out, g_norm2, w_ffn_in, w_ffn_out, g_final):
    depth = w_mod.shape[0]
    for l in range(depth):
        x = _layer(l, x, c, w_mod, b_mod, g_norm1, w_in, conv_w, conv_b, b_gates, sink, g_attn_out,
                   g_mlstm_out, w_out, g_norm2, w_ffn_in, w_ffn_out, g_final, final=(l == depth - 1))
    return x
```
